```python
import math
import jax, jax.numpy as jnp
from jax import lax
import numpy as np

D_MODEL = 2048
BATCH = 4
SEQ = 2048
DEPTH = 2
DEC_BATCH = 128
DEC_SEQ = 4
PAST_LEN = 2048
PAGE_SIZE = 128

D_MIX = D_MODEL
HEAD_DIM = 64
NSA_WIDTH = D_MIX // 4
NSA_HEADS = NSA_WIDTH // HEAD_DIM
NSA_KV_HEADS = 2
NSA_GROUP = NSA_HEADS // NSA_KV_HEADS
CMP_BLOCK = 32
CMP_STRIDE = 16
SEL_BLOCK = 64
N_SELECT = 16
WINDOW = 512
SSM_WIDTH = D_MIX // 2
SSM_GROUP_CH = 16
SSM_GROUPS = SSM_WIDTH // SSM_GROUP_CH
SSM_STATE = 64
DIFF_WIDTH = D_MIX - NSA_WIDTH - SSM_WIDTH
DIFF_HEAD_DIM = 64
DIFF_V_DIM = 2 * DIFF_HEAD_DIM
DIFF_HEADS = DIFF_WIDTH // DIFF_V_DIM
Q_BLOCK = 128
SCALE = HEAD_DIM ** -0.5
DIFF_SCALE = DIFF_HEAD_DIM ** -0.5
PROJ_SIZES = (NSA_WIDTH, 6 * NSA_KV_HEADS * HEAD_DIM, 3 * NSA_HEADS, NSA_WIDTH,
              SSM_WIDTH, SSM_WIDTH,
              DIFF_HEADS * 2 * DIFF_HEAD_DIM, DIFF_WIDTH, DIFF_WIDTH, DIFF_WIDTH)
PROJ_SPLITS = tuple(int(s) for s in np.cumsum(PROJ_SIZES)[:-1])
D_PROJ = int(sum(PROJ_SIZES))
NEG = -1e30
BIG = 1e9

kernel_name = 'hybrid_nsa_s5_diffattn_decoder_step'


def rms_norm(x, w, eps=1e-6):
    xf = x.astype(jnp.float32)
    y = xf * lax.rsqrt(jnp.mean(xf * xf, axis=-1, keepdims=True) + eps)
    return (y * w.astype(jnp.float32)).astype(x.dtype)


def alibi_slopes(n):
    return jnp.asarray(2.0 ** (-8.0 * np.arange(1, n + 1) / n), jnp.float32)


def masked_softmax(s, mask):
    p = jax.nn.softmax(jnp.where(mask, s, NEG), axis=-1)
    return jnp.where(mask, p, 0.0)


def project(xn, w_in):
    b, t = xn.shape[:2]
    p = jnp.einsum('btd,de->bte', xn, w_in)
    q, kv, g, z_nsa, u, z_ssm, dq, dk, dv, z_diff = jnp.split(p, PROJ_SPLITS, axis=-1)
    q = q.reshape(b, t, NSA_KV_HEADS, NSA_GROUP, HEAD_DIM)
    kv = kv.reshape(b, t, 6, NSA_KV_HEADS, HEAD_DIM)
    g = jax.nn.sigmoid(g).reshape(b, t, 3, NSA_KV_HEADS, NSA_GROUP)
    u = u.reshape(b, t, SSM_GROUPS, SSM_GROUP_CH)
    dq = dq.reshape(b, t, DIFF_HEADS, 2, DIFF_HEAD_DIM)
    diff_rows = jnp.stack([dk.reshape(b, t, DIFF_HEADS, DIFF_V_DIM),
                           dv.reshape(b, t, DIFF_HEADS, DIFF_V_DIM)], axis=2)
    return q, kv, g, z_nsa, u, z_ssm, dq, diff_rows, z_diff


def compress_rows(rows, idx_c, w1, pos, w2):
    blk = rows[:, idx_c]
    h = jax.nn.gelu(jnp.einsum('bnlhd,lde->bnhe', blk + pos[None, None, :, None, :], w1))
    return jnp.einsum('bnhe,ed->bnhd', h, w2)


def overlap_matrix(n_cmp, n_sel):
    cs = np.arange(n_cmp)[:, None] * CMP_STRIDE
    ss = np.arange(n_sel)[None, :] * SEL_BLOCK
    ov = np.clip(np.minimum(cs + CMP_BLOCK, ss + SEL_BLOCK) - np.maximum(cs, ss), 0, None)
    return jnp.asarray(ov / CMP_BLOCK, jnp.float32)


def nsa_selected(q, kb, vb, idx, q_pos, slopes):
    b, tq = q.shape[:2]
    gather = jax.vmap(jax.vmap(lambda blocks, ix: blocks[ix]))
    kg = gather(kb, idx)
    vg = gather(vb, idx)
    pos = idx[..., None] * SEL_BLOCK + jnp.arange(SEL_BLOCK)
    dist = jnp.asarray(q_pos)[None, None, :, None, None] - pos
    n_keys = idx.shape[-1] * SEL_BLOCK
    s = jnp.einsum('bqhgd,bhqksd->bhgqks', q, kg, preferred_element_type=jnp.float32) * SCALE
    s = s - slopes[None, :, :, None, None, None] * dist[:, :, None].astype(jnp.float32)
    s = s.reshape(b, NSA_KV_HEADS, NSA_GROUP, tq, n_keys)
    mask = (dist >= 0).reshape(b, NSA_KV_HEADS, 1, tq, n_keys)
    p = masked_softmax(s, mask)
    return jnp.einsum('bhgqn,bhqnd->bqhgd', p.astype(vb.dtype),
                      vg.reshape(b, NSA_KV_HEADS, tq, n_keys, HEAD_DIM))


def nsa_global(q, q_pos, rows, lp, slopes, q_block):
    b, tk = rows.shape[:2]
    tq = q.shape[1]
    n_cmp = (tk - CMP_BLOCK) // CMP_STRIDE + 1
    idx_c = np.arange(n_cmp)[:, None] * CMP_STRIDE + np.arange(CMP_BLOCK)[None, :]
    ends = idx_c[:, -1]
    kc = compress_rows(rows[:, :, 0], idx_c, lp['w_cmp1'][0], lp['pos_cmp'][0], lp['w_cmp2'][0])
    vc = compress_rows(rows[:, :, 1], idx_c, lp['w_cmp1'][1], lp['pos_cmp'][1], lp['w_cmp2'][1])
    dist_c = q_pos[:, None] - ends[None, :]
    s = jnp.einsum('bqhgd,bnhd->bhgqn', q, kc, preferred_element_type=jnp.float32) * SCALE
    s = s - slopes[None, :, :, None, None] * jnp.asarray(dist_c, jnp.float32)
    p_c = masked_softmax(s, dist_c >= 0)
    o_cmp = jnp.einsum('bhgqn,bnhd->bqhgd', p_c.astype(vc.dtype), vc)
    n_sel = -(-tk // SEL_BLOCK)
    score = jnp.einsum('bhqn,nj->bhqj', p_c.sum(axis=2), overlap_matrix(n_cmp, n_sel))
    j = np.arange(n_sel)[None, :]
    cur = (q_pos // SEL_BLOCK)[:, None]
    forced = (j == 0) | (j == cur) | (j == cur - 1)
    avail = j * SEL_BLOCK <= q_pos[:, None]
    score = jnp.where(forced, BIG, jnp.where(avail, score, -BIG))
    k_top = min(N_SELECT, n_sel)
    _, idx = lax.top_k(score, k_top)
    pad = n_sel * SEL_BLOCK - tk
    sel = jnp.pad(rows[:, :, 2:4], ((0, 0), (0, pad), (0, 0), (0, 0), (0, 0)))
    sel = sel.reshape(b, n_sel, SEL_BLOCK, 2, NSA_KV_HEADS, HEAD_DIM).transpose(3, 0, 4, 1, 2, 5)
    kb, vb = sel[0], sel[1]
    if q_block is None:
        o_sel = nsa_selected(q, kb, vb, idx, q_pos, slopes)
    else:
        nc = tq // q_block
        qs = q.reshape(b, nc, q_block, NSA_KV_HEADS, NSA_GROUP, HEAD_DIM).swapaxes(0, 1)
        ids = idx.reshape(b, NSA_KV_HEADS, nc, q_block, k_top).transpose(2, 0, 1, 3, 4)
        ps = jnp.asarray(q_pos.reshape(nc, q_block))
        o = lax.map(lambda a: nsa_selected(a[0], kb, vb, a[1], a[2], slopes), (qs, ids, ps))
        o_sel = o.swapaxes(0, 1).reshape(b, tq, NSA_KV_HEADS, NSA_GROUP, HEAD_DIM)
    return o_cmp, o_sel


def nsa_window(q, k, v, q_pos, k_pos, slopes):
    dist = q_pos[:, :, None] - k_pos[:, None, :]
    mask = (dist >= 0) & (dist < WINDOW) & (k_pos[:, None, :] >= 0)
    s = jnp.einsum('bqhgd,bkhd->bhgqk', q, k, preferred_element_type=jnp.float32) * SCALE
    s = s - slopes[None, :, :, None, None] * jnp.asarray(dist, jnp.float32)[:, None, None]
    p = masked_softmax(s, mask[:, None, None])
    return jnp.einsum('bhgqk,bkhd->bqhgd', p.astype(v.dtype), v)


def nsa_window_banded(q, kw, vw, slopes):
    b, t = q.shape[:2]
    nc = t // Q_BLOCK
    idx = np.arange(nc)[:, None] * Q_BLOCK + np.arange(Q_BLOCK + WINDOW)[None, :]
    pad = ((0, 0), (WINDOW, 0), (0, 0), (0, 0))
    kb = jnp.pad(kw, pad)[:, idx].reshape(b * nc, Q_BLOCK + WINDOW, NSA_KV_HEADS, HEAD_DIM)
    vb = jnp.pad(vw, pad)[:, idx].reshape(b * nc, Q_BLOCK + WINDOW, NSA_KV_HEADS, HEAD_DIM)
    qb = q.reshape(b * nc, Q_BLOCK, NSA_KV_HEADS, NSA_GROUP, HEAD_DIM)
    q_pos = np.tile(np.arange(t).reshape(nc, Q_BLOCK), (b, 1))
    k_pos = np.tile(idx - WINDOW, (b, 1))
    o = nsa_window(qb, kb, vb, q_pos, k_pos, slopes)
    return o.reshape(b, t, NSA_KV_HEADS, NSA_GROUP, HEAD_DIM)


def nsa_combine(g, o_cmp, o_sel, o_win, z):
    b, t = z.shape[:2]
    o = (g[:, :, 0, :, :, None] * o_cmp + g[:, :, 1, :, :, None] * o_sel
         + g[:, :, 2, :, :, None] * o_win)
    return o.reshape(b, t, NSA_WIDTH) * jax.nn.silu(z)


def s5_discretize(lre, lim, log_dt, bre, bim):
    dt = jnp.exp(log_dt)[:, None]
    mag = jnp.exp(lre * dt)
    ar = mag * jnp.cos(lim * dt)
    ai = mag * jnp.sin(lim * dt)
    den = lre * lre + lim * lim
    qr = ((ar - 1.0) * lre + ai * lim) / den
    qi = (ai * lre - (ar - 1.0) * lim) / den
    bbr = qr[..., None] * bre - qi[..., None] * bim
    bbi = qr[..., None] * bim + qi[..., None] * bre
    return ar, ai, bbr, bbi


def complex_affine_combine(e1, e2):
    a1r, a1i, b1r, b1i = e1
    a2r, a2i, b2r, b2i = e2
    return (a1r * a2r - a1i * a2i, a1r * a2i + a1i * a2r,
            a2r * b1r - a2i * b1i + b2r, a2r * b1i + a2i * b1r + b2i)


def s5_mixer(u, z, h0r, h0i, lp):
    b, t = u.shape[:2]
    ar, ai, bbr, bbi = s5_discretize(lp['lam_re'], lp['lam_im'], lp['log_dt'], lp['b_re'], lp['b_im'])
    bur = jnp.einsum('gpc,btgc->btgp', bbr, u)
    bui = jnp.einsum('gpc,btgc->btgp', bbi, u)
    elems = (jnp.broadcast_to(ar, bur.shape), jnp.broadcast_to(ai, bur.shape), bur, bui)
    acr, aci, bcr, bci = lax.associative_scan(complex_affine_combine, elems, axis=1)
    hr = acr * h0r[:, None] - aci * h0i[:, None] + bcr
    hi = acr * h0i[:, None] + aci * h0r[:, None] + bci
    y = (jnp.einsum('gcp,btgp->btgc', lp['c_re'], hr) - jnp.einsum('gcp,btgp->btgc', lp['c_im'], hi)
         + lp['ssm_d'] * u)
    y = jax.nn.gelu(y.reshape(b, t, SSM_WIDTH))
    y = y * jax.nn.sigmoid(jnp.einsum('btc,ce->bte', y, lp['w_glu']))
    return y * jax.nn.silu(z), hr[:, -1], hi[:, -1]


def diff_attend(q, k, v, q_pos, k_pos, lam, slopes):
    dist = jnp.asarray(q_pos)[:, None] - jnp.asarray(k_pos)[None, :]
    s = jnp.einsum('bqhcd,bkhcd->bhcqk', q, k, preferred_element_type=jnp.float32) * DIFF_SCALE
    s = s - slopes[None, :, None, None, None] * dist.astype(jnp.float32)
    p = masked_softmax(s, dist >= 0)
    a = p[:, :, 0] - lam * p[:, :, 1]
    return jnp.einsum('bhqk,bkhv->bqhv', a.astype(v.dtype), v)


def diff_mixer(dq, rows, q_pos, li, lp, z, q_block):
    b, tq = dq.shape[:2]
    tk = rows.shape[1]
    k = rows[:, :, 0].reshape(b, tk, DIFF_HEADS, 2, DIFF_HEAD_DIM)
    v = rows[:, :, 1]
    lam_init = 0.8 - 0.6 * math.exp(-0.3 * li)
    f32 = jnp.float32
    lam = (jnp.exp(jnp.sum(lp['lam_q1'].astype(f32) * lp['lam_k1'].astype(f32)))
           - jnp.exp(jnp.sum(lp['lam_q2'].astype(f32) * lp['lam_k2'].astype(f32))) + lam_init)
    slopes = alibi_slopes(DIFF_HEADS)
    k_pos = np.arange(tk)
    if q_block is None:
        o = diff_attend(dq, k, v, q_pos, k_pos, lam, slopes)
    else:
        nc = tq // q_block
        qs = dq.reshape(b, nc, q_block, DIFF_HEADS, 2, DIFF_HEAD_DIM).swapaxes(0, 1)
        ps = jnp.asarray(q_pos.reshape(nc, q_block))
        o = lax.map(lambda a: diff_attend(a[0], k, v, a[1], k_pos, lam, slopes), (qs, ps))
        o = o.swapaxes(0, 1).reshape(b, tq, DIFF_HEADS, DIFF_V_DIM)
    o = rms_norm(o, lp['diff_ln_w']) * (1.0 - lam_init)
    return o.reshape(b, tq, DIFF_WIDTH) * jax.nn.silu(z)


def merge(x, nsa_o, ssm_o, diff_o, w_out):
    cat = jnp.concatenate([nsa_o, ssm_o, diff_o], axis=-1)
    return x + jnp.einsum('btc,cd->btd', cat, w_out)


def prompt_layer(x, lp, li):
    b, t = x.shape[:2]
    xn = rms_norm(x, lp['norm_w'])
    q, kv, g, z_nsa, u, z_ssm, dq, diff_rows, z_diff = project(xn, lp['w_in'])
    q_pos = np.arange(t)
    slopes = alibi_slopes(NSA_HEADS).reshape(NSA_KV_HEADS, NSA_GROUP)
    o_cmp, o_sel = nsa_global(q, q_pos, kv[:, :, :4], lp, slopes, Q_BLOCK)
    o_win = nsa_window_banded(q, kv[:, :, 4], kv[:, :, 5], slopes)
    nsa_o = nsa_combine(g, o_cmp, o_sel, o_win, z_nsa)
    h0 = jnp.zeros((b, SSM_GROUPS, SSM_STATE), x.dtype)
    ssm_o, hr, hi = s5_mixer(u, z_ssm, h0, h0, lp)
    diff_o = diff_mixer(dq, diff_rows, q_pos, li, lp, z_diff, Q_BLOCK)
    y = merge(x, nsa_o, ssm_o, diff_o, lp['w_out'])
    return y, kv[:, :, :4], diff_rows, kv[:, t - min(WINDOW, t):, 4:6], hr, hi


def sample_layer(x, lp, li, nsa_cache, diff_cache, win_buf, h0r, h0i, page_table):
    b, t = x.shape[:2]
    past = page_table.shape[1] * PAGE_SIZE
    xn = rms_norm(x, lp['norm_w'])
    q, kv, g, z_nsa, u, z_ssm, dq, diff_rows, z_diff = project(xn, lp['w_in'])
    q_pos = past + np.arange(t)
    slopes = alibi_slopes(NSA_HEADS).reshape(NSA_KV_HEADS, NSA_GROUP)
    nsa_past = nsa_cache[page_table].reshape(b, past, 4, NSA_KV_HEADS, HEAD_DIM)
    rows = jnp.concatenate([nsa_past, kv[:, :, :4]], axis=1)
    o_cmp, o_sel = nsa_global(q, q_pos, rows, lp, slopes, None)
    wb = win_buf.shape[1]
    win = jnp.concatenate([win_buf, kv[:, :, 4:6]], axis=1)
    k_pos = past - wb + np.arange(wb + t)
    o_win = nsa_window(q, win[:, :, 0], win[:, :, 1], q_pos[None], k_pos[None], slopes)
    nsa_o = nsa_combine(g, o_cmp, o_sel, o_win, z_nsa)
    ssm_o, hr, hi = s5_mixer(u, z_ssm, h0r, h0i, lp)
    diff_past = diff_cache[page_table].reshape(b, past, 2, DIFF_HEADS, DIFF_V_DIM)
    diff_o = diff_mixer(dq, jnp.concatenate([diff_past, diff_rows], axis=1), q_pos, li, lp, z_diff, None)
    y = merge(x, nsa_o, ssm_o, diff_o, lp['w_out'])
    return y, kv[:, :, :4], diff_rows, win[:, win.shape[1] - wb:], hr, hi


def setup_inputs(seed: int = 0) -> dict:
    key = jax.random.key(seed)
    ks = jax.random.split(key, 32)
    f32 = jnp.float32
    n_pages = PAST_LEN // PAGE_SIZE
    n_used = DEC_BATCH * n_pages
    n_pool = n_used + max(1, n_used // 4)
    wb = min(WINDOW, PAST_LEN)

    def nrm(k, shape, scale):
        return jax.random.normal(k, shape, f32) * scale

    page_table = jax.random.permutation(ks[7], n_pool)[:n_used].reshape(DEC_BATCH, n_pages).astype(jnp.int32)
    lam_im = (jnp.broadcast_to(jnp.pi * jnp.arange(SSM_STATE, dtype=f32), (DEPTH, SSM_GROUPS, SSM_STATE))
              + nrm(ks[13], (DEPTH, SSM_GROUPS, SSM_STATE), 0.01))
    return {
        'x_prompt': nrm(ks[0], (BATCH, SEQ, D_MODEL), 1.0),
        'x_sample': nrm(ks[1], (DEC_BATCH, DEC_SEQ, D_MODEL), 1.0),
        'cache_nsa_kv': nrm(ks[2], (DEPTH, n_pool, PAGE_SIZE, 4, NSA_KV_HEADS, HEAD_DIM), 1.0),
        'cache_diff_kv': nrm(ks[3], (DEPTH, n_pool, PAGE_SIZE, 2, DIFF_HEADS, DIFF_V_DIM), 1.0),
        'state_nsa_win': nrm(ks[4], (DEPTH, DEC_BATCH, wb, 2, NSA_KV_HEADS, HEAD_DIM), 1.0),
        'state_ssm_re': nrm(ks[5], (DEPTH, DEC_BATCH, SSM_GROUPS, SSM_STATE), 0.5),
        'state_ssm_im': nrm(ks[6], (DEPTH, DEC_BATCH, SSM_GROUPS, SSM_STATE), 0.5),
        'page_table': page_table,
        'norm_w': 1.0 + nrm(ks[8], (DEPTH, D_MODEL), 0.02),
        'w_in': nrm(ks[9], (DEPTH, D_MODEL, D_PROJ), D_MODEL ** -0.5),
        'w_out': nrm(ks[10], (DEPTH, D_MIX, D_MODEL), D_MIX ** -0.5),
        'w_cmp1': nrm(ks[11], (DEPTH, 2, CMP_BLOCK, HEAD_DIM, HEAD_DIM), (CMP_BLOCK * HEAD_DIM) ** -0.5),
        'pos_cmp': nrm(ks[12], (DEPTH, 2, CMP_BLOCK, HEAD_DIM), 0.02),
        'w_cmp2': nrm(ks[14], (DEPTH, 2, HEAD_DIM, HEAD_DIM), HEAD_DIM ** -0.5),
        'ssm_lambda_re': -0.5 + nrm(ks[15], (DEPTH, SSM_GROUPS, SSM_STATE), 0.01),
        'ssm_lambda_im': lam_im,
        'ssm_log_dt': jax.random.uniform(ks[16], (DEPTH, SSM_GROUPS), f32, math.log(1e-3), math.log(1e-1)),
        'ssm_b_re': nrm(ks[17], (DEPTH, SSM_GROUPS, SSM_STATE, SSM_GROUP_CH), (2 * SSM_GROUP_CH) ** -0.5),
        'ssm_b_im': nrm(ks[18], (DEPTH, SSM_GROUPS, SSM_STATE, SSM_GROUP_CH), (2 * SSM_GROUP_CH) ** -0.5),
        'ssm_c_re': nrm(ks[19], (DEPTH, SSM_GROUPS, SSM_GROUP_CH, SSM_STATE), (2 * SSM_STATE) ** -0.5),
        'ssm_c_im': nrm(ks[20], (DEPTH, SSM_GROUPS, SSM_GROUP_CH, SSM_STATE), (2 * SSM_STATE) ** -0.5),
        'ssm_d': nrm(ks[21], (DEPTH, SSM_GROUPS, SSM_GROUP_CH), 0.5),
        'w_glu': nrm(ks[22], (DEPTH, SSM_WIDTH, SSM_WIDTH), SSM_WIDTH ** -0.5),
        'lam_q1': nrm(ks[23], (DEPTH, DIFF_HEAD_DIM), 0.1),
        'lam_k1': nrm(ks[24], (DEPTH, DIFF_HEAD_DIM), 0.1),
        'lam_q2': nrm(ks[25], (DEPTH, DIFF_HEAD_DIM), 0.1),
        'lam_k2': nrm(ks[26], (DEPTH, DIFF_HEAD_DIM), 0.1),
        'diff_ln_w': 1.0 + nrm(ks[27], (DEPTH, DIFF_V_DIM), 0.02),
        'final_norm_w': 1.0 + nrm(ks[28], (D_MODEL,), 0.02),
    }


def reference(x_prompt, x_sample, cache_nsa_kv, cache_diff_kv, state_nsa_win, state_ssm_re, state_ssm_im,
              page_table, norm_w, w_in, w_out, w_cmp1, pos_cmp, w_cmp2, ssm_lambda_re, ssm_lambda_im,
              ssm_log_dt, ssm_b_re, ssm_b_im, ssm_c_re, ssm_c_im, ssm_d, w_glu, lam_q1, lam_k1, lam_q2,
              lam_k2, diff_ln_w, final_norm_w):
    xp, xs = x_prompt, x_sample
    p_kv, p_dkv, p_win, p_re, p_im = [], [], [], [], []
    s_kv, s_dkv, s_win, s_re, s_im = [], [], [], [], []
    for li in range(DEPTH):
        lp = {'norm_w': norm_w[li], 'w_in': w_in[li], 'w_out': w_out[li],
              'w_cmp1': w_cmp1[li], 'pos_cmp': pos_cmp[li], 'w_cmp2': w_cmp2[li],
              'lam_re': ssm_lambda_re[li], 'lam_im': ssm_lambda_im[li], 'log_dt': ssm_log_dt[li],
              'b_re': ssm_b_re[li], 'b_im': ssm_b_im[li], 'c_re': ssm_c_re[li], 'c_im': ssm_c_im[li],
              'ssm_d': ssm_d[li], 'w_glu': w_glu[li],
              'lam_q1': lam_q1[li], 'lam_k1': lam_k1[li], 'lam_q2': lam_q2[li], 'lam_k2': lam_k2[li],
              'diff_ln_w': diff_ln_w[li]}
        xp, kv_rows, d_rows, win_rows, hr, hi = prompt_layer(xp, lp, li)
        p_kv.append(kv_rows); p_dkv.append(d_rows); p_win.append(win_rows); p_re.append(hr); p_im.append(hi)
        xs, kv_rows, d_rows, win_rows, hr, hi = sample_layer(
            xs, lp, li, cache_nsa_kv[li], cache_diff_kv[li], state_nsa_win[li],
            state_ssm_re[li], state_ssm_im[li], page_table)
        s_kv.append(kv_rows); s_dkv.append(d_rows); s_win.append(win_rows); s_re.append(hr); s_im.append(hi)
    y_prompt = rms_norm(xp, final_norm_w)
    y_sample = rms_norm(xs, final_norm_w)
    return (y_prompt, y_sample,
            jnp.stack(p_kv), jnp.stack(p_dkv), jnp.stack(p_win), jnp.stack(p_re), jnp.stack(p_im),
            jnp.stack(s_kv), jnp.stack(s_dkv), jnp.stack(s_win), jnp.stack(s_re), jnp.stack(s_im))
```

```python
import functools
import math

import numpy as np
import jax
import jax.numpy as jnp
from jax import lax
from jax.experimental import pallas as pl
from jax.experimental.pallas import tpu as pltpu

F32 = jnp.float32
BF16 = jnp.bfloat16
I32 = jnp.int32

D_MODEL = 2048
HEAD_DIM = 64
NSA_WIDTH = 512
NSA_HEADS = 8
NSA_KV_HEADS = 2
NSA_GROUP = 4
CMP_BLOCK = 32
CMP_STRIDE = 16
SEL_BLOCK = 64
N_SELECT = 16
WINDOW = 512
SSM_WIDTH = 1024
SSM_GROUP_CH = 16
SSM_GROUPS = 64
SSM_STATE = 64
N_STATE = SSM_GROUPS * SSM_STATE
DIFF_WIDTH = 512
DIFF_HEADS = 4
DIFF_HEAD_DIM = 64
DIFF_V_DIM = 128
PAGE_SIZE = 128
SCALE = HEAD_DIM ** -0.5
DIFF_SCALE = DIFF_HEAD_DIM ** -0.5
NEG = -1e30
BIG = 1e9
EPS = 1e-6
NSA_SLOPES = tuple(float(2.0 ** (-8.0 * (k + 1) / NSA_HEADS)) for k in range(NSA_HEADS))
DIFF_SLOPES = tuple(float(2.0 ** (-8.0 * (k + 1) / DIFF_HEADS)) for k in range(DIFF_HEADS))

LANES = 128
SUBLANES = 8
VMEM_LIMIT = 56 * 1024 * 1024

ORIG_SIZES = (NSA_WIDTH, 6 * NSA_KV_HEADS * HEAD_DIM, 3 * NSA_HEADS, NSA_WIDTH, SSM_WIDTH, SSM_WIDTH,
              DIFF_HEADS * 2 * DIFF_HEAD_DIM, DIFF_WIDTH, DIFF_WIDTH, DIFF_WIDTH)
ORIG_OFFS = tuple(int(v) for v in np.concatenate([[0], np.cumsum(ORIG_SIZES)]))
NEW_ORDER = (0, 3, 6, 7, 8, 9, 4, 5, 1, 2)
C_Q, C_ZN, C_DQ, C_DK, C_DV, C_ZD, C_U, C_ZS, C_KV, C_G = 0, 512, 1024, 1536, 2048, 2560, 3072, 4096, 5120, 5888
DP = 6144
N_CMP_PAD = 128
NSEL_PAD = 40
SEL_CHUNK = 512
T_PAD = 8


def _perm_index():
    idx = np.concatenate([np.arange(ORIG_OFFS[s], ORIG_OFFS[s + 1]) for s in NEW_ORDER])
    return idx


def _cparams(sem):
    return pltpu.CompilerParams(dimension_semantics=sem, vmem_limit_bytes=VMEM_LIMIT)


def _dot(a, b):
    return jnp.dot(a, b, preferred_element_type=F32)


def _dot_nt(a, b):
    return lax.dot_general(a, b, (((1,), (1,)), ((), ())), preferred_element_type=F32)


def _gelu(x):
    return 0.5 * x * (1.0 + jnp.tanh(math.sqrt(2.0 / math.pi) * (x + 0.044715 * (x * x * x))))


def _sigmoid(x):
    return 1.0 / (1.0 + jnp.exp(-x))


def _silu(x):
    return x * _sigmoid(x)


def _msoftmax(s, mask):
    s = jnp.where(mask, s, NEG)
    m = jnp.max(s, axis=-1, keepdims=True)
    e = jnp.exp(s - m)
    den = jnp.sum(e, axis=-1, keepdims=True)
    return jnp.where(mask, e * (1.0 / den), 0.0)


def _proj_body(x_ref, nw_ref, w_ref, o_ref, xn_ref):
    @pl.when(pl.program_id(1) == 0)
    def _():
        x = x_ref[...]
        ms = jnp.mean(x * x, axis=-1, keepdims=True)
        xn_ref[...] = (x * lax.rsqrt(ms + EPS) * nw_ref[...]).astype(BF16)

    o_ref[...] = _dot(xn_ref[...], w_ref[...])


def _norm_project(x2d, nw_row, w_perm):
    n = x2d.shape[0]
    tm = min(n, 1024)
    tn = 512
    return pl.pallas_call(
        _proj_body,
        grid=(n // tm, DP // tn),
        in_specs=[pl.BlockSpec((tm, D_MODEL), lambda i, j: (i, 0)),
                  pl.BlockSpec((1, D_MODEL), lambda i, j: (0, 0)),
                  pl.BlockSpec((D_MODEL, tn), lambda i, j: (0, j))],
        out_specs=pl.BlockSpec((tm, tn), lambda i, j: (i, j)),
        out_shape=jax.ShapeDtypeStruct((n, DP), F32),
        scratch_shapes=[pltpu.VMEM((tm, D_MODEL), BF16)],
        compiler_params=_cparams(("parallel", "arbitrary")),
    )(x2d, nw_row, w_perm)


def _out_body(nsa_ref, ssm_ref, diff_ref, x_ref, w_ref, fw_ref, y_ref, *, final):
    acc = x_ref[...]
    acc = acc + _dot(nsa_ref[...].astype(BF16), w_ref[0:NSA_WIDTH, :])
    acc = acc + _dot(ssm_ref[...].astype(BF16), w_ref[NSA_WIDTH:NSA_WIDTH + SSM_WIDTH, :])
    acc = acc + _dot(diff_ref[...].astype(BF16), w_ref[NSA_WIDTH + SSM_WIDTH:, :])
    if final:
        ms = jnp.mean(acc * acc, axis=-1, keepdims=True)
        acc = acc * lax.rsqrt(ms + EPS) * fw_ref[...]
    y_ref[...] = acc


def _merge_out(nsa_o, ssm_o, diff_o, x2d, w_out_bf, fw_row, final):
    n = x2d.shape[0]
    tm = min(n, 512)
    return pl.pallas_call(
        functools.partial(_out_body, final=final),
        grid=(n // tm,),
        in_specs=[pl.BlockSpec((tm, NSA_WIDTH), lambda i: (i, 0)),
                  pl.BlockSpec((tm, SSM_WIDTH), lambda i: (i, 0)),
                  pl.BlockSpec((tm, DIFF_WIDTH), lambda i: (i, 0)),
                  pl.BlockSpec((tm, D_MODEL), lambda i: (i, 0)),
                  pl.BlockSpec((D_MODEL, D_MODEL), lambda i: (0, 0)),
                  pl.BlockSpec((1, D_MODEL), lambda i: (0, 0))],
        out_specs=pl.BlockSpec((tm, D_MODEL), lambda i: (i, 0)),
        out_shape=jax.ShapeDtypeStruct((n, D_MODEL), F32),
        compiler_params=_cparams(("parallel",)),
    )(nsa_o, ssm_o, diff_o, x2d, w_out_bf, fw_row)


def _pad_heads(qb):
    tq = qb.shape[0]
    low = lax.broadcasted_iota(I32, (tq, LANES), 1) < HEAD_DIM
    outs = []
    for hd in range(NSA_HEADS):
        h = hd // NSA_GROUP
        slab = qb[:, (hd // 2) * LANES:(hd // 2 + 1) * LANES]
        if hd % 2 != h:
            slab = pltpu.roll(slab, HEAD_DIM, 1)
        keep = low if h == 0 else jnp.logical_not(low)
        outs.append(jnp.where(keep, slab, 0.0).astype(BF16))
    return outs


def _compress_tokens(x_bf, wab, pos_a, pos_b, w2bd):
    ab = _dot(x_bf, wab)
    pc = _dot(pos_a, wab)[0:1, 0:LANES] + _dot(pos_b, wab)[0:1, LANES:2 * LANES]
    pre = ab[:, 0:LANES] + pltpu.roll(ab[:, LANES:2 * LANES], N_CMP_PAD - 1, 0) + pc
    return _dot(_gelu(pre).astype(BF16), w2bd).astype(BF16)


def _cmp_branch(qh, kc, vc, qpos_col):
    n_iota = lax.broadcasted_iota(I32, (1, N_CMP_PAD), 1)
    ends = n_iota * CMP_STRIDE + (CMP_BLOCK - 1)
    endsf = ends.astype(F32)
    mask = ends <= qpos_col
    o_list, p_list = [], []
    for hd in range(NSA_HEADS):
        s = _dot_nt(qh[hd], kc) + NSA_SLOPES[hd] * endsf
        p = _msoftmax(s, mask)
        o_list.append(_dot(p.astype(BF16), vc))
        p_list.append(p)
    return o_list, p_list


def _select_blocks(psum, qpos_row, ovt, n_sel):
    hi = psum.astype(BF16)
    r1 = psum - hi.astype(F32)
    mid = r1.astype(BF16)
    lo = (r1 - mid.astype(F32)).astype(BF16)
    sc = _dot_nt(ovt, hi) + _dot_nt(ovt, mid) + _dot_nt(ovt, lo)
    j = lax.broadcasted_iota(I32, (NSEL_PAD, LANES), 0)
    cur = qpos_row // SEL_BLOCK
    forced = (j == 0) | (j == cur) | (j == cur - 1)
    avail = j * SEL_BLOCK <= qpos_row
    sc = jnp.where(forced, BIG, jnp.where(avail, sc, -BIG))
    cnt = jnp.zeros((NSEL_PAD, LANES), I32)
    for ii in range(n_sel):
        row = sc[ii:ii + 1, :]
        beats = (row > sc) | ((row == sc) & (ii < j))
        cnt = cnt + beats.astype(I32)
    k_top = min(N_SELECT, n_sel)
    sel_t = jnp.where((cnt < k_top) & (j < n_sel), 1.0, 0.0).astype(F32)
    sel_t = jnp.concatenate([sel_t, jnp.zeros((LANES - NSEL_PAD, LANES), F32)], axis=0)
    return sel_t.T


def _combine_heads(gs, o_cmp, o_sel, o_win, z):
    tq = z.shape[0]
    low = lax.broadcasted_iota(I32, (tq, LANES), 1) < HEAD_DIM
    placed = []
    for hd in range(NSA_HEADS):
        h = hd // NSA_GROUP
        o = (gs[:, hd:hd + 1] * o_cmp[hd] + gs[:, NSA_HEADS + hd:NSA_HEADS + hd + 1] * o_sel[hd]
             + gs[:, 2 * NSA_HEADS + hd:2 * NSA_HEADS + hd + 1] * o_win[hd])
        if hd % 2 != h:
            o = pltpu.roll(o, HEAD_DIM, 1)
        placed.append(o)
    slabs = [jnp.where(low, placed[2 * k], placed[2 * k + 1]) for k in range(NSA_HEADS // 2)]
    return jnp.concatenate(slabs, axis=1) * _silu(z)


def _compress_prompt_body(rows_ref, wab_ref, pa_ref, pb_ref, w2_ref, o_ref):
    xs = [rows_ref[pl.ds(l, N_CMP_PAD, stride=CMP_STRIDE), :].astype(BF16) for l in range(CMP_STRIDE)]
    x = jnp.concatenate(xs, axis=1)
    o_ref[...] = _compress_tokens(x, wab_ref[...], pa_ref[...], pb_ref[...], w2_ref[...])


def _compress_prompt(p3, wab, pos_a, pos_b, w2bd):
    b, t, _ = p3.shape
    kvblk = C_KV // LANES
    return pl.pallas_call(
        _compress_prompt_body,
        grid=(b, 2),
        in_specs=[pl.BlockSpec((None, t, LANES), lambda i, k: (i, 0, kvblk + k)),
                  pl.BlockSpec((None, CMP_STRIDE * LANES, 2 * LANES), lambda i, k: (k, 0, 0)),
                  pl.BlockSpec((None, SUBLANES, CMP_STRIDE * LANES), lambda i, k: (k, 0, 0)),
                  pl.BlockSpec((None, SUBLANES, CMP_STRIDE * LANES), lambda i, k: (k, 0, 0)),
                  pl.BlockSpec((None, LANES, LANES), lambda i, k: (k, 0, 0))],
        out_specs=pl.BlockSpec((None, None, N_CMP_PAD, LANES), lambda i, k: (i, k, 0, 0)),
        out_shape=jax.ShapeDtypeStruct((b, 2, N_CMP_PAD, LANES), BF16),
        compiler_params=_cparams(("parallel", "parallel")),
    )(p3, wab, pos_a, pos_b, w2bd)


def _nsa_prompt_body(q_ref, z_ref, g_ref, cmp_ref, sk_ref, sv_ref, wk_ref, wv_ref, ovt_ref, e_ref, o_ref,
                     *, n_sel):
    tq = 128
    i = pl.program_id(1)
    q0 = i * tq
    qh = _pad_heads(q_ref[...] * SCALE)
    gs = _sigmoid(g_ref[...])
    qpos_col = q0 + lax.broadcasted_iota(I32, (tq, 1), 0)
    qpos_row = q0 + lax.broadcasted_iota(I32, (1, LANES), 1)
    o_cmp, p_cmp = _cmp_branch(qh, cmp_ref[0], cmp_ref[1], qpos_col)
    ovt = ovt_ref[...]

    w0 = pl.multiple_of(jnp.maximum(q0 - WINDOW, 0), LANES)
    wwid = WINDOW + tq
    kpos_w = w0 + lax.broadcasted_iota(I32, (1, wwid), 1)
    kposf_w = kpos_w.astype(F32)
    kwin = wk_ref[pl.ds(w0, wwid), :].astype(BF16)
    vwin = wv_ref[pl.ds(w0, wwid), :].astype(BF16)
    nch = (i + SEL_CHUNK // tq) // (SEL_CHUNK // tq)

    o_sel, o_win = [], []
    for h in range(NSA_KV_HEADS):
        heads = range(h * NSA_GROUP, (h + 1) * NSA_GROUP)
        psum = p_cmp[h * NSA_GROUP]
        for hd in list(heads)[1:]:
            psum = psum + p_cmp[hd]
        sel = _select_blocks(psum, qpos_row, ovt, n_sel)
        sel4 = jnp.concatenate([sel] * NSA_GROUP, axis=0).astype(BF16)
        q4 = jnp.concatenate([qh[hd] for hd in heads], axis=0)
        slope_col = jnp.concatenate([jnp.full((tq, 1), NSA_SLOPES[hd], F32) for hd in heads], axis=0)
        qpos4 = jnp.concatenate([qpos_col] * NSA_GROUP, axis=0)

        def body(c, carry, q4=q4, sel4=sel4, slope_col=slope_col, qpos4=qpos4):
            m, l, acc = carry
            k0 = pl.multiple_of(c * SEL_CHUNK, SEL_CHUNK)
            kch = sk_ref[pl.ds(k0, SEL_CHUNK), :].astype(BF16)
            vch = sv_ref[pl.ds(k0, SEL_CHUNK), :].astype(BF16)
            kpos = k0 + lax.broadcasted_iota(I32, (1, SEL_CHUNK), 1)
            s = _dot_nt(q4, kch) + slope_col * kpos.astype(F32)
            mexp = _dot(sel4, e_ref[c])
            mask = (mexp > 0.5) & (kpos <= qpos4)
            s = jnp.where(mask, s, NEG)
            m_new = jnp.maximum(m, jnp.max(s, axis=-1, keepdims=True))
            alpha = jnp.exp(m - m_new)
            p = jnp.where(mask, jnp.exp(s - m_new), 0.0)
            l = alpha * l + jnp.sum(p, axis=-1, keepdims=True)
            acc = alpha * acc + _dot(p.astype(BF16), vch)
            return m_new, l, acc

        init = (jnp.full((NSA_GROUP * tq, 1), NEG, F32), jnp.zeros((NSA_GROUP * tq, 1), F32),
                jnp.zeros((NSA_GROUP * tq, LANES), F32))
        _, l, acc = lax.fori_loop(0, nch, body, init)
        osel4 = acc * jnp.where(l > 0.0, 1.0 / l, 0.0)

        sw = _dot_nt(q4, kwin) + slope_col * kposf_w
        dist = qpos4 - kpos_w
        pw = _msoftmax(sw, (dist >= 0) & (dist < WINDOW))
        owin4 = _dot(pw.astype(BF16), vwin)
        for g in range(NSA_GROUP):
            o_sel.append(osel4[g * tq:(g + 1) * tq])
            o_win.append(owin4[g * tq:(g + 1) * tq])

    o_ref[...] = _combine_heads(gs, o_cmp, o_sel, o_win, z_ref[...]).astype(o_ref.dtype)


def _nsa_prompt(p3, cmp_tok, ovt, e3):
    b, t, _ = p3.shape
    tq = 128
    n_sel = -(-t // SEL_BLOCK)
    kvb = C_KV // LANES
    return pl.pallas_call(
        functools.partial(_nsa_prompt_body, n_sel=n_sel),
        grid=(b, t // tq),
        in_specs=[pl.BlockSpec((None, tq, NSA_WIDTH), lambda bi, i: (bi, i, C_Q // NSA_WIDTH)),
                  pl.BlockSpec((None, tq, NSA_WIDTH), lambda bi, i: (bi, i, C_ZN // NSA_WIDTH)),
                  pl.BlockSpec((None, tq, LANES), lambda bi, i: (bi, i, C_G // LANES)),
                  pl.BlockSpec((None, 2, N_CMP_PAD, LANES), lambda bi, i: (bi, 0, 0, 0)),
                  pl.BlockSpec((None, t, LANES), lambda bi, i: (bi, 0, kvb + 2)),
                  pl.BlockSpec((None, t, LANES), lambda bi, i: (bi, 0, kvb + 3)),
                  pl.BlockSpec((None, t, LANES), lambda bi, i: (bi, 0, kvb + 4)),
                  pl.BlockSpec((None, t, LANES), lambda bi, i: (bi, 0, kvb + 5)),
                  pl.BlockSpec((NSEL_PAD, LANES), lambda bi, i: (0, 0)),
                  pl.BlockSpec(e3.shape, lambda bi, i: (0, 0, 0))],
        out_specs=pl.BlockSpec((None, tq, NSA_WIDTH), lambda bi, i: (bi, i, 0)),
        out_shape=jax.ShapeDtypeStruct((b, t, NSA_WIDTH), BF16),
        compiler_params=_cparams(("parallel", "arbitrary")),
    )(p3, p3, p3, cmp_tok, p3, p3, p3, p3, ovt, e3)


def _diff_lambda(lam_ref, lam_init):
    a = lam_ref[...]
    s1 = jnp.sum(a[0:1] * a[1:2], axis=-1, keepdims=True)
    s2 = jnp.sum(a[2:3] * a[3:4], axis=-1, keepdims=True)
    return jnp.exp(s1) - jnp.exp(s2) + lam_init


def _diff_finish(o, lnw, z, lam_init):
    ms = jnp.mean(o * o, axis=-1, keepdims=True)
    return o * lax.rsqrt(ms + EPS) * lnw * (1.0 - lam_init) * _silu(z)


def _diff_prompt_body(lam_ref, q_ref, z_ref, k_ref, v_ref, lnw_ref, o_ref, *, lam_init):
    tq = 128
    i = pl.program_id(1)
    q0 = i * tq
    lam = _diff_lambda(lam_ref, lam_init)
    low = lax.broadcasted_iota(I32, (tq, LANES), 1) < DIFF_HEAD_DIM
    qpos2 = q0 + (lax.broadcasted_iota(I32, (2 * tq, 1), 0) % tq)
    nch = (i + SEL_CHUNK // tq) // (SEL_CHUNK // tq)
    lnw = lnw_ref[...]
    slabs = []
    for hh in range(DIFF_HEADS):
        cs = slice(hh * LANES, (hh + 1) * LANES)
        qs = q_ref[:, cs] * DIFF_SCALE
        q2 = jnp.concatenate([jnp.where(low, qs, 0.0), jnp.where(low, 0.0, qs)], axis=0).astype(BF16)

        def body(c, carry, q2=q2, cs=cs, hh=hh):
            m, l, acc = carry
            k0 = pl.multiple_of(c * SEL_CHUNK, SEL_CHUNK)
            kch = k_ref[pl.ds(k0, SEL_CHUNK), cs].astype(BF16)
            vch = v_ref[pl.ds(k0, SEL_CHUNK), cs].astype(BF16)
            kpos = k0 + lax.broadcasted_iota(I32, (1, SEL_CHUNK), 1)
            s = _dot_nt(q2, kch) + DIFF_SLOPES[hh] * kpos.astype(F32)
            mask = kpos <= qpos2
            s = jnp.where(mask, s, NEG)
            m_new = jnp.maximum(m, jnp.max(s, axis=-1, keepdims=True))
            alpha = jnp.exp(m - m_new)
            p = jnp.where(mask, jnp.exp(s - m_new), 0.0)
            l = alpha * l + jnp.sum(p, axis=-1, keepdims=True)
            acc = alpha * acc + _dot(p.astype(BF16), vch)
            return m_new, l, acc

        init = (jnp.full((2 * tq, 1), NEG, F32), jnp.zeros((2 * tq, 1), F32), jnp.zeros((2 * tq, LANES), F32))
        _, l, acc = lax.fori_loop(0, nch, body, init)
        on = acc * (1.0 / l)
        o = on[0:tq] - lam * on[tq:2 * tq]
        slabs.append(_diff_finish(o, lnw, z_ref[:, cs], lam_init))
    o_ref[...] = jnp.concatenate(slabs, axis=1).astype(o_ref.dtype)


def _diff_prompt(p3, lam4, lnw_row, lam_init):
    b, t, _ = p3.shape
    tq = 128
    return pl.pallas_call(
        functools.partial(_diff_prompt_body, lam_init=lam_init),
        grid=(b, t // tq),
        in_specs=[pl.BlockSpec((4, DIFF_HEAD_DIM), lambda bi, i: (0, 0)),
                  pl.BlockSpec((None, tq, DIFF_WIDTH), lambda bi, i: (bi, i, C_DQ // DIFF_WIDTH)),
                  pl.BlockSpec((None, tq, DIFF_WIDTH), lambda bi, i: (bi, i, C_ZD // DIFF_WIDTH)),
                  pl.BlockSpec((None, t, DIFF_WIDTH), lambda bi, i: (bi, 0, C_DK // DIFF_WIDTH)),
                  pl.BlockSpec((None, t, DIFF_WIDTH), lambda bi, i: (bi, 0, C_DV // DIFF_WIDTH)),
                  pl.BlockSpec((1, DIFF_V_DIM), lambda bi, i: (0, 0))],
        out_specs=pl.BlockSpec((None, tq, DIFF_WIDTH), lambda bi, i: (bi, i, 0)),
        out_shape=jax.ShapeDtypeStruct((b, t, DIFF_WIDTH), BF16),
        compiler_params=_cparams(("parallel", "arbitrary")),
    )(lam4, p3, p3, p3, p3, lnw_row)


SSM_BLOCKS = SSM_WIDTH // LANES
SSM_BS = N_STATE // SSM_BLOCKS
SSM_CHUNK = 256
SSM_SEG = SSM_CHUNK // SUBLANES


def _discretize(lre, lim, ldt):
    dt = jnp.exp(ldt)
    mag = jnp.exp(lre * dt)
    ar = mag * jnp.cos(lim * dt)
    ai = mag * jnp.sin(lim * dt)
    den = lre * lre + lim * lim
    qr = ((ar - 1.0) * lre + ai * lim) / den
    qi = (ai * lre - (ar - 1.0) * lim) / den
    return ar, ai, qr, qi


def _bbar_block(qr, qi, bre, bim):
    return jnp.concatenate([qr * bre - qi * bim, qr * bim + qi * bre], axis=1).astype(BF16)


def _glu_tail(y, wglu, z):
    gate = _sigmoid(_dot(y.astype(BF16), wglu))
    return y * gate * _silu(z)


def _ssm_prompt_body(u_ref, z_ref, lre_ref, lim_ref, ldt_ref, bre_ref, bim_ref, cmat_ref, d_ref, wglu_ref,
                     o_ref, hr_ref, hi_ref,
                     bbar_s, apr_s, api_s, bu_s, y_s, carr_s, cari_s):
    first = (pl.program_id(0) == 0) & (pl.program_id(1) == 0)

    @pl.when(first)
    def _():
        ar, ai, qr, qi = _discretize(lre_ref[...], lim_ref[...], ldt_ref[...])
        for k in range(SSM_BLOCKS):
            cs = slice(k * SSM_BS, (k + 1) * SSM_BS)
            bbar_s[k] = _bbar_block(qr[:, cs], qi[:, cs], bre_ref[k], bim_ref[k])
        pr, pi = ar, ai
        apr_s[0:1, :] = pr
        api_s[0:1, :] = pi
        for k in range(1, SSM_SEG):
            pr, pi = pr * ar - pi * ai, pr * ai + pi * ar
            apr_s[k:k + 1, :] = pr
            api_s[k:k + 1, :] = pi

    @pl.when(pl.program_id(1) == 0)
    def _():
        carr_s[...] = jnp.zeros_like(carr_s)
        cari_s[...] = jnp.zeros_like(cari_s)

    nslab = SSM_BS // LANES

    def seg_get(t, base):
        rows = pl.ds(t, SUBLANES, stride=SSM_SEG)
        return jnp.concatenate([bu_s[base + c, rows, :] for c in range(nslab)], axis=1)

    def seg_put(t, base, val):
        rows = pl.ds(t, SUBLANES, stride=SSM_SEG)
        for c in range(nslab):
            bu_s[base + c, rows, :] = val[:, c * LANES:(c + 1) * LANES]

    for k in range(SSM_BLOCKS):
        cs = slice(k * SSM_BS, (k + 1) * SSM_BS)
        us = slice(k * LANES, (k + 1) * LANES)
        uk = u_ref[:, us]
        bu = _dot(uk.astype(BF16), bbar_s[k])
        for c in range(2 * nslab):
            bu_s[c] = bu[:, c * LANES:(c + 1) * LANES]
        ar_b = jnp.broadcast_to(apr_s[0:1, cs], (SUBLANES, SSM_BS))
        ai_b = jnp.broadcast_to(api_s[0:1, cs], (SUBLANES, SSM_BS))

        def step(t, st, ar_b=ar_b, ai_b=ai_b):
            hr, hi = st
            nr = ar_b * hr - ai_b * hi + seg_get(t, 0)
            ni = ar_b * hi + ai_b * hr + seg_get(t, nslab)
            seg_put(t, 0, nr)
            seg_put(t, nslab, ni)
            return nr, ni

        zero = jnp.zeros((SUBLANES, SSM_BS), F32)
        er, ei = lax.fori_loop(0, SSM_SEG, step, (zero, zero))
        a32r = apr_s[SSM_SEG - 1:SSM_SEG, cs]
        a32i = api_s[SSM_SEG - 1:SSM_SEG, cs]
        rows_r = [carr_s[:, cs]]
        rows_i = [cari_s[:, cs]]
        for s in range(1, SUBLANES + 1):
            pr, pi = rows_r[-1], rows_i[-1]
            rows_r.append(er[s - 1:s] + a32r * pr - a32i * pi)
            rows_i.append(ei[s - 1:s] + a32r * pi + a32i * pr)
        carr_s[:, cs] = rows_r[SUBLANES]
        cari_s[:, cs] = rows_i[SUBLANES]
        hin_r = jnp.concatenate(rows_r[:SUBLANES], axis=0)
        hin_i = jnp.concatenate(rows_i[:SUBLANES], axis=0)

        def fix(t, carry, hin_r=hin_r, hin_i=hin_i, cs=cs):
            pr = apr_s[pl.ds(t, 1), cs]
            pi = api_s[pl.ds(t, 1), cs]
            seg_put(t, 0, seg_get(t, 0) + pr * hin_r - pi * hin_i)
            seg_put(t, nslab, seg_get(t, nslab) + pr * hin_i + pi * hin_r)
            return carry

        lax.fori_loop(0, SSM_SEG, fix, 0)
        hcat = jnp.concatenate([bu_s[c] for c in range(2 * nslab)], axis=1).astype(BF16)
        yk = _dot(hcat, cmat_ref[k]) + d_ref[:, us] * uk
        y_s[:, us] = _gelu(yk)

    o_ref[...] = _glu_tail(y_s[...], wglu_ref[...], z_ref[...]).astype(o_ref.dtype)
    hr_ref[...] = carr_s[...]
    hi_ref[...] = cari_s[...]


def _ssm_prompt(p3, sp):
    b, t, _ = p3.shape
    full = lambda shape: pl.BlockSpec(shape, lambda bi, j: (0,) * len(shape))
    out, hr, hi = pl.pallas_call(
        _ssm_prompt_body,
        grid=(b, t // SSM_CHUNK),
        in_specs=[pl.BlockSpec((None, SSM_CHUNK, SSM_WIDTH), lambda bi, j: (bi, j, C_U // SSM_WIDTH)),
                  pl.BlockSpec((None, SSM_CHUNK, SSM_WIDTH), lambda bi, j: (bi, j, C_ZS // SSM_WIDTH)),
                  full((1, N_STATE)), full((1, N_STATE)), full((1, N_STATE)),
                  full((SSM_BLOCKS, LANES, SSM_BS)), full((SSM_BLOCKS, LANES, SSM_BS)),
                  full((SSM_BLOCKS, 2 * SSM_BS, LANES)), full((1, SSM_WIDTH)),
                  full((SSM_WIDTH, SSM_WIDTH))],
        out_specs=[pl.BlockSpec((None, SSM_CHUNK, SSM_WIDTH), lambda bi, j: (bi, j, 0)),
                   pl.BlockSpec((None, 1, N_STATE), lambda bi, j: (bi, 0, 0)),
                   pl.BlockSpec((None, 1, N_STATE), lambda bi, j: (bi, 0, 0))],
        out_shape=[jax.ShapeDtypeStruct((b, t, SSM_WIDTH), BF16),
                   jax.ShapeDtypeStruct((b, 1, N_STATE), F32),
                   jax.ShapeDtypeStruct((b, 1, N_STATE), F32)],
        scratch_shapes=[pltpu.VMEM((SSM_BLOCKS, LANES, 2 * SSM_BS), BF16),
                        pltpu.VMEM((SSM_SEG, N_STATE), F32), pltpu.VMEM((SSM_SEG, N_STATE), F32),
                        pltpu.VMEM((2 * SSM_BS // LANES, SSM_CHUNK, LANES), F32),
                        pltpu.VMEM((SSM_CHUNK, SSM_WIDTH), F32),
                        pltpu.VMEM((1, N_STATE), F32), pltpu.VMEM((1, N_STATE), F32)],
        compiler_params=_cparams(("arbitrary", "arbitrary")),
    )(p3, p3, sp['lre'], sp['lim'], sp['ldt'], sp['bre'], sp['bim'], sp['cmat'], sp['d'], sp['wglu'])
    return out, hr, hi


def _ssm_sample_body(u_ref, z_ref, h0r_ref, h0i_ref, lre_ref, lim_ref, ldt_ref, bre_ref, bim_ref, cmat_ref,
                     d_ref, wglu_ref, o_ref, hr_ref, hi_ref, y_s, *, n_t):
    k = pl.program_id(0)
    ar, ai, qr, qi = _discretize(lre_ref[...], lim_ref[...], ldt_ref[...])
    bbar = _bbar_block(qr, qi, bre_ref[...], bim_ref[...])
    hr = h0r_ref[...]
    hi = h0i_ref[...]
    cmat = cmat_ref[...]
    for t in range(n_t):
        ut = u_ref[t]
        bu = _dot(ut.astype(BF16), bbar)
        hr, hi = ar * hr - ai * hi + bu[:, 0:SSM_BS], ar * hi + ai * hr + bu[:, SSM_BS:]
        hcat = jnp.concatenate([hr, hi], axis=1).astype(BF16)
        y_s[t, k] = _gelu(_dot(hcat, cmat) + d_ref[...] * ut)
    hr_ref[...] = hr
    hi_ref[...] = hi

    @pl.when(k == SSM_BLOCKS - 1)
    def _():
        wglu = wglu_ref[...]
        for t in range(n_t):
            y = jnp.concatenate([y_s[t, kk] for kk in range(SSM_BLOCKS)], axis=1)
            o_ref[t] = _glu_tail(y, wglu, z_ref[t])


def _ssm_sample(ut, zt, h0r, h0i, sp):
    n_t, nb, _ = ut.shape
    out, hr, hi = pl.pallas_call(
        functools.partial(_ssm_sample_body, n_t=n_t),
        grid=(SSM_BLOCKS,),
        in_specs=[pl.BlockSpec((n_t, nb, LANES), lambda k: (0, 0, k)),
                  pl.BlockSpec((n_t, nb, SSM_WIDTH), lambda k: (0, 0, 0)),
                  pl.BlockSpec((nb, SSM_BS), lambda k: (0, k)),
                  pl.BlockSpec((nb, SSM_BS), lambda k: (0, k)),
                  pl.BlockSpec((1, SSM_BS), lambda k: (0, k)),
                  pl.BlockSpec((1, SSM_BS), lambda k: (0, k)),
                  pl.BlockSpec((1, SSM_BS), lambda k: (0, k)),
                  pl.BlockSpec((None, LANES, SSM_BS), lambda k: (k, 0, 0)),
                  pl.BlockSpec((None, LANES, SSM_BS), lambda k: (k, 0, 0)),
                  pl.BlockSpec((None, 2 * SSM_BS, LANES), lambda k: (k, 0, 0)),
                  pl.BlockSpec((1, LANES), lambda k: (0, k)),
                  pl.BlockSpec((SSM_WIDTH, SSM_WIDTH), lambda k: (0, 0))],
        out_specs=[pl.BlockSpec((n_t, nb, SSM_WIDTH), lambda k: (0, 0, 0)),
                   pl.BlockSpec((nb, SSM_BS), lambda k: (0, k)),
                   pl.BlockSpec((nb, SSM_BS), lambda k: (0, k))],
        out_shape=[jax.ShapeDtypeStruct((n_t, nb, SSM_WIDTH), F32),
                   jax.ShapeDtypeStruct((nb, N_STATE), F32),
                   jax.ShapeDtypeStruct((nb, N_STATE), F32)],
        scratch_shapes=[pltpu.VMEM((n_t, SSM_BLOCKS, nb, LANES), F32)],
        compiler_params=_cparams(("arbitrary",)),
    )(ut, zt, h0r, h0i, sp['lre'], sp['lim'], sp['ldt'], sp['bre'], sp['bim'], sp['cmat'], sp['d'], sp['wglu'])
    return out, hr, hi


def _page_map(pg, b, pt):
    return (pt[b, pg], 0, 0)


def _page_map_col(pg, col, b, pt):
    return (pt[b, pg], 0, col)


def _pad_rows(x, rows):
    return jnp.concatenate([x, jnp.zeros((rows - x.shape[0], x.shape[1]), x.dtype)], axis=0)


def _nsa_sample_body(pt_ref, *refs, n_pages, n_sel):
    cmp_pages = (refs[:n_pages], refs[n_pages:2 * n_pages])
    pages = refs[2 * n_pages:3 * n_pages]
    (q_ref, z_ref, g_ref, snew_ref, wnew_ref, wb_ref, wab_ref, pa_ref, pb_ref, w2_ref, ovt_ref, e_ref,
     o_ref, s_s) = refs[3 * n_pages:]
    past = n_pages * PAGE_SIZE
    nk = past + LANES
    qh = _pad_heads(q_ref[...] * SCALE)
    gs = _sigmoid(g_ref[...])
    t_col = lax.broadcasted_iota(I32, (T_PAD, 1), 0)
    qpos_col = past + t_col

    toks = []
    for kv in range(2):
        xs = []
        for l in range(CMP_STRIDE):
            xs.append(jnp.concatenate(
                [cmp_pages[kv][pg][pl.ds(l, PAGE_SIZE // CMP_STRIDE, stride=CMP_STRIDE), :]
                 for pg in range(n_pages)], axis=0).astype(BF16))
        toks.append(_compress_tokens(jnp.concatenate(xs, axis=1), wab_ref[kv], pa_ref[kv], pb_ref[kv],
                                     w2_ref[kv]))
    o_cmp, p_cmp = _cmp_branch(qh, toks[0], toks[1], qpos_col)

    psum = jnp.concatenate(
        [p_cmp[0] + p_cmp[1] + p_cmp[2] + p_cmp[3], p_cmp[4] + p_cmp[5] + p_cmp[6] + p_cmp[7],
         jnp.zeros((LANES - 2 * T_PAD, N_CMP_PAD), F32)], axis=0)
    qpos_row = past + (lax.broadcasted_iota(I32, (1, LANES), 1) % T_PAD)
    sel = _select_blocks(psum, qpos_row, ovt_ref[...], n_sel)
    mexp = _dot(sel[0:2 * T_PAD].astype(BF16), e_ref[...])
    memb = jnp.concatenate([mexp[0:T_PAD]] * NSA_GROUP + [mexp[T_PAD:2 * T_PAD]] * NSA_GROUP, axis=0) > 0.5

    q64 = jnp.concatenate(qh, axis=0)
    slope_col = jnp.concatenate([jnp.full((T_PAD, 1), NSA_SLOPES[hd], F32) for hd in range(NSA_HEADS)], axis=0)
    qpos64 = jnp.concatenate([qpos_col] * NSA_HEADS, axis=0)
    sk_c = pl.ds(0, LANES)
    sv_c = pl.ds(LANES, LANES)
    for pg in range(n_pages):
        s_s[:, pg * PAGE_SIZE:(pg + 1) * PAGE_SIZE] = _dot_nt(q64, pages[pg][:, sk_c].astype(BF16))
    knew = _pad_rows(snew_ref[:, 0:LANES], LANES).astype(BF16)
    vnew = _pad_rows(snew_ref[:, LANES:2 * LANES], LANES).astype(BF16)
    s_s[:, past:nk] = _dot_nt(q64, knew)
    kpos = lax.broadcasted_iota(I32, (1, nk), 1)
    s = s_s[...] + slope_col * kpos.astype(F32)
    p = _msoftmax(s, memb & (kpos <= qpos64)).astype(BF16)
    osel = _dot(p[:, past:nk], vnew)
    for pg in range(n_pages):
        osel = osel + _dot(p[:, pg * PAGE_SIZE:(pg + 1) * PAGE_SIZE], pages[pg][:, sv_c].astype(BF16))

    wb = wb_ref.shape[0]
    kw = jnp.concatenate([wb_ref[:, 0:LANES], _pad_rows(wnew_ref[:, 0:LANES], LANES)], axis=0).astype(BF16)
    vw = jnp.concatenate([wb_ref[:, LANES:2 * LANES], _pad_rows(wnew_ref[:, LANES:2 * LANES], LANES)],
                         axis=0).astype(BF16)
    kpos_w = (past - wb) + lax.broadcasted_iota(I32, (1, wb + LANES), 1)
    sw = _dot_nt(q64, kw) + slope_col * kpos_w.astype(F32)
    dist = qpos64 - kpos_w
    pw = _msoftmax(sw, (dist >= 0) & (dist < WINDOW) & (kpos_w >= 0))
    owin = _dot(pw.astype(BF16), vw)

    o_sel = [osel[hd * T_PAD:(hd + 1) * T_PAD] for hd in range(NSA_HEADS)]
    o_win = [owin[hd * T_PAD:(hd + 1) * T_PAD] for hd in range(NSA_HEADS)]
    o_ref[...] = _combine_heads(gs, o_cmp, o_sel, o_win, z_ref[...])


def _nsa_sample(p2, cache3, win_state, page_table, wab, pos_a, pos_b, w2bd, ovt, e2, n_t):
    n = p2.shape[0]
    nb, n_pages = page_table.shape
    past = n_pages * PAGE_SIZE
    n_sel = -(-(past + n_t) // SEL_BLOCK)
    nk = past + LANES
    wb = win_state.shape[1]
    kvb = C_KV // (2 * LANES)
    page_specs = (
        [pl.BlockSpec((None, PAGE_SIZE, LANES), functools.partial(_page_map_col, pg, 0)) for pg in range(n_pages)]
        + [pl.BlockSpec((None, PAGE_SIZE, LANES), functools.partial(_page_map_col, pg, 1)) for pg in range(n_pages)]
        + [pl.BlockSpec((None, PAGE_SIZE, 2 * LANES), functools.partial(_page_map_col, pg, 1))
           for pg in range(n_pages)])
    const = lambda shape: pl.BlockSpec(shape, lambda b, pt: (0,) * len(shape))
    in_specs = page_specs + [
        pl.BlockSpec((T_PAD, NSA_WIDTH), lambda b, pt: (b, C_Q // NSA_WIDTH)),
        pl.BlockSpec((T_PAD, NSA_WIDTH), lambda b, pt: (b, C_ZN // NSA_WIDTH)),
        pl.BlockSpec((T_PAD, LANES), lambda b, pt: (b, C_G // LANES)),
        pl.BlockSpec((T_PAD, 2 * LANES), lambda b, pt: (b, kvb + 1)),
        pl.BlockSpec((T_PAD, 2 * LANES), lambda b, pt: (b, kvb + 2)),
        pl.BlockSpec((None, wb, 2 * LANES), lambda b, pt: (b, 0, 0)),
        const(wab.shape), const(pos_a.shape), const(pos_b.shape), const(w2bd.shape),
        const(ovt.shape), const(e2.shape)]
    return pl.pallas_call(
        functools.partial(_nsa_sample_body, n_pages=n_pages, n_sel=n_sel),
        grid_spec=pltpu.PrefetchScalarGridSpec(
            num_scalar_prefetch=1, grid=(nb,), in_specs=in_specs,
            out_specs=pl.BlockSpec((T_PAD, NSA_WIDTH), lambda b, pt: (b, 0)),
            scratch_shapes=[pltpu.VMEM((NSA_HEADS * T_PAD, nk), F32)]),
        out_shape=jax.ShapeDtypeStruct((n, NSA_WIDTH), F32),
        compiler_params=_cparams(("parallel",)),
    )(page_table, *([cache3] * (3 * n_pages)), p2, p2, p2, p2, p2, win_state, wab, pos_a, pos_b, w2bd, ovt, e2)


def _diff_sample_body(pt_ref, *refs, n_pages, lam_init):
    pages = refs[:n_pages]
    lam_ref, q_ref, z_ref, knew_ref, vnew_ref, lnw_ref, o_ref, s_s = refs[n_pages:]
    past = n_pages * PAGE_SIZE
    nk = past + LANES
    lam = _diff_lambda(lam_ref, lam_init)
    low = lax.broadcasted_iota(I32, (T_PAD, LANES), 1) < DIFF_HEAD_DIM
    rows_h = 2 * T_PAD
    q16 = []
    for hh in range(DIFF_HEADS):
        qs = q_ref[:, hh * LANES:(hh + 1) * LANES] * DIFF_SCALE
        q16.append(jnp.concatenate([jnp.where(low, qs, 0.0), jnp.where(low, 0.0, qs)], axis=0).astype(BF16))
    for pg in range(n_pages):
        for hh in range(DIFF_HEADS):
            kp = pages[pg][:, pl.ds(hh * LANES, LANES)].astype(BF16)
            s_s[hh * rows_h:(hh + 1) * rows_h, pg * PAGE_SIZE:(pg + 1) * PAGE_SIZE] = _dot_nt(q16[hh], kp)
    for hh in range(DIFF_HEADS):
        kn = _pad_rows(knew_ref[:, hh * LANES:(hh + 1) * LANES], LANES).astype(BF16)
        s_s[hh * rows_h:(hh + 1) * rows_h, past:nk] = _dot_nt(q16[hh], kn)
    kpos = lax.broadcasted_iota(I32, (1, nk), 1)
    slope_col = jnp.concatenate([jnp.full((rows_h, 1), DIFF_SLOPES[hh], F32) for hh in range(DIFF_HEADS)], axis=0)
    qpos = past + (lax.broadcasted_iota(I32, (DIFF_HEADS * rows_h, 1), 0) % T_PAD)
    p = _msoftmax(s_s[...] + slope_col * kpos.astype(F32), kpos <= qpos)
    lnw = lnw_ref[...]
    slabs = []
    for hh in range(DIFF_HEADS):
        a = (p[hh * rows_h:hh * rows_h + T_PAD] - lam * p[hh * rows_h + T_PAD:(hh + 1) * rows_h]).astype(BF16)
        vn = _pad_rows(vnew_ref[:, hh * LANES:(hh + 1) * LANES], LANES).astype(BF16)
        o = _dot(a[:, past:nk], vn)
        for pg in range(n_pages):
            vp = pages[pg][:, pl.ds(DIFF_WIDTH + hh * LANES, LANES)].astype(BF16)
            o = o + _dot(a[:, pg * PAGE_SIZE:(pg + 1) * PAGE_SIZE], vp)
        slabs.append(_diff_finish(o, lnw, z_ref[:, hh * LANES:(hh + 1) * LANES], lam_init))
    o_ref[...] = jnp.concatenate(slabs, axis=1)


def _diff_sample(p2, cache3, page_table, lam4, lnw_row, lam_init):
    n = p2.shape[0]
    nb, n_pages = page_table.shape
    nk = n_pages * PAGE_SIZE + LANES
    page_specs = [pl.BlockSpec((None, PAGE_SIZE, 2 * DIFF_WIDTH), functools.partial(_page_map, pg))
                  for pg in range(n_pages)]
    in_specs = page_specs + [
        pl.BlockSpec((4, DIFF_HEAD_DIM), lambda b, pt: (0, 0)),
        pl.BlockSpec((T_PAD, DIFF_WIDTH), lambda b, pt: (b, C_DQ // DIFF_WIDTH)),
        pl.BlockSpec((T_PAD, DIFF_WIDTH), lambda b, pt: (b, C_ZD // DIFF_WIDTH)),
        pl.BlockSpec((T_PAD, DIFF_WIDTH), lambda b, pt: (b, C_DK // DIFF_WIDTH)),
        pl.BlockSpec((T_PAD, DIFF_WIDTH), lambda b, pt: (b, C_DV // DIFF_WIDTH)),
        pl.BlockSpec((1, DIFF_V_DIM), lambda b, pt: (0, 0))]
    return pl.pallas_call(
        functools.partial(_diff_sample_body, n_pages=n_pages, lam_init=lam_init),
        grid_spec=pltpu.PrefetchScalarGridSpec(
            num_scalar_prefetch=1, grid=(nb,), in_specs=in_specs,
            out_specs=pl.BlockSpec((T_PAD, DIFF_WIDTH), lambda b, pt: (b, 0)),
            scratch_shapes=[pltpu.VMEM((DIFF_HEADS * 2 * T_PAD, nk), F32)]),
        out_shape=jax.ShapeDtypeStruct((n, DIFF_WIDTH), F32),
        compiler_params=_cparams(("parallel",)),
    )(page_table, *([cache3] * n_pages), lam4, p2, p2, p2, p2, lnw_row)


def _overlap_t(n_sel):
    n = np.arange(N_CMP_PAD)[None, :]
    j = np.arange(NSEL_PAD)[:, None]
    cs = n * CMP_STRIDE
    ss = j * SEL_BLOCK
    ov = np.clip(np.minimum(cs + CMP_BLOCK, ss + SEL_BLOCK) - np.maximum(cs, ss), 0, None) / CMP_BLOCK
    ov = np.where((n < N_CMP_PAD - 1) & (j < n_sel), ov, 0.0)
    return jnp.asarray(ov, BF16)


def _block_expander(n_keys):
    j = np.arange(LANES)[:, None]
    k = np.arange(n_keys)[None, :]
    return jnp.asarray((k // SEL_BLOCK) == j, BF16)


def _layer_weights(li, norm_w, w_in, w_out, w_cmp1, pos_cmp, w_cmp2, lre, lim, ldt, bre, bim, cre, cim, sd,
                   w_glu, lam_q1, lam_k1, lam_q2, lam_k2, diff_ln_w):
    w = {}
    w['nw'] = norm_w[li][None, :]
    w['w_in'] = jnp.pad(w_in[li][:, _perm_index()], ((0, 0), (0, DP - len(_perm_index())))).astype(BF16)
    w['w_out'] = w_out[li].astype(BF16)
    eye2 = jnp.eye(NSA_KV_HEADS, dtype=F32)
    w1 = w_cmp1[li].reshape(2, 2, CMP_STRIDE, HEAD_DIM, HEAD_DIM)
    w['wab'] = jnp.einsum('kaldf,hg->klhdagf', w1, eye2).reshape(
        2, CMP_STRIDE * LANES, 2 * LANES).astype(BF16)
    pos = pos_cmp[li].reshape(2, 2, CMP_STRIDE, 1, HEAD_DIM)
    pos = jnp.broadcast_to(pos, (2, 2, CMP_STRIDE, NSA_KV_HEADS, HEAD_DIM)).reshape(2, 2, 1, CMP_STRIDE * LANES)
    pos = jnp.broadcast_to(pos, (2, 2, SUBLANES, CMP_STRIDE * LANES)).astype(BF16)
    w['pos_a'] = pos[:, 0]
    w['pos_b'] = pos[:, 1]
    w['w2bd'] = jnp.einsum('ked,hg->khegd', w_cmp2[li], eye2).reshape(2, LANES, LANES).astype(BF16)
    eye8 = jnp.eye(SSM_BLOCKS, dtype=F32)
    gl = SSM_BLOCKS

    def compact_b(bm):
        bm = bm.reshape(SSM_BLOCKS, gl, SSM_STATE, SSM_GROUP_CH)
        return jnp.einsum('kgpc,gh->kgchp', bm, eye8).reshape(SSM_BLOCKS, LANES, SSM_BS)

    def compact_c(cm):
        cm = cm.reshape(SSM_BLOCKS, gl, SSM_GROUP_CH, SSM_STATE)
        return jnp.einsum('kgcp,gh->khpgc', cm, eye8).reshape(SSM_BLOCKS, SSM_BS, LANES)

    sp = {'lre': lre[li].reshape(1, N_STATE), 'lim': lim[li].reshape(1, N_STATE),
          'ldt': jnp.repeat(ldt[li], SSM_STATE).reshape(1, N_STATE),
          'bre': compact_b(bre[li]), 'bim': compact_b(bim[li]),
          'cmat': jnp.concatenate([compact_c(cre[li]), -compact_c(cim[li])], axis=1).astype(BF16),
          'd': sd[li].reshape(1, SSM_WIDTH), 'wglu': w_glu[li].astype(BF16)}
    w['ssm'] = sp
    w['lam4'] = jnp.stack([lam_q1[li], lam_k1[li], lam_q2[li], lam_k2[li]])
    w['lnw'] = diff_ln_w[li][None, :]
    w['lam_init'] = 0.8 - 0.6 * math.exp(-0.3 * li)
    return w


def kernel(x_prompt, x_sample, cache_nsa_kv, cache_diff_kv, state_nsa_win, state_ssm_re, state_ssm_im, page_table, norm_w, w_in, w_out, w_cmp1, pos_cmp, w_cmp2, ssm_lambda_re, ssm_lambda_im, ssm_log_dt, ssm_b_re, ssm_b_im, ssm_c_re, ssm_c_im, ssm_d, w_glu, lam_q1, lam_k1, lam_q2, lam_k2, diff_ln_w, final_norm_w):
    depth = norm_w.shape[0]
    b, t, _ = x_prompt.shape
    nb, n_t, _ = x_sample.shape
    n_pool = cache_nsa_kv.shape[1]
    n_pages = page_table.shape[1]
    past = n_pages * PAGE_SIZE
    wb = state_nsa_win.shape[2]
    assert t % SSM_CHUNK == 0 and t >= WINDOW + 128 and n_t <= T_PAD

    ovt_p = _overlap_t(-(-t // SEL_BLOCK))
    ovt_s = _overlap_t(-(-(past + n_t) // SEL_BLOCK))
    e_p = _block_expander(t).reshape(LANES, t // SEL_CHUNK, SEL_CHUNK).transpose(1, 0, 2)
    e_s = _block_expander(past + LANES)
    fw = final_norm_w[None, :]

    xp = x_prompt.reshape(b * t, D_MODEL)
    xs = jnp.pad(x_sample, ((0, 0), (0, T_PAD - n_t), (0, 0))).reshape(nb * T_PAD, D_MODEL)
    outs = {k: [] for k in ('p_kv', 'p_dkv', 'p_win', 'p_re', 'p_im', 's_kv', 's_dkv', 's_win', 's_re', 's_im')}
    for li in range(depth):
        w = _layer_weights(li, norm_w, w_in, w_out, w_cmp1, pos_cmp, w_cmp2, ssm_lambda_re, ssm_lambda_im,
                           ssm_log_dt, ssm_b_re, ssm_b_im, ssm_c_re, ssm_c_im, ssm_d, w_glu,
                           lam_q1, lam_k1, lam_q2, lam_k2, diff_ln_w)
        final = li == depth - 1
        pp = _norm_project(xp, w['nw'], w['w_in'])
        p3 = pp.reshape(b, t, DP)
        cmp_tok = _compress_prompt(p3, w['wab'], w['pos_a'], w['pos_b'], w['w2bd'])
        nsa_o = _nsa_prompt(p3, cmp_tok, ovt_p, e_p)
        ssm_o, hr, hi = _ssm_prompt(p3, w['ssm'])
        diff_o = _diff_prompt(p3, w['lam4'], w['lnw'], w['lam_init'])
        xp = _merge_out(nsa_o.reshape(b * t, -1), ssm_o.reshape(b * t, -1), diff_o.reshape(b * t, -1), xp,
                        w['w_out'], fw, final)
        kv = p3[:, :, C_KV:C_KV + 6 * LANES]
        outs['p_kv'].append(kv[:, :, :4 * LANES].reshape(b, t, 4, NSA_KV_HEADS, HEAD_DIM))
        outs['p_dkv'].append(p3[:, :, C_DK:C_DK + 2 * DIFF_WIDTH].reshape(b, t, 2, DIFF_HEADS, DIFF_V_DIM))
        outs['p_win'].append(kv[:, t - min(WINDOW, t):, 4 * LANES:].reshape(b, -1, 2, NSA_KV_HEADS, HEAD_DIM))
        outs['p_re'].append(hr.reshape(b, SSM_GROUPS, SSM_STATE))
        outs['p_im'].append(hi.reshape(b, SSM_GROUPS, SSM_STATE))
        ps = _norm_project(xs, w['nw'], w['w_in'])
        win_state = state_nsa_win[li].reshape(nb, wb, 2 * LANES)
        nsa_s = _nsa_sample(ps, cache_nsa_kv[li].reshape(n_pool, PAGE_SIZE, 4 * LANES), win_state, page_table,
                            w['wab'], w['pos_a'], w['pos_b'], w['w2bd'], ovt_s, e_s, n_t)
        ps_t = ps.reshape(nb, T_PAD, DP)[:, :n_t].transpose(1, 0, 2)
        ssm_t, shr, shi = _ssm_sample(ps_t[:, :, C_U:C_U + SSM_WIDTH], ps_t[:, :, C_ZS:C_ZS + SSM_WIDTH],
                                      state_ssm_re[li].reshape(nb, N_STATE),
                                      state_ssm_im[li].reshape(nb, N_STATE), w['ssm'])
        ssm_s = jnp.pad(ssm_t.transpose(1, 0, 2), ((0, 0), (0, T_PAD - n_t), (0, 0))).reshape(nb * T_PAD, SSM_WIDTH)
        diff_s = _diff_sample(ps, cache_diff_kv[li].reshape(n_pool, PAGE_SIZE, 2 * DIFF_WIDTH), page_table,
                              w['lam4'], w['lnw'], w['lam_init'])
        xs = _merge_out(nsa_s, ssm_s, diff_s, xs, w['w_out'], fw, final)
        ps3 = ps.reshape(nb, T_PAD, DP)[:, :n_t]
        skv = ps3[:, :, C_KV:C_KV + 6 * LANES]
        outs['s_kv'].append(skv[:, :, :4 * LANES].reshape(nb, n_t, 4, NSA_KV_HEADS, HEAD_DIM))
        outs['s_dkv'].append(ps3[:, :, C_DK:C_DK + 2 * DIFF_WIDTH].reshape(nb, n_t, 2, DIFF_HEADS, DIFF_V_DIM))
        win_all = jnp.concatenate([win_state, skv[:, :, 4 * LANES:]], axis=1)
        outs['s_win'].append(win_all[:, n_t:].reshape(nb, wb, 2, NSA_KV_HEADS, HEAD_DIM))
        outs['s_re'].append(shr.reshape(nb, SSM_GROUPS, SSM_STATE))
        outs['s_im'].append(shi.reshape(nb, SSM_GROUPS, SSM_STATE))
    y_prompt = xp.reshape(b, t, D_MODEL)
    y_sample = xs.reshape(nb, T_PAD, D_MODEL)[:, :n_t]
    st = lambda k: jnp.stack(outs[k])
    return (y_prompt, y_sample, st('p_kv'), st('p_dkv'), st('p_win'), st('p_re'), st('p_im'),
            st('s_kv'), st('s_dkv'), st('s_win'), st('s_re'), st('s_im'))
```

```python
import functools
import math

import numpy as np
import jax
import jax.numpy as jnp
from jax import lax
from jax.experimental import pallas as pl
from jax.experimental.pallas import tpu as pltpu

F32 = jnp.float32
BF16 = jnp.bfloat16
I32 = jnp.int32

D_MODEL = 2048
HEAD_DIM = 64
NSA_WIDTH = 512
NSA_HEADS = 8
NSA_KV_HEADS = 2
NSA_GROUP = 4
CMP_BLOCK = 32
CMP_STRIDE = 16
SEL_BLOCK = 64
N_SELECT = 16
WINDOW = 512
SSM_WIDTH = 1024
SSM_GROUP_CH = 16
SSM_GROUPS = 64
SSM_STATE = 64
N_STATE = SSM_GROUPS * SSM_STATE
DIFF_WIDTH = 512
DIFF_HEADS = 4
DIFF_HEAD_DIM = 64
DIFF_V_DIM = 128
DIFF_SLABS = 2 * DIFF_HEADS
PAGE_SIZE = 128
SCALE = HEAD_DIM ** -0.5
DIFF_SCALE = DIFF_HEAD_DIM ** -0.5
NEG = -1e30
BIG = 1e9
EPS = 1e-6
NSA_SLOPES = tuple(float(2.0 ** (-8.0 * (k + 1) / NSA_HEADS)) for k in range(NSA_HEADS))
DIFF_SLOPES = tuple(float(2.0 ** (-8.0 * (k + 1) / DIFF_HEADS)) for k in range(DIFF_HEADS))

LANES = 128
SUBLANES = 8
VMEM_LIMIT = 56 * 1024 * 1024

ORIG_SIZES = (NSA_WIDTH, 6 * NSA_KV_HEADS * HEAD_DIM, 3 * NSA_HEADS, NSA_WIDTH, SSM_WIDTH, SSM_WIDTH,
              DIFF_HEADS * 2 * DIFF_HEAD_DIM, DIFF_WIDTH, DIFF_WIDTH, DIFF_WIDTH)
ORIG_OFFS = tuple(int(v) for v in np.concatenate([[0], np.cumsum(ORIG_SIZES)]))
NEW_ORDER = (0, 3, 6, 7, 8, 9, 4, 5, 1, 2)
C_Q, C_ZN, C_DQ, C_DK, C_DV, C_ZD, C_U, C_ZS, C_KV, C_G = 0, 512, 1024, 1536, 2048, 2560, 3072, 4096, 5120, 5888
DP = 6144
N_CMP_PAD = 128
NSEL_PAD = 40
SEL_CHUNK = 512
T_PAD = 8
NSA_DEC_SEQS = 2


def _perm_index():
    idx = np.concatenate([np.arange(ORIG_OFFS[s], ORIG_OFFS[s + 1]) for s in NEW_ORDER])
    return idx


def _cparams(sem):
    return pltpu.CompilerParams(dimension_semantics=sem, vmem_limit_bytes=VMEM_LIMIT)


def _dot(a, b):
    return jnp.dot(a, b, preferred_element_type=F32)


def _dot_nt(a, b):
    return lax.dot_general(a, b, (((1,), (1,)), ((), ())), preferred_element_type=F32)


def _gelu(x):
    return 0.5 * x * (1.0 + jnp.tanh(math.sqrt(2.0 / math.pi) * (x + 0.044715 * (x * x * x))))


def _sigmoid(x):
    return 1.0 / (1.0 + jnp.exp(-x))


def _silu(x):
    return x * _sigmoid(x)


def _msoftmax(s, mask):
    s = jnp.where(mask, s, NEG)
    m = jnp.max(s, axis=-1, keepdims=True)
    e = jnp.exp(s - m)
    den = jnp.sum(e, axis=-1, keepdims=True)
    return jnp.where(mask, e * (1.0 / den), 0.0)


def _proj_body(x_ref, nw_ref, w_ref, o_ref, xn_ref):
    @pl.when(pl.program_id(1) == 0)
    def _():
        x = x_ref[...]
        ms = jnp.mean(x * x, axis=-1, keepdims=True)
        xn_ref[...] = (x * lax.rsqrt(ms + EPS) * nw_ref[...]).astype(BF16)

    o_ref[...] = _dot(xn_ref[...], w_ref[...])


def _norm_project(x2d, nw_row, w_perm):
    n = x2d.shape[0]
    tm = min(n, 1024)
    tn = 512
    return pl.pallas_call(
        _proj_body,
        grid=(n // tm, DP // tn),
        in_specs=[pl.BlockSpec((tm, D_MODEL), lambda i, j: (i, 0)),
                  pl.BlockSpec((1, D_MODEL), lambda i, j: (0, 0)),
                  pl.BlockSpec((D_MODEL, tn), lambda i, j: (0, j))],
        out_specs=pl.BlockSpec((tm, tn), lambda i, j: (i, j)),
        out_shape=jax.ShapeDtypeStruct((n, DP), F32),
        scratch_shapes=[pltpu.VMEM((tm, D_MODEL), BF16)],
        compiler_params=_cparams(("parallel", "arbitrary")),
    )(x2d, nw_row, w_perm)


def _out_body(nsa_ref, ssm_ref, diff_ref, x_ref, w_ref, fw_ref, y_ref, *, final):
    acc = x_ref[...]
    acc = acc + _dot(nsa_ref[...].astype(BF16), w_ref[0:NSA_WIDTH, :])
    acc = acc + _dot(ssm_ref[...].astype(BF16), w_ref[NSA_WIDTH:NSA_WIDTH + SSM_WIDTH, :])
    acc = acc + _dot(diff_ref[...].astype(BF16), w_ref[NSA_WIDTH + SSM_WIDTH:, :])
    if final:
        ms = jnp.mean(acc * acc, axis=-1, keepdims=True)
        acc = acc * lax.rsqrt(ms + EPS) * fw_ref[...]
    y_ref[...] = acc


def _merge_out(nsa_o, ssm_o, diff_o, x2d, w_out_bf, fw_row, final):
    n = x2d.shape[0]
    tm = min(n, 512)
    return pl.pallas_call(
        functools.partial(_out_body, final=final),
        grid=(n // tm,),
        in_specs=[pl.BlockSpec((tm, NSA_WIDTH), lambda i: (i, 0)),
                  pl.BlockSpec((tm, SSM_WIDTH), lambda i: (i, 0)),
                  pl.BlockSpec((tm, DIFF_WIDTH), lambda i: (i, 0)),
                  pl.BlockSpec((tm, D_MODEL), lambda i: (i, 0)),
                  pl.BlockSpec((D_MODEL, D_MODEL), lambda i: (0, 0)),
                  pl.BlockSpec((1, D_MODEL), lambda i: (0, 0))],
        out_specs=pl.BlockSpec((tm, D_MODEL), lambda i: (i, 0)),
        out_shape=jax.ShapeDtypeStruct((n, D_MODEL), F32),
        compiler_params=_cparams(("parallel",)),
    )(nsa_o, ssm_o, diff_o, x2d, w_out_bf, fw_row)


def _pad_heads(qb):
    tq = qb.shape[0]
    low = lax.broadcasted_iota(I32, (tq, LANES), 1) < HEAD_DIM
    outs = []
    for hd in range(NSA_HEADS):
        h = hd // NSA_GROUP
        slab = qb[:, (hd // 2) * LANES:(hd // 2 + 1) * LANES]
        if hd % 2 != h:
            slab = pltpu.roll(slab, HEAD_DIM, 1)
        keep = low if h == 0 else jnp.logical_not(low)
        outs.append(jnp.where(keep, slab, 0.0).astype(BF16))
    return outs


def _compress_tokens(x_bf, wab, pos_a, pos_b, w2bd):
    ab = _dot(x_bf, wab)
    pc = _dot(pos_a, wab)[0:1, 0:LANES] + _dot(pos_b, wab)[0:1, LANES:2 * LANES]
    pre = ab[:, 0:LANES] + pltpu.roll(ab[:, LANES:2 * LANES], N_CMP_PAD - 1, 0) + pc
    return _dot(_gelu(pre).astype(BF16), w2bd).astype(BF16)


def _cmp_branch(qh, kc, vc, qpos_col):
    n_iota = lax.broadcasted_iota(I32, (1, N_CMP_PAD), 1)
    ends = n_iota * CMP_STRIDE + (CMP_BLOCK - 1)
    endsf = ends.astype(F32)
    mask = ends <= qpos_col
    o_list, p_list = [], []
    for hd in range(NSA_HEADS):
        s = _dot_nt(qh[hd], kc) + NSA_SLOPES[hd] * endsf
        p = _msoftmax(s, mask)
        o_list.append(_dot(p.astype(BF16), vc))
        p_list.append(p)
    return o_list, p_list


def _select_blocks(psum, qpos_row, ovt, n_sel):
    hi = psum.astype(BF16)
    r1 = psum - hi.astype(F32)
    mid = r1.astype(BF16)
    lo = (r1 - mid.astype(F32)).astype(BF16)
    sc = _dot_nt(ovt, hi) + _dot_nt(ovt, mid) + _dot_nt(ovt, lo)
    j = lax.broadcasted_iota(I32, (NSEL_PAD, LANES), 0)
    cur = qpos_row // SEL_BLOCK
    forced = (j == 0) | (j == cur) | (j == cur - 1)
    avail = j * SEL_BLOCK <= qpos_row
    sc = jnp.where(forced, BIG, jnp.where(avail, sc, -BIG))
    cnt = jnp.zeros((NSEL_PAD, LANES), I32)
    for ii in range(n_sel):
        row = sc[ii:ii + 1, :]
        beats = (row > sc) | ((row == sc) & (ii < j))
        cnt = cnt + beats.astype(I32)
    k_top = min(N_SELECT, n_sel)
    sel_t = jnp.where((cnt < k_top) & (j < n_sel), 1.0, 0.0).astype(F32)
    sel_t = jnp.concatenate([sel_t, jnp.zeros((LANES - NSEL_PAD, LANES), F32)], axis=0)
    return sel_t.T


def _combine_heads(gs, o_cmp, o_sel, o_win, z):
    tq = z.shape[0]
    low = lax.broadcasted_iota(I32, (tq, LANES), 1) < HEAD_DIM
    placed = []
    for hd in range(NSA_HEADS):
        h = hd // NSA_GROUP
        o = (gs[:, hd:hd + 1] * o_cmp[hd] + gs[:, NSA_HEADS + hd:NSA_HEADS + hd + 1] * o_sel[hd]
             + gs[:, 2 * NSA_HEADS + hd:2 * NSA_HEADS + hd + 1] * o_win[hd])
        if hd % 2 != h:
            o = pltpu.roll(o, HEAD_DIM, 1)
        placed.append(o)
    slabs = [jnp.where(low, placed[2 * k], placed[2 * k + 1]) for k in range(NSA_HEADS // 2)]
    return jnp.concatenate(slabs, axis=1) * _silu(z)


def _compress_prompt_body(rows_ref, wab_ref, pa_ref, pb_ref, w2_ref, o_ref):
    xs = [rows_ref[pl.ds(l, N_CMP_PAD, stride=CMP_STRIDE), :].astype(BF16) for l in range(CMP_STRIDE)]
    x = jnp.concatenate(xs, axis=1)
    o_ref[...] = _compress_tokens(x, wab_ref[...], pa_ref[...], pb_ref[...], w2_ref[...])


def _compress_prompt(p3, wab, pos_a, pos_b, w2bd):
    b, t, _ = p3.shape
    kvblk = C_KV // LANES
    return pl.pallas_call(
        _compress_prompt_body,
        grid=(b, 2),
        in_specs=[pl.BlockSpec((None, t, LANES), lambda i, k: (i, 0, kvblk + k)),
                  pl.BlockSpec((None, CMP_STRIDE * LANES, 2 * LANES), lambda i, k: (k, 0, 0)),
                  pl.BlockSpec((None, SUBLANES, CMP_STRIDE * LANES), lambda i, k: (k, 0, 0)),
                  pl.BlockSpec((None, SUBLANES, CMP_STRIDE * LANES), lambda i, k: (k, 0, 0)),
                  pl.BlockSpec((None, LANES, LANES), lambda i, k: (k, 0, 0))],
        out_specs=pl.BlockSpec((None, None, N_CMP_PAD, LANES), lambda i, k: (i, k, 0, 0)),
        out_shape=jax.ShapeDtypeStruct((b, 2, N_CMP_PAD, LANES), BF16),
        compiler_params=_cparams(("parallel", "parallel")),
    )(p3, wab, pos_a, pos_b, w2bd)


def _nsa_prompt_body(q_ref, z_ref, g_ref, cmp_ref, sk_ref, sv_ref, wk_ref, wv_ref, ovt_ref, e_ref, o_ref,
                     *, n_sel):
    tq = 128
    i = pl.program_id(1)
    q0 = i * tq
    qh = _pad_heads(q_ref[...] * SCALE)
    gs = _sigmoid(g_ref[...])
    qpos_col = q0 + lax.broadcasted_iota(I32, (tq, 1), 0)
    qpos_row = q0 + lax.broadcasted_iota(I32, (1, LANES), 1)
    o_cmp, p_cmp = _cmp_branch(qh, cmp_ref[0], cmp_ref[1], qpos_col)
    ovt = ovt_ref[...]

    w0 = pl.multiple_of(jnp.maximum(q0 - WINDOW, 0), LANES)
    wwid = WINDOW + tq
    kpos_w = w0 + lax.broadcasted_iota(I32, (1, wwid), 1)
    kposf_w = kpos_w.astype(F32)
    kwin = wk_ref[pl.ds(w0, wwid), :].astype(BF16)
    vwin = wv_ref[pl.ds(w0, wwid), :].astype(BF16)
    nch = (i + SEL_CHUNK // tq) // (SEL_CHUNK // tq)

    o_sel, o_win = [], []
    for h in range(NSA_KV_HEADS):
        heads = range(h * NSA_GROUP, (h + 1) * NSA_GROUP)
        psum = p_cmp[h * NSA_GROUP]
        for hd in list(heads)[1:]:
            psum = psum + p_cmp[hd]
        sel = _select_blocks(psum, qpos_row, ovt, n_sel)
        sel4 = jnp.concatenate([sel] * NSA_GROUP, axis=0).astype(BF16)
        q4 = jnp.concatenate([qh[hd] for hd in heads], axis=0)
        slope_col = jnp.concatenate([jnp.full((tq, 1), NSA_SLOPES[hd], F32) for hd in heads], axis=0)
        qpos4 = jnp.concatenate([qpos_col] * NSA_GROUP, axis=0)

        def body(c, carry, q4=q4, sel4=sel4, slope_col=slope_col, qpos4=qpos4):
            m, l, acc = carry
            k0 = pl.multiple_of(c * SEL_CHUNK, SEL_CHUNK)
            kch = sk_ref[pl.ds(k0, SEL_CHUNK), :].astype(BF16)
            vch = sv_ref[pl.ds(k0, SEL_CHUNK), :].astype(BF16)
            kpos = k0 + lax.broadcasted_iota(I32, (1, SEL_CHUNK), 1)
            s = _dot_nt(q4, kch) + slope_col * kpos.astype(F32)
            mexp = _dot(sel4, e_ref[c])
            mask = (mexp > 0.5) & (kpos <= qpos4)
            s = jnp.where(mask, s, NEG)
            m_new = jnp.maximum(m, jnp.max(s, axis=-1, keepdims=True))
            alpha = jnp.exp(m - m_new)
            p = jnp.where(mask, jnp.exp(s - m_new), 0.0)
            l = alpha * l + jnp.sum(p, axis=-1, keepdims=True)
            acc = alpha * acc + _dot(p.astype(BF16), vch)
            return m_new, l, acc

        init = (jnp.full((NSA_GROUP * tq, 1), NEG, F32), jnp.zeros((NSA_GROUP * tq, 1), F32),
                jnp.zeros((NSA_GROUP * tq, LANES), F32))
        _, l, acc = lax.fori_loop(0, nch, body, init)
        osel4 = acc * jnp.where(l > 0.0, 1.0 / l, 0.0)

        sw = _dot_nt(q4, kwin) + slope_col * kposf_w
        dist = qpos4 - kpos_w
        pw = _msoftmax(sw, (dist >= 0) & (dist < WINDOW))
        owin4 = _dot(pw.astype(BF16), vwin)
        for g in range(NSA_GROUP):
            o_sel.append(osel4[g * tq:(g + 1) * tq])
            o_win.append(owin4[g * tq:(g + 1) * tq])

    o_ref[...] = _combine_heads(gs, o_cmp, o_sel, o_win, z_ref[...]).astype(o_ref.dtype)


def _nsa_prompt(p3, cmp_tok, ovt, e3):
    b, t, _ = p3.shape
    tq = 128
    n_sel = -(-t // SEL_BLOCK)
    kvb = C_KV // LANES
    return pl.pallas_call(
        functools.partial(_nsa_prompt_body, n_sel=n_sel),
        grid=(b, t // tq),
        in_specs=[pl.BlockSpec((None, tq, NSA_WIDTH), lambda bi, i: (bi, i, C_Q // NSA_WIDTH)),
                  pl.BlockSpec((None, tq, NSA_WIDTH), lambda bi, i: (bi, i, C_ZN // NSA_WIDTH)),
                  pl.BlockSpec((None, tq, LANES), lambda bi, i: (bi, i, C_G // LANES)),
                  pl.BlockSpec((None, 2, N_CMP_PAD, LANES), lambda bi, i: (bi, 0, 0, 0)),
                  pl.BlockSpec((None, t, LANES), lambda bi, i: (bi, 0, kvb + 2)),
                  pl.BlockSpec((None, t, LANES), lambda bi, i: (bi, 0, kvb + 3)),
                  pl.BlockSpec((None, t, LANES), lambda bi, i: (bi, 0, kvb + 4)),
                  pl.BlockSpec((None, t, LANES), lambda bi, i: (bi, 0, kvb + 5)),
                  pl.BlockSpec((NSEL_PAD, LANES), lambda bi, i: (0, 0)),
                  pl.BlockSpec(e3.shape, lambda bi, i: (0, 0, 0))],
        out_specs=pl.BlockSpec((None, tq, NSA_WIDTH), lambda bi, i: (bi, i, 0)),
        out_shape=jax.ShapeDtypeStruct((b, t, NSA_WIDTH), BF16),
        compiler_params=_cparams(("parallel", "arbitrary")),
    )(p3, p3, p3, cmp_tok, p3, p3, p3, p3, ovt, e3)


def _diff_lambda(lam_ref, lam_init):
    a = lam_ref[...]
    s1 = jnp.sum(a[0:1] * a[1:2], axis=-1, keepdims=True)
    s2 = jnp.sum(a[2:3] * a[3:4], axis=-1, keepdims=True)
    return jnp.exp(s1) - jnp.exp(s2) + lam_init


def _diff_finish(o, lnw, z, lam_init):
    ms = jnp.mean(o * o, axis=-1, keepdims=True)
    return o * lax.rsqrt(ms + EPS) * lnw * (1.0 - lam_init) * _silu(z)


def _diff_prompt_body(lam_ref, q_ref, z_ref, k_ref, v_ref, lnw_ref, o_ref, *, lam_init):
    tq = 128
    i = pl.program_id(1)
    q0 = i * tq
    lam = _diff_lambda(lam_ref, lam_init)
    low = lax.broadcasted_iota(I32, (tq, LANES), 1) < DIFF_HEAD_DIM
    qpos2 = q0 + (lax.broadcasted_iota(I32, (2 * tq, 1), 0) % tq)
    nch = (i + SEL_CHUNK // tq) // (SEL_CHUNK // tq)
    lnw = lnw_ref[...]
    slabs = []
    for hh in range(DIFF_HEADS):
        cs = slice(hh * LANES, (hh + 1) * LANES)
        qs = q_ref[:, cs] * DIFF_SCALE
        q2 = jnp.concatenate([jnp.where(low, qs, 0.0), jnp.where(low, 0.0, qs)], axis=0).astype(BF16)

        def body(c, carry, q2=q2, cs=cs, hh=hh):
            m, l, acc = carry
            k0 = pl.multiple_of(c * SEL_CHUNK, SEL_CHUNK)
            kch = k_ref[pl.ds(k0, SEL_CHUNK), cs].astype(BF16)
            vch = v_ref[pl.ds(k0, SEL_CHUNK), cs].astype(BF16)
            kpos = k0 + lax.broadcasted_iota(I32, (1, SEL_CHUNK), 1)
            s = _dot_nt(q2, kch) + DIFF_SLOPES[hh] * kpos.astype(F32)
            mask = kpos <= qpos2
            s = jnp.where(mask, s, NEG)
            m_new = jnp.maximum(m, jnp.max(s, axis=-1, keepdims=True))
            alpha = jnp.exp(m - m_new)
            p = jnp.where(mask, jnp.exp(s - m_new), 0.0)
            l = alpha * l + jnp.sum(p, axis=-1, keepdims=True)
            acc = alpha * acc + _dot(p.astype(BF16), vch)
            return m_new, l, acc

        init = (jnp.full((2 * tq, 1), NEG, F32), jnp.zeros((2 * tq, 1), F32), jnp.zeros((2 * tq, LANES), F32))
        _, l, acc = lax.fori_loop(0, nch, body, init)
        on = acc * (1.0 / l)
        o = on[0:tq] - lam * on[tq:2 * tq]
        slabs.append(_diff_finish(o, lnw, z_ref[:, cs], lam_init))
    o_ref[...] = jnp.concatenate(slabs, axis=1).astype(o_ref.dtype)


def _diff_prompt(p3, lam4, lnw_row, lam_init):
    b, t, _ = p3.shape
    tq = 128
    return pl.pallas_call(
        functools.partial(_diff_prompt_body, lam_init=lam_init),
        grid=(b, t // tq),
        in_specs=[pl.BlockSpec((4, DIFF_HEAD_DIM), lambda bi, i: (0, 0)),
                  pl.BlockSpec((None, tq, DIFF_WIDTH), lambda bi, i: (bi, i, C_DQ // DIFF_WIDTH)),
                  pl.BlockSpec((None, tq, DIFF_WIDTH), lambda bi, i: (bi, i, C_ZD // DIFF_WIDTH)),
                  pl.BlockSpec((None, t, DIFF_WIDTH), lambda bi, i: (bi, 0, C_DK // DIFF_WIDTH)),
                  pl.BlockSpec((None, t, DIFF_WIDTH), lambda bi, i: (bi, 0, C_DV // DIFF_WIDTH)),
                  pl.BlockSpec((1, DIFF_V_DIM), lambda bi, i: (0, 0))],
        out_specs=pl.BlockSpec((None, tq, DIFF_WIDTH), lambda bi, i: (bi, i, 0)),
        out_shape=jax.ShapeDtypeStruct((b, t, DIFF_WIDTH), BF16),
        compiler_params=_cparams(("parallel", "arbitrary")),
    )(lam4, p3, p3, p3, p3, lnw_row)


SSM_BLOCKS = SSM_WIDTH // LANES
SSM_BS = N_STATE // SSM_BLOCKS
SSM_CHUNK = 256
SSM_SEG = SSM_CHUNK // SUBLANES
SSM_UNROLL = 8


def _discretize(lre, lim, ldt):
    dt = jnp.exp(ldt)
    mag = jnp.exp(lre * dt)
    ar = mag * jnp.cos(lim * dt)
    ai = mag * jnp.sin(lim * dt)
    den = lre * lre + lim * lim
    qr = ((ar - 1.0) * lre + ai * lim) / den
    qi = (ai * lre - (ar - 1.0) * lim) / den
    return ar, ai, qr, qi


def _bbar_block(qr, qi, bre, bim):
    return jnp.concatenate([qr * bre - qi * bim, qr * bim + qi * bre], axis=1).astype(BF16)


def _glu_tail(y, wglu, z):
    gate = _sigmoid(_dot(y.astype(BF16), wglu))
    return y * gate * _silu(z)


def _ssm_prompt_body(u_ref, z_ref, lre_ref, lim_ref, ldt_ref, bre_ref, bim_ref, cmat_ref, d_ref, wglu_ref,
                     perm_ref, o_ref, hr_ref, hi_ref,
                     bbar_s, apr_s, api_s, bu_s, y_s, carr_s, cari_s):
    first = (pl.program_id(0) == 0) & (pl.program_id(1) == 0)

    @pl.when(first)
    def _():
        ar, ai, qr, qi = _discretize(lre_ref[...], lim_ref[...], ldt_ref[...])
        for k in range(SSM_BLOCKS):
            cs = slice(k * SSM_BS, (k + 1) * SSM_BS)
            bbar_s[k] = _bbar_block(qr[:, cs], qi[:, cs], bre_ref[k], bim_ref[k])
        pr, pi = ar, ai
        apr_s[0:1, :] = pr
        api_s[0:1, :] = pi
        for k in range(1, SSM_SEG):
            pr, pi = pr * ar - pi * ai, pr * ai + pi * ar
            apr_s[k:k + 1, :] = pr
            api_s[k:k + 1, :] = pi

    @pl.when(pl.program_id(1) == 0)
    def _():
        carr_s[...] = jnp.zeros_like(carr_s)
        cari_s[...] = jnp.zeros_like(cari_s)

    re_c = slice(0, SSM_BS)
    im_c = slice(SSM_BS, 2 * SSM_BS)
    u_perm = _dot(perm_ref[0], u_ref[...].astype(BF16)).astype(BF16)
    unperm = perm_ref[1]

    for k in range(SSM_BLOCKS):
        cs = slice(k * SSM_BS, (k + 1) * SSM_BS)
        us = slice(k * LANES, (k + 1) * LANES)
        bu_s[...] = _dot(u_perm[:, us], bbar_s[k])
        ar_b = jnp.broadcast_to(apr_s[0:1, cs], (SUBLANES, SSM_BS))
        ai_b = jnp.broadcast_to(api_s[0:1, cs], (SUBLANES, SSM_BS))

        def step(t, st, ar_b=ar_b, ai_b=ai_b):
            hr, hi = st
            rows = pl.ds(pl.multiple_of(t * SUBLANES, SUBLANES), SUBLANES)
            nr = ar_b * hr - ai_b * hi + bu_s[rows, re_c]
            ni = ar_b * hi + ai_b * hr + bu_s[rows, im_c]
            bu_s[rows, re_c] = nr
            bu_s[rows, im_c] = ni
            return nr, ni

        zero = jnp.zeros((SUBLANES, SSM_BS), F32)
        er, ei = lax.fori_loop(0, SSM_SEG, step, (zero, zero), unroll=SSM_UNROLL)
        a32r = apr_s[SSM_SEG - 1:SSM_SEG, cs]
        a32i = api_s[SSM_SEG - 1:SSM_SEG, cs]
        rows_r = [carr_s[:, cs]]
        rows_i = [cari_s[:, cs]]
        for s in range(1, SUBLANES + 1):
            pr, pi = rows_r[-1], rows_i[-1]
            rows_r.append(er[s - 1:s] + a32r * pr - a32i * pi)
            rows_i.append(ei[s - 1:s] + a32r * pi + a32i * pr)
        carr_s[:, cs] = rows_r[SUBLANES]
        cari_s[:, cs] = rows_i[SUBLANES]
        hin_r = jnp.concatenate(rows_r[:SUBLANES], axis=0)
        hin_i = jnp.concatenate(rows_i[:SUBLANES], axis=0)

        def fix(t, carry, hin_r=hin_r, hin_i=hin_i, cs=cs):
            rows = pl.ds(pl.multiple_of(t * SUBLANES, SUBLANES), SUBLANES)
            pr = apr_s[pl.ds(t, 1), cs]
            pi = api_s[pl.ds(t, 1), cs]
            bu_s[rows, re_c] = bu_s[rows, re_c] + pr * hin_r - pi * hin_i
            bu_s[rows, im_c] = bu_s[rows, im_c] + pr * hin_i + pi * hin_r
            return carry

        lax.fori_loop(0, SSM_SEG, fix, 0, unroll=SSM_UNROLL)
        yp = _dot(bu_s[...].astype(BF16), cmat_ref[k])
        y_hi = yp.astype(BF16)
        r1 = yp - y_hi.astype(F32)
        y_mid = r1.astype(BF16)
        y_lo = (r1 - y_mid.astype(F32)).astype(BF16)
        yk = _dot(unperm, y_hi) + _dot(unperm, y_mid) + _dot(unperm, y_lo) + d_ref[:, us] * u_ref[:, us]
        y_s[:, us] = _gelu(yk)

    o_ref[...] = _glu_tail(y_s[...], wglu_ref[...], z_ref[...]).astype(o_ref.dtype)
    hr_ref[...] = carr_s[...]
    hi_ref[...] = cari_s[...]


def _ssm_prompt(p3, sp):
    b, t, _ = p3.shape
    full = lambda shape: pl.BlockSpec(shape, lambda bi, j: (0,) * len(shape))
    out, hr, hi = pl.pallas_call(
        _ssm_prompt_body,
        grid=(b, t // SSM_CHUNK),
        in_specs=[pl.BlockSpec((None, SSM_CHUNK, SSM_WIDTH), lambda bi, j: (bi, j, C_U // SSM_WIDTH)),
                  pl.BlockSpec((None, SSM_CHUNK, SSM_WIDTH), lambda bi, j: (bi, j, C_ZS // SSM_WIDTH)),
                  full((1, N_STATE)), full((1, N_STATE)), full((1, N_STATE)),
                  full((SSM_BLOCKS, LANES, SSM_BS)), full((SSM_BLOCKS, LANES, SSM_BS)),
                  full((SSM_BLOCKS, 2 * SSM_BS, LANES)), full((1, SSM_WIDTH)),
                  full((SSM_WIDTH, SSM_WIDTH)), full((2, SSM_CHUNK, SSM_CHUNK))],
        out_specs=[pl.BlockSpec((None, SSM_CHUNK, SSM_WIDTH), lambda bi, j: (bi, j, 0)),
                   pl.BlockSpec((None, 1, N_STATE), lambda bi, j: (bi, 0, 0)),
                   pl.BlockSpec((None, 1, N_STATE), lambda bi, j: (bi, 0, 0))],
        out_shape=[jax.ShapeDtypeStruct((b, t, SSM_WIDTH), BF16),
                   jax.ShapeDtypeStruct((b, 1, N_STATE), F32),
                   jax.ShapeDtypeStruct((b, 1, N_STATE), F32)],
        scratch_shapes=[pltpu.VMEM((SSM_BLOCKS, LANES, 2 * SSM_BS), BF16),
                        pltpu.VMEM((SSM_SEG, N_STATE), F32), pltpu.VMEM((SSM_SEG, N_STATE), F32),
                        pltpu.VMEM((SSM_CHUNK, 2 * SSM_BS), F32),
                        pltpu.VMEM((SSM_CHUNK, SSM_WIDTH), F32),
                        pltpu.VMEM((1, N_STATE), F32), pltpu.VMEM((1, N_STATE), F32)],
        compiler_params=_cparams(("arbitrary", "arbitrary")),
    )(p3, p3, sp['lre'], sp['lim'], sp['ldt'], sp['bre'], sp['bim'], sp['cmat'], sp['d'], sp['wglu'],
      _segment_permutation())
    return out, hr, hi


def _segment_permutation():
    r = np.arange(SSM_CHUNK)
    src = (r % SUBLANES) * SSM_SEG + r // SUBLANES
    pm = np.zeros((SSM_CHUNK, SSM_CHUNK), np.float32)
    pm[r, src] = 1.0
    return jnp.asarray(np.stack([pm, pm.T]), BF16)


def _ssm_sample_body(u_ref, z_ref, h0r_ref, h0i_ref, lre_ref, lim_ref, ldt_ref, bre_ref, bim_ref, cmat_ref,
                     d_ref, wglu_ref, o_ref, hr_ref, hi_ref, y_s, *, n_t):
    k = pl.program_id(0)
    ar, ai, qr, qi = _discretize(lre_ref[...], lim_ref[...], ldt_ref[...])
    bbar = _bbar_block(qr, qi, bre_ref[...], bim_ref[...])
    hr = h0r_ref[...]
    hi = h0i_ref[...]
    cmat = cmat_ref[...]
    for t in range(n_t):
        ut = u_ref[t]
        bu = _dot(ut.astype(BF16), bbar)
        hr, hi = ar * hr - ai * hi + bu[:, 0:SSM_BS], ar * hi + ai * hr + bu[:, SSM_BS:]
        hcat = jnp.concatenate([hr, hi], axis=1).astype(BF16)
        y_s[t, k] = _gelu(_dot(hcat, cmat) + d_ref[...] * ut)
    hr_ref[...] = hr
    hi_ref[...] = hi

    @pl.when(k == SSM_BLOCKS - 1)
    def _():
        wglu = wglu_ref[...]
        for t in range(n_t):
            y = jnp.concatenate([y_s[t, kk] for kk in range(SSM_BLOCKS)], axis=1)
            o_ref[t] = _glu_tail(y, wglu, z_ref[t])


def _ssm_sample(ut, zt, h0r, h0i, sp):
    n_t, nb, _ = ut.shape
    out, hr, hi = pl.pallas_call(
        functools.partial(_ssm_sample_body, n_t=n_t),
        grid=(SSM_BLOCKS,),
        in_specs=[pl.BlockSpec((n_t, nb, LANES), lambda k: (0, 0, k)),
                  pl.BlockSpec((n_t, nb, SSM_WIDTH), lambda k: (0, 0, 0)),
                  pl.BlockSpec((nb, SSM_BS), lambda k: (0, k)),
                  pl.BlockSpec((nb, SSM_BS), lambda k: (0, k)),
                  pl.BlockSpec((1, SSM_BS), lambda k: (0, k)),
                  pl.BlockSpec((1, SSM_BS), lambda k: (0, k)),
                  pl.BlockSpec((1, SSM_BS), lambda k: (0, k)),
                  pl.BlockSpec((None, LANES, SSM_BS), lambda k: (k, 0, 0)),
                  pl.BlockSpec((None, LANES, SSM_BS), lambda k: (k, 0, 0)),
                  pl.BlockSpec((None, 2 * SSM_BS, LANES), lambda k: (k, 0, 0)),
                  pl.BlockSpec((1, LANES), lambda k: (0, k)),
                  pl.BlockSpec((SSM_WIDTH, SSM_WIDTH), lambda k: (0, 0))],
        out_specs=[pl.BlockSpec((n_t, nb, SSM_WIDTH), lambda k: (0, 0, 0)),
                   pl.BlockSpec((nb, SSM_BS), lambda k: (0, k)),
                   pl.BlockSpec((nb, SSM_BS), lambda k: (0, k))],
        out_shape=[jax.ShapeDtypeStruct((n_t, nb, SSM_WIDTH), F32),
                   jax.ShapeDtypeStruct((nb, N_STATE), F32),
                   jax.ShapeDtypeStruct((nb, N_STATE), F32)],
        scratch_shapes=[pltpu.VMEM((n_t, SSM_BLOCKS, nb, LANES), F32)],
        compiler_params=_cparams(("arbitrary",)),
    )(ut, zt, h0r, h0i, sp['lre'], sp['lim'], sp['ldt'], sp['bre'], sp['bim'], sp['cmat'], sp['d'], sp['wglu'])
    return out, hr, hi


def _page_map(li, pg, b, pt):
    return (li, pt[b, pg], 0, 0)


def _page_map_seq(li, pg, sq, b, pt):
    return (li, pt[b * NSA_DEC_SEQS + sq, pg], 0, 0)


def _pad_rows(x, rows):
    return jnp.concatenate([x, jnp.zeros((rows - x.shape[0], x.shape[1]), x.dtype)], axis=0)


def _nsa_sample_body(pt_ref, *refs, n_pages, n_sel):
    (q_ref, z_ref, g_ref, snew_ref, wnew_ref, wb_ref, wab_ref, pa_ref, pb_ref, w2_ref, ovt_ref, e_ref,
     o_ref, s_s, rows_s) = refs[NSA_DEC_SEQS * n_pages:]
    for sq in range(NSA_DEC_SEQS):
        rs = slice(sq * T_PAD, (sq + 1) * T_PAD)
        o_ref[rs, :] = _nsa_sample_one(
            refs[sq * n_pages:(sq + 1) * n_pages], q_ref[rs, :], z_ref[rs, :], g_ref[rs, :], snew_ref[rs, :],
            wnew_ref[rs, :], wb_ref.at[sq], wab_ref, pa_ref, pb_ref, w2_ref, ovt_ref, e_ref,
            s_s.at[sq], rows_s.at[sq], n_pages=n_pages, n_sel=n_sel)


def _nsa_sample_one(pages, q, z, g, snew, wnew, wb_ref, wab_ref, pa_ref, pb_ref, w2_ref, ovt_ref, e_ref,
                    s_s, rows_s, *, n_pages, n_sel):
    past = n_pages * PAGE_SIZE
    nk = past + LANES
    qh = _pad_heads(q * SCALE)
    gs = _sigmoid(g)
    t_col = lax.broadcasted_iota(I32, (T_PAD, 1), 0)
    qpos_col = past + t_col

    toks = []
    for kv in range(2):
        for pg in range(n_pages):
            rows_s[kv, pg * PAGE_SIZE:(pg + 1) * PAGE_SIZE, :] = pages[pg][kv * LANES:(kv + 1) * LANES, :].T
        xs = [rows_s[kv, pl.ds(l, N_CMP_PAD, stride=CMP_STRIDE), :].astype(BF16) for l in range(CMP_STRIDE)]
        toks.append(_compress_tokens(jnp.concatenate(xs, axis=1), wab_ref[kv], pa_ref[kv], pb_ref[kv],
                                     w2_ref[kv]))
    o_cmp, p_cmp = _cmp_branch(qh, toks[0], toks[1], qpos_col)

    psum = jnp.concatenate(
        [p_cmp[0] + p_cmp[1] + p_cmp[2] + p_cmp[3], p_cmp[4] + p_cmp[5] + p_cmp[6] + p_cmp[7],
         jnp.zeros((LANES - 2 * T_PAD, N_CMP_PAD), F32)], axis=0)
    qpos_row = past + (lax.broadcasted_iota(I32, (1, LANES), 1) % T_PAD)
    sel = _select_blocks(psum, qpos_row, ovt_ref[...], n_sel)
    mexp = _dot(sel[0:2 * T_PAD].astype(BF16), e_ref[...])
    memb = jnp.concatenate([mexp[0:T_PAD]] * NSA_GROUP + [mexp[T_PAD:2 * T_PAD]] * NSA_GROUP, axis=0) > 0.5

    q64 = jnp.concatenate(qh, axis=0)
    slope_col = jnp.concatenate([jnp.full((T_PAD, 1), NSA_SLOPES[hd], F32) for hd in range(NSA_HEADS)], axis=0)
    qpos64 = jnp.concatenate([qpos_col] * NSA_HEADS, axis=0)
    sk_r = pl.ds(2 * LANES, LANES)
    sv_r = pl.ds(3 * LANES, LANES)
    for pg in range(n_pages):
        s_s[:, pg * PAGE_SIZE:(pg + 1) * PAGE_SIZE] = _dot(q64, pages[pg][sk_r, :].astype(BF16))
    knew = _pad_rows(snew[:, 0:LANES], LANES).astype(BF16)
    vnew = _pad_rows(snew[:, LANES:2 * LANES], LANES).astype(BF16)
    s_s[:, past:nk] = _dot_nt(q64, knew)
    kpos = lax.broadcasted_iota(I32, (1, nk), 1)
    s = s_s[...] + slope_col * kpos.astype(F32)
    p = _msoftmax(s, memb & (kpos <= qpos64)).astype(BF16)
    osel = _dot(p[:, past:nk], vnew)
    for pg in range(n_pages):
        osel = osel + _dot_nt(p[:, pg * PAGE_SIZE:(pg + 1) * PAGE_SIZE], pages[pg][sv_r, :].astype(BF16))

    wb = wb_ref.shape[1]
    kwn = _pad_rows(wnew[:, 0:LANES], LANES).astype(BF16)
    vwn = _pad_rows(wnew[:, LANES:2 * LANES], LANES).astype(BF16)
    kpos_w = (past - wb) + lax.broadcasted_iota(I32, (1, wb + LANES), 1)
    sw = jnp.concatenate([_dot(q64, wb_ref[0:LANES, :].astype(BF16)), _dot_nt(q64, kwn)], axis=1)
    sw = sw + slope_col * kpos_w.astype(F32)
    dist = qpos64 - kpos_w
    pw = _msoftmax(sw, (dist >= 0) & (dist < WINDOW) & (kpos_w >= 0)).astype(BF16)
    owin = _dot_nt(pw[:, 0:wb], wb_ref[LANES:2 * LANES, :].astype(BF16)) + _dot(pw[:, wb:], vwn)

    o_sel = [osel[hd * T_PAD:(hd + 1) * T_PAD] for hd in range(NSA_HEADS)]
    o_win = [owin[hd * T_PAD:(hd + 1) * T_PAD] for hd in range(NSA_HEADS)]
    return _combine_heads(gs, o_cmp, o_sel, o_win, z)


def _nsa_sample(p2, cache_t, li, win_t, page_table, wab, pos_a, pos_b, w2bd, ovt, e2, n_t):
    n = p2.shape[0]
    nb, n_pages = page_table.shape
    past = n_pages * PAGE_SIZE
    n_sel = -(-(past + n_t) // SEL_BLOCK)
    nk = past + LANES
    wb = win_t.shape[3]
    kvb = C_KV // (2 * LANES)
    nsq = NSA_DEC_SEQS
    assert nb % nsq == 0
    page_specs = [pl.BlockSpec((None, None, 4 * LANES, PAGE_SIZE), functools.partial(_page_map_seq, li, pg, sq))
                  for sq in range(nsq) for pg in range(n_pages)]
    const = lambda shape: pl.BlockSpec(shape, lambda b, pt: (0,) * len(shape))
    rows = nsq * T_PAD
    in_specs = page_specs + [
        pl.BlockSpec((rows, NSA_WIDTH), lambda b, pt: (b, C_Q // NSA_WIDTH)),
        pl.BlockSpec((rows, NSA_WIDTH), lambda b, pt: (b, C_ZN // NSA_WIDTH)),
        pl.BlockSpec((rows, LANES), lambda b, pt: (b, C_G // LANES)),
        pl.BlockSpec((rows, 2 * LANES), lambda b, pt: (b, kvb + 1)),
        pl.BlockSpec((rows, 2 * LANES), lambda b, pt: (b, kvb + 2)),
        pl.BlockSpec((None, nsq, 2 * LANES, wb), lambda b, pt: (li, b, 0, 0)),
        const(wab.shape), const(pos_a.shape), const(pos_b.shape), const(w2bd.shape),
        const(ovt.shape), const(e2.shape)]
    return pl.pallas_call(
        functools.partial(_nsa_sample_body, n_pages=n_pages, n_sel=n_sel),
        grid_spec=pltpu.PrefetchScalarGridSpec(
            num_scalar_prefetch=1, grid=(nb // nsq,), in_specs=in_specs,
            out_specs=pl.BlockSpec((rows, NSA_WIDTH), lambda b, pt: (b, 0)),
            scratch_shapes=[pltpu.VMEM((nsq, NSA_HEADS * T_PAD, nk), F32),
                            pltpu.VMEM((nsq, 2, past, LANES), F32)]),
        out_shape=jax.ShapeDtypeStruct((n, NSA_WIDTH), F32),
        compiler_params=_cparams(("parallel",)),
    )(page_table, *([cache_t] * (nsq * n_pages)), p2, p2, p2, p2, p2, win_t, wab, pos_a, pos_b, w2bd, ovt, e2)


def _diff_sample_body(pt_ref, *refs, n_pages, lam_init):
    pages = refs[:n_pages]
    lam_ref, q_ref, z_ref, knew_ref, vnew_ref, lnw_ref, o_ref, s_s = refs[n_pages:]
    past = n_pages * PAGE_SIZE
    nk = past + LANES
    lam = _diff_lambda(lam_ref, lam_init)
    low = lax.broadcasted_iota(I32, (T_PAD, LANES), 1) < DIFF_HEAD_DIM
    rows_h = 2 * T_PAD
    q16 = []
    for hh in range(DIFF_HEADS):
        qs = q_ref[:, hh * LANES:(hh + 1) * LANES] * DIFF_SCALE
        q16.append(jnp.concatenate([jnp.where(low, qs, 0.0), jnp.where(low, 0.0, qs)], axis=0).astype(BF16))
    for pg in range(n_pages):
        for hh in range(DIFF_HEADS):
            kp = pages[pg][pl.ds(hh, PAGE_SIZE, stride=DIFF_SLABS), :].astype(BF16)
            s_s[hh * rows_h:(hh + 1) * rows_h, pg * PAGE_SIZE:(pg + 1) * PAGE_SIZE] = _dot_nt(q16[hh], kp)
    for hh in range(DIFF_HEADS):
        kn = _pad_rows(knew_ref[:, hh * LANES:(hh + 1) * LANES], LANES).astype(BF16)
        s_s[hh * rows_h:(hh + 1) * rows_h, past:nk] = _dot_nt(q16[hh], kn)
    kpos = lax.broadcasted_iota(I32, (1, nk), 1)
    slope_col = jnp.concatenate([jnp.full((rows_h, 1), DIFF_SLOPES[hh], F32) for hh in range(DIFF_HEADS)], axis=0)
    qpos = past + (lax.broadcasted_iota(I32, (DIFF_HEADS * rows_h, 1), 0) % T_PAD)
    p = _msoftmax(s_s[...] + slope_col * kpos.astype(F32), kpos <= qpos)
    lnw = lnw_ref[...]
    slabs = []
    for hh in range(DIFF_HEADS):
        a = (p[hh * rows_h:hh * rows_h + T_PAD] - lam * p[hh * rows_h + T_PAD:(hh + 1) * rows_h]).astype(BF16)
        vn = _pad_rows(vnew_ref[:, hh * LANES:(hh + 1) * LANES], LANES).astype(BF16)
        o = _dot(a[:, past:nk], vn)
        for pg in range(n_pages):
            vp = pages[pg][pl.ds(DIFF_HEADS + hh, PAGE_SIZE, stride=DIFF_SLABS), :].astype(BF16)
            o = o + _dot(a[:, pg * PAGE_SIZE:(pg + 1) * PAGE_SIZE], vp)
        slabs.append(_diff_finish(o, lnw, z_ref[:, hh * LANES:(hh + 1) * LANES], lam_init))
    o_ref[...] = jnp.concatenate(slabs, axis=1)


def _diff_sample(p2, cache_r, li, page_table, lam4, lnw_row, lam_init):
    n = p2.shape[0]
    nb, n_pages = page_table.shape
    nk = n_pages * PAGE_SIZE + LANES
    page_specs = [pl.BlockSpec((None, None, PAGE_SIZE * DIFF_SLABS, LANES), functools.partial(_page_map, li, pg))
                  for pg in range(n_pages)]
    in_specs = page_specs + [
        pl.BlockSpec((4, DIFF_HEAD_DIM), lambda b, pt: (0, 0)),
        pl.BlockSpec((T_PAD, DIFF_WIDTH), lambda b, pt: (b, C_DQ // DIFF_WIDTH)),
        pl.BlockSpec((T_PAD, DIFF_WIDTH), lambda b, pt: (b, C_ZD // DIFF_WIDTH)),
        pl.BlockSpec((T_PAD, DIFF_WIDTH), lambda b, pt: (b, C_DK // DIFF_WIDTH)),
        pl.BlockSpec((T_PAD, DIFF_WIDTH), lambda b, pt: (b, C_DV // DIFF_WIDTH)),
        pl.BlockSpec((1, DIFF_V_DIM), lambda b, pt: (0, 0))]
    return pl.pallas_call(
        functools.partial(_diff_sample_body, n_pages=n_pages, lam_init=lam_init),
        grid_spec=pltpu.PrefetchScalarGridSpec(
            num_scalar_prefetch=1, grid=(nb,), in_specs=in_specs,
            out_specs=pl.BlockSpec((T_PAD, DIFF_WIDTH), lambda b, pt: (b, 0)),
            scratch_shapes=[pltpu.VMEM((DIFF_HEADS * 2 * T_PAD, nk), F32)]),
        out_shape=jax.ShapeDtypeStruct((n, DIFF_WIDTH), F32),
        compiler_params=_cparams(("parallel",)),
    )(page_table, *([cache_r] * n_pages), lam4, p2, p2, p2, p2, lnw_row)


def _overlap_t(n_sel):
    n = np.arange(N_CMP_PAD)[None, :]
    j = np.arange(NSEL_PAD)[:, None]
    cs = n * CMP_STRIDE
    ss = j * SEL_BLOCK
    ov = np.clip(np.minimum(cs + CMP_BLOCK, ss + SEL_BLOCK) - np.maximum(cs, ss), 0, None) / CMP_BLOCK
    ov = np.where((n < N_CMP_PAD - 1) & (j < n_sel), ov, 0.0)
    return jnp.asarray(ov, BF16)


def _block_expander(n_keys):
    j = np.arange(LANES)[:, None]
    k = np.arange(n_keys)[None, :]
    return jnp.asarray((k // SEL_BLOCK) == j, BF16)


def _layer_weights(li, norm_w, w_in, w_out, w_cmp1, pos_cmp, w_cmp2, lre, lim, ldt, bre, bim, cre, cim, sd,
                   w_glu, lam_q1, lam_k1, lam_q2, lam_k2, diff_ln_w):
    w = {}
    w['nw'] = norm_w[li][None, :]
    w['w_in'] = jnp.pad(w_in[li][:, _perm_index()], ((0, 0), (0, DP - len(_perm_index())))).astype(BF16)
    w['w_out'] = w_out[li].astype(BF16)
    eye2 = jnp.eye(NSA_KV_HEADS, dtype=F32)
    w1 = w_cmp1[li].reshape(2, 2, CMP_STRIDE, HEAD_DIM, HEAD_DIM)
    w['wab'] = jnp.einsum('kaldf,hg->klhdagf', w1, eye2).reshape(
        2, CMP_STRIDE * LANES, 2 * LANES).astype(BF16)
    pos = pos_cmp[li].reshape(2, 2, CMP_STRIDE, 1, HEAD_DIM)
    pos = jnp.broadcast_to(pos, (2, 2, CMP_STRIDE, NSA_KV_HEADS, HEAD_DIM)).reshape(2, 2, 1, CMP_STRIDE * LANES)
    pos = jnp.broadcast_to(pos, (2, 2, SUBLANES, CMP_STRIDE * LANES)).astype(BF16)
    w['pos_a'] = pos[:, 0]
    w['pos_b'] = pos[:, 1]
    w['w2bd'] = jnp.einsum('ked,hg->khegd', w_cmp2[li], eye2).reshape(2, LANES, LANES).astype(BF16)
    eye8 = jnp.eye(SSM_BLOCKS, dtype=F32)
    gl = SSM_BLOCKS

    def compact_b(bm):
        bm = bm.reshape(SSM_BLOCKS, gl, SSM_STATE, SSM_GROUP_CH)
        return jnp.einsum('kgpc,gh->kgchp', bm, eye8).reshape(SSM_BLOCKS, LANES, SSM_BS)

    def compact_c(cm):
        cm = cm.reshape(SSM_BLOCKS, gl, SSM_GROUP_CH, SSM_STATE)
        return jnp.einsum('kgcp,gh->khpgc', cm, eye8).reshape(SSM_BLOCKS, SSM_BS, LANES)

    sp = {'lre': lre[li].reshape(1, N_STATE), 'lim': lim[li].reshape(1, N_STATE),
          'ldt': jnp.repeat(ldt[li], SSM_STATE).reshape(1, N_STATE),
          'bre': compact_b(bre[li]), 'bim': compact_b(bim[li]),
          'cmat': jnp.concatenate([compact_c(cre[li]), -compact_c(cim[li])], axis=1).astype(BF16),
          'd': sd[li].reshape(1, SSM_WIDTH), 'wglu': w_glu[li].astype(BF16)}
    w['ssm'] = sp
    w['lam4'] = jnp.stack([lam_q1[li], lam_k1[li], lam_q2[li], lam_k2[li]])
    w['lnw'] = diff_ln_w[li][None, :]
    w['lam_init'] = 0.8 - 0.6 * math.exp(-0.3 * li)
    return w


def kernel(x_prompt, x_sample, cache_nsa_kv, cache_diff_kv, state_nsa_win, state_ssm_re, state_ssm_im, page_table, norm_w, w_in, w_out, w_cmp1, pos_cmp, w_cmp2, ssm_lambda_re, ssm_lambda_im, ssm_log_dt, ssm_b_re, ssm_b_im, ssm_c_re, ssm_c_im, ssm_d, w_glu, lam_q1, lam_k1, lam_q2, lam_k2, diff_ln_w, final_norm_w):
    depth = norm_w.shape[0]
    b, t, _ = x_prompt.shape
    nb, n_t, _ = x_sample.shape
    n_pool = cache_nsa_kv.shape[1]
    n_pages = page_table.shape[1]
    past = n_pages * PAGE_SIZE
    wb = state_nsa_win.shape[2]
    assert t % SSM_CHUNK == 0 and t >= WINDOW + 128 and n_t <= T_PAD

    ovt_p = _overlap_t(-(-t // SEL_BLOCK))
    ovt_s = _overlap_t(-(-(past + n_t) // SEL_BLOCK))
    e_p = _block_expander(t).reshape(LANES, t // SEL_CHUNK, SEL_CHUNK).transpose(1, 0, 2)
    e_s = _block_expander(past + LANES)
    fw = final_norm_w[None, :]
    nsa_cache_t = cache_nsa_kv.transpose(0, 1, 3, 4, 5, 2).reshape(depth, n_pool, 4 * LANES, PAGE_SIZE)
    win_t = state_nsa_win.transpose(0, 1, 3, 4, 5, 2).reshape(depth, nb, 2 * LANES, wb)
    diff_cache_r = cache_diff_kv.reshape(depth, n_pool, PAGE_SIZE * DIFF_SLABS, LANES)

    xp = x_prompt.reshape(b * t, D_MODEL)
    xs = jnp.pad(x_sample, ((0, 0), (0, T_PAD - n_t), (0, 0))).reshape(nb * T_PAD, D_MODEL)
    outs = {k: [] for k in ('p_kv', 'p_dkv', 'p_win', 'p_re', 'p_im', 's_kv', 's_dkv', 's_win', 's_re', 's_im')}
    for li in range(depth):
        w = _layer_weights(li, norm_w, w_in, w_out, w_cmp1, pos_cmp, w_cmp2, ssm_lambda_re, ssm_lambda_im,
                           ssm_log_dt, ssm_b_re, ssm_b_im, ssm_c_re, ssm_c_im, ssm_d, w_glu,
                           lam_q1, lam_k1, lam_q2, lam_k2, diff_ln_w)
        final = li == depth - 1
        pp = _norm_project(xp, w['nw'], w['w_in'])
        p3 = pp.reshape(b, t, DP)
        cmp_tok = _compress_prompt(p3, w['wab'], w['pos_a'], w['pos_b'], w['w2bd'])
        nsa_o = _nsa_prompt(p3, cmp_tok, ovt_p, e_p)
        ssm_o, hr, hi = _ssm_prompt(p3, w['ssm'])
        diff_o = _diff_prompt(p3, w['lam4'], w['lnw'], w['lam_init'])
        xp = _merge_out(nsa_o.reshape(b * t, -1), ssm_o.reshape(b * t, -1), diff_o.reshape(b * t, -1), xp,
                        w['w_out'], fw, final)
        kv = p3[:, :, C_KV:C_KV + 6 * LANES]
        outs['p_kv'].append(kv[:, :, :4 * LANES].reshape(b, t, 4, NSA_KV_HEADS, HEAD_DIM))
        outs['p_dkv'].append(p3[:, :, C_DK:C_DK + 2 * DIFF_WIDTH].reshape(b, t, 2, DIFF_HEADS, DIFF_V_DIM))
        outs['p_win'].append(kv[:, t - min(WINDOW, t):, 4 * LANES:].reshape(b, -1, 2, NSA_KV_HEADS, HEAD_DIM))
        outs['p_re'].append(hr.reshape(b, SSM_GROUPS, SSM_STATE))
        outs['p_im'].append(hi.reshape(b, SSM_GROUPS, SSM_STATE))
        ps = _norm_project(xs, w['nw'], w['w_in'])
        nsa_s = _nsa_sample(ps, nsa_cache_t, li, win_t, page_table,
                            w['wab'], w['pos_a'], w['pos_b'], w['w2bd'], ovt_s, e_s, n_t)
        ps_t = ps.reshape(nb, T_PAD, DP)[:, :n_t].transpose(1, 0, 2)
        ssm_t, shr, shi = _ssm_sample(ps_t[:, :, C_U:C_U + SSM_WIDTH], ps_t[:, :, C_ZS:C_ZS + SSM_WIDTH],
                                      state_ssm_re[li].reshape(nb, N_STATE),
                                      state_ssm_im[li].reshape(nb, N_STATE), w['ssm'])
        ssm_s = jnp.pad(ssm_t.transpose(1, 0, 2), ((0, 0), (0, T_PAD - n_t), (0, 0))).reshape(nb * T_PAD, SSM_WIDTH)
        diff_s = _diff_sample(ps, diff_cache_r, li, page_table, w['lam4'], w['lnw'], w['lam_init'])
        xs = _merge_out(nsa_s, ssm_s, diff_s, xs, w['w_out'], fw, final)
        ps3 = ps.reshape(nb, T_PAD, DP)[:, :n_t]
        skv = ps3[:, :, C_KV:C_KV + 6 * LANES]
        outs['s_kv'].append(skv[:, :, :4 * LANES].reshape(nb, n_t, 4, NSA_KV_HEADS, HEAD_DIM))
        outs['s_dkv'].append(ps3[:, :, C_DK:C_DK + 2 * DIFF_WIDTH].reshape(nb, n_t, 2, DIFF_HEADS, DIFF_V_DIM))
        win_new_t = jnp.concatenate([win_t[li][:, :, n_t:], skv[:, :, 4 * LANES:].transpose(0, 2, 1)], axis=2)
        outs['s_win'].append(win_new_t.reshape(nb, 2, NSA_KV_HEADS, HEAD_DIM, wb).transpose(0, 4, 1, 2, 3))
        outs['s_re'].append(shr.reshape(nb, SSM_GROUPS, SSM_STATE))
        outs['s_im'].append(shi.reshape(nb, SSM_GROUPS, SSM_STATE))
    y_prompt = xp.reshape(b, t, D_MODEL)
    y_sample = xs.reshape(nb, T_PAD, D_MODEL)[:, :n_t]
    st = lambda k: jnp.stack(outs[k])
    return (y_prompt, y_sample, st('p_kv'), st('p_dkv'), st('p_win'), st('p_re'), st('p_im'),
            st('s_kv'), st('s_dkv'), st('s_win'), st('s_re'), st('s_im'))
```

```python
import functools
import math

import numpy as np
import jax
import jax.numpy as jnp
from jax import lax
from jax.experimental import pallas as pl
from jax.experimental.pallas import tpu as pltpu

F32 = jnp.float32
BF16 = jnp.bfloat16
I32 = jnp.int32

D_MODEL = 2048
HEAD_DIM = 64
NSA_WIDTH = 512
NSA_HEADS = 8
NSA_KV_HEADS = 2
NSA_GROUP = 4
CMP_BLOCK = 32
CMP_STRIDE = 16
SEL_BLOCK = 64
N_SELECT = 16
WINDOW = 512
SSM_WIDTH = 1024
SSM_GROUP_CH = 16
SSM_GROUPS = 64
SSM_STATE = 64
N_STATE = SSM_GROUPS * SSM_STATE
DIFF_WIDTH = 512
DIFF_HEADS = 4
DIFF_HEAD_DIM = 64
DIFF_V_DIM = 128
DIFF_SLABS = 2 * DIFF_HEADS
PAGE_SIZE = 128
SCALE = HEAD_DIM ** -0.5
DIFF_SCALE = DIFF_HEAD_DIM ** -0.5
NEG = -1e30
BIG = 1e9
EPS = 1e-6
NSA_SLOPES = tuple(float(2.0 ** (-8.0 * (k + 1) / NSA_HEADS)) for k in range(NSA_HEADS))
DIFF_SLOPES = tuple(float(2.0 ** (-8.0 * (k + 1) / DIFF_HEADS)) for k in range(DIFF_HEADS))

LANES = 128
SUBLANES = 8
VMEM_LIMIT = 56 * 1024 * 1024

ORIG_SIZES = (NSA_WIDTH, 6 * NSA_KV_HEADS * HEAD_DIM, 3 * NSA_HEADS, NSA_WIDTH, SSM_WIDTH, SSM_WIDTH,
              DIFF_HEADS * 2 * DIFF_HEAD_DIM, DIFF_WIDTH, DIFF_WIDTH, DIFF_WIDTH)
ORIG_OFFS = tuple(int(v) for v in np.concatenate([[0], np.cumsum(ORIG_SIZES)]))
NEW_ORDER = (0, 3, 6, 7, 8, 9, 4, 5, 1, 2)
C_Q, C_ZN, C_DQ, C_DK, C_DV, C_ZD, C_U, C_ZS, C_KV, C_G = 0, 512, 1024, 1536, 2048, 2560, 3072, 4096, 5120, 5888
DP = 6144
N_CMP_PAD = 128
NSEL_PAD = 40
SEL_CHUNK = 512
T_PAD = 8
NSA_DEC_SEQS = 2


def _cparams(sem):
    return pltpu.CompilerParams(dimension_semantics=sem, vmem_limit_bytes=VMEM_LIMIT)


def _dot(a, b):
    return jnp.dot(a, b, preferred_element_type=F32)


def _dot_nt(a, b):
    return lax.dot_general(a, b, (((1,), (1,)), ((), ())), preferred_element_type=F32)


def _gelu(x):
    return 0.5 * x * (1.0 + jnp.tanh(math.sqrt(2.0 / math.pi) * (x + 0.044715 * (x * x * x))))


def _sigmoid(x):
    return 1.0 / (1.0 + jnp.exp(-x))


def _silu(x):
    return x * _sigmoid(x)


def _msoftmax(s, mask):
    s = jnp.where(mask, s, NEG)
    m = jnp.max(s, axis=-1, keepdims=True)
    e = jnp.exp(s - m)
    den = jnp.sum(e, axis=-1, keepdims=True)
    return jnp.where(mask, e * (1.0 / den), 0.0)


def _proj_body(x_ref, nw_ref, w_ref, o_ref, xn_ref):
    @pl.when(pl.program_id(1) == 0)
    def _():
        x = x_ref[...]
        ms = jnp.mean(x * x, axis=-1, keepdims=True)
        xn_ref[...] = (x * lax.rsqrt(ms + EPS) * nw_ref[...]).astype(BF16)

    o_ref[...] = _dot_nt(xn_ref[...], w_ref[...])


def _norm_project(x2d, nw_row, w_t, li):
    n = x2d.shape[0]
    tm = min(n, 1024)
    tn = 512
    return pl.pallas_call(
        _proj_body,
        grid=(n // tm, DP // tn),
        in_specs=[pl.BlockSpec((tm, D_MODEL), lambda i, j: (i, 0)),
                  pl.BlockSpec((1, D_MODEL), lambda i, j: (0, 0)),
                  pl.BlockSpec((None, tn, D_MODEL), lambda i, j: (li, j, 0))],
        out_specs=pl.BlockSpec((tm, tn), lambda i, j: (i, j)),
        out_shape=jax.ShapeDtypeStruct((n, DP), F32),
        scratch_shapes=[pltpu.VMEM((tm, D_MODEL), BF16)],
        compiler_params=_cparams(("parallel", "arbitrary")),
    )(x2d, nw_row, w_t)


def _out_body(nsa_ref, ssm_ref, diff_ref, x_ref, w_ref, fw_ref, y_ref, *, final):
    acc = x_ref[...]
    acc = acc + _dot(nsa_ref[...].astype(BF16), w_ref[0:NSA_WIDTH, :])
    acc = acc + _dot(ssm_ref[...].astype(BF16), w_ref[NSA_WIDTH:NSA_WIDTH + SSM_WIDTH, :])
    acc = acc + _dot(diff_ref[...].astype(BF16), w_ref[NSA_WIDTH + SSM_WIDTH:, :])
    if final:
        ms = jnp.mean(acc * acc, axis=-1, keepdims=True)
        acc = acc * lax.rsqrt(ms + EPS) * fw_ref[...]
    y_ref[...] = acc


def _merge_out(nsa_o, ssm_o, diff_o, x2d, w_out_bf, fw_row, final):
    n = x2d.shape[0]
    tm = min(n, 512)
    return pl.pallas_call(
        functools.partial(_out_body, final=final),
        grid=(n // tm,),
        in_specs=[pl.BlockSpec((tm, NSA_WIDTH), lambda i: (i, 0)),
                  pl.BlockSpec((tm, SSM_WIDTH), lambda i: (i, 0)),
                  pl.BlockSpec((tm, DIFF_WIDTH), lambda i: (i, 0)),
                  pl.BlockSpec((tm, D_MODEL), lambda i: (i, 0)),
                  pl.BlockSpec((D_MODEL, D_MODEL), lambda i: (0, 0)),
                  pl.BlockSpec((1, D_MODEL), lambda i: (0, 0))],
        out_specs=pl.BlockSpec((tm, D_MODEL), lambda i: (i, 0)),
        out_shape=jax.ShapeDtypeStruct((n, D_MODEL), F32),
        compiler_params=_cparams(("parallel",)),
    )(nsa_o, ssm_o, diff_o, x2d, w_out_bf, fw_row)


def _pad_heads(qb):
    tq = qb.shape[0]
    low = lax.broadcasted_iota(I32, (tq, LANES), 1) < HEAD_DIM
    outs = []
    for hd in range(NSA_HEADS):
        h = hd // NSA_GROUP
        slab = qb[:, (hd // 2) * LANES:(hd // 2 + 1) * LANES]
        if hd % 2 != h:
            slab = pltpu.roll(slab, HEAD_DIM, 1)
        keep = low if h == 0 else jnp.logical_not(low)
        outs.append(jnp.where(keep, slab, 0.0).astype(BF16))
    return outs


def _compress_tokens(x_bf, wab, pos_a, pos_b, w2bd):
    ab = _dot(x_bf, wab)
    pc = _dot(pos_a, wab)[0:1, 0:LANES] + _dot(pos_b, wab)[0:1, LANES:2 * LANES]
    pre = ab[:, 0:LANES] + pltpu.roll(ab[:, LANES:2 * LANES], N_CMP_PAD - 1, 0) + pc
    return _dot(_gelu(pre).astype(BF16), w2bd).astype(BF16)


def _cmp_branch(qh, kc, vc, qpos_col):
    tq = qpos_col.shape[0]
    n_iota = lax.broadcasted_iota(I32, (1, N_CMP_PAD), 1)
    ends = n_iota * CMP_STRIDE + (CMP_BLOCK - 1)
    q_all = jnp.concatenate(qh, axis=0)
    slope_col = jnp.concatenate([jnp.full((tq, 1), NSA_SLOPES[hd], F32) for hd in range(NSA_HEADS)], axis=0)
    qpos_all = jnp.concatenate([qpos_col] * NSA_HEADS, axis=0)
    s = _dot_nt(q_all, kc) + slope_col * ends.astype(F32)
    p = _msoftmax(s, ends <= qpos_all)
    o = _dot(p.astype(BF16), vc)
    rows = [slice(hd * tq, (hd + 1) * tq) for hd in range(NSA_HEADS)]
    return [o[r] for r in rows], [p[r] for r in rows]


def _select_blocks(psum, qpos_row, ovt, n_sel):
    hi = psum.astype(BF16)
    r1 = psum - hi.astype(F32)
    mid = r1.astype(BF16)
    lo = (r1 - mid.astype(F32)).astype(BF16)
    sc = _dot_nt(ovt, hi) + _dot_nt(ovt, mid) + _dot_nt(ovt, lo)
    j = lax.broadcasted_iota(I32, (NSEL_PAD, LANES), 0)
    cur = qpos_row // SEL_BLOCK
    forced = (j == 0) | (j == cur) | (j == cur - 1)
    avail = j * SEL_BLOCK <= qpos_row
    sc = jnp.where(forced, BIG, jnp.where(avail, sc, -BIG))
    cnt = jnp.zeros((NSEL_PAD, LANES), I32)
    for ii in range(n_sel):
        row = sc[ii:ii + 1, :]
        beats = (row > sc) | ((row == sc) & (ii < j))
        cnt = cnt + beats.astype(I32)
    k_top = min(N_SELECT, n_sel)
    sel_t = jnp.where((cnt < k_top) & (j < n_sel), 1.0, 0.0).astype(F32)
    sel_t = jnp.concatenate([sel_t, jnp.zeros((LANES - NSEL_PAD, LANES), F32)], axis=0)
    return sel_t.T


def _combine_heads(gs, o_cmp, o_sel, o_win, z):
    tq = z.shape[0]
    low = lax.broadcasted_iota(I32, (tq, LANES), 1) < HEAD_DIM
    placed = []
    for hd in range(NSA_HEADS):
        h = hd // NSA_GROUP
        o = (gs[:, hd:hd + 1] * o_cmp[hd] + gs[:, NSA_HEADS + hd:NSA_HEADS + hd + 1] * o_sel[hd]
             + gs[:, 2 * NSA_HEADS + hd:2 * NSA_HEADS + hd + 1] * o_win[hd])
        if hd % 2 != h:
            o = pltpu.roll(o, HEAD_DIM, 1)
        placed.append(o)
    slabs = [jnp.where(low, placed[2 * k], placed[2 * k + 1]) for k in range(NSA_HEADS // 2)]
    return jnp.concatenate(slabs, axis=1) * _silu(z)


def _compress_prompt_body(rows_ref, wab_ref, pa_ref, pb_ref, w2_ref, o_ref):
    xs = [rows_ref[pl.ds(l, N_CMP_PAD, stride=CMP_STRIDE), :].astype(BF16) for l in range(CMP_STRIDE)]
    x = jnp.concatenate(xs, axis=1)
    o_ref[...] = _compress_tokens(x, wab_ref[...], pa_ref[...], pb_ref[...], w2_ref[...])


def _compress_prompt(p3, wab, pos_a, pos_b, w2bd):
    b, t, _ = p3.shape
    kvblk = C_KV // LANES
    return pl.pallas_call(
        _compress_prompt_body,
        grid=(b, 2),
        in_specs=[pl.BlockSpec((None, t, LANES), lambda i, k: (i, 0, kvblk + k)),
                  pl.BlockSpec((None, CMP_STRIDE * LANES, 2 * LANES), lambda i, k: (k, 0, 0)),
                  pl.BlockSpec((None, SUBLANES, CMP_STRIDE * LANES), lambda i, k: (k, 0, 0)),
                  pl.BlockSpec((None, SUBLANES, CMP_STRIDE * LANES), lambda i, k: (k, 0, 0)),
                  pl.BlockSpec((None, LANES, LANES), lambda i, k: (k, 0, 0))],
        out_specs=pl.BlockSpec((None, None, N_CMP_PAD, LANES), lambda i, k: (i, k, 0, 0)),
        out_shape=jax.ShapeDtypeStruct((b, 2, N_CMP_PAD, LANES), BF16),
        compiler_params=_cparams(("parallel", "parallel")),
    )(p3, wab, pos_a, pos_b, w2bd)


def _nsa_prompt_body(q_ref, z_ref, g_ref, cmp_ref, sk_ref, sv_ref, wk_ref, wv_ref, ovt_ref, e_ref, o_ref,
                     *, n_sel):
    tq = 128
    i = pl.program_id(1)
    q0 = i * tq
    qh = _pad_heads(q_ref[...] * SCALE)
    gs = _sigmoid(g_ref[...])
    qpos_col = q0 + lax.broadcasted_iota(I32, (tq, 1), 0)
    qpos_row = q0 + lax.broadcasted_iota(I32, (1, LANES), 1)
    o_cmp, p_cmp = _cmp_branch(qh, cmp_ref[0], cmp_ref[1], qpos_col)
    ovt = ovt_ref[...]

    w0 = pl.multiple_of(jnp.maximum(q0 - WINDOW, 0), LANES)
    wwid = WINDOW + tq
    kpos_w = w0 + lax.broadcasted_iota(I32, (1, wwid), 1)
    kposf_w = kpos_w.astype(F32)
    kwin = wk_ref[pl.ds(w0, wwid), :].astype(BF16)
    vwin = wv_ref[pl.ds(w0, wwid), :].astype(BF16)
    nch = (i + SEL_CHUNK // tq) // (SEL_CHUNK // tq)

    o_sel, o_win = [], []
    for h in range(NSA_KV_HEADS):
        heads = range(h * NSA_GROUP, (h + 1) * NSA_GROUP)
        psum = p_cmp[h * NSA_GROUP]
        for hd in list(heads)[1:]:
            psum = psum + p_cmp[hd]
        sel = _select_blocks(psum, qpos_row, ovt, n_sel)
        sel4 = jnp.concatenate([sel] * NSA_GROUP, axis=0).astype(BF16)
        q4 = jnp.concatenate([qh[hd] for hd in heads], axis=0)
        slope_col = jnp.concatenate([jnp.full((tq, 1), NSA_SLOPES[hd], F32) for hd in heads], axis=0)
        qpos4 = jnp.concatenate([qpos_col] * NSA_GROUP, axis=0)

        def body(c, carry, q4=q4, sel4=sel4, slope_col=slope_col, qpos4=qpos4):
            m, l, acc = carry
            k0 = pl.multiple_of(c * SEL_CHUNK, SEL_CHUNK)
            kch = sk_ref[pl.ds(k0, SEL_CHUNK), :].astype(BF16)
            vch = sv_ref[pl.ds(k0, SEL_CHUNK), :].astype(BF16)
            kpos = k0 + lax.broadcasted_iota(I32, (1, SEL_CHUNK), 1)
            s = _dot_nt(q4, kch) + slope_col * kpos.astype(F32)
            mexp = _dot(sel4, e_ref[c])
            mask = (mexp > 0.5) & (kpos <= qpos4)
            s = jnp.where(mask, s, NEG)
            m_new = jnp.maximum(m, jnp.max(s, axis=-1, keepdims=True))
            alpha = jnp.exp(m - m_new)
            p = jnp.where(mask, jnp.exp(s - m_new), 0.0)
            l = alpha * l + jnp.sum(p, axis=-1, keepdims=True)
            acc = alpha * acc + _dot(p.astype(BF16), vch)
            return m_new, l, acc

        init = (jnp.full((NSA_GROUP * tq, 1), NEG, F32), jnp.zeros((NSA_GROUP * tq, 1), F32),
                jnp.zeros((NSA_GROUP * tq, LANES), F32))
        _, l, acc = lax.fori_loop(0, nch, body, init)
        osel4 = acc * jnp.where(l > 0.0, 1.0 / l, 0.0)

        sw = _dot_nt(q4, kwin) + slope_col * kposf_w
        dist = qpos4 - kpos_w
        pw = _msoftmax(sw, (dist >= 0) & (dist < WINDOW))
        owin4 = _dot(pw.astype(BF16), vwin)
        for g in range(NSA_GROUP):
            o_sel.append(osel4[g * tq:(g + 1) * tq])
            o_win.append(owin4[g * tq:(g + 1) * tq])

    o_ref[...] = _combine_heads(gs, o_cmp, o_sel, o_win, z_ref[...]).astype(o_ref.dtype)


def _nsa_prompt(p3, cmp_tok, ovt, e3):
    b, t, _ = p3.shape
    tq = 128
    n_sel = -(-t // SEL_BLOCK)
    kvb = C_KV // LANES
    return pl.pallas_call(
        functools.partial(_nsa_prompt_body, n_sel=n_sel),
        grid=(b, t // tq),
        in_specs=[pl.BlockSpec((None, tq, NSA_WIDTH), lambda bi, i: (bi, i, C_Q // NSA_WIDTH)),
                  pl.BlockSpec((None, tq, NSA_WIDTH), lambda bi, i: (bi, i, C_ZN // NSA_WIDTH)),
                  pl.BlockSpec((None, tq, LANES), lambda bi, i: (bi, i, C_G // LANES)),
                  pl.BlockSpec((None, 2, N_CMP_PAD, LANES), lambda bi, i: (bi, 0, 0, 0)),
                  pl.BlockSpec((None, t, LANES), lambda bi, i: (bi, 0, kvb + 2)),
                  pl.BlockSpec((None, t, LANES), lambda bi, i: (bi, 0, kvb + 3)),
                  pl.BlockSpec((None, t, LANES), lambda bi, i: (bi, 0, kvb + 4)),
                  pl.BlockSpec((None, t, LANES), lambda bi, i: (bi, 0, kvb + 5)),
                  pl.BlockSpec((NSEL_PAD, LANES), lambda bi, i: (0, 0)),
                  pl.BlockSpec(e3.shape, lambda bi, i: (0, 0, 0))],
        out_specs=pl.BlockSpec((None, tq, NSA_WIDTH), lambda bi, i: (bi, i, 0)),
        out_shape=jax.ShapeDtypeStruct((b, t, NSA_WIDTH), BF16),
        compiler_params=_cparams(("parallel", "arbitrary")),
    )(p3, p3, p3, cmp_tok, p3, p3, p3, p3, ovt, e3)


def _diff_lambda(lam_ref, lam_init):
    a = lam_ref[...]
    s1 = jnp.sum(a[0:1] * a[1:2], axis=-1, keepdims=True)
    s2 = jnp.sum(a[2:3] * a[3:4], axis=-1, keepdims=True)
    return jnp.exp(s1) - jnp.exp(s2) + lam_init


def _diff_finish(o, lnw, z, lam_init):
    ms = jnp.mean(o * o, axis=-1, keepdims=True)
    return o * lax.rsqrt(ms + EPS) * lnw * (1.0 - lam_init) * _silu(z)


def _diff_prompt_body(lam_ref, q_ref, z_ref, k_ref, v_ref, lnw_ref, o_ref, *, lam_init):
    tq = 128
    i = pl.program_id(1)
    q0 = i * tq
    lam = _diff_lambda(lam_ref, lam_init)
    low = lax.broadcasted_iota(I32, (tq, LANES), 1) < DIFF_HEAD_DIM
    qpos2 = q0 + (lax.broadcasted_iota(I32, (2 * tq, 1), 0) % tq)
    nch = (i + SEL_CHUNK // tq) // (SEL_CHUNK // tq)
    lnw = lnw_ref[...]
    slabs = []
    for hh in range(DIFF_HEADS):
        cs = slice(hh * LANES, (hh + 1) * LANES)
        qs = q_ref[:, cs] * DIFF_SCALE
        q2 = jnp.concatenate([jnp.where(low, qs, 0.0), jnp.where(low, 0.0, qs)], axis=0).astype(BF16)

        def body(c, carry, q2=q2, cs=cs, hh=hh):
            m, l, acc = carry
            k0 = pl.multiple_of(c * SEL_CHUNK, SEL_CHUNK)
            kch = k_ref[pl.ds(k0, SEL_CHUNK), cs].astype(BF16)
            vch = v_ref[pl.ds(k0, SEL_CHUNK), cs].astype(BF16)
            kpos = k0 + lax.broadcasted_iota(I32, (1, SEL_CHUNK), 1)
            s = _dot_nt(q2, kch) + DIFF_SLOPES[hh] * kpos.astype(F32)
            mask = kpos <= qpos2
            s = jnp.where(mask, s, NEG)
            m_new = jnp.maximum(m, jnp.max(s, axis=-1, keepdims=True))
            alpha = jnp.exp(m - m_new)
            p = jnp.where(mask, jnp.exp(s - m_new), 0.0)
            l = alpha * l + jnp.sum(p, axis=-1, keepdims=True)
            acc = alpha * acc + _dot(p.astype(BF16), vch)
            return m_new, l, acc

        init = (jnp.full((2 * tq, 1), NEG, F32), jnp.zeros((2 * tq, 1), F32), jnp.zeros((2 * tq, LANES), F32))
        _, l, acc = lax.fori_loop(0, nch, body, init)
        on = acc * (1.0 / l)
        o = on[0:tq] - lam * on[tq:2 * tq]
        slabs.append(_diff_finish(o, lnw, z_ref[:, cs], lam_init))
    o_ref[...] = jnp.concatenate(slabs, axis=1).astype(o_ref.dtype)


def _diff_prompt(p3, lam4, lnw_row, lam_init):
    b, t, _ = p3.shape
    tq = 128
    return pl.pallas_call(
        functools.partial(_diff_prompt_body, lam_init=lam_init),
        grid=(b, t // tq),
        in_specs=[pl.BlockSpec((4, DIFF_HEAD_DIM), lambda bi, i: (0, 0)),
                  pl.BlockSpec((None, tq, DIFF_WIDTH), lambda bi, i: (bi, i, C_DQ // DIFF_WIDTH)),
                  pl.BlockSpec((None, tq, DIFF_WIDTH), lambda bi, i: (bi, i, C_ZD // DIFF_WIDTH)),
                  pl.BlockSpec((None, t, DIFF_WIDTH), lambda bi, i: (bi, 0, C_DK // DIFF_WIDTH)),
                  pl.BlockSpec((None, t, DIFF_WIDTH), lambda bi, i: (bi, 0, C_DV // DIFF_WIDTH)),
                  pl.BlockSpec((1, DIFF_V_DIM), lambda bi, i: (0, 0))],
        out_specs=pl.BlockSpec((None, tq, DIFF_WIDTH), lambda bi, i: (bi, i, 0)),
        out_shape=jax.ShapeDtypeStruct((b, t, DIFF_WIDTH), BF16),
        compiler_params=_cparams(("parallel", "arbitrary")),
    )(lam4, p3, p3, p3, p3, lnw_row)


SSM_BLOCKS = SSM_WIDTH // LANES
SSM_BS = N_STATE // SSM_BLOCKS
SSM_CHUNK = 256
SSM_SEG = SSM_CHUNK // SUBLANES
SSM_UNROLL = 8


def _discretize(lre, lim, ldt):
    dt = jnp.exp(ldt)
    mag = jnp.exp(lre * dt)
    ar = mag * jnp.cos(lim * dt)
    ai = mag * jnp.sin(lim * dt)
    den = lre * lre + lim * lim
    qr = ((ar - 1.0) * lre + ai * lim) / den
    qi = (ai * lre - (ar - 1.0) * lim) / den
    return ar, ai, qr, qi


def _bbar_block(qr, qi, bre, bim):
    return jnp.concatenate([qr * bre - qi * bim, qr * bim + qi * bre], axis=1).astype(BF16)


def _glu_tail(y, wglu, z):
    gate = _sigmoid(_dot(y.astype(BF16), wglu))
    return y * gate * _silu(z)


def _ssm_prompt_body(u_ref, z_ref, lre_ref, lim_ref, ldt_ref, bre_ref, bim_ref, cmat_ref, d_ref, wglu_ref,
                     perm_ref, o_ref, hr_ref, hi_ref,
                     bbar_s, apr_s, api_s, bu_s, y_s, carr_s, cari_s):
    first = (pl.program_id(0) == 0) & (pl.program_id(1) == 0)

    @pl.when(first)
    def _():
        ar, ai, qr, qi = _discretize(lre_ref[...], lim_ref[...], ldt_ref[...])
        for k in range(SSM_BLOCKS):
            cs = slice(k * SSM_BS, (k + 1) * SSM_BS)
            bbar_s[k] = _bbar_block(qr[:, cs], qi[:, cs], bre_ref[k], bim_ref[k])
        pr, pi = ar, ai
        apr_s[0:1, :] = pr
        api_s[0:1, :] = pi
        for k in range(1, SSM_SEG):
            pr, pi = pr * ar - pi * ai, pr * ai + pi * ar
            apr_s[k:k + 1, :] = pr
            api_s[k:k + 1, :] = pi

    @pl.when(pl.program_id(1) == 0)
    def _():
        carr_s[...] = jnp.zeros_like(carr_s)
        cari_s[...] = jnp.zeros_like(cari_s)

    re_c = slice(0, SSM_BS)
    im_c = slice(SSM_BS, 2 * SSM_BS)
    u_perm = _dot(perm_ref[0], u_ref[...].astype(BF16)).astype(BF16)
    unperm = perm_ref[1]

    for k in range(SSM_BLOCKS):
        cs = slice(k * SSM_BS, (k + 1) * SSM_BS)
        us = slice(k * LANES, (k + 1) * LANES)
        bu_s[...] = _dot(u_perm[:, us], bbar_s[k])
        ar_b = jnp.broadcast_to(apr_s[0:1, cs], (SUBLANES, SSM_BS))
        ai_b = jnp.broadcast_to(api_s[0:1, cs], (SUBLANES, SSM_BS))

        def step(t, st, ar_b=ar_b, ai_b=ai_b):
            hr, hi = st
            rows = pl.ds(pl.multiple_of(t * SUBLANES, SUBLANES), SUBLANES)
            nr = ar_b * hr - ai_b * hi + bu_s[rows, re_c]
            ni = ar_b * hi + ai_b * hr + bu_s[rows, im_c]
            bu_s[rows, re_c] = nr
            bu_s[rows, im_c] = ni
            return nr, ni

        zero = jnp.zeros((SUBLANES, SSM_BS), F32)
        er, ei = lax.fori_loop(0, SSM_SEG, step, (zero, zero), unroll=SSM_UNROLL)
        a32r = apr_s[SSM_SEG - 1:SSM_SEG, cs]
        a32i = api_s[SSM_SEG - 1:SSM_SEG, cs]
        rows_r = [carr_s[:, cs]]
        rows_i = [cari_s[:, cs]]
        for s in range(1, SUBLANES + 1):
            pr, pi = rows_r[-1], rows_i[-1]
            rows_r.append(er[s - 1:s] + a32r * pr - a32i * pi)
            rows_i.append(ei[s - 1:s] + a32r * pi + a32i * pr)
        carr_s[:, cs] = rows_r[SUBLANES]
        cari_s[:, cs] = rows_i[SUBLANES]
        hin_r = jnp.concatenate(rows_r[:SUBLANES], axis=0)
        hin_i = jnp.concatenate(rows_i[:SUBLANES], axis=0)

        def fix(t, carry, hin_r=hin_r, hin_i=hin_i, cs=cs):
            rows = pl.ds(pl.multiple_of(t * SUBLANES, SUBLANES), SUBLANES)
            pr = apr_s[pl.ds(t, 1), cs]
            pi = api_s[pl.ds(t, 1), cs]
            bu_s[rows, re_c] = bu_s[rows, re_c] + pr * hin_r - pi * hin_i
            bu_s[rows, im_c] = bu_s[rows, im_c] + pr * hin_i + pi * hin_r
            return carry

        lax.fori_loop(0, SSM_SEG, fix, 0, unroll=SSM_UNROLL)
        yp = _dot(bu_s[...].astype(BF16), cmat_ref[k])
        y_hi = yp.astype(BF16)
        r1 = yp - y_hi.astype(F32)
        y_mid = r1.astype(BF16)
        y_lo = (r1 - y_mid.astype(F32)).astype(BF16)
        yk = _dot(unperm, y_hi) + _dot(unperm, y_mid) + _dot(unperm, y_lo) + d_ref[:, us] * u_ref[:, us]
        y_s[:, us] = _gelu(yk)

    o_ref[...] = _glu_tail(y_s[...], wglu_ref[...], z_ref[...]).astype(o_ref.dtype)
    hr_ref[...] = carr_s[...]
    hi_ref[...] = cari_s[...]


def _ssm_prompt(p3, sp):
    b, t, _ = p3.shape
    full = lambda shape: pl.BlockSpec(shape, lambda bi, j: (0,) * len(shape))
    out, hr, hi = pl.pallas_call(
        _ssm_prompt_body,
        grid=(b, t // SSM_CHUNK),
        in_specs=[pl.BlockSpec((None, SSM_CHUNK, SSM_WIDTH), lambda bi, j: (bi, j, C_U // SSM_WIDTH)),
                  pl.BlockSpec((None, SSM_CHUNK, SSM_WIDTH), lambda bi, j: (bi, j, C_ZS // SSM_WIDTH)),
                  full((1, N_STATE)), full((1, N_STATE)), full((1, N_STATE)),
                  full((SSM_BLOCKS, LANES, SSM_BS)), full((SSM_BLOCKS, LANES, SSM_BS)),
                  full((SSM_BLOCKS, 2 * SSM_BS, LANES)), full((1, SSM_WIDTH)),
                  full((SSM_WIDTH, SSM_WIDTH)), full((2, SSM_CHUNK, SSM_CHUNK))],
        out_specs=[pl.BlockSpec((None, SSM_CHUNK, SSM_WIDTH), lambda bi, j: (bi, j, 0)),
                   pl.BlockSpec((None, 1, N_STATE), lambda bi, j: (bi, 0, 0)),
                   pl.BlockSpec((None, 1, N_STATE), lambda bi, j: (bi, 0, 0))],
        out_shape=[jax.ShapeDtypeStruct((b, t, SSM_WIDTH), BF16),
                   jax.ShapeDtypeStruct((b, 1, N_STATE), F32),
                   jax.ShapeDtypeStruct((b, 1, N_STATE), F32)],
        scratch_shapes=[pltpu.VMEM((SSM_BLOCKS, LANES, 2 * SSM_BS), BF16),
                        pltpu.VMEM((SSM_SEG, N_STATE), F32), pltpu.VMEM((SSM_SEG, N_STATE), F32),
                        pltpu.VMEM((SSM_CHUNK, 2 * SSM_BS), F32),
                        pltpu.VMEM((SSM_CHUNK, SSM_WIDTH), F32),
                        pltpu.VMEM((1, N_STATE), F32), pltpu.VMEM((1, N_STATE), F32)],
        compiler_params=_cparams(("arbitrary", "arbitrary")),
    )(p3, p3, sp['lre'], sp['lim'], sp['ldt'], sp['bre'], sp['bim'], sp['cmat'], sp['d'], sp['wglu'],
      _segment_permutation())
    return out, hr, hi


def _segment_permutation():
    r = np.arange(SSM_CHUNK)
    src = (r % SUBLANES) * SSM_SEG + r // SUBLANES
    pm = np.zeros((SSM_CHUNK, SSM_CHUNK), np.float32)
    pm[r, src] = 1.0
    return jnp.asarray(np.stack([pm, pm.T]), BF16)


def _ssm_sample_body(u_ref, z_ref, h0r_ref, h0i_ref, lre_ref, lim_ref, ldt_ref, bre_ref, bim_ref, cmat_ref,
                     d_ref, wglu_ref, o_ref, hr_ref, hi_ref, y_s, *, n_t):
    k = pl.program_id(0)
    ar, ai, qr, qi = _discretize(lre_ref[...], lim_ref[...], ldt_ref[...])
    bbar = _bbar_block(qr, qi, bre_ref[...], bim_ref[...])
    hr = h0r_ref[...]
    hi = h0i_ref[...]
    cmat = cmat_ref[...]
    for t in range(n_t):
        ut = u_ref[t]
        bu = _dot(ut.astype(BF16), bbar)
        hr, hi = ar * hr - ai * hi + bu[:, 0:SSM_BS], ar * hi + ai * hr + bu[:, SSM_BS:]
        hcat = jnp.concatenate([hr, hi], axis=1).astype(BF16)
        y_s[t, k] = _gelu(_dot(hcat, cmat) + d_ref[...] * ut)
    hr_ref[...] = hr
    hi_ref[...] = hi

    @pl.when(k == SSM_BLOCKS - 1)
    def _():
        wglu = wglu_ref[...]
        for t in range(n_t):
            y = jnp.concatenate([y_s[t, kk] for kk in range(SSM_BLOCKS)], axis=1)
            o_ref[t] = _glu_tail(y, wglu, z_ref[t])


def _ssm_sample(ut, zt, h0r, h0i, sp):
    n_t, nb, _ = ut.shape
    out, hr, hi = pl.pallas_call(
        functools.partial(_ssm_sample_body, n_t=n_t),
        grid=(SSM_BLOCKS,),
        in_specs=[pl.BlockSpec((n_t, nb, LANES), lambda k: (0, 0, k)),
                  pl.BlockSpec((n_t, nb, SSM_WIDTH), lambda k: (0, 0, 0)),
                  pl.BlockSpec((nb, SSM_BS), lambda k: (0, k)),
                  pl.BlockSpec((nb, SSM_BS), lambda k: (0, k)),
                  pl.BlockSpec((1, SSM_BS), lambda k: (0, k)),
                  pl.BlockSpec((1, SSM_BS), lambda k: (0, k)),
                  pl.BlockSpec((1, SSM_BS), lambda k: (0, k)),
                  pl.BlockSpec((None, LANES, SSM_BS), lambda k: (k, 0, 0)),
                  pl.BlockSpec((None, LANES, SSM_BS), lambda k: (k, 0, 0)),
                  pl.BlockSpec((None, 2 * SSM_BS, LANES), lambda k: (k, 0, 0)),
                  pl.BlockSpec((1, LANES), lambda k: (0, k)),
                  pl.BlockSpec((SSM_WIDTH, SSM_WIDTH), lambda k: (0, 0))],
        out_specs=[pl.BlockSpec((n_t, nb, SSM_WIDTH), lambda k: (0, 0, 0)),
                   pl.BlockSpec((nb, SSM_BS), lambda k: (0, k)),
                   pl.BlockSpec((nb, SSM_BS), lambda k: (0, k))],
        out_shape=[jax.ShapeDtypeStruct((n_t, nb, SSM_WIDTH), F32),
                   jax.ShapeDtypeStruct((nb, N_STATE), F32),
                   jax.ShapeDtypeStruct((nb, N_STATE), F32)],
        scratch_shapes=[pltpu.VMEM((n_t, SSM_BLOCKS, nb, LANES), F32)],
        compiler_params=_cparams(("arbitrary",)),
    )(ut, zt, h0r, h0i, sp['lre'], sp['lim'], sp['ldt'], sp['bre'], sp['bim'], sp['cmat'], sp['d'], sp['wglu'])
    return out, hr, hi


def _page_map(li, pg, b, pt):
    return (li, pt[b, pg], 0, 0)


def _page_map_seq(li, pg, sq, b, pt):
    return (li, pt[b * NSA_DEC_SEQS + sq, pg], 0, 0)


def _pad_rows(x, rows):
    return jnp.concatenate([x, jnp.zeros((rows - x.shape[0], x.shape[1]), x.dtype)], axis=0)


def _nsa_sample_body(pt_ref, *refs, n_pages, n_sel):
    nsq = NSA_DEC_SEQS
    (q_ref, z_ref, g_ref, snew_ref, wnew_ref, wb_ref, wab_ref, pa_ref, pb_ref, w2_ref, ovt_ref, e_ref,
     o_ref, s_s, rows_s) = refs[nsq * n_pages:]
    seqs = range(nsq)
    pages = [refs[sq * n_pages:(sq + 1) * n_pages] for sq in seqs]
    rs = [slice(sq * T_PAD, (sq + 1) * T_PAD) for sq in seqs]
    past = n_pages * PAGE_SIZE
    nk = past + LANES
    wb = wb_ref.shape[2]
    qpos_col = past + lax.broadcasted_iota(I32, (T_PAD, 1), 0)
    qpos64 = jnp.concatenate([qpos_col] * NSA_HEADS, axis=0)
    qpos_row = past + (lax.broadcasted_iota(I32, (1, LANES), 1) % T_PAD)
    slope_col = jnp.concatenate([jnp.full((T_PAD, 1), NSA_SLOPES[hd], F32) for hd in range(NSA_HEADS)], axis=0)
    kpos = lax.broadcasted_iota(I32, (1, nk), 1)
    kbias = slope_col * kpos.astype(F32)
    kpos_w = (past - wb) + lax.broadcasted_iota(I32, (1, wb + LANES), 1)
    wbias = slope_col * kpos_w.astype(F32)
    dist = qpos64 - kpos_w
    wmask = (dist >= 0) & (dist < WINDOW) & (kpos_w >= 0)
    causal = kpos <= qpos64
    sk_r = pl.ds(2 * LANES, LANES)
    sv_r = pl.ds(3 * LANES, LANES)

    qh = [_pad_heads(q_ref[rs[sq], :] * SCALE) for sq in seqs]
    q64 = [jnp.concatenate(qh[sq], axis=0) for sq in seqs]

    for kv in range(2):
        for sq in seqs:
            for pg in range(n_pages):
                rows_s[sq, kv, pg * PAGE_SIZE:(pg + 1) * PAGE_SIZE, :] = (
                    pages[sq][pg][kv * LANES:(kv + 1) * LANES, :].T)

    owin = []
    for sq in seqs:
        wnew = wnew_ref[rs[sq], :]
        kwn = _pad_rows(wnew[:, 0:LANES], LANES).astype(BF16)
        vwn = _pad_rows(wnew[:, LANES:2 * LANES], LANES).astype(BF16)
        sw = jnp.concatenate([_dot(q64[sq], wb_ref[sq, 0:LANES, :].astype(BF16)), _dot_nt(q64[sq], kwn)], axis=1)
        pw = _msoftmax(sw + wbias, wmask).astype(BF16)
        owin.append(_dot_nt(pw[:, 0:wb], wb_ref[sq, LANES:2 * LANES, :].astype(BF16)) + _dot(pw[:, wb:], vwn))

    for sq in seqs:
        for pg in range(n_pages):
            s_s[sq, :, pg * PAGE_SIZE:(pg + 1) * PAGE_SIZE] = _dot(q64[sq], pages[sq][pg][sk_r, :].astype(BF16))
        knew = _pad_rows(snew_ref[rs[sq], 0:LANES], LANES).astype(BF16)
        s_s[sq, :, past:nk] = _dot_nt(q64[sq], knew)

    toks = [[], []]
    for kv in range(2):
        for sq in seqs:
            xs = [rows_s[sq, kv, pl.ds(l, N_CMP_PAD, stride=CMP_STRIDE), :].astype(BF16)
                  for l in range(CMP_STRIDE)]
            toks[kv].append(_compress_tokens(jnp.concatenate(xs, axis=1), wab_ref[kv], pa_ref[kv], pb_ref[kv],
                                             w2_ref[kv]))
    cmp = [_cmp_branch(qh[sq], toks[0][sq], toks[1][sq], qpos_col) for sq in seqs]

    memb = []
    for sq in seqs:
        p_cmp = cmp[sq][1]
        psum = jnp.concatenate(
            [p_cmp[0] + p_cmp[1] + p_cmp[2] + p_cmp[3], p_cmp[4] + p_cmp[5] + p_cmp[6] + p_cmp[7],
             jnp.zeros((LANES - 2 * T_PAD, N_CMP_PAD), F32)], axis=0)
        sel = _select_blocks(psum, qpos_row, ovt_ref[...], n_sel)
        mexp = _dot(sel[0:2 * T_PAD].astype(BF16), e_ref[...])
        memb.append(
            jnp.concatenate([mexp[0:T_PAD]] * NSA_GROUP + [mexp[T_PAD:2 * T_PAD]] * NSA_GROUP, axis=0) > 0.5)

    for sq in seqs:
        p = _msoftmax(s_s[sq] + kbias, memb[sq] & causal).astype(BF16)
        vnew = _pad_rows(snew_ref[rs[sq], LANES:2 * LANES], LANES).astype(BF16)
        osel = _dot(p[:, past:nk], vnew)
        for pg in range(n_pages):
            osel = osel + _dot_nt(p[:, pg * PAGE_SIZE:(pg + 1) * PAGE_SIZE], pages[sq][pg][sv_r, :].astype(BF16))
        o_sel = [osel[hd * T_PAD:(hd + 1) * T_PAD] for hd in range(NSA_HEADS)]
        o_win = [owin[sq][hd * T_PAD:(hd + 1) * T_PAD] for hd in range(NSA_HEADS)]
        o_ref[rs[sq], :] = _combine_heads(_sigmoid(g_ref[rs[sq], :]), cmp[sq][0], o_sel, o_win, z_ref[rs[sq], :])


def _nsa_sample(p2, cache_t, li, win_t, page_table, wab, pos_a, pos_b, w2bd, ovt, e2, n_t):
    n = p2.shape[0]
    nb, n_pages = page_table.shape
    past = n_pages * PAGE_SIZE
    n_sel = -(-(past + n_t) // SEL_BLOCK)
    nk = past + LANES
    wb = win_t.shape[3]
    kvb = C_KV // (2 * LANES)
    nsq = NSA_DEC_SEQS
    assert nb % nsq == 0
    page_specs = [pl.BlockSpec((None, None, 4 * LANES, PAGE_SIZE), functools.partial(_page_map_seq, li, pg, sq))
                  for sq in range(nsq) for pg in range(n_pages)]
    const = lambda shape: pl.BlockSpec(shape, lambda b, pt: (0,) * len(shape))
    rows = nsq * T_PAD
    in_specs = page_specs + [
        pl.BlockSpec((rows, NSA_WIDTH), lambda b, pt: (b, C_Q // NSA_WIDTH)),
        pl.BlockSpec((rows, NSA_WIDTH), lambda b, pt: (b, C_ZN // NSA_WIDTH)),
        pl.BlockSpec((rows, LANES), lambda b, pt: (b, C_G // LANES)),
        pl.BlockSpec((rows, 2 * LANES), lambda b, pt: (b, kvb + 1)),
        pl.BlockSpec((rows, 2 * LANES), lambda b, pt: (b, kvb + 2)),
        pl.BlockSpec((None, nsq, 2 * LANES, wb), lambda b, pt: (li, b, 0, 0)),
        const(wab.shape), const(pos_a.shape), const(pos_b.shape), const(w2bd.shape),
        const(ovt.shape), const(e2.shape)]
    return pl.pallas_call(
        functools.partial(_nsa_sample_body, n_pages=n_pages, n_sel=n_sel),
        grid_spec=pltpu.PrefetchScalarGridSpec(
            num_scalar_prefetch=1, grid=(nb // nsq,), in_specs=in_specs,
            out_specs=pl.BlockSpec((rows, NSA_WIDTH), lambda b, pt: (b, 0)),
            scratch_shapes=[pltpu.VMEM((nsq, NSA_HEADS * T_PAD, nk), F32),
                            pltpu.VMEM((nsq, 2, past, LANES), F32)]),
        out_shape=jax.ShapeDtypeStruct((n, NSA_WIDTH), F32),
        compiler_params=_cparams(("parallel",)),
    )(page_table, *([cache_t] * (nsq * n_pages)), p2, p2, p2, p2, p2, win_t, wab, pos_a, pos_b, w2bd, ovt, e2)


def _diff_sample_body(pt_ref, *refs, n_pages, lam_init):
    pages = refs[:n_pages]
    lam_ref, q_ref, z_ref, knew_ref, vnew_ref, lnw_ref, o_ref, s_s = refs[n_pages:]
    past = n_pages * PAGE_SIZE
    nk = past + LANES
    lam = _diff_lambda(lam_ref, lam_init)
    low = lax.broadcasted_iota(I32, (T_PAD, LANES), 1) < DIFF_HEAD_DIM
    rows_h = 2 * T_PAD
    q16 = []
    for hh in range(DIFF_HEADS):
        qs = q_ref[:, hh * LANES:(hh + 1) * LANES] * DIFF_SCALE
        q16.append(jnp.concatenate([jnp.where(low, qs, 0.0), jnp.where(low, 0.0, qs)], axis=0).astype(BF16))
    for pg in range(n_pages):
        for hh in range(DIFF_HEADS):
            kp = pages[pg][pl.ds(hh, PAGE_SIZE, stride=DIFF_SLABS), :].astype(BF16)
            s_s[hh * rows_h:(hh + 1) * rows_h, pg * PAGE_SIZE:(pg + 1) * PAGE_SIZE] = _dot_nt(q16[hh], kp)
    for hh in range(DIFF_HEADS):
        kn = _pad_rows(knew_ref[:, hh * LANES:(hh + 1) * LANES], LANES).astype(BF16)
        s_s[hh * rows_h:(hh + 1) * rows_h, past:nk] = _dot_nt(q16[hh], kn)
    kpos = lax.broadcasted_iota(I32, (1, nk), 1)
    slope_col = jnp.concatenate([jnp.full((rows_h, 1), DIFF_SLOPES[hh], F32) for hh in range(DIFF_HEADS)], axis=0)
    qpos = past + (lax.broadcasted_iota(I32, (DIFF_HEADS * rows_h, 1), 0) % T_PAD)
    p = _msoftmax(s_s[...] + slope_col * kpos.astype(F32), kpos <= qpos)
    lnw = lnw_ref[...]
    slabs = []
    for hh in range(DIFF_HEADS):
        a = (p[hh * rows_h:hh * rows_h + T_PAD] - lam * p[hh * rows_h + T_PAD:(hh + 1) * rows_h]).astype(BF16)
        vn = _pad_rows(vnew_ref[:, hh * LANES:(hh + 1) * LANES], LANES).astype(BF16)
        o = _dot(a[:, past:nk], vn)
        for pg in range(n_pages):
            vp = pages[pg][pl.ds(DIFF_HEADS + hh, PAGE_SIZE, stride=DIFF_SLABS), :].astype(BF16)
            o = o + _dot(a[:, pg * PAGE_SIZE:(pg + 1) * PAGE_SIZE], vp)
        slabs.append(_diff_finish(o, lnw, z_ref[:, hh * LANES:(hh + 1) * LANES], lam_init))
    o_ref[...] = jnp.concatenate(slabs, axis=1)


def _diff_sample(p2, cache_r, li, page_table, lam4, lnw_row, lam_init):
    n = p2.shape[0]
    nb, n_pages = page_table.shape
    nk = n_pages * PAGE_SIZE + LANES
    page_specs = [pl.BlockSpec((None, None, PAGE_SIZE * DIFF_SLABS, LANES), functools.partial(_page_map, li, pg))
                  for pg in range(n_pages)]
    in_specs = page_specs + [
        pl.BlockSpec((4, DIFF_HEAD_DIM), lambda b, pt: (0, 0)),
        pl.BlockSpec((T_PAD, DIFF_WIDTH), lambda b, pt: (b, C_DQ // DIFF_WIDTH)),
        pl.BlockSpec((T_PAD, DIFF_WIDTH), lambda b, pt: (b, C_ZD // DIFF_WIDTH)),
        pl.BlockSpec((T_PAD, DIFF_WIDTH), lambda b, pt: (b, C_DK // DIFF_WIDTH)),
        pl.BlockSpec((T_PAD, DIFF_WIDTH), lambda b, pt: (b, C_DV // DIFF_WIDTH)),
        pl.BlockSpec((1, DIFF_V_DIM), lambda b, pt: (0, 0))]
    return pl.pallas_call(
        functools.partial(_diff_sample_body, n_pages=n_pages, lam_init=lam_init),
        grid_spec=pltpu.PrefetchScalarGridSpec(
            num_scalar_prefetch=1, grid=(nb,), in_specs=in_specs,
            out_specs=pl.BlockSpec((T_PAD, DIFF_WIDTH), lambda b, pt: (b, 0)),
            scratch_shapes=[pltpu.VMEM((DIFF_HEADS * 2 * T_PAD, nk), F32)]),
        out_shape=jax.ShapeDtypeStruct((n, DIFF_WIDTH), F32),
        compiler_params=_cparams(("parallel",)),
    )(page_table, *([cache_r] * n_pages), lam4, p2, p2, p2, p2, lnw_row)


def _overlap_t(n_sel):
    n = np.arange(N_CMP_PAD)[None, :]
    j = np.arange(NSEL_PAD)[:, None]
    cs = n * CMP_STRIDE
    ss = j * SEL_BLOCK
    ov = np.clip(np.minimum(cs + CMP_BLOCK, ss + SEL_BLOCK) - np.maximum(cs, ss), 0, None) / CMP_BLOCK
    ov = np.where((n < N_CMP_PAD - 1) & (j < n_sel), ov, 0.0)
    return jnp.asarray(ov, BF16)


def _block_expander(n_keys):
    j = np.arange(LANES)[:, None]
    k = np.arange(n_keys)[None, :]
    return jnp.asarray((k // SEL_BLOCK) == j, BF16)


def _layer_weights(li, norm_w, w_in, w_out, w_cmp1, pos_cmp, w_cmp2, lre, lim, ldt, bre, bim, cre, cim, sd,
                   w_glu, lam_q1, lam_k1, lam_q2, lam_k2, diff_ln_w):
    w = {}
    w['nw'] = norm_w[li][None, :]
    w['w_out'] = w_out[li].astype(BF16)
    eye2 = jnp.eye(NSA_KV_HEADS, dtype=F32)
    w1 = w_cmp1[li].reshape(2, 2, CMP_STRIDE, HEAD_DIM, HEAD_DIM)
    w['wab'] = jnp.einsum('kaldf,hg->klhdagf', w1, eye2).reshape(
        2, CMP_STRIDE * LANES, 2 * LANES).astype(BF16)
    pos = pos_cmp[li].reshape(2, 2, CMP_STRIDE, 1, HEAD_DIM)
    pos = jnp.broadcast_to(pos, (2, 2, CMP_STRIDE, NSA_KV_HEADS, HEAD_DIM)).reshape(2, 2, 1, CMP_STRIDE * LANES)
    pos = jnp.broadcast_to(pos, (2, 2, SUBLANES, CMP_STRIDE * LANES)).astype(BF16)
    w['pos_a'] = pos[:, 0]
    w['pos_b'] = pos[:, 1]
    w['w2bd'] = jnp.einsum('ked,hg->khegd', w_cmp2[li], eye2).reshape(2, LANES, LANES).astype(BF16)
    eye8 = jnp.eye(SSM_BLOCKS, dtype=F32)
    gl = SSM_BLOCKS

    def compact_b(bm):
        bm = bm.reshape(SSM_BLOCKS, gl, SSM_STATE, SSM_GROUP_CH)
        return jnp.einsum('kgpc,gh->kgchp', bm, eye8).reshape(SSM_BLOCKS, LANES, SSM_BS)

    def compact_c(cm):
        cm = cm.reshape(SSM_BLOCKS, gl, SSM_GROUP_CH, SSM_STATE)
        return jnp.einsum('kgcp,gh->khpgc', cm, eye8).reshape(SSM_BLOCKS, SSM_BS, LANES)

    sp = {'lre': lre[li].reshape(1, N_STATE), 'lim': lim[li].reshape(1, N_STATE),
          'ldt': jnp.repeat(ldt[li], SSM_STATE).reshape(1, N_STATE),
          'bre': compact_b(bre[li]), 'bim': compact_b(bim[li]),
          'cmat': jnp.concatenate([compact_c(cre[li]), -compact_c(cim[li])], axis=1).astype(BF16),
          'd': sd[li].reshape(1, SSM_WIDTH), 'wglu': w_glu[li].astype(BF16)}
    w['ssm'] = sp
    w['lam4'] = jnp.stack([lam_q1[li], lam_k1[li], lam_q2[li], lam_k2[li]])
    w['lnw'] = diff_ln_w[li][None, :]
    w['lam_init'] = 0.8 - 0.6 * math.exp(-0.3 * li)
    return w


def kernel(x_prompt, x_sample, cache_nsa_kv, cache_diff_kv, state_nsa_win, state_ssm_re, state_ssm_im, page_table, norm_w, w_in, w_out, w_cmp1, pos_cmp, w_cmp2, ssm_lambda_re, ssm_lambda_im, ssm_log_dt, ssm_b_re, ssm_b_im, ssm_c_re, ssm_c_im, ssm_d, w_glu, lam_q1, lam_k1, lam_q2, lam_k2, diff_ln_w, final_norm_w):
    depth = norm_w.shape[0]
    b, t, _ = x_prompt.shape
    nb, n_t, _ = x_sample.shape
    n_pool = cache_nsa_kv.shape[1]
    n_pages = page_table.shape[1]
    past = n_pages * PAGE_SIZE
    wb = state_nsa_win.shape[2]
    assert t % SSM_CHUNK == 0 and t >= WINDOW + 128 and n_t <= T_PAD

    ovt_p = _overlap_t(-(-t // SEL_BLOCK))
    ovt_s = _overlap_t(-(-(past + n_t) // SEL_BLOCK))
    e_p = _block_expander(t).reshape(LANES, t // SEL_CHUNK, SEL_CHUNK).transpose(1, 0, 2)
    e_s = _block_expander(past + LANES)
    fw = final_norm_w[None, :]
    nsa_cache_t = cache_nsa_kv.transpose(0, 1, 3, 4, 5, 2).reshape(depth, n_pool, 4 * LANES, PAGE_SIZE)
    win_t = state_nsa_win.transpose(0, 1, 3, 4, 5, 2).reshape(depth, nb, 2 * LANES, wb)
    diff_cache_r = cache_diff_kv.reshape(depth, n_pool, PAGE_SIZE * DIFF_SLABS, LANES)
    w_rows = w_in.transpose(0, 2, 1)
    w_in_t = jnp.concatenate(
        [w_rows[:, ORIG_OFFS[s]:ORIG_OFFS[s + 1]] for s in NEW_ORDER]
        + [jnp.zeros((depth, DP - ORIG_OFFS[-1], D_MODEL), w_in.dtype)], axis=1).astype(BF16)

    xp = x_prompt.reshape(b * t, D_MODEL)
    xs = jnp.pad(x_sample, ((0, 0), (0, T_PAD - n_t), (0, 0))).reshape(nb * T_PAD, D_MODEL)
    outs = {k: [] for k in ('p_kv', 'p_dkv', 'p_win', 'p_re', 'p_im', 's_kv', 's_dkv', 's_win', 's_re', 's_im')}
    for li in range(depth):
        w = _layer_weights(li, norm_w, w_in, w_out, w_cmp1, pos_cmp, w_cmp2, ssm_lambda_re, ssm_lambda_im,
                           ssm_log_dt, ssm_b_re, ssm_b_im, ssm_c_re, ssm_c_im, ssm_d, w_glu,
                           lam_q1, lam_k1, lam_q2, lam_k2, diff_ln_w)
        final = li == depth - 1
        pp = _norm_project(xp, w['nw'], w_in_t, li)
        p3 = pp.reshape(b, t, DP)
        cmp_tok = _compress_prompt(p3, w['wab'], w['pos_a'], w['pos_b'], w['w2bd'])
        nsa_o = _nsa_prompt(p3, cmp_tok, ovt_p, e_p)
        ssm_o, hr, hi = _ssm_prompt(p3, w['ssm'])
        diff_o = _diff_prompt(p3, w['lam4'], w['lnw'], w['lam_init'])
        xp = _merge_out(nsa_o.reshape(b * t, -1), ssm_o.reshape(b * t, -1), diff_o.reshape(b * t, -1), xp,
                        w['w_out'], fw, final)
        kv = p3[:, :, C_KV:C_KV + 6 * LANES]
        outs['p_kv'].append(kv[:, :, :4 * LANES].reshape(b, t, 4, NSA_KV_HEADS, HEAD_DIM))
        outs['p_dkv'].append(p3[:, :, C_DK:C_DK + 2 * DIFF_WIDTH].reshape(b, t, 2, DIFF_HEADS, DIFF_V_DIM))
        outs['p_win'].append(kv[:, t - min(WINDOW, t):, 4 * LANES:].reshape(b, -1, 2, NSA_KV_HEADS, HEAD_DIM))
        outs['p_re'].append(hr.reshape(b, SSM_GROUPS, SSM_STATE))
        outs['p_im'].append(hi.reshape(b, SSM_GROUPS, SSM_STATE))
        ps = _norm_project(xs, w['nw'], w_in_t, li)
        nsa_s = _nsa_sample(ps, nsa_cache_t, li, win_t, page_table,
                            w['wab'], w['pos_a'], w['pos_b'], w['w2bd'], ovt_s, e_s, n_t)
        ps_t = ps.reshape(nb, T_PAD, DP)[:, :n_t].transpose(1, 0, 2)
        ssm_t, shr, shi = _ssm_sample(ps_t[:, :, C_U:C_U + SSM_WIDTH], ps_t[:, :, C_ZS:C_ZS + SSM_WIDTH],
                                      state_ssm_re[li].reshape(nb, N_STATE),
                                      state_ssm_im[li].reshape(nb, N_STATE), w['ssm'])
        ssm_s = jnp.pad(ssm_t.transpose(1, 0, 2), ((0, 0), (0, T_PAD - n_t), (0, 0))).reshape(nb * T_PAD, SSM_WIDTH)
        diff_s = _diff_sample(ps, diff_cache_r, li, page_table, w['lam4'], w['lnw'], w['lam_init'])
        xs = _merge_out(nsa_s, ssm_s, diff_s, xs, w['w_out'], fw, final)
        ps3 = ps.reshape(nb, T_PAD, DP)[:, :n_t]
        skv = ps3[:, :, C_KV:C_KV + 6 * LANES]
        outs['s_kv'].append(skv[:, :, :4 * LANES].reshape(nb, n_t, 4, NSA_KV_HEADS, HEAD_DIM))
        outs['s_dkv'].append(ps3[:, :, C_DK:C_DK + 2 * DIFF_WIDTH].reshape(nb, n_t, 2, DIFF_HEADS, DIFF_V_DIM))
        outs['s_win'].append(skv[:, :, 4 * LANES:].transpose(0, 2, 1))
        outs['s_re'].append(shr.reshape(nb, SSM_GROUPS, SSM_STATE))
        outs['s_im'].append(shi.reshape(nb, SSM_GROUPS, SSM_STATE))
    y_prompt = xp.reshape(b, t, D_MODEL)
    y_sample = xs.reshape(nb, T_PAD, D_MODEL)[:, :n_t]
    st = lambda k: jnp.stack(outs[k])
    s_win_t = jnp.concatenate([win_t[:, :, :, n_t:], st('s_win')], axis=3)
    s_win = s_win_t.reshape(depth, nb, 2, NSA_KV_HEADS, HEAD_DIM, wb).transpose(0, 1, 5, 2, 3, 4)
    return (y_prompt, y_sample, st('p_kv'), st('p_dkv'), st('p_win'), st('p_re'), st('p_im'),
            st('s_kv'), st('s_dkv'), s_win, st('s_re'), st('s_im'))
```

```python
import functools
import math

import numpy as np
import jax
import jax.numpy as jnp
from jax import lax
from jax.experimental import pallas as pl
from jax.experimental.pallas import tpu as pltpu

F32 = jnp.float32
BF16 = jnp.bfloat16
I32 = jnp.int32

D_MODEL = 2048
HEAD_DIM = 64
NSA_WIDTH = 512
NSA_HEADS = 8
NSA_KV_HEADS = 2
NSA_GROUP = 4
CMP_BLOCK = 32
CMP_STRIDE = 16
SEL_BLOCK = 64
N_SELECT = 16
WINDOW = 512
SSM_WIDTH = 1024
SSM_GROUP_CH = 16
SSM_GROUPS = 64
SSM_STATE = 64
N_STATE = SSM_GROUPS * SSM_STATE
DIFF_WIDTH = 512
DIFF_HEADS = 4
DIFF_HEAD_DIM = 64
DIFF_V_DIM = 128
DIFF_SLABS = 2 * DIFF_HEADS
PAGE_SIZE = 128
SCALE = HEAD_DIM ** -0.5
DIFF_SCALE = DIFF_HEAD_DIM ** -0.5
NEG = -1e30
BIG = 1e9
EPS = 1e-6
NSA_SLOPES = tuple(float(2.0 ** (-8.0 * (k + 1) / NSA_HEADS)) for k in range(NSA_HEADS))
DIFF_SLOPES = tuple(float(2.0 ** (-8.0 * (k + 1) / DIFF_HEADS)) for k in range(DIFF_HEADS))

LANES = 128
SUBLANES = 8
VMEM_LIMIT = 56 * 1024 * 1024

ORIG_SIZES = (NSA_WIDTH, 6 * NSA_KV_HEADS * HEAD_DIM, 3 * NSA_HEADS, NSA_WIDTH, SSM_WIDTH, SSM_WIDTH,
              DIFF_HEADS * 2 * DIFF_HEAD_DIM, DIFF_WIDTH, DIFF_WIDTH, DIFF_WIDTH)
ORIG_OFFS = tuple(int(v) for v in np.concatenate([[0], np.cumsum(ORIG_SIZES)]))
NEW_ORDER = (0, 3, 6, 7, 8, 9, 4, 5, 1, 2)
C_Q, C_ZN, C_DQ, C_DK, C_DV, C_ZD, C_U, C_ZS, C_KV, C_G = 0, 512, 1024, 1536, 2048, 2560, 3072, 4096, 5120, 5888
DP = 6144
N_CMP_PAD = 128
NSEL_PAD = 40
SEL_CHUNK = 256
T_PAD = 8
NSA_DEC_SEQS = 2


def _cparams(sem):
    return pltpu.CompilerParams(dimension_semantics=sem, vmem_limit_bytes=VMEM_LIMIT)


def _dot(a, b):
    return jnp.dot(a, b, preferred_element_type=F32)


def _dot_nt(a, b):
    return lax.dot_general(a, b, (((1,), (1,)), ((), ())), preferred_element_type=F32)


def _gelu(x):
    return 0.5 * x * (1.0 + jnp.tanh(math.sqrt(2.0 / math.pi) * (x + 0.044715 * (x * x * x))))


def _sigmoid(x):
    return 1.0 / (1.0 + jnp.exp(-x))


def _silu(x):
    return x * _sigmoid(x)


def _msoftmax(s, mask):
    s = jnp.where(mask, s, NEG)
    m = jnp.max(s, axis=-1, keepdims=True)
    e = jnp.exp(s - m)
    den = jnp.sum(e, axis=-1, keepdims=True)
    return jnp.where(mask, e * (1.0 / den), 0.0)


def _proj_body(x_ref, nw_ref, w_ref, o_ref, xn_ref):
    @pl.when(pl.program_id(1) == 0)
    def _():
        x = x_ref[...]
        ms = jnp.mean(x * x, axis=-1, keepdims=True)
        xn_ref[...] = (x * lax.rsqrt(ms + EPS) * nw_ref[...]).astype(BF16)

    o_ref[...] = _dot_nt(xn_ref[...], w_ref[...])


def _norm_project(x2d, nw_row, w_t, li):
    n = x2d.shape[0]
    tm = min(n, 1024)
    tn = 512
    return pl.pallas_call(
        _proj_body,
        grid=(n // tm, DP // tn),
        in_specs=[pl.BlockSpec((tm, D_MODEL), lambda i, j: (i, 0)),
                  pl.BlockSpec((1, D_MODEL), lambda i, j: (0, 0)),
                  pl.BlockSpec((None, tn, D_MODEL), lambda i, j: (li, j, 0))],
        out_specs=pl.BlockSpec((tm, tn), lambda i, j: (i, j)),
        out_shape=jax.ShapeDtypeStruct((n, DP), F32),
        scratch_shapes=[pltpu.VMEM((tm, D_MODEL), BF16)],
        compiler_params=_cparams(("parallel", "arbitrary")),
    )(x2d, nw_row, w_t)


def _out_body(nsa_ref, ssm_ref, diff_ref, x_ref, w_ref, fw_ref, y_ref, *, final):
    acc = x_ref[...]
    acc = acc + _dot(nsa_ref[...].astype(BF16), w_ref[0:NSA_WIDTH, :])
    acc = acc + _dot(ssm_ref[...].astype(BF16), w_ref[NSA_WIDTH:NSA_WIDTH + SSM_WIDTH, :])
    acc = acc + _dot(diff_ref[...].astype(BF16), w_ref[NSA_WIDTH + SSM_WIDTH:, :])
    if final:
        ms = jnp.mean(acc * acc, axis=-1, keepdims=True)
        acc = acc * lax.rsqrt(ms + EPS) * fw_ref[...]
    y_ref[...] = acc


def _merge_out(nsa_o, ssm_o, diff_o, x2d, w_out_bf, fw_row, final):
    n = x2d.shape[0]
    tm = min(n, 512)
    return pl.pallas_call(
        functools.partial(_out_body, final=final),
        grid=(n // tm,),
        in_specs=[pl.BlockSpec((tm, NSA_WIDTH), lambda i: (i, 0)),
                  pl.BlockSpec((tm, SSM_WIDTH), lambda i: (i, 0)),
                  pl.BlockSpec((tm, DIFF_WIDTH), lambda i: (i, 0)),
                  pl.BlockSpec((tm, D_MODEL), lambda i: (i, 0)),
                  pl.BlockSpec((D_MODEL, D_MODEL), lambda i: (0, 0)),
                  pl.BlockSpec((1, D_MODEL), lambda i: (0, 0))],
        out_specs=pl.BlockSpec((tm, D_MODEL), lambda i: (i, 0)),
        out_shape=jax.ShapeDtypeStruct((n, D_MODEL), F32),
        compiler_params=_cparams(("parallel",)),
    )(nsa_o, ssm_o, diff_o, x2d, w_out_bf, fw_row)


def _pad_heads(qb):
    tq = qb.shape[0]
    low = lax.broadcasted_iota(I32, (tq, LANES), 1) < HEAD_DIM
    outs = []
    for hd in range(NSA_HEADS):
        h = hd // NSA_GROUP
        slab = qb[:, (hd // 2) * LANES:(hd // 2 + 1) * LANES]
        if hd % 2 != h:
            slab = pltpu.roll(slab, HEAD_DIM, 1)
        keep = low if h == 0 else jnp.logical_not(low)
        outs.append(jnp.where(keep, slab, 0.0).astype(BF16))
    return outs


def _compress_tokens(x_bf, wab, pos_a, pos_b, w2bd):
    ab = _dot(x_bf, wab)
    pc = _dot(pos_a, wab)[0:1, 0:LANES] + _dot(pos_b, wab)[0:1, LANES:2 * LANES]
    pre = ab[:, 0:LANES] + pltpu.roll(ab[:, LANES:2 * LANES], N_CMP_PAD - 1, 0) + pc
    return _dot(_gelu(pre).astype(BF16), w2bd).astype(BF16)


def _cmp_branch(qh, kc, vc, qpos_col):
    tq = qpos_col.shape[0]
    n_iota = lax.broadcasted_iota(I32, (1, N_CMP_PAD), 1)
    ends = n_iota * CMP_STRIDE + (CMP_BLOCK - 1)
    q_all = jnp.concatenate(qh, axis=0)
    slope_col = jnp.concatenate([jnp.full((tq, 1), NSA_SLOPES[hd], F32) for hd in range(NSA_HEADS)], axis=0)
    qpos_all = jnp.concatenate([qpos_col] * NSA_HEADS, axis=0)
    s = _dot_nt(q_all, kc) + slope_col * ends.astype(F32)
    p = _msoftmax(s, ends <= qpos_all)
    o = _dot(p.astype(BF16), vc)
    rows = [slice(hd * tq, (hd + 1) * tq) for hd in range(NSA_HEADS)]
    return [o[r] for r in rows], [p[r] for r in rows]


def _select_blocks(psum, qpos_row, ovt, n_sel, queries_on_rows=True):
    hi = psum.astype(BF16)
    r1 = psum - hi.astype(F32)
    mid = r1.astype(BF16)
    lo = (r1 - mid.astype(F32)).astype(BF16)
    sc = _dot_nt(ovt, hi) + _dot_nt(ovt, mid) + _dot_nt(ovt, lo)
    j = lax.broadcasted_iota(I32, (NSEL_PAD, LANES), 0)
    cur = qpos_row // SEL_BLOCK
    forced = (j == 0) | (j == cur) | (j == cur - 1)
    avail = j * SEL_BLOCK <= qpos_row
    sc = jnp.where(forced, BIG, jnp.where(avail, sc, -BIG))
    cnt = jnp.zeros((NSEL_PAD, LANES), I32)
    for ii in range(n_sel):
        row = sc[ii:ii + 1, :]
        beats = (row > sc) | ((row == sc) & (ii < j))
        cnt = cnt + beats.astype(I32)
    k_top = min(N_SELECT, n_sel)
    sel_t = jnp.where((cnt < k_top) & (j < n_sel), 1.0, 0.0).astype(F32)
    sel_t = jnp.concatenate([sel_t, jnp.zeros((LANES - NSEL_PAD, LANES), F32)], axis=0)
    return sel_t.T if queries_on_rows else sel_t


def _combine_heads(gs, o_cmp, o_sel, o_win, z):
    tq = z.shape[0]
    low = lax.broadcasted_iota(I32, (tq, LANES), 1) < HEAD_DIM
    placed = []
    for hd in range(NSA_HEADS):
        h = hd // NSA_GROUP
        o = (gs[:, hd:hd + 1] * o_cmp[hd] + gs[:, NSA_HEADS + hd:NSA_HEADS + hd + 1] * o_sel[hd]
             + gs[:, 2 * NSA_HEADS + hd:2 * NSA_HEADS + hd + 1] * o_win[hd])
        if hd % 2 != h:
            o = pltpu.roll(o, HEAD_DIM, 1)
        placed.append(o)
    slabs = [jnp.where(low, placed[2 * k], placed[2 * k + 1]) for k in range(NSA_HEADS // 2)]
    return jnp.concatenate(slabs, axis=1) * _silu(z)


def _compress_prompt_body(rows_ref, wab_ref, pa_ref, pb_ref, w2_ref, o_ref):
    xs = [rows_ref[pl.ds(l, N_CMP_PAD, stride=CMP_STRIDE), :].astype(BF16) for l in range(CMP_STRIDE)]
    x = jnp.concatenate(xs, axis=1)
    o_ref[...] = _compress_tokens(x, wab_ref[...], pa_ref[...], pb_ref[...], w2_ref[...])


def _compress_prompt(p3, wab, pos_a, pos_b, w2bd):
    b, t, _ = p3.shape
    kvblk = C_KV // LANES
    return pl.pallas_call(
        _compress_prompt_body,
        grid=(b, 2),
        in_specs=[pl.BlockSpec((None, t, LANES), lambda i, k: (i, 0, kvblk + k)),
                  pl.BlockSpec((None, CMP_STRIDE * LANES, 2 * LANES), lambda i, k: (k, 0, 0)),
                  pl.BlockSpec((None, SUBLANES, CMP_STRIDE * LANES), lambda i, k: (k, 0, 0)),
                  pl.BlockSpec((None, SUBLANES, CMP_STRIDE * LANES), lambda i, k: (k, 0, 0)),
                  pl.BlockSpec((None, LANES, LANES), lambda i, k: (k, 0, 0))],
        out_specs=pl.BlockSpec((None, None, N_CMP_PAD, LANES), lambda i, k: (i, k, 0, 0)),
        out_shape=jax.ShapeDtypeStruct((b, 2, N_CMP_PAD, LANES), BF16),
        compiler_params=_cparams(("parallel", "parallel")),
    )(p3, wab, pos_a, pos_b, w2bd)


def _nsa_prompt_body(q_ref, z_ref, g_ref, cmp_ref, sk_ref, sv_ref, wk_ref, wv_ref, ovt_ref, et_ref, o_ref,
                     svt_s, wvt_s, acc_s, *, n_sel):
    tq = 128
    i = pl.program_id(1)
    q0 = i * tq
    qh = _pad_heads(q_ref[...] * SCALE)
    gs = _sigmoid(g_ref[...])
    qpos_col = q0 + lax.broadcasted_iota(I32, (tq, 1), 0)
    qpos_row = q0 + lax.broadcasted_iota(I32, (1, LANES), 1)
    o_cmp, p_cmp = _cmp_branch(qh, cmp_ref[0], cmp_ref[1], qpos_col)
    ovt = ovt_ref[...]

    @pl.when(i == 0)
    def _():
        for c in range(svt_s.shape[0]):
            rows = slice(c * LANES, (c + 1) * LANES)
            svt_s[c] = sv_ref[rows, :].T.astype(BF16)
            wvt_s[c] = wv_ref[rows, :].T.astype(BF16)

    w0 = pl.multiple_of(jnp.maximum(q0 - WINDOW, 0), LANES)
    wwid = WINDOW + tq
    kwin = wk_ref[pl.ds(w0, wwid), :].astype(BF16)
    kp_w = w0 + lax.broadcasted_iota(I32, (wwid, LANES), 0)
    dist = qpos_row - kp_w
    maskw = jnp.where((dist >= 0) & (dist < WINDOW), 0.0, NEG)
    kbw = kp_w.astype(F32)
    wc0 = w0 // LANES
    sub = SEL_CHUNK // LANES
    nch = (i + 2 * sub) // (2 * sub)
    hq = NSA_HEADS * tq
    cols = [slice(hd * tq, (hd + 1) * tq) for hd in range(NSA_HEADS)]
    q8 = jnp.concatenate(qh, axis=0)

    sel_t = []
    for h in range(NSA_KV_HEADS):
        psum = p_cmp[h * NSA_GROUP]
        for hd in range(h * NSA_GROUP + 1, (h + 1) * NSA_GROUP):
            psum = psum + p_cmp[hd]
        sel_t.append(_select_blocks(psum, qpos_row, ovt, n_sel, queries_on_rows=False).astype(BF16))

    stw = _dot_nt(kwin, q8)
    pws, linv = [], []
    for hd in range(NSA_HEADS):
        s = stw[:, cols[hd]] + NSA_SLOPES[hd] * kbw + maskw
        p = jnp.exp(s - jnp.max(s, axis=0, keepdims=True))
        linv.append(1.0 / jnp.sum(p, axis=0, keepdims=True))
        pws.append(p.astype(BF16))
    ptw = jnp.concatenate(pws, axis=1)
    owin_t = _dot(wvt_s[wc0], ptw[0:LANES])
    for k in range(1, wwid // LANES):
        owin_t = owin_t + _dot(wvt_s[wc0 + k], ptw[k * LANES:(k + 1) * LANES])
    owin_t = owin_t * jnp.concatenate(linv, axis=1)

    acc_s[...] = jnp.zeros_like(acc_s)

    def scores(c):
        k0 = pl.multiple_of(c * SEL_CHUNK, SEL_CHUNK)
        kch = sk_ref[pl.ds(k0, SEL_CHUNK), :].astype(BF16)
        et = et_ref[pl.ds(k0, SEL_CHUNK), :]
        st = _dot_nt(kch, q8)
        mexp = [_dot(et, sel_t[h]) for h in range(NSA_KV_HEADS)]
        return st, mexp

    def update(c, st, mexp, m, l):
        k0 = c * SEL_CHUNK
        kp = k0 + lax.broadcasted_iota(I32, (SEL_CHUNK, LANES), 0)
        causal = kp <= qpos_row
        kbf = kp.astype(F32)
        ms, ls, alphas, ps = [], [], [], []
        for hd in range(NSA_HEADS):
            if hd % NSA_GROUP == 0:
                maskb = jnp.where((mexp[hd // NSA_GROUP] > 0.5) & causal, 0.0, NEG)
            s = st[:, cols[hd]] + NSA_SLOPES[hd] * kbf + maskb
            m_new = jnp.maximum(m[:, cols[hd]], jnp.max(s, axis=0, keepdims=True))
            alpha = jnp.exp(m[:, cols[hd]] - m_new)
            p = jnp.exp(s - m_new)
            ms.append(m_new)
            ls.append(alpha * l[:, cols[hd]] + jnp.sum(p, axis=0, keepdims=True))
            alphas.append(alpha)
            ps.append(p.astype(BF16))
        pt = jnp.concatenate(ps, axis=1)
        ot = _dot(svt_s[c * sub], pt[0:LANES])
        for k in range(1, sub):
            ot = ot + _dot(svt_s[c * sub + k], pt[k * LANES:(k + 1) * LANES])
        acc_s[...] = acc_s[...] * jnp.concatenate(alphas, axis=1) + ot
        return jnp.concatenate(ms, axis=1), jnp.concatenate(ls, axis=1)

    def body(c2, carry):
        m, l = carry
        a = scores(2 * c2)
        b = scores(2 * c2 + 1)
        m, l = update(2 * c2, a[0], a[1], m, l)
        return update(2 * c2 + 1, b[0], b[1], m, l)

    init = (jnp.full((1, hq), NEG, F32), jnp.zeros((1, hq), F32))
    _, l = lax.fori_loop(0, nch, body, init)
    osel_t = acc_s[...] * jnp.where(l > 0.0, 1.0 / l, 0.0)

    o_sel = [osel_t[:, cols[hd]].T for hd in range(NSA_HEADS)]
    o_win = [owin_t[:, cols[hd]].T for hd in range(NSA_HEADS)]
    o_ref[...] = _combine_heads(gs, o_cmp, o_sel, o_win, z_ref[...]).astype(o_ref.dtype)


def _nsa_prompt(p3, cmp_tok, ovt, e3):
    b, t, _ = p3.shape
    tq = 128
    n_sel = -(-t // SEL_BLOCK)
    kvb = C_KV // LANES
    return pl.pallas_call(
        functools.partial(_nsa_prompt_body, n_sel=n_sel),
        grid=(b, t // tq),
        scratch_shapes=[pltpu.VMEM((t // LANES, LANES, LANES), BF16),
                        pltpu.VMEM((t // LANES, LANES, LANES), BF16),
                        pltpu.VMEM((LANES, NSA_HEADS * tq), F32)],
        in_specs=[pl.BlockSpec((None, tq, NSA_WIDTH), lambda bi, i: (bi, i, C_Q // NSA_WIDTH)),
                  pl.BlockSpec((None, tq, NSA_WIDTH), lambda bi, i: (bi, i, C_ZN // NSA_WIDTH)),
                  pl.BlockSpec((None, tq, LANES), lambda bi, i: (bi, i, C_G // LANES)),
                  pl.BlockSpec((None, 2, N_CMP_PAD, LANES), lambda bi, i: (bi, 0, 0, 0)),
                  pl.BlockSpec((None, t, LANES), lambda bi, i: (bi, 0, kvb + 2)),
                  pl.BlockSpec((None, t, LANES), lambda bi, i: (bi, 0, kvb + 3)),
                  pl.BlockSpec((None, t, LANES), lambda bi, i: (bi, 0, kvb + 4)),
                  pl.BlockSpec((None, t, LANES), lambda bi, i: (bi, 0, kvb + 5)),
                  pl.BlockSpec((NSEL_PAD, LANES), lambda bi, i: (0, 0)),
                  pl.BlockSpec(e3.shape, lambda bi, i: (0, 0))],
        out_specs=pl.BlockSpec((None, tq, NSA_WIDTH), lambda bi, i: (bi, i, 0)),
        out_shape=jax.ShapeDtypeStruct((b, t, NSA_WIDTH), BF16),
        compiler_params=_cparams(("parallel", "arbitrary")),
    )(p3, p3, p3, cmp_tok, p3, p3, p3, p3, ovt, e3)


def _diff_lambda(lam_ref, lam_init):
    a = lam_ref[...]
    s1 = jnp.sum(a[0:1] * a[1:2], axis=-1, keepdims=True)
    s2 = jnp.sum(a[2:3] * a[3:4], axis=-1, keepdims=True)
    return jnp.exp(s1) - jnp.exp(s2) + lam_init


def _diff_finish(o, lnw, z, lam_init):
    ms = jnp.mean(o * o, axis=-1, keepdims=True)
    return o * lax.rsqrt(ms + EPS) * lnw * (1.0 - lam_init) * _silu(z)


def _diff_prompt_body(lam_ref, q_ref, z_ref, k_ref, v_ref, lnw_ref, o_ref, vt_s, acc_s, *, lam_init):
    tq = 128
    i = pl.program_id(1)
    q0 = i * tq

    @pl.when(i == 0)
    def _():
        for hh in range(DIFF_HEADS):
            for c in range(vt_s.shape[1]):
                vt_s[hh, c] = v_ref[c * LANES:(c + 1) * LANES, hh * LANES:(hh + 1) * LANES].T.astype(BF16)

    lam = _diff_lambda(lam_ref, lam_init)
    low = lax.broadcasted_iota(I32, (tq, LANES), 1) < DIFF_HEAD_DIM
    qpos_row = q0 + lax.broadcasted_iota(I32, (1, LANES), 1)
    sub = SEL_CHUNK // LANES
    nch = (i + 2 * sub) // (2 * sub)
    q2 = []
    for hh in range(DIFF_HEADS):
        qs = q_ref[:, hh * LANES:(hh + 1) * LANES] * DIFF_SCALE
        q2.append(jnp.concatenate([jnp.where(low, qs, 0.0), jnp.where(low, 0.0, qs)], axis=0).astype(BF16))
    acc_s[...] = jnp.zeros_like(acc_s)
    ncol = 2 * tq

    def scores(c):
        k0 = pl.multiple_of(c * SEL_CHUNK, SEL_CHUNK)
        return [_dot_nt(k_ref[pl.ds(k0, SEL_CHUNK), hh * LANES:(hh + 1) * LANES].astype(BF16), q2[hh])
                for hh in range(DIFF_HEADS)]

    def update(c, sts, m, l):
        kp = c * SEL_CHUNK + lax.broadcasted_iota(I32, (SEL_CHUNK, LANES), 0)
        maskb = jnp.where(kp <= qpos_row, 0.0, NEG)
        kbf = kp.astype(F32)
        ms, ls = [], []
        for hh in range(DIFF_HEADS):
            st = sts[hh]
            bias = DIFF_SLOPES[hh] * kbf + maskb
            alphas, ps = [], []
            for comp in range(2):
                cs = slice(comp * tq, (comp + 1) * tq)
                gs_ = slice(hh * ncol + comp * tq, hh * ncol + (comp + 1) * tq)
                s = st[:, cs] + bias
                m_new = jnp.maximum(m[:, gs_], jnp.max(s, axis=0, keepdims=True))
                alpha = jnp.exp(m[:, gs_] - m_new)
                p = jnp.exp(s - m_new)
                ms.append(m_new)
                ls.append(alpha * l[:, gs_] + jnp.sum(p, axis=0, keepdims=True))
                alphas.append(alpha)
                ps.append(p.astype(BF16))
            pt = jnp.concatenate(ps, axis=1)
            ot = _dot(vt_s[hh, c * sub], pt[0:LANES])
            for k in range(1, sub):
                ot = ot + _dot(vt_s[hh, c * sub + k], pt[k * LANES:(k + 1) * LANES])
            acc_s[hh] = acc_s[hh] * jnp.concatenate(alphas, axis=1) + ot
        return jnp.concatenate(ms, axis=1), jnp.concatenate(ls, axis=1)

    def body(c2, carry):
        m, l = carry
        a = scores(2 * c2)
        b = scores(2 * c2 + 1)
        m, l = update(2 * c2, a, m, l)
        return update(2 * c2 + 1, b, m, l)

    init = (jnp.full((1, DIFF_HEADS * ncol), NEG, F32), jnp.zeros((1, DIFF_HEADS * ncol), F32))
    _, l = lax.fori_loop(0, nch, body, init)
    lnw = lnw_ref[...]
    slabs = []
    for hh in range(DIFF_HEADS):
        on = acc_s[hh] * (1.0 / l[:, hh * ncol:(hh + 1) * ncol])
        o = (on[:, 0:tq] - lam * on[:, tq:2 * tq]).T
        slabs.append(_diff_finish(o, lnw, z_ref[:, hh * LANES:(hh + 1) * LANES], lam_init))
    o_ref[...] = jnp.concatenate(slabs, axis=1).astype(o_ref.dtype)


def _diff_prompt(p3, lam4, lnw_row, lam_init):
    b, t, _ = p3.shape
    tq = 128
    return pl.pallas_call(
        functools.partial(_diff_prompt_body, lam_init=lam_init),
        grid=(b, t // tq),
        in_specs=[pl.BlockSpec((4, DIFF_HEAD_DIM), lambda bi, i: (0, 0)),
                  pl.BlockSpec((None, tq, DIFF_WIDTH), lambda bi, i: (bi, i, C_DQ // DIFF_WIDTH)),
                  pl.BlockSpec((None, tq, DIFF_WIDTH), lambda bi, i: (bi, i, C_ZD // DIFF_WIDTH)),
                  pl.BlockSpec((None, t, DIFF_WIDTH), lambda bi, i: (bi, 0, C_DK // DIFF_WIDTH)),
                  pl.BlockSpec((None, t, DIFF_WIDTH), lambda bi, i: (bi, 0, C_DV // DIFF_WIDTH)),
                  pl.BlockSpec((1, DIFF_V_DIM), lambda bi, i: (0, 0))],
        out_specs=pl.BlockSpec((None, tq, DIFF_WIDTH), lambda bi, i: (bi, i, 0)),
        out_shape=jax.ShapeDtypeStruct((b, t, DIFF_WIDTH), BF16),
        scratch_shapes=[pltpu.VMEM((DIFF_HEADS, t // LANES, LANES, LANES), BF16),
                        pltpu.VMEM((DIFF_HEADS, DIFF_V_DIM, 2 * tq), F32)],
        compiler_params=_cparams(("parallel", "arbitrary")),
    )(lam4, p3, p3, p3, p3, lnw_row)


SSM_BLOCKS = SSM_WIDTH // LANES
SSM_BS = N_STATE // SSM_BLOCKS
SSM_CHUNK = 256
SSM_SEG = SSM_CHUNK // SUBLANES
SSM_UNROLL = 8


def _discretize(lre, lim, ldt):
    dt = jnp.exp(ldt)
    mag = jnp.exp(lre * dt)
    ar = mag * jnp.cos(lim * dt)
    ai = mag * jnp.sin(lim * dt)
    den = lre * lre + lim * lim
    qr = ((ar - 1.0) * lre + ai * lim) / den
    qi = (ai * lre - (ar - 1.0) * lim) / den
    return ar, ai, qr, qi


def _bbar_block(qr, qi, bre, bim):
    return jnp.concatenate([qr * bre - qi * bim, qr * bim + qi * bre], axis=1).astype(BF16)


def _glu_tail(y, wglu, z):
    gate = _sigmoid(_dot(y.astype(BF16), wglu))
    return y * gate * _silu(z)


def _ssm_prompt_body(u_ref, z_ref, lre_ref, lim_ref, ldt_ref, bre_ref, bim_ref, cmat_ref, d_ref, wglu_ref,
                     perm_ref, o_ref, hr_ref, hi_ref,
                     bbar_s, apr_s, api_s, bu_s, y_s, carr_s, cari_s):
    first = (pl.program_id(0) == 0) & (pl.program_id(1) == 0)

    @pl.when(first)
    def _():
        ar, ai, qr, qi = _discretize(lre_ref[...], lim_ref[...], ldt_ref[...])
        for k in range(SSM_BLOCKS):
            cs = slice(k * SSM_BS, (k + 1) * SSM_BS)
            bbar_s[k] = _bbar_block(qr[:, cs], qi[:, cs], bre_ref[k], bim_ref[k])
        pr, pi = ar, ai
        apr_s[0:1, :] = pr
        api_s[0:1, :] = pi
        for k in range(1, SSM_SEG):
            pr, pi = pr * ar - pi * ai, pr * ai + pi * ar
            apr_s[k:k + 1, :] = pr
            api_s[k:k + 1, :] = pi

    @pl.when(pl.program_id(1) == 0)
    def _():
        carr_s[...] = jnp.zeros_like(carr_s)
        cari_s[...] = jnp.zeros_like(cari_s)

    re_c = slice(0, SSM_BS)
    im_c = slice(SSM_BS, 2 * SSM_BS)
    u_perm = _dot(perm_ref[0], u_ref[...].astype(BF16)).astype(BF16)
    unperm = perm_ref[1]

    for k in range(SSM_BLOCKS):
        cs = slice(k * SSM_BS, (k + 1) * SSM_BS)
        us = slice(k * LANES, (k + 1) * LANES)
        bu_s[...] = _dot(u_perm[:, us], bbar_s[k])
        ar_b = jnp.broadcast_to(apr_s[0:1, cs], (SUBLANES, SSM_BS))
        ai_b = jnp.broadcast_to(api_s[0:1, cs], (SUBLANES, SSM_BS))

        def step(t, st, ar_b=ar_b, ai_b=ai_b):
            hr, hi = st
            rows = pl.ds(pl.multiple_of(t * SUBLANES, SUBLANES), SUBLANES)
            nr = ar_b * hr - ai_b * hi + bu_s[rows, re_c]
            ni = ar_b * hi + ai_b * hr + bu_s[rows, im_c]
            bu_s[rows, re_c] = nr
            bu_s[rows, im_c] = ni
            return nr, ni

        zero = jnp.zeros((SUBLANES, SSM_BS), F32)
        er, ei = lax.fori_loop(0, SSM_SEG, step, (zero, zero), unroll=SSM_UNROLL)
        a32r = apr_s[SSM_SEG - 1:SSM_SEG, cs]
        a32i = api_s[SSM_SEG - 1:SSM_SEG, cs]
        rows_r = [carr_s[:, cs]]
        rows_i = [cari_s[:, cs]]
        for s in range(1, SUBLANES + 1):
            pr, pi = rows_r[-1], rows_i[-1]
            rows_r.append(er[s - 1:s] + a32r * pr - a32i * pi)
            rows_i.append(ei[s - 1:s] + a32r * pi + a32i * pr)
        carr_s[:, cs] = rows_r[SUBLANES]
        cari_s[:, cs] = rows_i[SUBLANES]
        hin_r = jnp.concatenate(rows_r[:SUBLANES], axis=0)
        hin_i = jnp.concatenate(rows_i[:SUBLANES], axis=0)

        def fix(t, carry, hin_r=hin_r, hin_i=hin_i, cs=cs):
            rows = pl.ds(pl.multiple_of(t * SUBLANES, SUBLANES), SUBLANES)
            pr = apr_s[pl.ds(t, 1), cs]
            pi = api_s[pl.ds(t, 1), cs]
            bu_s[rows, re_c] = bu_s[rows, re_c] + pr * hin_r - pi * hin_i
            bu_s[rows, im_c] = bu_s[rows, im_c] + pr * hin_i + pi * hin_r
            return carry

        lax.fori_loop(0, SSM_SEG, fix, 0, unroll=SSM_UNROLL)
        yp = _dot(bu_s[...].astype(BF16), cmat_ref[k])
        y_hi = yp.astype(BF16)
        r1 = yp - y_hi.astype(F32)
        y_mid = r1.astype(BF16)
        y_lo = (r1 - y_mid.astype(F32)).astype(BF16)
        yk = _dot(unperm, y_hi) + _dot(unperm, y_mid) + _dot(unperm, y_lo) + d_ref[:, us] * u_ref[:, us]
        y_s[:, us] = _gelu(yk)

    o_ref[...] = _glu_tail(y_s[...], wglu_ref[...], z_ref[...]).astype(o_ref.dtype)
    hr_ref[...] = carr_s[...]
    hi_ref[...] = cari_s[...]


def _ssm_prompt(p3, sp):
    b, t, _ = p3.shape
    full = lambda shape: pl.BlockSpec(shape, lambda bi, j: (0,) * len(shape))
    out, hr, hi = pl.pallas_call(
        _ssm_prompt_body,
        grid=(b, t // SSM_CHUNK),
        in_specs=[pl.BlockSpec((None, SSM_CHUNK, SSM_WIDTH), lambda bi, j: (bi, j, C_U // SSM_WIDTH)),
                  pl.BlockSpec((None, SSM_CHUNK, SSM_WIDTH), lambda bi, j: (bi, j, C_ZS // SSM_WIDTH)),
                  full((1, N_STATE)), full((1, N_STATE)), full((1, N_STATE)),
                  full((SSM_BLOCKS, LANES, SSM_BS)), full((SSM_BLOCKS, LANES, SSM_BS)),
                  full((SSM_BLOCKS, 2 * SSM_BS, LANES)), full((1, SSM_WIDTH)),
                  full((SSM_WIDTH, SSM_WIDTH)), full((2, SSM_CHUNK, SSM_CHUNK))],
        out_specs=[pl.BlockSpec((None, SSM_CHUNK, SSM_WIDTH), lambda bi, j: (bi, j, 0)),
                   pl.BlockSpec((None, 1, N_STATE), lambda bi, j: (bi, 0, 0)),
                   pl.BlockSpec((None, 1, N_STATE), lambda bi, j: (bi, 0, 0))],
        out_shape=[jax.ShapeDtypeStruct((b, t, SSM_WIDTH), BF16),
                   jax.ShapeDtypeStruct((b, 1, N_STATE), F32),
                   jax.ShapeDtypeStruct((b, 1, N_STATE), F32)],
        scratch_shapes=[pltpu.VMEM((SSM_BLOCKS, LANES, 2 * SSM_BS), BF16),
                        pltpu.VMEM((SSM_SEG, N_STATE), F32), pltpu.VMEM((SSM_SEG, N_STATE), F32),
                        pltpu.VMEM((SSM_CHUNK, 2 * SSM_BS), F32),
                        pltpu.VMEM((SSM_CHUNK, SSM_WIDTH), F32),
                        pltpu.VMEM((1, N_STATE), F32), pltpu.VMEM((1, N_STATE), F32)],
        compiler_params=_cparams(("arbitrary", "arbitrary")),
    )(p3, p3, sp['lre'], sp['lim'], sp['ldt'], sp['bre'], sp['bim'], sp['cmat'], sp['d'], sp['wglu'],
      _segment_permutation())
    return out, hr, hi


def _segment_permutation():
    r = np.arange(SSM_CHUNK)
    src = (r % SUBLANES) * SSM_SEG + r // SUBLANES
    pm = np.zeros((SSM_CHUNK, SSM_CHUNK), np.float32)
    pm[r, src] = 1.0
    return jnp.asarray(np.stack([pm, pm.T]), BF16)


def _ssm_sample_body(u_ref, z_ref, h0r_ref, h0i_ref, lre_ref, lim_ref, ldt_ref, bre_ref, bim_ref, cmat_ref,
                     d_ref, wglu_ref, o_ref, hr_ref, hi_ref, y_s, *, n_t):
    k = pl.program_id(0)
    ar, ai, qr, qi = _discretize(lre_ref[...], lim_ref[...], ldt_ref[...])
    bbar = _bbar_block(qr, qi, bre_ref[...], bim_ref[...])
    hr = h0r_ref[...]
    hi = h0i_ref[...]
    cmat = cmat_ref[...]
    for t in range(n_t):
        ut = u_ref[t]
        bu = _dot(ut.astype(BF16), bbar)
        hr, hi = ar * hr - ai * hi + bu[:, 0:SSM_BS], ar * hi + ai * hr + bu[:, SSM_BS:]
        hcat = jnp.concatenate([hr, hi], axis=1).astype(BF16)
        y_s[t, k] = _gelu(_dot(hcat, cmat) + d_ref[...] * ut)
    hr_ref[...] = hr
    hi_ref[...] = hi

    @pl.when(k == SSM_BLOCKS - 1)
    def _():
        wglu = wglu_ref[...]
        for t in range(n_t):
            y = jnp.concatenate([y_s[t, kk] for kk in range(SSM_BLOCKS)], axis=1)
            o_ref[t] = _glu_tail(y, wglu, z_ref[t])


def _ssm_sample(ut, zt, h0r, h0i, sp):
    n_t, nb, _ = ut.shape
    out, hr, hi = pl.pallas_call(
        functools.partial(_ssm_sample_body, n_t=n_t),
        grid=(SSM_BLOCKS,),
        in_specs=[pl.BlockSpec((n_t, nb, LANES), lambda k: (0, 0, k)),
                  pl.BlockSpec((n_t, nb, SSM_WIDTH), lambda k: (0, 0, 0)),
                  pl.BlockSpec((nb, SSM_BS), lambda k: (0, k)),
                  pl.BlockSpec((nb, SSM_BS), lambda k: (0, k)),
                  pl.BlockSpec((1, SSM_BS), lambda k: (0, k)),
                  pl.BlockSpec((1, SSM_BS), lambda k: (0, k)),
                  pl.BlockSpec((1, SSM_BS), lambda k: (0, k)),
                  pl.BlockSpec((None, LANES, SSM_BS), lambda k: (k, 0, 0)),
                  pl.BlockSpec((None, LANES, SSM_BS), lambda k: (k, 0, 0)),
                  pl.BlockSpec((None, 2 * SSM_BS, LANES), lambda k: (k, 0, 0)),
                  pl.BlockSpec((1, LANES), lambda k: (0, k)),
                  pl.BlockSpec((SSM_WIDTH, SSM_WIDTH), lambda k: (0, 0))],
        out_specs=[pl.BlockSpec((n_t, nb, SSM_WIDTH), lambda k: (0, 0, 0)),
                   pl.BlockSpec((nb, SSM_BS), lambda k: (0, k)),
                   pl.BlockSpec((nb, SSM_BS), lambda k: (0, k))],
        out_shape=[jax.ShapeDtypeStruct((n_t, nb, SSM_WIDTH), F32),
                   jax.ShapeDtypeStruct((nb, N_STATE), F32),
                   jax.ShapeDtypeStruct((nb, N_STATE), F32)],
        scratch_shapes=[pltpu.VMEM((n_t, SSM_BLOCKS, nb, LANES), F32)],
        compiler_params=_cparams(("arbitrary",)),
    )(ut, zt, h0r, h0i, sp['lre'], sp['lim'], sp['ldt'], sp['bre'], sp['bim'], sp['cmat'], sp['d'], sp['wglu'])
    return out, hr, hi


def _page_map(li, pg, b, pt):
    return (li, pt[b, pg], 0, 0)


def _page_map_seq(li, pg, sq, b, pt):
    return (li, pt[b * NSA_DEC_SEQS + sq, pg], 0, 0)


def _pad_rows(x, rows):
    return jnp.concatenate([x, jnp.zeros((rows - x.shape[0], x.shape[1]), x.dtype)], axis=0)


def _nsa_sample_body(pt_ref, *refs, n_pages, n_sel):
    nsq = NSA_DEC_SEQS
    (q_ref, z_ref, g_ref, snew_ref, wnew_ref, wb_ref, wab_ref, pa_ref, pb_ref, w2_ref, ovt_ref, e_ref,
     o_ref, s_s, rows_s) = refs[nsq * n_pages:]
    seqs = range(nsq)
    pages = [refs[sq * n_pages:(sq + 1) * n_pages] for sq in seqs]
    rs = [slice(sq * T_PAD, (sq + 1) * T_PAD) for sq in seqs]
    past = n_pages * PAGE_SIZE
    nk = past + LANES
    wb = wb_ref.shape[2]
    qpos_col = past + lax.broadcasted_iota(I32, (T_PAD, 1), 0)
    qpos64 = jnp.concatenate([qpos_col] * NSA_HEADS, axis=0)
    qpos_row = past + (lax.broadcasted_iota(I32, (1, LANES), 1) % T_PAD)
    slope_col = jnp.concatenate([jnp.full((T_PAD, 1), NSA_SLOPES[hd], F32) for hd in range(NSA_HEADS)], axis=0)
    kpos = lax.broadcasted_iota(I32, (1, nk), 1)
    kbias = slope_col * kpos.astype(F32)
    kpos_w = (past - wb) + lax.broadcasted_iota(I32, (1, wb + LANES), 1)
    wbias = slope_col * kpos_w.astype(F32)
    dist = qpos64 - kpos_w
    wmask = (dist >= 0) & (dist < WINDOW) & (kpos_w >= 0)
    causal = kpos <= qpos64
    sk_r = pl.ds(2 * LANES, LANES)
    sv_r = pl.ds(3 * LANES, LANES)

    qh = [_pad_heads(q_ref[rs[sq], :] * SCALE) for sq in seqs]
    q64 = [jnp.concatenate(qh[sq], axis=0) for sq in seqs]

    for kv in range(2):
        for sq in seqs:
            for pg in range(n_pages):
                rows_s[sq, kv, pg * PAGE_SIZE:(pg + 1) * PAGE_SIZE, :] = (
                    pages[sq][pg][kv * LANES:(kv + 1) * LANES, :].T)

    owin = []
    for sq in seqs:
        wnew = wnew_ref[rs[sq], :]
        kwn = _pad_rows(wnew[:, 0:LANES], LANES).astype(BF16)
        vwn = _pad_rows(wnew[:, LANES:2 * LANES], LANES).astype(BF16)
        sw = jnp.concatenate([_dot(q64[sq], wb_ref[sq, 0:LANES, :].astype(BF16)), _dot_nt(q64[sq], kwn)], axis=1)
        pw = _msoftmax(sw + wbias, wmask).astype(BF16)
        owin.append(_dot_nt(pw[:, 0:wb], wb_ref[sq, LANES:2 * LANES, :].astype(BF16)) + _dot(pw[:, wb:], vwn))

    for sq in seqs:
        for pg in range(n_pages):
            s_s[sq, :, pg * PAGE_SIZE:(pg + 1) * PAGE_SIZE] = _dot(q64[sq], pages[sq][pg][sk_r, :].astype(BF16))
        knew = _pad_rows(snew_ref[rs[sq], 0:LANES], LANES).astype(BF16)
        s_s[sq, :, past:nk] = _dot_nt(q64[sq], knew)

    toks = [[], []]
    for kv in range(2):
        for sq in seqs:
            xs = [rows_s[sq, kv, pl.ds(l, N_CMP_PAD, stride=CMP_STRIDE), :].astype(BF16)
                  for l in range(CMP_STRIDE)]
            toks[kv].append(_compress_tokens(jnp.concatenate(xs, axis=1), wab_ref[kv], pa_ref[kv], pb_ref[kv],
                                             w2_ref[kv]))
    cmp = [_cmp_branch(qh[sq], toks[0][sq], toks[1][sq], qpos_col) for sq in seqs]

    memb = []
    for sq in seqs:
        p_cmp = cmp[sq][1]
        psum = jnp.concatenate(
            [p_cmp[0] + p_cmp[1] + p_cmp[2] + p_cmp[3], p_cmp[4] + p_cmp[5] + p_cmp[6] + p_cmp[7],
             jnp.zeros((LANES - 2 * T_PAD, N_CMP_PAD), F32)], axis=0)
        sel = _select_blocks(psum, qpos_row, ovt_ref[...], n_sel)
        mexp = _dot(sel[0:2 * T_PAD].astype(BF16), e_ref[...])
        memb.append(
            jnp.concatenate([mexp[0:T_PAD]] * NSA_GROUP + [mexp[T_PAD:2 * T_PAD]] * NSA_GROUP, axis=0) > 0.5)

    for sq in seqs:
        p = _msoftmax(s_s[sq] + kbias, memb[sq] & causal).astype(BF16)
        vnew = _pad_rows(snew_ref[rs[sq], LANES:2 * LANES], LANES).astype(BF16)
        osel = _dot(p[:, past:nk], vnew)
        for pg in range(n_pages):
            osel = osel + _dot_nt(p[:, pg * PAGE_SIZE:(pg + 1) * PAGE_SIZE], pages[sq][pg][sv_r, :].astype(BF16))
        o_sel = [osel[hd * T_PAD:(hd + 1) * T_PAD] for hd in range(NSA_HEADS)]
        o_win = [owin[sq][hd * T_PAD:(hd + 1) * T_PAD] for hd in range(NSA_HEADS)]
        o_ref[rs[sq], :] = _combine_heads(_sigmoid(g_ref[rs[sq], :]), cmp[sq][0], o_sel, o_win, z_ref[rs[sq], :])


def _nsa_sample(p2, cache_t, li, win_t, page_table, wab, pos_a, pos_b, w2bd, ovt, e2, n_t):
    n = p2.shape[0]
    nb, n_pages = page_table.shape
    past = n_pages * PAGE_SIZE
    n_sel = -(-(past + n_t) // SEL_BLOCK)
    nk = past + LANES
    wb = win_t.shape[3]
    kvb = C_KV // (2 * LANES)
    nsq = NSA_DEC_SEQS
    assert nb % nsq == 0
    page_specs = [pl.BlockSpec((None, None, 4 * LANES, PAGE_SIZE), functools.partial(_page_map_seq, li, pg, sq))
                  for sq in range(nsq) for pg in range(n_pages)]
    const = lambda shape: pl.BlockSpec(shape, lambda b, pt: (0,) * len(shape))
    rows = nsq * T_PAD
    in_specs = page_specs + [
        pl.BlockSpec((rows, NSA_WIDTH), lambda b, pt: (b, C_Q // NSA_WIDTH)),
        pl.BlockSpec((rows, NSA_WIDTH), lambda b, pt: (b, C_ZN // NSA_WIDTH)),
        pl.BlockSpec((rows, LANES), lambda b, pt: (b, C_G // LANES)),
        pl.BlockSpec((rows, 2 * LANES), lambda b, pt: (b, kvb + 1)),
        pl.BlockSpec((rows, 2 * LANES), lambda b, pt: (b, kvb + 2)),
        pl.BlockSpec((None, nsq, 2 * LANES, wb), lambda b, pt: (li, b, 0, 0)),
        const(wab.shape), const(pos_a.shape), const(pos_b.shape), const(w2bd.shape),
        const(ovt.shape), const(e2.shape)]
    return pl.pallas_call(
        functools.partial(_nsa_sample_body, n_pages=n_pages, n_sel=n_sel),
        grid_spec=pltpu.PrefetchScalarGridSpec(
            num_scalar_prefetch=1, grid=(nb // nsq,), in_specs=in_specs,
            out_specs=pl.BlockSpec((rows, NSA_WIDTH), lambda b, pt: (b, 0)),
            scratch_shapes=[pltpu.VMEM((nsq, NSA_HEADS * T_PAD, nk), F32),
                            pltpu.VMEM((nsq, 2, past, LANES), F32)]),
        out_shape=jax.ShapeDtypeStruct((n, NSA_WIDTH), F32),
        compiler_params=_cparams(("parallel",)),
    )(page_table, *([cache_t] * (nsq * n_pages)), p2, p2, p2, p2, p2, win_t, wab, pos_a, pos_b, w2bd, ovt, e2)


def _diff_sample_body(pt_ref, *refs, n_pages, lam_init):
    pages = refs[:n_pages]
    lam_ref, q_ref, z_ref, knew_ref, vnew_ref, lnw_ref, o_ref, s_s = refs[n_pages:]
    past = n_pages * PAGE_SIZE
    nk = past + LANES
    lam = _diff_lambda(lam_ref, lam_init)
    low = lax.broadcasted_iota(I32, (T_PAD, LANES), 1) < DIFF_HEAD_DIM
    rows_h = 2 * T_PAD
    q16 = []
    for hh in range(DIFF_HEADS):
        qs = q_ref[:, hh * LANES:(hh + 1) * LANES] * DIFF_SCALE
        q16.append(jnp.concatenate([jnp.where(low, qs, 0.0), jnp.where(low, 0.0, qs)], axis=0).astype(BF16))
    for pg in range(n_pages):
        for hh in range(DIFF_HEADS):
            kp = pages[pg][pl.ds(hh, PAGE_SIZE, stride=DIFF_SLABS), :].astype(BF16)
            s_s[hh * rows_h:(hh + 1) * rows_h, pg * PAGE_SIZE:(pg + 1) * PAGE_SIZE] = _dot_nt(q16[hh], kp)
    for hh in range(DIFF_HEADS):
        kn = _pad_rows(knew_ref[:, hh * LANES:(hh + 1) * LANES], LANES).astype(BF16)
        s_s[hh * rows_h:(hh + 1) * rows_h, past:nk] = _dot_nt(q16[hh], kn)
    kpos = lax.broadcasted_iota(I32, (1, nk), 1)
    slope_col = jnp.concatenate([jnp.full((rows_h, 1), DIFF_SLOPES[hh], F32) for hh in range(DIFF_HEADS)], axis=0)
    qpos = past + (lax.broadcasted_iota(I32, (DIFF_HEADS * rows_h, 1), 0) % T_PAD)
    p = _msoftmax(s_s[...] + slope_col * kpos.astype(F32), kpos <= qpos)
    lnw = lnw_ref[...]
    slabs = []
    for hh in range(DIFF_HEADS):
        a = (p[hh * rows_h:hh * rows_h + T_PAD] - lam * p[hh * rows_h + T_PAD:(hh + 1) * rows_h]).astype(BF16)
        vn = _pad_rows(vnew_ref[:, hh * LANES:(hh + 1) * LANES], LANES).astype(BF16)
        o = _dot(a[:, past:nk], vn)
        for pg in range(n_pages):
            vp = pages[pg][pl.ds(DIFF_HEADS + hh, PAGE_SIZE, stride=DIFF_SLABS), :].astype(BF16)
            o = o + _dot(a[:, pg * PAGE_SIZE:(pg + 1) * PAGE_SIZE], vp)
        slabs.append(_diff_finish(o, lnw, z_ref[:, hh * LANES:(hh + 1) * LANES], lam_init))
    o_ref[...] = jnp.concatenate(slabs, axis=1)


def _diff_sample(p2, cache_r, li, page_table, lam4, lnw_row, lam_init):
    n = p2.shape[0]
    nb, n_pages = page_table.shape
    nk = n_pages * PAGE_SIZE + LANES
    page_specs = [pl.BlockSpec((None, None, PAGE_SIZE * DIFF_SLABS, LANES), functools.partial(_page_map, li, pg))
                  for pg in range(n_pages)]
    in_specs = page_specs + [
        pl.BlockSpec((4, DIFF_HEAD_DIM), lambda b, pt: (0, 0)),
        pl.BlockSpec((T_PAD, DIFF_WIDTH), lambda b, pt: (b, C_DQ // DIFF_WIDTH)),
        pl.BlockSpec((T_PAD, DIFF_WIDTH), lambda b, pt: (b, C_ZD // DIFF_WIDTH)),
        pl.BlockSpec((T_PAD, DIFF_WIDTH), lambda b, pt: (b, C_DK // DIFF_WIDTH)),
        pl.BlockSpec((T_PAD, DIFF_WIDTH), lambda b, pt: (b, C_DV // DIFF_WIDTH)),
        pl.BlockSpec((1, DIFF_V_DIM), lambda b, pt: (0, 0))]
    return pl.pallas_call(
        functools.partial(_diff_sample_body, n_pages=n_pages, lam_init=lam_init),
        grid_spec=pltpu.PrefetchScalarGridSpec(
            num_scalar_prefetch=1, grid=(nb,), in_specs=in_specs,
            out_specs=pl.BlockSpec((T_PAD, DIFF_WIDTH), lambda b, pt: (b, 0)),
            scratch_shapes=[pltpu.VMEM((DIFF_HEADS * 2 * T_PAD, nk), F32)]),
        out_shape=jax.ShapeDtypeStruct((n, DIFF_WIDTH), F32),
        compiler_params=_cparams(("parallel",)),
    )(page_table, *([cache_r] * n_pages), lam4, p2, p2, p2, p2, lnw_row)


def _overlap_t(n_sel):
    n = np.arange(N_CMP_PAD)[None, :]
    j = np.arange(NSEL_PAD)[:, None]
    cs = n * CMP_STRIDE
    ss = j * SEL_BLOCK
    ov = np.clip(np.minimum(cs + CMP_BLOCK, ss + SEL_BLOCK) - np.maximum(cs, ss), 0, None) / CMP_BLOCK
    ov = np.where((n < N_CMP_PAD - 1) & (j < n_sel), ov, 0.0)
    return jnp.asarray(ov, BF16)


def _block_expander(n_keys):
    j = np.arange(LANES)[:, None]
    k = np.arange(n_keys)[None, :]
    return jnp.asarray((k // SEL_BLOCK) == j, BF16)


def _layer_weights(li, norm_w, w_in, w_out, w_cmp1, pos_cmp, w_cmp2, lre, lim, ldt, bre, bim, cre, cim, sd,
                   w_glu, lam_q1, lam_k1, lam_q2, lam_k2, diff_ln_w):
    w = {}
    w['nw'] = norm_w[li][None, :]
    w['w_out'] = w_out[li].astype(BF16)
    eye2 = jnp.eye(NSA_KV_HEADS, dtype=F32)
    w1 = w_cmp1[li].reshape(2, 2, CMP_STRIDE, HEAD_DIM, HEAD_DIM)
    w['wab'] = jnp.einsum('kaldf,hg->klhdagf', w1, eye2).reshape(
        2, CMP_STRIDE * LANES, 2 * LANES).astype(BF16)
    pos = pos_cmp[li].reshape(2, 2, CMP_STRIDE, 1, HEAD_DIM)
    pos = jnp.broadcast_to(pos, (2, 2, CMP_STRIDE, NSA_KV_HEADS, HEAD_DIM)).reshape(2, 2, 1, CMP_STRIDE * LANES)
    pos = jnp.broadcast_to(pos, (2, 2, SUBLANES, CMP_STRIDE * LANES)).astype(BF16)
    w['pos_a'] = pos[:, 0]
    w['pos_b'] = pos[:, 1]
    w['w2bd'] = jnp.einsum('ked,hg->khegd', w_cmp2[li], eye2).reshape(2, LANES, LANES).astype(BF16)
    eye8 = jnp.eye(SSM_BLOCKS, dtype=F32)
    gl = SSM_BLOCKS

    def compact_b(bm):
        bm = bm.reshape(SSM_BLOCKS, gl, SSM_STATE, SSM_GROUP_CH)
        return jnp.einsum('kgpc,gh->kgchp', bm, eye8).reshape(SSM_BLOCKS, LANES, SSM_BS)

    def compact_c(cm):
        cm = cm.reshape(SSM_BLOCKS, gl, SSM_GROUP_CH, SSM_STATE)
        return jnp.einsum('kgcp,gh->khpgc', cm, eye8).reshape(SSM_BLOCKS, SSM_BS, LANES)

    sp = {'lre': lre[li].reshape(1, N_STATE), 'lim': lim[li].reshape(1, N_STATE),
          'ldt': jnp.repeat(ldt[li], SSM_STATE).reshape(1, N_STATE),
          'bre': compact_b(bre[li]), 'bim': compact_b(bim[li]),
          'cmat': jnp.concatenate([compact_c(cre[li]), -compact_c(cim[li])], axis=1).astype(BF16),
          'd': sd[li].reshape(1, SSM_WIDTH), 'wglu': w_glu[li].astype(BF16)}
    w['ssm'] = sp
    w['lam4'] = jnp.stack([lam_q1[li], lam_k1[li], lam_q2[li], lam_k2[li]])
    w['lnw'] = diff_ln_w[li][None, :]
    w['lam_init'] = 0.8 - 0.6 * math.exp(-0.3 * li)
    return w


def kernel(x_prompt, x_sample, cache_nsa_kv, cache_diff_kv, state_nsa_win, state_ssm_re, state_ssm_im, page_table, norm_w, w_in, w_out, w_cmp1, pos_cmp, w_cmp2, ssm_lambda_re, ssm_lambda_im, ssm_log_dt, ssm_b_re, ssm_b_im, ssm_c_re, ssm_c_im, ssm_d, w_glu, lam_q1, lam_k1, lam_q2, lam_k2, diff_ln_w, final_norm_w):
    depth = norm_w.shape[0]
    b, t, _ = x_prompt.shape
    nb, n_t, _ = x_sample.shape
    n_pool = cache_nsa_kv.shape[1]
    n_pages = page_table.shape[1]
    past = n_pages * PAGE_SIZE
    wb = state_nsa_win.shape[2]
    assert t % SSM_CHUNK == 0 and t >= WINDOW + 128 and n_t <= T_PAD

    ovt_p = _overlap_t(-(-t // SEL_BLOCK))
    ovt_s = _overlap_t(-(-(past + n_t) // SEL_BLOCK))
    e_p = _block_expander(t).T
    e_s = _block_expander(past + LANES)
    fw = final_norm_w[None, :]
    nsa_cache_t = cache_nsa_kv.transpose(0, 1, 3, 4, 5, 2).reshape(depth, n_pool, 4 * LANES, PAGE_SIZE)
    win_t = state_nsa_win.transpose(0, 1, 3, 4, 5, 2).reshape(depth, nb, 2 * LANES, wb)
    diff_cache_r = cache_diff_kv.reshape(depth, n_pool, PAGE_SIZE * DIFF_SLABS, LANES)
    w_rows = w_in.transpose(0, 2, 1)
    w_in_t = jnp.concatenate(
        [w_rows[:, ORIG_OFFS[s]:ORIG_OFFS[s + 1]] for s in NEW_ORDER]
        + [jnp.zeros((depth, DP - ORIG_OFFS[-1], D_MODEL), w_in.dtype)], axis=1).astype(BF16)

    xp = x_prompt.reshape(b * t, D_MODEL)
    xs = jnp.pad(x_sample, ((0, 0), (0, T_PAD - n_t), (0, 0))).reshape(nb * T_PAD, D_MODEL)
    outs = {k: [] for k in ('p_kv', 'p_dkv', 'p_win', 'p_re', 'p_im', 's_kv', 's_dkv', 's_win', 's_re', 's_im')}
    for li in range(depth):
        w = _layer_weights(li, norm_w, w_in, w_out, w_cmp1, pos_cmp, w_cmp2, ssm_lambda_re, ssm_lambda_im,
                           ssm_log_dt, ssm_b_re, ssm_b_im, ssm_c_re, ssm_c_im, ssm_d, w_glu,
                           lam_q1, lam_k1, lam_q2, lam_k2, diff_ln_w)
        final = li == depth - 1
        pp = _norm_project(xp, w['nw'], w_in_t, li)
        p3 = pp.reshape(b, t, DP)
        cmp_tok = _compress_prompt(p3, w['wab'], w['pos_a'], w['pos_b'], w['w2bd'])
        nsa_o = _nsa_prompt(p3, cmp_tok, ovt_p, e_p)
        ssm_o, hr, hi = _ssm_prompt(p3, w['ssm'])
        diff_o = _diff_prompt(p3, w['lam4'], w['lnw'], w['lam_init'])
        xp = _merge_out(nsa_o.reshape(b * t, -1), ssm_o.reshape(b * t, -1), diff_o.reshape(b * t, -1), xp,
                        w['w_out'], fw, final)
        kv = p3[:, :, C_KV:C_KV + 6 * LANES]
        outs['p_kv'].append(kv[:, :, :4 * LANES].reshape(b, t, 4, NSA_KV_HEADS, HEAD_DIM))
        outs['p_dkv'].append(p3[:, :, C_DK:C_DK + 2 * DIFF_WIDTH].reshape(b, t, 2, DIFF_HEADS, DIFF_V_DIM))
        outs['p_win'].append(kv[:, t - min(WINDOW, t):, 4 * LANES:].reshape(b, -1, 2, NSA_KV_HEADS, HEAD_DIM))
        outs['p_re'].append(hr.reshape(b, SSM_GROUPS, SSM_STATE))
        outs['p_im'].append(hi.reshape(b, SSM_GROUPS, SSM_STATE))
        ps = _norm_project(xs, w['nw'], w_in_t, li)
        nsa_s = _nsa_sample(ps, nsa_cache_t, li, win_t, page_table,
                            w['wab'], w['pos_a'], w['pos_b'], w['w2bd'], ovt_s, e_s, n_t)
        ps_t = ps.reshape(nb, T_PAD, DP)[:, :n_t].transpose(1, 0, 2)
        ssm_t, shr, shi = _ssm_sample(ps_t[:, :, C_U:C_U + SSM_WIDTH], ps_t[:, :, C_ZS:C_ZS + SSM_WIDTH],
                                      state_ssm_re[li].reshape(nb, N_STATE),
                                      state_ssm_im[li].reshape(nb, N_STATE), w['ssm'])
        ssm_s = jnp.pad(ssm_t.transpose(1, 0, 2), ((0, 0), (0, T_PAD - n_t), (0, 0))).reshape(nb * T_PAD, SSM_WIDTH)
        diff_s = _diff_sample(ps, diff_cache_r, li, page_table, w['lam4'], w['lnw'], w['lam_init'])
        xs = _merge_out(nsa_s, ssm_s, diff_s, xs, w['w_out'], fw, final)
        ps3 = ps.reshape(nb, T_PAD, DP)[:, :n_t]
        skv = ps3[:, :, C_KV:C_KV + 6 * LANES]
        outs['s_kv'].append(skv[:, :, :4 * LANES].reshape(nb, n_t, 4, NSA_KV_HEADS, HEAD_DIM))
        outs['s_dkv'].append(ps3[:, :, C_DK:C_DK + 2 * DIFF_WIDTH].reshape(nb, n_t, 2, DIFF_HEADS, DIFF_V_DIM))
        outs['s_win'].append(skv[:, :, 4 * LANES:].transpose(0, 2, 1))
        outs['s_re'].append(shr.reshape(nb, SSM_GROUPS, SSM_STATE))
        outs['s_im'].append(shi.reshape(nb, SSM_GROUPS, SSM_STATE))
    y_prompt = xp.reshape(b, t, D_MODEL)
    y_sample = xs.reshape(nb, T_PAD, D_MODEL)[:, :n_t]
    st = lambda k: jnp.stack(outs[k])
    s_win_t = jnp.concatenate([win_t[:, :, :, n_t:], st('s_win')], axis=3)
    s_win = s_win_t.reshape(depth, nb, 2, NSA_KV_HEADS, HEAD_DIM, wb).transpose(0, 1, 5, 2, 3, 4)
    return (y_prompt, y_sample, st('p_kv'), st('p_dkv'), st('p_win'), st('p_re'), st('p_im'),
            st('s_kv'), st('s_dkv'), s_win, st('s_re'), st('s_im'))
```

```python
import functools
import math

import numpy as np
import jax
import jax.numpy as jnp
from jax import lax
from jax.experimental import pallas as pl
from jax.experimental.pallas import tpu as pltpu

F32 = jnp.float32
BF16 = jnp.bfloat16
I32 = jnp.int32

D_MODEL = 2048
HEAD_DIM = 64
NSA_WIDTH = 512
NSA_HEADS = 8
NSA_KV_HEADS = 2
NSA_GROUP = 4
CMP_BLOCK = 32
CMP_STRIDE = 16
SEL_BLOCK = 64
N_SELECT = 16
WINDOW = 512
SSM_WIDTH = 1024
SSM_GROUP_CH = 16
SSM_GROUPS = 64
SSM_STATE = 64
N_STATE = SSM_GROUPS * SSM_STATE
DIFF_WIDTH = 512
DIFF_HEADS = 4
DIFF_HEAD_DIM = 64
DIFF_V_DIM = 128
DIFF_SLABS = 2 * DIFF_HEADS
PAGE_SIZE = 128
SCALE = HEAD_DIM ** -0.5
DIFF_SCALE = DIFF_HEAD_DIM ** -0.5
NEG = -1e30
BIG = 1e9
EPS = 1e-6
NSA_SLOPES = tuple(float(2.0 ** (-8.0 * (k + 1) / NSA_HEADS)) for k in range(NSA_HEADS))
DIFF_SLOPES = tuple(float(2.0 ** (-8.0 * (k + 1) / DIFF_HEADS)) for k in range(DIFF_HEADS))

LANES = 128
SUBLANES = 8
VMEM_LIMIT = 56 * 1024 * 1024

ORIG_SIZES = (NSA_WIDTH, 6 * NSA_KV_HEADS * HEAD_DIM, 3 * NSA_HEADS, NSA_WIDTH, SSM_WIDTH, SSM_WIDTH,
              DIFF_HEADS * 2 * DIFF_HEAD_DIM, DIFF_WIDTH, DIFF_WIDTH, DIFF_WIDTH)
ORIG_OFFS = tuple(int(v) for v in np.concatenate([[0], np.cumsum(ORIG_SIZES)]))
NEW_ORDER = (0, 3, 6, 7, 8, 9, 4, 5, 1, 2)
C_Q, C_ZN, C_DQ, C_DK, C_DV, C_ZD, C_U, C_ZS, C_KV, C_G = 0, 512, 1024, 1536, 2048, 2560, 3072, 4096, 5120, 5888
DP = 6144
N_CMP_PAD = 128
NSEL_PAD = 40
SEL_CHUNK = 256
T_PAD = 8
NSA_DEC_SEQS = 2


def _cparams(sem):
    return pltpu.CompilerParams(dimension_semantics=sem, vmem_limit_bytes=VMEM_LIMIT)


def _dot(a, b):
    return jnp.dot(a, b, preferred_element_type=F32)


def _dot_nt(a, b):
    return lax.dot_general(a, b, (((1,), (1,)), ((), ())), preferred_element_type=F32)


def _gelu(x):
    return 0.5 * x * (1.0 + jnp.tanh(math.sqrt(2.0 / math.pi) * (x + 0.044715 * (x * x * x))))


def _sigmoid(x):
    return 1.0 / (1.0 + jnp.exp(-x))


def _silu(x):
    return x * _sigmoid(x)


def _msoftmax(s, mask):
    s = jnp.where(mask, s, NEG)
    m = jnp.max(s, axis=-1, keepdims=True)
    e = jnp.exp(s - m)
    den = jnp.sum(e, axis=-1, keepdims=True)
    return jnp.where(mask, e * (1.0 / den), 0.0)


def _proj_body(x_ref, nw_ref, w_ref, o_ref, xn_ref):
    @pl.when(pl.program_id(1) == 0)
    def _():
        x = x_ref[...]
        ms = jnp.mean(x * x, axis=-1, keepdims=True)
        xn_ref[...] = (x * lax.rsqrt(ms + EPS) * nw_ref[...]).astype(BF16)

    o_ref[...] = _dot_nt(xn_ref[...], w_ref[...])


def _norm_project(x2d, nw_row, w_t, li):
    n = x2d.shape[0]
    tm = min(n, 1024)
    tn = 512
    return pl.pallas_call(
        _proj_body,
        grid=(n // tm, DP // tn),
        in_specs=[pl.BlockSpec((tm, D_MODEL), lambda i, j: (i, 0)),
                  pl.BlockSpec((1, D_MODEL), lambda i, j: (0, 0)),
                  pl.BlockSpec((None, tn, D_MODEL), lambda i, j: (li, j, 0))],
        out_specs=pl.BlockSpec((tm, tn), lambda i, j: (i, j)),
        out_shape=jax.ShapeDtypeStruct((n, DP), F32),
        scratch_shapes=[pltpu.VMEM((tm, D_MODEL), BF16)],
        compiler_params=_cparams(("parallel", "arbitrary")),
    )(x2d, nw_row, w_t)


def _out_body(nsa_ref, ssm_ref, diff_ref, x_ref, w_ref, fw_ref, y_ref, *, final):
    acc = x_ref[...]
    acc = acc + _dot(nsa_ref[...].astype(BF16), w_ref[0:NSA_WIDTH, :])
    acc = acc + _dot(ssm_ref[...].astype(BF16), w_ref[NSA_WIDTH:NSA_WIDTH + SSM_WIDTH, :])
    acc = acc + _dot(diff_ref[...].astype(BF16), w_ref[NSA_WIDTH + SSM_WIDTH:, :])
    if final:
        ms = jnp.mean(acc * acc, axis=-1, keepdims=True)
        acc = acc * lax.rsqrt(ms + EPS) * fw_ref[...]
    y_ref[...] = acc


def _merge_out(nsa_o, ssm_o, diff_o, x2d, w_out_bf, fw_row, final):
    n = x2d.shape[0]
    tm = min(n, 512)
    return pl.pallas_call(
        functools.partial(_out_body, final=final),
        grid=(n // tm,),
        in_specs=[pl.BlockSpec((tm, NSA_WIDTH), lambda i: (i, 0)),
                  pl.BlockSpec((tm, SSM_WIDTH), lambda i: (i, 0)),
                  pl.BlockSpec((tm, DIFF_WIDTH), lambda i: (i, 0)),
                  pl.BlockSpec((tm, D_MODEL), lambda i: (i, 0)),
                  pl.BlockSpec((D_MODEL, D_MODEL), lambda i: (0, 0)),
                  pl.BlockSpec((1, D_MODEL), lambda i: (0, 0))],
        out_specs=pl.BlockSpec((tm, D_MODEL), lambda i: (i, 0)),
        out_shape=jax.ShapeDtypeStruct((n, D_MODEL), F32),
        compiler_params=_cparams(("parallel",)),
    )(nsa_o, ssm_o, diff_o, x2d, w_out_bf, fw_row)


def _pad_heads(qb):
    tq = qb.shape[0]
    low = lax.broadcasted_iota(I32, (tq, LANES), 1) < HEAD_DIM
    outs = []
    for hd in range(NSA_HEADS):
        h = hd // NSA_GROUP
        slab = qb[:, (hd // 2) * LANES:(hd // 2 + 1) * LANES]
        if hd % 2 != h:
            slab = pltpu.roll(slab, HEAD_DIM, 1)
        keep = low if h == 0 else jnp.logical_not(low)
        outs.append(jnp.where(keep, slab, 0.0).astype(BF16))
    return outs


def _compress_tokens(xs, wab, pc, w2bd):
    ab = _dot(jnp.concatenate(xs, axis=0), wab)
    pre = []
    for s in range(len(xs)):
        rows = slice(s * N_CMP_PAD, (s + 1) * N_CMP_PAD)
        pre.append(ab[rows, 0:LANES] + pltpu.roll(ab[rows, LANES:2 * LANES], N_CMP_PAD - 1, 0) + pc[0:1])
    tok = _dot(_gelu(jnp.concatenate(pre, axis=0)).astype(BF16), w2bd).astype(BF16)
    return [tok[s * N_CMP_PAD:(s + 1) * N_CMP_PAD] for s in range(len(xs))]


def _pos_const_body(pa_ref, pb_ref, wab_ref, o_ref):
    wab = wab_ref[...]
    o_ref[...] = _dot(pa_ref[...], wab)[:, 0:LANES] + _dot(pb_ref[...], wab)[:, LANES:2 * LANES]


def _compress_pos_const(pos_a, pos_b, wab):
    return pl.pallas_call(
        _pos_const_body,
        grid=(2,),
        in_specs=[pl.BlockSpec((None, SUBLANES, CMP_STRIDE * LANES), lambda k: (k, 0, 0)),
                  pl.BlockSpec((None, SUBLANES, CMP_STRIDE * LANES), lambda k: (k, 0, 0)),
                  pl.BlockSpec((None, CMP_STRIDE * LANES, 2 * LANES), lambda k: (k, 0, 0))],
        out_specs=pl.BlockSpec((None, SUBLANES, LANES), lambda k: (k, 0, 0)),
        out_shape=jax.ShapeDtypeStruct((2, SUBLANES, LANES), F32),
        compiler_params=_cparams(("parallel",)),
    )(pos_a, pos_b, wab)


def _cmp_branch(qh, kc, vc, qpos_col):
    tq = qpos_col.shape[0]
    n_iota = lax.broadcasted_iota(I32, (1, N_CMP_PAD), 1)
    ends = n_iota * CMP_STRIDE + (CMP_BLOCK - 1)
    q_all = jnp.concatenate(qh, axis=0)
    slope_col = jnp.concatenate([jnp.full((tq, 1), NSA_SLOPES[hd], F32) for hd in range(NSA_HEADS)], axis=0)
    qpos_all = jnp.concatenate([qpos_col] * NSA_HEADS, axis=0)
    s = _dot_nt(q_all, kc) + slope_col * ends.astype(F32)
    p = _msoftmax(s, ends <= qpos_all)
    o = _dot(p.astype(BF16), vc)
    rows = [slice(hd * tq, (hd + 1) * tq) for hd in range(NSA_HEADS)]
    return [o[r] for r in rows], [p[r] for r in rows]


def _select_blocks(psum, qpos_row, ovt, n_sel, queries_on_rows=True):
    hi = psum.astype(BF16)
    r1 = psum - hi.astype(F32)
    mid = r1.astype(BF16)
    lo = (r1 - mid.astype(F32)).astype(BF16)
    sc = _dot_nt(ovt, hi) + _dot_nt(ovt, mid) + _dot_nt(ovt, lo)
    j = lax.broadcasted_iota(I32, (NSEL_PAD, LANES), 0)
    cur = qpos_row // SEL_BLOCK
    forced = (j == 0) | (j == cur) | (j == cur - 1)
    avail = j * SEL_BLOCK <= qpos_row
    sc = jnp.where(forced, BIG, jnp.where(avail, sc, -BIG))
    cnt = jnp.zeros((NSEL_PAD, LANES), I32)
    for ii in range(n_sel):
        row = sc[ii:ii + 1, :]
        beats = (row > sc) | ((row == sc) & (ii < j))
        cnt = cnt + beats.astype(I32)
    k_top = min(N_SELECT, n_sel)
    sel_t = jnp.where((cnt < k_top) & (j < n_sel), 1.0, 0.0).astype(F32)
    sel_t = jnp.concatenate([sel_t, jnp.zeros((LANES - NSEL_PAD, LANES), F32)], axis=0)
    return sel_t.T if queries_on_rows else sel_t


def _combine_heads(gs, o_cmp, o_sel, o_win, z):
    tq = z.shape[0]
    low = lax.broadcasted_iota(I32, (tq, LANES), 1) < HEAD_DIM
    placed = []
    for hd in range(NSA_HEADS):
        h = hd // NSA_GROUP
        o = (gs[:, hd:hd + 1] * o_cmp[hd] + gs[:, NSA_HEADS + hd:NSA_HEADS + hd + 1] * o_sel[hd]
             + gs[:, 2 * NSA_HEADS + hd:2 * NSA_HEADS + hd + 1] * o_win[hd])
        if hd % 2 != h:
            o = pltpu.roll(o, HEAD_DIM, 1)
        placed.append(o)
    slabs = [jnp.where(low, placed[2 * k], placed[2 * k + 1]) for k in range(NSA_HEADS // 2)]
    return jnp.concatenate(slabs, axis=1) * _silu(z)


def _compress_prompt_body(rows_ref, wab_ref, pc_ref, w2_ref, o_ref):
    xs = [rows_ref[pl.ds(l, N_CMP_PAD, stride=CMP_STRIDE), :].astype(BF16) for l in range(CMP_STRIDE)]
    x = jnp.concatenate(xs, axis=1)
    o_ref[...] = _compress_tokens([x], wab_ref[...], pc_ref[...], w2_ref[...])[0]


def _compress_prompt(p3, wab, pc, w2bd):
    b, t, _ = p3.shape
    kvblk = C_KV // LANES
    return pl.pallas_call(
        _compress_prompt_body,
        grid=(b, 2),
        in_specs=[pl.BlockSpec((None, t, LANES), lambda i, k: (i, 0, kvblk + k)),
                  pl.BlockSpec((None, CMP_STRIDE * LANES, 2 * LANES), lambda i, k: (k, 0, 0)),
                  pl.BlockSpec((None, SUBLANES, LANES), lambda i, k: (k, 0, 0)),
                  pl.BlockSpec((None, LANES, LANES), lambda i, k: (k, 0, 0))],
        out_specs=pl.BlockSpec((None, None, N_CMP_PAD, LANES), lambda i, k: (i, k, 0, 0)),
        out_shape=jax.ShapeDtypeStruct((b, 2, N_CMP_PAD, LANES), BF16),
        compiler_params=_cparams(("parallel", "parallel")),
    )(p3, wab, pc, w2bd)


def _nsa_prompt_body(q_ref, z_ref, g_ref, cmp_ref, sk_ref, sv_ref, wk_ref, wv_ref, ovt_ref, et_ref, o_ref,
                     svt_s, wvt_s, acc_s, *, n_sel):
    tq = 128
    i = pl.program_id(1)
    q0 = i * tq
    qh = _pad_heads(q_ref[...] * SCALE)
    gs = _sigmoid(g_ref[...])
    qpos_col = q0 + lax.broadcasted_iota(I32, (tq, 1), 0)
    qpos_row = q0 + lax.broadcasted_iota(I32, (1, LANES), 1)
    o_cmp, p_cmp = _cmp_branch(qh, cmp_ref[0], cmp_ref[1], qpos_col)
    ovt = ovt_ref[...]

    @pl.when(i == 0)
    def _():
        for c in range(svt_s.shape[0]):
            rows = slice(c * LANES, (c + 1) * LANES)
            svt_s[c] = sv_ref[rows, :].T.astype(BF16)
            wvt_s[c] = wv_ref[rows, :].T.astype(BF16)

    w0 = pl.multiple_of(jnp.maximum(q0 - WINDOW, 0), LANES)
    wwid = WINDOW + tq
    kwin = wk_ref[pl.ds(w0, wwid), :].astype(BF16)
    kp_w = w0 + lax.broadcasted_iota(I32, (wwid, LANES), 0)
    dist = qpos_row - kp_w
    maskw = jnp.where((dist >= 0) & (dist < WINDOW), 0.0, NEG)
    kbw = kp_w.astype(F32)
    wc0 = w0 // LANES
    sub = SEL_CHUNK // LANES
    nch = (i + 2 * sub) // (2 * sub)
    hq = NSA_HEADS * tq
    cols = [slice(hd * tq, (hd + 1) * tq) for hd in range(NSA_HEADS)]
    q8 = jnp.concatenate(qh, axis=0)

    sel_t = []
    for h in range(NSA_KV_HEADS):
        psum = p_cmp[h * NSA_GROUP]
        for hd in range(h * NSA_GROUP + 1, (h + 1) * NSA_GROUP):
            psum = psum + p_cmp[hd]
        sel_t.append(_select_blocks(psum, qpos_row, ovt, n_sel, queries_on_rows=False).astype(BF16))

    stw = _dot_nt(kwin, q8)
    pws, linv = [], []
    for hd in range(NSA_HEADS):
        s = stw[:, cols[hd]] + NSA_SLOPES[hd] * kbw + maskw
        p = jnp.exp(s - jnp.max(s, axis=0, keepdims=True))
        linv.append(1.0 / jnp.sum(p, axis=0, keepdims=True))
        pws.append(p.astype(BF16))
    ptw = jnp.concatenate(pws, axis=1)
    owin_t = _dot(wvt_s[wc0], ptw[0:LANES])
    for k in range(1, wwid // LANES):
        owin_t = owin_t + _dot(wvt_s[wc0 + k], ptw[k * LANES:(k + 1) * LANES])
    owin_t = owin_t * jnp.concatenate(linv, axis=1)

    acc_s[...] = jnp.zeros_like(acc_s)

    def scores(c):
        k0 = pl.multiple_of(c * SEL_CHUNK, SEL_CHUNK)
        kch = sk_ref[pl.ds(k0, SEL_CHUNK), :].astype(BF16)
        et = et_ref[pl.ds(k0, SEL_CHUNK), :]
        st = _dot_nt(kch, q8)
        mexp = [_dot(et, sel_t[h]) for h in range(NSA_KV_HEADS)]
        return st, mexp

    def update(c, st, mexp, m, l):
        k0 = c * SEL_CHUNK
        kp = k0 + lax.broadcasted_iota(I32, (SEL_CHUNK, LANES), 0)
        causal = kp <= qpos_row
        kbf = kp.astype(F32)
        ms, ls, alphas, ps = [], [], [], []
        for hd in range(NSA_HEADS):
            if hd % NSA_GROUP == 0:
                maskb = jnp.where((mexp[hd // NSA_GROUP] > 0.5) & causal, 0.0, NEG)
            s = st[:, cols[hd]] + NSA_SLOPES[hd] * kbf + maskb
            m_new = jnp.maximum(m[:, cols[hd]], jnp.max(s, axis=0, keepdims=True))
            alpha = jnp.exp(m[:, cols[hd]] - m_new)
            p = jnp.exp(s - m_new)
            ms.append(m_new)
            ls.append(alpha * l[:, cols[hd]] + jnp.sum(p, axis=0, keepdims=True))
            alphas.append(alpha)
            ps.append(p.astype(BF16))
        pt = jnp.concatenate(ps, axis=1)
        ot = _dot(svt_s[c * sub], pt[0:LANES])
        for k in range(1, sub):
            ot = ot + _dot(svt_s[c * sub + k], pt[k * LANES:(k + 1) * LANES])
        acc_s[...] = acc_s[...] * jnp.concatenate(alphas, axis=1) + ot
        return jnp.concatenate(ms, axis=1), jnp.concatenate(ls, axis=1)

    def body(c2, carry):
        m, l = carry
        a = scores(2 * c2)
        b = scores(2 * c2 + 1)
        m, l = update(2 * c2, a[0], a[1], m, l)
        return update(2 * c2 + 1, b[0], b[1], m, l)

    init = (jnp.full((1, hq), NEG, F32), jnp.zeros((1, hq), F32))
    _, l = lax.fori_loop(0, nch, body, init)
    osel_t = acc_s[...] * jnp.where(l > 0.0, 1.0 / l, 0.0)

    o_sel = [osel_t[:, cols[hd]].T for hd in range(NSA_HEADS)]
    o_win = [owin_t[:, cols[hd]].T for hd in range(NSA_HEADS)]
    o_ref[...] = _combine_heads(gs, o_cmp, o_sel, o_win, z_ref[...]).astype(o_ref.dtype)


def _nsa_prompt(p3, cmp_tok, ovt, e3):
    b, t, _ = p3.shape
    tq = 128
    n_sel = -(-t // SEL_BLOCK)
    kvb = C_KV // LANES
    return pl.pallas_call(
        functools.partial(_nsa_prompt_body, n_sel=n_sel),
        grid=(b, t // tq),
        scratch_shapes=[pltpu.VMEM((t // LANES, LANES, LANES), BF16),
                        pltpu.VMEM((t // LANES, LANES, LANES), BF16),
                        pltpu.VMEM((LANES, NSA_HEADS * tq), F32)],
        in_specs=[pl.BlockSpec((None, tq, NSA_WIDTH), lambda bi, i: (bi, i, C_Q // NSA_WIDTH)),
                  pl.BlockSpec((None, tq, NSA_WIDTH), lambda bi, i: (bi, i, C_ZN // NSA_WIDTH)),
                  pl.BlockSpec((None, tq, LANES), lambda bi, i: (bi, i, C_G // LANES)),
                  pl.BlockSpec((None, 2, N_CMP_PAD, LANES), lambda bi, i: (bi, 0, 0, 0)),
                  pl.BlockSpec((None, t, LANES), lambda bi, i: (bi, 0, kvb + 2)),
                  pl.BlockSpec((None, t, LANES), lambda bi, i: (bi, 0, kvb + 3)),
                  pl.BlockSpec((None, t, LANES), lambda bi, i: (bi, 0, kvb + 4)),
                  pl.BlockSpec((None, t, LANES), lambda bi, i: (bi, 0, kvb + 5)),
                  pl.BlockSpec((NSEL_PAD, LANES), lambda bi, i: (0, 0)),
                  pl.BlockSpec(e3.shape, lambda bi, i: (0, 0))],
        out_specs=pl.BlockSpec((None, tq, NSA_WIDTH), lambda bi, i: (bi, i, 0)),
        out_shape=jax.ShapeDtypeStruct((b, t, NSA_WIDTH), BF16),
        compiler_params=_cparams(("parallel", "arbitrary")),
    )(p3, p3, p3, cmp_tok, p3, p3, p3, p3, ovt, e3)


def _diff_lambda(lam_ref, lam_init):
    a = lam_ref[...]
    s1 = jnp.sum(a[0:1] * a[1:2], axis=-1, keepdims=True)
    s2 = jnp.sum(a[2:3] * a[3:4], axis=-1, keepdims=True)
    return jnp.exp(s1) - jnp.exp(s2) + lam_init


def _diff_finish(o, lnw, z, lam_init):
    ms = jnp.mean(o * o, axis=-1, keepdims=True)
    return o * lax.rsqrt(ms + EPS) * lnw * (1.0 - lam_init) * _silu(z)


def _diff_prompt_body(lam_ref, q_ref, z_ref, k_ref, v_ref, lnw_ref, o_ref, vt_s, acc_s, *, lam_init):
    tq = 128
    i = pl.program_id(1)
    q0 = i * tq

    @pl.when(i == 0)
    def _():
        for hh in range(DIFF_HEADS):
            for c in range(vt_s.shape[1]):
                vt_s[hh, c] = v_ref[c * LANES:(c + 1) * LANES, hh * LANES:(hh + 1) * LANES].T.astype(BF16)

    lam = _diff_lambda(lam_ref, lam_init)
    low = lax.broadcasted_iota(I32, (tq, LANES), 1) < DIFF_HEAD_DIM
    qpos_row = q0 + lax.broadcasted_iota(I32, (1, LANES), 1)
    sub = SEL_CHUNK // LANES
    nch = (i + 2 * sub) // (2 * sub)
    q2 = []
    for hh in range(DIFF_HEADS):
        qs = q_ref[:, hh * LANES:(hh + 1) * LANES] * DIFF_SCALE
        q2.append(jnp.concatenate([jnp.where(low, qs, 0.0), jnp.where(low, 0.0, qs)], axis=0).astype(BF16))
    acc_s[...] = jnp.zeros_like(acc_s)
    ncol = 2 * tq

    def scores(c):
        k0 = pl.multiple_of(c * SEL_CHUNK, SEL_CHUNK)
        return [_dot_nt(k_ref[pl.ds(k0, SEL_CHUNK), hh * LANES:(hh + 1) * LANES].astype(BF16), q2[hh])
                for hh in range(DIFF_HEADS)]

    def update(c, sts, m, l):
        kp = c * SEL_CHUNK + lax.broadcasted_iota(I32, (SEL_CHUNK, LANES), 0)
        maskb = jnp.where(kp <= qpos_row, 0.0, NEG)
        kbf = kp.astype(F32)
        ms, ls = [], []
        for hh in range(DIFF_HEADS):
            st = sts[hh]
            bias = DIFF_SLOPES[hh] * kbf + maskb
            alphas, ps = [], []
            for comp in range(2):
                cs = slice(comp * tq, (comp + 1) * tq)
                gs_ = slice(hh * ncol + comp * tq, hh * ncol + (comp + 1) * tq)
                s = st[:, cs] + bias
                m_new = jnp.maximum(m[:, gs_], jnp.max(s, axis=0, keepdims=True))
                alpha = jnp.exp(m[:, gs_] - m_new)
                p = jnp.exp(s - m_new)
                ms.append(m_new)
                ls.append(alpha * l[:, gs_] + jnp.sum(p, axis=0, keepdims=True))
                alphas.append(alpha)
                ps.append(p.astype(BF16))
            pt = jnp.concatenate(ps, axis=1)
            ot = _dot(vt_s[hh, c * sub], pt[0:LANES])
            for k in range(1, sub):
                ot = ot + _dot(vt_s[hh, c * sub + k], pt[k * LANES:(k + 1) * LANES])
            acc_s[hh] = acc_s[hh] * jnp.concatenate(alphas, axis=1) + ot
        return jnp.concatenate(ms, axis=1), jnp.concatenate(ls, axis=1)

    def body(c2, carry):
        m, l = carry
        a = scores(2 * c2)
        b = scores(2 * c2 + 1)
        m, l = update(2 * c2, a, m, l)
        return update(2 * c2 + 1, b, m, l)

    init = (jnp.full((1, DIFF_HEADS * ncol), NEG, F32), jnp.zeros((1, DIFF_HEADS * ncol), F32))
    _, l = lax.fori_loop(0, nch, body, init)
    lnw = lnw_ref[...]
    slabs = []
    for hh in range(DIFF_HEADS):
        on = acc_s[hh] * (1.0 / l[:, hh * ncol:(hh + 1) * ncol])
        o = (on[:, 0:tq] - lam * on[:, tq:2 * tq]).T
        slabs.append(_diff_finish(o, lnw, z_ref[:, hh * LANES:(hh + 1) * LANES], lam_init))
    o_ref[...] = jnp.concatenate(slabs, axis=1).astype(o_ref.dtype)


def _diff_prompt(p3, lam4, lnw_row, lam_init):
    b, t, _ = p3.shape
    tq = 128
    return pl.pallas_call(
        functools.partial(_diff_prompt_body, lam_init=lam_init),
        grid=(b, t // tq),
        in_specs=[pl.BlockSpec((4, DIFF_HEAD_DIM), lambda bi, i: (0, 0)),
                  pl.BlockSpec((None, tq, DIFF_WIDTH), lambda bi, i: (bi, i, C_DQ // DIFF_WIDTH)),
                  pl.BlockSpec((None, tq, DIFF_WIDTH), lambda bi, i: (bi, i, C_ZD // DIFF_WIDTH)),
                  pl.BlockSpec((None, t, DIFF_WIDTH), lambda bi, i: (bi, 0, C_DK // DIFF_WIDTH)),
                  pl.BlockSpec((None, t, DIFF_WIDTH), lambda bi, i: (bi, 0, C_DV // DIFF_WIDTH)),
                  pl.BlockSpec((1, DIFF_V_DIM), lambda bi, i: (0, 0))],
        out_specs=pl.BlockSpec((None, tq, DIFF_WIDTH), lambda bi, i: (bi, i, 0)),
        out_shape=jax.ShapeDtypeStruct((b, t, DIFF_WIDTH), BF16),
        scratch_shapes=[pltpu.VMEM((DIFF_HEADS, t // LANES, LANES, LANES), BF16),
                        pltpu.VMEM((DIFF_HEADS, DIFF_V_DIM, 2 * tq), F32)],
        compiler_params=_cparams(("parallel", "arbitrary")),
    )(lam4, p3, p3, p3, p3, lnw_row)


SSM_BLOCKS = SSM_WIDTH // LANES
SSM_BS = N_STATE // SSM_BLOCKS
SSM_CHUNK = 256
SSM_SEG = SSM_CHUNK // SUBLANES
SSM_UNROLL = 32


def _discretize(lre, lim, ldt):
    dt = jnp.exp(ldt)
    mag = jnp.exp(lre * dt)
    ar = mag * jnp.cos(lim * dt)
    ai = mag * jnp.sin(lim * dt)
    den = lre * lre + lim * lim
    qr = ((ar - 1.0) * lre + ai * lim) / den
    qi = (ai * lre - (ar - 1.0) * lim) / den
    return ar, ai, qr, qi


def _bbar_block(qr, qi, bre, bim):
    return jnp.concatenate([qr * bre - qi * bim, qr * bim + qi * bre], axis=1).astype(BF16)


def _glu_tail(y, wglu, z):
    gate = _sigmoid(_dot(y.astype(BF16), wglu))
    return y * gate * _silu(z)


def _ssm_prompt_body(u_ref, z_ref, lre_ref, lim_ref, ldt_ref, bre_ref, bim_ref, cmat_ref, d_ref, wglu_ref,
                     perm_ref, o_ref, hr_ref, hi_ref,
                     bbar_s, apr_s, api_s, bu_s, y_s, carr_s, cari_s):
    first = (pl.program_id(0) == 0) & (pl.program_id(1) == 0)

    @pl.when(first)
    def _():
        ar, ai, qr, qi = _discretize(lre_ref[...], lim_ref[...], ldt_ref[...])
        for k in range(SSM_BLOCKS):
            cs = slice(k * SSM_BS, (k + 1) * SSM_BS)
            bbar_s[k] = _bbar_block(qr[:, cs], qi[:, cs], bre_ref[k], bim_ref[k])
        pr, pi = ar, ai
        apr_s[0:1, :] = pr
        api_s[0:1, :] = pi
        for k in range(1, SSM_SEG):
            pr, pi = pr * ar - pi * ai, pr * ai + pi * ar
            apr_s[k:k + 1, :] = pr
            api_s[k:k + 1, :] = pi

    @pl.when(pl.program_id(1) == 0)
    def _():
        carr_s[...] = jnp.zeros_like(carr_s)
        cari_s[...] = jnp.zeros_like(cari_s)

    re_c = slice(0, SSM_BS)
    im_c = slice(SSM_BS, 2 * SSM_BS)
    u_perm = _dot(perm_ref[0], u_ref[...].astype(BF16)).astype(BF16)
    unperm = perm_ref[1]

    for k in range(SSM_BLOCKS):
        cs = slice(k * SSM_BS, (k + 1) * SSM_BS)
        us = slice(k * LANES, (k + 1) * LANES)
        bu_s[...] = _dot(u_perm[:, us], bbar_s[k])
        ar_b = jnp.broadcast_to(apr_s[0:1, cs], (SUBLANES, SSM_BS))
        ai_b = jnp.broadcast_to(api_s[0:1, cs], (SUBLANES, SSM_BS))

        def step(t, st, ar_b=ar_b, ai_b=ai_b):
            hr, hi = st
            rows = pl.ds(pl.multiple_of(t * SUBLANES, SUBLANES), SUBLANES)
            nr = ar_b * hr - ai_b * hi + bu_s[rows, re_c]
            ni = ar_b * hi + ai_b * hr + bu_s[rows, im_c]
            bu_s[rows, re_c] = nr
            bu_s[rows, im_c] = ni
            return nr, ni

        zero = jnp.zeros((SUBLANES, SSM_BS), F32)
        er, ei = lax.fori_loop(0, SSM_SEG, step, (zero, zero), unroll=SSM_UNROLL)
        a32r = apr_s[SSM_SEG - 1:SSM_SEG, cs]
        a32i = api_s[SSM_SEG - 1:SSM_SEG, cs]
        rows_r = [carr_s[:, cs]]
        rows_i = [cari_s[:, cs]]
        for s in range(1, SUBLANES + 1):
            pr, pi = rows_r[-1], rows_i[-1]
            rows_r.append(er[s - 1:s] + a32r * pr - a32i * pi)
            rows_i.append(ei[s - 1:s] + a32r * pi + a32i * pr)
        carr_s[:, cs] = rows_r[SUBLANES]
        cari_s[:, cs] = rows_i[SUBLANES]
        hin_r = jnp.concatenate(rows_r[:SUBLANES], axis=0)
        hin_i = jnp.concatenate(rows_i[:SUBLANES], axis=0)

        def fix(t, carry, hin_r=hin_r, hin_i=hin_i, cs=cs):
            rows = pl.ds(pl.multiple_of(t * SUBLANES, SUBLANES), SUBLANES)
            pr = apr_s[pl.ds(t, 1), cs]
            pi = api_s[pl.ds(t, 1), cs]
            bu_s[rows, re_c] = bu_s[rows, re_c] + pr * hin_r - pi * hin_i
            bu_s[rows, im_c] = bu_s[rows, im_c] + pr * hin_i + pi * hin_r
            return carry

        lax.fori_loop(0, SSM_SEG, fix, 0, unroll=SSM_UNROLL)
        yp = _dot(bu_s[...].astype(BF16), cmat_ref[k])
        y_hi = yp.astype(BF16)
        r1 = yp - y_hi.astype(F32)
        y_mid = r1.astype(BF16)
        y_lo = (r1 - y_mid.astype(F32)).astype(BF16)
        yk = _dot(unperm, y_hi) + _dot(unperm, y_mid) + _dot(unperm, y_lo) + d_ref[:, us] * u_ref[:, us]
        y_s[:, us] = _gelu(yk)

    o_ref[...] = _glu_tail(y_s[...], wglu_ref[...], z_ref[...]).astype(o_ref.dtype)
    hr_ref[...] = carr_s[...]
    hi_ref[...] = cari_s[...]


def _ssm_prompt(p3, sp):
    b, t, _ = p3.shape
    full = lambda shape: pl.BlockSpec(shape, lambda bi, j: (0,) * len(shape))
    out, hr, hi = pl.pallas_call(
        _ssm_prompt_body,
        grid=(b, t // SSM_CHUNK),
        in_specs=[pl.BlockSpec((None, SSM_CHUNK, SSM_WIDTH), lambda bi, j: (bi, j, C_U // SSM_WIDTH)),
                  pl.BlockSpec((None, SSM_CHUNK, SSM_WIDTH), lambda bi, j: (bi, j, C_ZS // SSM_WIDTH)),
                  full((1, N_STATE)), full((1, N_STATE)), full((1, N_STATE)),
                  full((SSM_BLOCKS, LANES, SSM_BS)), full((SSM_BLOCKS, LANES, SSM_BS)),
                  full((SSM_BLOCKS, 2 * SSM_BS, LANES)), full((1, SSM_WIDTH)),
                  full((SSM_WIDTH, SSM_WIDTH)), full((2, SSM_CHUNK, SSM_CHUNK))],
        out_specs=[pl.BlockSpec((None, SSM_CHUNK, SSM_WIDTH), lambda bi, j: (bi, j, 0)),
                   pl.BlockSpec((None, 1, N_STATE), lambda bi, j: (bi, 0, 0)),
                   pl.BlockSpec((None, 1, N_STATE), lambda bi, j: (bi, 0, 0))],
        out_shape=[jax.ShapeDtypeStruct((b, t, SSM_WIDTH), BF16),
                   jax.ShapeDtypeStruct((b, 1, N_STATE), F32),
                   jax.ShapeDtypeStruct((b, 1, N_STATE), F32)],
        scratch_shapes=[pltpu.VMEM((SSM_BLOCKS, LANES, 2 * SSM_BS), BF16),
                        pltpu.VMEM((SSM_SEG, N_STATE), F32), pltpu.VMEM((SSM_SEG, N_STATE), F32),
                        pltpu.VMEM((SSM_CHUNK, 2 * SSM_BS), F32),
                        pltpu.VMEM((SSM_CHUNK, SSM_WIDTH), F32),
                        pltpu.VMEM((1, N_STATE), F32), pltpu.VMEM((1, N_STATE), F32)],
        compiler_params=_cparams(("arbitrary", "arbitrary")),
    )(p3, p3, sp['lre'], sp['lim'], sp['ldt'], sp['bre'], sp['bim'], sp['cmat'], sp['d'], sp['wglu'],
      _segment_permutation())
    return out, hr, hi


def _segment_permutation():
    r = np.arange(SSM_CHUNK)
    src = (r % SUBLANES) * SSM_SEG + r // SUBLANES
    pm = np.zeros((SSM_CHUNK, SSM_CHUNK), np.float32)
    pm[r, src] = 1.0
    return jnp.asarray(np.stack([pm, pm.T]), BF16)


def _ssm_sample_body(u_ref, z_ref, h0r_ref, h0i_ref, lre_ref, lim_ref, ldt_ref, bre_ref, bim_ref, cmat_ref,
                     d_ref, wglu_ref, o_ref, hr_ref, hi_ref, y_s, *, n_t):
    k = pl.program_id(0)
    ar, ai, qr, qi = _discretize(lre_ref[...], lim_ref[...], ldt_ref[...])
    bbar = _bbar_block(qr, qi, bre_ref[...], bim_ref[...])
    hr = h0r_ref[...]
    hi = h0i_ref[...]
    cmat = cmat_ref[...]
    for t in range(n_t):
        ut = u_ref[t]
        bu = _dot(ut.astype(BF16), bbar)
        hr, hi = ar * hr - ai * hi + bu[:, 0:SSM_BS], ar * hi + ai * hr + bu[:, SSM_BS:]
        hcat = jnp.concatenate([hr, hi], axis=1).astype(BF16)
        y_s[t, k] = _gelu(_dot(hcat, cmat) + d_ref[...] * ut)
    hr_ref[...] = hr
    hi_ref[...] = hi

    @pl.when(k == SSM_BLOCKS - 1)
    def _():
        wglu = wglu_ref[...]
        for t in range(n_t):
            y = jnp.concatenate([y_s[t, kk] for kk in range(SSM_BLOCKS)], axis=1)
            o_ref[t] = _glu_tail(y, wglu, z_ref[t])


def _ssm_sample(ut, zt, h0r, h0i, sp):
    n_t, nb, _ = ut.shape
    out, hr, hi = pl.pallas_call(
        functools.partial(_ssm_sample_body, n_t=n_t),
        grid=(SSM_BLOCKS,),
        in_specs=[pl.BlockSpec((n_t, nb, LANES), lambda k: (0, 0, k)),
                  pl.BlockSpec((n_t, nb, SSM_WIDTH), lambda k: (0, 0, 0)),
                  pl.BlockSpec((nb, SSM_BS), lambda k: (0, k)),
                  pl.BlockSpec((nb, SSM_BS), lambda k: (0, k)),
                  pl.BlockSpec((1, SSM_BS), lambda k: (0, k)),
                  pl.BlockSpec((1, SSM_BS), lambda k: (0, k)),
                  pl.BlockSpec((1, SSM_BS), lambda k: (0, k)),
                  pl.BlockSpec((None, LANES, SSM_BS), lambda k: (k, 0, 0)),
                  pl.BlockSpec((None, LANES, SSM_BS), lambda k: (k, 0, 0)),
                  pl.BlockSpec((None, 2 * SSM_BS, LANES), lambda k: (k, 0, 0)),
                  pl.BlockSpec((1, LANES), lambda k: (0, k)),
                  pl.BlockSpec((SSM_WIDTH, SSM_WIDTH), lambda k: (0, 0))],
        out_specs=[pl.BlockSpec((n_t, nb, SSM_WIDTH), lambda k: (0, 0, 0)),
                   pl.BlockSpec((nb, SSM_BS), lambda k: (0, k)),
                   pl.BlockSpec((nb, SSM_BS), lambda k: (0, k))],
        out_shape=[jax.ShapeDtypeStruct((n_t, nb, SSM_WIDTH), F32),
                   jax.ShapeDtypeStruct((nb, N_STATE), F32),
                   jax.ShapeDtypeStruct((nb, N_STATE), F32)],
        scratch_shapes=[pltpu.VMEM((n_t, SSM_BLOCKS, nb, LANES), F32)],
        compiler_params=_cparams(("arbitrary",)),
    )(ut, zt, h0r, h0i, sp['lre'], sp['lim'], sp['ldt'], sp['bre'], sp['bim'], sp['cmat'], sp['d'], sp['wglu'])
    return out, hr, hi


def _page_map(li, pg, b, pt):
    return (li, pt[b, pg], 0, 0)


def _page_map_seq(li, pg, sq, b, pt):
    return (li, pt[b * NSA_DEC_SEQS + sq, pg], 0, 0)


def _pad_rows(x, rows):
    return jnp.concatenate([x, jnp.zeros((rows - x.shape[0], x.shape[1]), x.dtype)], axis=0)


def _nsa_sample_body(pt_ref, *refs, n_pages, n_sel):
    nsq = NSA_DEC_SEQS
    (q_ref, z_ref, g_ref, snew_ref, wnew_ref, wb_ref, wab_ref, pc_ref, w2_ref, ovt_ref, e_ref,
     gperm_ref, o_ref, s_s) = refs[nsq * n_pages:]
    seqs = range(nsq)
    pages = [refs[sq * n_pages:(sq + 1) * n_pages] for sq in seqs]
    rs = [slice(sq * T_PAD, (sq + 1) * T_PAD) for sq in seqs]
    past = n_pages * PAGE_SIZE
    nk = past + LANES
    wb = wb_ref.shape[2]
    qpos_col = past + lax.broadcasted_iota(I32, (T_PAD, 1), 0)
    qpos64 = jnp.concatenate([qpos_col] * NSA_HEADS, axis=0)
    qpos_row = past + (lax.broadcasted_iota(I32, (1, LANES), 1) % T_PAD)
    slope_col = jnp.concatenate([jnp.full((T_PAD, 1), NSA_SLOPES[hd], F32) for hd in range(NSA_HEADS)], axis=0)
    kpos = lax.broadcasted_iota(I32, (1, nk), 1)
    kbias = slope_col * kpos.astype(F32)
    kpos_w = (past - wb) + lax.broadcasted_iota(I32, (1, wb + LANES), 1)
    wbias = slope_col * kpos_w.astype(F32)
    dist = qpos64 - kpos_w
    wmask = (dist >= 0) & (dist < WINDOW) & (kpos_w >= 0)
    causal = kpos <= qpos64
    sk_r = pl.ds(2 * LANES, LANES)
    sv_r = pl.ds(3 * LANES, LANES)

    qh = [_pad_heads(q_ref[rs[sq], :] * SCALE) for sq in seqs]
    q64 = [jnp.concatenate(qh[sq], axis=0) for sq in seqs]

    gperm = gperm_ref[...]
    gathered = [[[_dot(pages[sq][pg][kv * LANES:(kv + 1) * LANES, :].astype(BF16), gperm).T
                  for pg in range(n_pages)] for sq in seqs] for kv in range(2)]

    owin = []
    for sq in seqs:
        wnew = wnew_ref[rs[sq], :]
        kwn = _pad_rows(wnew[:, 0:LANES], LANES).astype(BF16)
        vwn = _pad_rows(wnew[:, LANES:2 * LANES], LANES).astype(BF16)
        sw = jnp.concatenate([_dot(q64[sq], wb_ref[sq, 0:LANES, :].astype(BF16)), _dot_nt(q64[sq], kwn)], axis=1)
        pw = _msoftmax(sw + wbias, wmask).astype(BF16)
        owin.append(_dot_nt(pw[:, 0:wb], wb_ref[sq, LANES:2 * LANES, :].astype(BF16)) + _dot(pw[:, wb:], vwn))

    for sq in seqs:
        for pg in range(n_pages):
            s_s[sq, :, pg * PAGE_SIZE:(pg + 1) * PAGE_SIZE] = _dot(q64[sq], pages[sq][pg][sk_r, :].astype(BF16))
        knew = _pad_rows(snew_ref[rs[sq], 0:LANES], LANES).astype(BF16)
        s_s[sq, :, past:nk] = _dot_nt(q64[sq], knew)

    toks = []
    npc = PAGE_SIZE // CMP_STRIDE
    for kv in range(2):
        xseq = []
        for sq in seqs:
            xs = [jnp.concatenate([gathered[kv][sq][pg][l * npc:(l + 1) * npc] for pg in range(n_pages)],
                                  axis=0).astype(BF16) for l in range(CMP_STRIDE)]
            xseq.append(jnp.concatenate(xs, axis=1))
        toks.append(_compress_tokens(xseq, wab_ref[kv], pc_ref[kv], w2_ref[kv]))
    cmp = [_cmp_branch(qh[sq], toks[0][sq], toks[1][sq], qpos_col) for sq in seqs]

    memb = []
    for sq in seqs:
        p_cmp = cmp[sq][1]
        psum = jnp.concatenate(
            [p_cmp[0] + p_cmp[1] + p_cmp[2] + p_cmp[3], p_cmp[4] + p_cmp[5] + p_cmp[6] + p_cmp[7],
             jnp.zeros((LANES - 2 * T_PAD, N_CMP_PAD), F32)], axis=0)
        sel = _select_blocks(psum, qpos_row, ovt_ref[...], n_sel)
        mexp = _dot(sel[0:2 * T_PAD].astype(BF16), e_ref[...])
        memb.append(
            jnp.concatenate([mexp[0:T_PAD]] * NSA_GROUP + [mexp[T_PAD:2 * T_PAD]] * NSA_GROUP, axis=0) > 0.5)

    for sq in seqs:
        p = _msoftmax(s_s[sq] + kbias, memb[sq] & causal).astype(BF16)
        vnew = _pad_rows(snew_ref[rs[sq], LANES:2 * LANES], LANES).astype(BF16)
        osel = _dot(p[:, past:nk], vnew)
        for pg in range(n_pages):
            osel = osel + _dot_nt(p[:, pg * PAGE_SIZE:(pg + 1) * PAGE_SIZE], pages[sq][pg][sv_r, :].astype(BF16))
        o_sel = [osel[hd * T_PAD:(hd + 1) * T_PAD] for hd in range(NSA_HEADS)]
        o_win = [owin[sq][hd * T_PAD:(hd + 1) * T_PAD] for hd in range(NSA_HEADS)]
        o_ref[rs[sq], :] = _combine_heads(_sigmoid(g_ref[rs[sq], :]), cmp[sq][0], o_sel, o_win, z_ref[rs[sq], :])


def _nsa_sample(p2, cache_t, li, win_t, page_table, wab, pc, w2bd, ovt, e2, n_t):
    n = p2.shape[0]
    nb, n_pages = page_table.shape
    past = n_pages * PAGE_SIZE
    n_sel = -(-(past + n_t) // SEL_BLOCK)
    nk = past + LANES
    wb = win_t.shape[3]
    kvb = C_KV // (2 * LANES)
    nsq = NSA_DEC_SEQS
    assert nb % nsq == 0
    page_specs = [pl.BlockSpec((None, None, 4 * LANES, PAGE_SIZE), functools.partial(_page_map_seq, li, pg, sq))
                  for sq in range(nsq) for pg in range(n_pages)]
    const = lambda shape: pl.BlockSpec(shape, lambda b, pt: (0,) * len(shape))
    rows = nsq * T_PAD
    in_specs = page_specs + [
        pl.BlockSpec((rows, NSA_WIDTH), lambda b, pt: (b, C_Q // NSA_WIDTH)),
        pl.BlockSpec((rows, NSA_WIDTH), lambda b, pt: (b, C_ZN // NSA_WIDTH)),
        pl.BlockSpec((rows, LANES), lambda b, pt: (b, C_G // LANES)),
        pl.BlockSpec((rows, 2 * LANES), lambda b, pt: (b, kvb + 1)),
        pl.BlockSpec((rows, 2 * LANES), lambda b, pt: (b, kvb + 2)),
        pl.BlockSpec((None, nsq, 2 * LANES, wb), lambda b, pt: (li, b, 0, 0)),
        const(wab.shape), const(pc.shape), const(w2bd.shape),
        const(ovt.shape), const(e2.shape), const((PAGE_SIZE, PAGE_SIZE))]
    return pl.pallas_call(
        functools.partial(_nsa_sample_body, n_pages=n_pages, n_sel=n_sel),
        grid_spec=pltpu.PrefetchScalarGridSpec(
            num_scalar_prefetch=1, grid=(nb // nsq,), in_specs=in_specs,
            out_specs=pl.BlockSpec((rows, NSA_WIDTH), lambda b, pt: (b, 0)),
            scratch_shapes=[pltpu.VMEM((nsq, NSA_HEADS * T_PAD, nk), F32)]),
        out_shape=jax.ShapeDtypeStruct((n, NSA_WIDTH), F32),
        compiler_params=_cparams(("parallel",)),
    )(page_table, *([cache_t] * (nsq * n_pages)), p2, p2, p2, p2, p2, win_t, wab, pc, w2bd, ovt, e2,
      _chunk_gather_permutation())


def _chunk_gather_permutation():
    k = np.arange(PAGE_SIZE)
    pm = np.zeros((PAGE_SIZE, PAGE_SIZE), np.float32)
    pm[k, (k % CMP_STRIDE) * (PAGE_SIZE // CMP_STRIDE) + k // CMP_STRIDE] = 1.0
    return jnp.asarray(pm, BF16)


def _diff_sample_body(pt_ref, *refs, n_pages, lam_init):
    pages = refs[:n_pages]
    lam_ref, q_ref, z_ref, knew_ref, vnew_ref, lnw_ref, o_ref, s_s = refs[n_pages:]
    past = n_pages * PAGE_SIZE
    nk = past + LANES
    lam = _diff_lambda(lam_ref, lam_init)
    low = lax.broadcasted_iota(I32, (T_PAD, LANES), 1) < DIFF_HEAD_DIM
    rows_h = 2 * T_PAD
    q16 = []
    for hh in range(DIFF_HEADS):
        qs = q_ref[:, hh * LANES:(hh + 1) * LANES] * DIFF_SCALE
        q16.append(jnp.concatenate([jnp.where(low, qs, 0.0), jnp.where(low, 0.0, qs)], axis=0).astype(BF16))
    for pg in range(n_pages):
        for hh in range(DIFF_HEADS):
            kp = pages[pg][pl.ds(hh, PAGE_SIZE, stride=DIFF_SLABS), :].astype(BF16)
            s_s[hh * rows_h:(hh + 1) * rows_h, pg * PAGE_SIZE:(pg + 1) * PAGE_SIZE] = _dot_nt(q16[hh], kp)
    for hh in range(DIFF_HEADS):
        kn = _pad_rows(knew_ref[:, hh * LANES:(hh + 1) * LANES], LANES).astype(BF16)
        s_s[hh * rows_h:(hh + 1) * rows_h, past:nk] = _dot_nt(q16[hh], kn)
    kpos = lax.broadcasted_iota(I32, (1, nk), 1)
    slope_col = jnp.concatenate([jnp.full((rows_h, 1), DIFF_SLOPES[hh], F32) for hh in range(DIFF_HEADS)], axis=0)
    qpos = past + (lax.broadcasted_iota(I32, (DIFF_HEADS * rows_h, 1), 0) % T_PAD)
    p = _msoftmax(s_s[...] + slope_col * kpos.astype(F32), kpos <= qpos)
    lnw = lnw_ref[...]
    slabs = []
    for hh in range(DIFF_HEADS):
        a = (p[hh * rows_h:hh * rows_h + T_PAD] - lam * p[hh * rows_h + T_PAD:(hh + 1) * rows_h]).astype(BF16)
        vn = _pad_rows(vnew_ref[:, hh * LANES:(hh + 1) * LANES], LANES).astype(BF16)
        o = _dot(a[:, past:nk], vn)
        for pg in range(n_pages):
            vp = pages[pg][pl.ds(DIFF_HEADS + hh, PAGE_SIZE, stride=DIFF_SLABS), :].astype(BF16)
            o = o + _dot(a[:, pg * PAGE_SIZE:(pg + 1) * PAGE_SIZE], vp)
        slabs.append(_diff_finish(o, lnw, z_ref[:, hh * LANES:(hh + 1) * LANES], lam_init))
    o_ref[...] = jnp.concatenate(slabs, axis=1)


def _diff_sample(p2, cache_r, li, page_table, lam4, lnw_row, lam_init):
    n = p2.shape[0]
    nb, n_pages = page_table.shape
    nk = n_pages * PAGE_SIZE + LANES
    page_specs = [pl.BlockSpec((None, None, PAGE_SIZE * DIFF_SLABS, LANES), functools.partial(_page_map, li, pg))
                  for pg in range(n_pages)]
    in_specs = page_specs + [
        pl.BlockSpec((4, DIFF_HEAD_DIM), lambda b, pt: (0, 0)),
        pl.BlockSpec((T_PAD, DIFF_WIDTH), lambda b, pt: (b, C_DQ // DIFF_WIDTH)),
        pl.BlockSpec((T_PAD, DIFF_WIDTH), lambda b, pt: (b, C_ZD // DIFF_WIDTH)),
        pl.BlockSpec((T_PAD, DIFF_WIDTH), lambda b, pt: (b, C_DK // DIFF_WIDTH)),
        pl.BlockSpec((T_PAD, DIFF_WIDTH), lambda b, pt: (b, C_DV // DIFF_WIDTH)),
        pl.BlockSpec((1, DIFF_V_DIM), lambda b, pt: (0, 0))]
    return pl.pallas_call(
        functools.partial(_diff_sample_body, n_pages=n_pages, lam_init=lam_init),
        grid_spec=pltpu.PrefetchScalarGridSpec(
            num_scalar_prefetch=1, grid=(nb,), in_specs=in_specs,
            out_specs=pl.BlockSpec((T_PAD, DIFF_WIDTH), lambda b, pt: (b, 0)),
            scratch_shapes=[pltpu.VMEM((DIFF_HEADS * 2 * T_PAD, nk), F32)]),
        out_shape=jax.ShapeDtypeStruct((n, DIFF_WIDTH), F32),
        compiler_params=_cparams(("parallel",)),
    )(page_table, *([cache_r] * n_pages), lam4, p2, p2, p2, p2, lnw_row)


def _overlap_t(n_sel):
    n = np.arange(N_CMP_PAD)[None, :]
    j = np.arange(NSEL_PAD)[:, None]
    cs = n * CMP_STRIDE
    ss = j * SEL_BLOCK
    ov = np.clip(np.minimum(cs + CMP_BLOCK, ss + SEL_BLOCK) - np.maximum(cs, ss), 0, None) / CMP_BLOCK
    ov = np.where((n < N_CMP_PAD - 1) & (j < n_sel), ov, 0.0)
    return jnp.asarray(ov, BF16)


def _block_expander(n_keys):
    j = np.arange(LANES)[:, None]
    k = np.arange(n_keys)[None, :]
    return jnp.asarray((k // SEL_BLOCK) == j, BF16)


def _layer_weights(li, norm_w, w_in, w_out, w_cmp1, pos_cmp, w_cmp2, lre, lim, ldt, bre, bim, cre, cim, sd,
                   w_glu, lam_q1, lam_k1, lam_q2, lam_k2, diff_ln_w):
    w = {}
    w['nw'] = norm_w[li][None, :]
    w['w_out'] = w_out[li].astype(BF16)
    eye2 = jnp.eye(NSA_KV_HEADS, dtype=F32)
    w1 = w_cmp1[li].reshape(2, 2, CMP_STRIDE, HEAD_DIM, HEAD_DIM)
    w['wab'] = jnp.einsum('kaldf,hg->klhdagf', w1, eye2).reshape(
        2, CMP_STRIDE * LANES, 2 * LANES).astype(BF16)
    pos = pos_cmp[li].reshape(2, 2, CMP_STRIDE, 1, HEAD_DIM)
    pos = jnp.broadcast_to(pos, (2, 2, CMP_STRIDE, NSA_KV_HEADS, HEAD_DIM)).reshape(2, 2, 1, CMP_STRIDE * LANES)
    pos = jnp.broadcast_to(pos, (2, 2, SUBLANES, CMP_STRIDE * LANES)).astype(BF16)
    w['pos_a'] = pos[:, 0]
    w['pos_b'] = pos[:, 1]
    w['w2bd'] = jnp.einsum('ked,hg->khegd', w_cmp2[li], eye2).reshape(2, LANES, LANES).astype(BF16)
    eye8 = jnp.eye(SSM_BLOCKS, dtype=F32)
    gl = SSM_BLOCKS

    def compact_b(bm):
        bm = bm.reshape(SSM_BLOCKS, gl, SSM_STATE, SSM_GROUP_CH)
        return jnp.einsum('kgpc,gh->kgchp', bm, eye8).reshape(SSM_BLOCKS, LANES, SSM_BS)

    def compact_c(cm):
        cm = cm.reshape(SSM_BLOCKS, gl, SSM_GROUP_CH, SSM_STATE)
        return jnp.einsum('kgcp,gh->khpgc', cm, eye8).reshape(SSM_BLOCKS, SSM_BS, LANES)

    sp = {'lre': lre[li].reshape(1, N_STATE), 'lim': lim[li].reshape(1, N_STATE),
          'ldt': jnp.repeat(ldt[li], SSM_STATE).reshape(1, N_STATE),
          'bre': compact_b(bre[li]), 'bim': compact_b(bim[li]),
          'cmat': jnp.concatenate([compact_c(cre[li]), -compact_c(cim[li])], axis=1).astype(BF16),
          'd': sd[li].reshape(1, SSM_WIDTH), 'wglu': w_glu[li].astype(BF16)}
    w['ssm'] = sp
    w['lam4'] = jnp.stack([lam_q1[li], lam_k1[li], lam_q2[li], lam_k2[li]])
    w['lnw'] = diff_ln_w[li][None, :]
    w['lam_init'] = 0.8 - 0.6 * math.exp(-0.3 * li)
    return w


def kernel(x_prompt, x_sample, cache_nsa_kv, cache_diff_kv, state_nsa_win, state_ssm_re, state_ssm_im, page_table, norm_w, w_in, w_out, w_cmp1, pos_cmp, w_cmp2, ssm_lambda_re, ssm_lambda_im, ssm_log_dt, ssm_b_re, ssm_b_im, ssm_c_re, ssm_c_im, ssm_d, w_glu, lam_q1, lam_k1, lam_q2, lam_k2, diff_ln_w, final_norm_w):
    depth = norm_w.shape[0]
    b, t, _ = x_prompt.shape
    nb, n_t, _ = x_sample.shape
    n_pool = cache_nsa_kv.shape[1]
    n_pages = page_table.shape[1]
    past = n_pages * PAGE_SIZE
    wb = state_nsa_win.shape[2]
    assert t % SSM_CHUNK == 0 and t >= WINDOW + 128 and n_t <= T_PAD

    ovt_p = _overlap_t(-(-t // SEL_BLOCK))
    ovt_s = _overlap_t(-(-(past + n_t) // SEL_BLOCK))
    e_p = _block_expander(t).T
    e_s = _block_expander(past + LANES)
    fw = final_norm_w[None, :]
    nsa_cache_t = cache_nsa_kv.transpose(0, 1, 3, 4, 5, 2).reshape(depth, n_pool, 4 * LANES, PAGE_SIZE)
    win_t = state_nsa_win.transpose(0, 1, 3, 4, 5, 2).reshape(depth, nb, 2 * LANES, wb)
    diff_cache_r = cache_diff_kv.reshape(depth, n_pool, PAGE_SIZE * DIFF_SLABS, LANES)
    w_rows = w_in.transpose(0, 2, 1)
    w_in_t = jnp.concatenate(
        [w_rows[:, ORIG_OFFS[s]:ORIG_OFFS[s + 1]] for s in NEW_ORDER]
        + [jnp.zeros((depth, DP - ORIG_OFFS[-1], D_MODEL), w_in.dtype)], axis=1).astype(BF16)

    xp = x_prompt.reshape(b * t, D_MODEL)
    xs = jnp.pad(x_sample, ((0, 0), (0, T_PAD - n_t), (0, 0))).reshape(nb * T_PAD, D_MODEL)
    outs = {k: [] for k in ('p_kv', 'p_dkv', 'p_win', 'p_re', 'p_im', 's_kv', 's_dkv', 's_win', 's_re', 's_im')}
    for li in range(depth):
        w = _layer_weights(li, norm_w, w_in, w_out, w_cmp1, pos_cmp, w_cmp2, ssm_lambda_re, ssm_lambda_im,
                           ssm_log_dt, ssm_b_re, ssm_b_im, ssm_c_re, ssm_c_im, ssm_d, w_glu,
                           lam_q1, lam_k1, lam_q2, lam_k2, diff_ln_w)
        final = li == depth - 1
        pp = _norm_project(xp, w['nw'], w_in_t, li)
        p3 = pp.reshape(b, t, DP)
        pos_c = _compress_pos_const(w['pos_a'], w['pos_b'], w['wab'])
        cmp_tok = _compress_prompt(p3, w['wab'], pos_c, w['w2bd'])
        nsa_o = _nsa_prompt(p3, cmp_tok, ovt_p, e_p)
        ssm_o, hr, hi = _ssm_prompt(p3, w['ssm'])
        diff_o = _diff_prompt(p3, w['lam4'], w['lnw'], w['lam_init'])
        xp = _merge_out(nsa_o.reshape(b * t, -1), ssm_o.reshape(b * t, -1), diff_o.reshape(b * t, -1), xp,
                        w['w_out'], fw, final)
        kv = p3[:, :, C_KV:C_KV + 6 * LANES]
        outs['p_kv'].append(kv[:, :, :4 * LANES].reshape(b, t, 4, NSA_KV_HEADS, HEAD_DIM))
        outs['p_dkv'].append(p3[:, :, C_DK:C_DK + 2 * DIFF_WIDTH].reshape(b, t, 2, DIFF_HEADS, DIFF_V_DIM))
        outs['p_win'].append(kv[:, t - min(WINDOW, t):, 4 * LANES:].reshape(b, -1, 2, NSA_KV_HEADS, HEAD_DIM))
        outs['p_re'].append(hr.reshape(b, SSM_GROUPS, SSM_STATE))
        outs['p_im'].append(hi.reshape(b, SSM_GROUPS, SSM_STATE))
        ps = _norm_project(xs, w['nw'], w_in_t, li)
        nsa_s = _nsa_sample(ps, nsa_cache_t, li, win_t, page_table,
                            w['wab'], pos_c, w['w2bd'], ovt_s, e_s, n_t)
        ps_t = ps.reshape(nb, T_PAD, DP)[:, :n_t].transpose(1, 0, 2)
        ssm_t, shr, shi = _ssm_sample(ps_t[:, :, C_U:C_U + SSM_WIDTH], ps_t[:, :, C_ZS:C_ZS + SSM_WIDTH],
                                      state_ssm_re[li].reshape(nb, N_STATE),
                                      state_ssm_im[li].reshape(nb, N_STATE), w['ssm'])
        ssm_s = jnp.pad(ssm_t.transpose(1, 0, 2), ((0, 0), (0, T_PAD - n_t), (0, 0))).reshape(nb * T_PAD, SSM_WIDTH)
        diff_s = _diff_sample(ps, diff_cache_r, li, page_table, w['lam4'], w['lnw'], w['lam_init'])
        xs = _merge_out(nsa_s, ssm_s, diff_s, xs, w['w_out'], fw, final)
        ps3 = ps.reshape(nb, T_PAD, DP)[:, :n_t]
        skv = ps3[:, :, C_KV:C_KV + 6 * LANES]
        outs['s_kv'].append(skv[:, :, :4 * LANES].reshape(nb, n_t, 4, NSA_KV_HEADS, HEAD_DIM))
        outs['s_dkv'].append(ps3[:, :, C_DK:C_DK + 2 * DIFF_WIDTH].reshape(nb, n_t, 2, DIFF_HEADS, DIFF_V_DIM))
        outs['s_win'].append(skv[:, :, 4 * LANES:].transpose(0, 2, 1))
        outs['s_re'].append(shr.reshape(nb, SSM_GROUPS, SSM_STATE))
        outs['s_im'].append(shi.reshape(nb, SSM_GROUPS, SSM_STATE))
    y_prompt = xp.reshape(b, t, D_MODEL)
    y_sample = xs.reshape(nb, T_PAD, D_MODEL)[:, :n_t]
    st = lambda k: jnp.stack(outs[k])
    s_win_t = jnp.concatenate([win_t[:, :, :, n_t:], st('s_win')], axis=3)
    s_win = s_win_t.reshape(depth, nb, 2, NSA_KV_HEADS, HEAD_DIM, wb).transpose(0, 1, 5, 2, 3, 4)
    return (y_prompt, y_sample, st('p_kv'), st('p_dkv'), st('p_win'), st('p_re'), st('p_im'),
            st('s_kv'), st('s_dkv'), s_win, st('s_re'), st('s_im'))
```

```python
import functools
import math

import numpy as np
import jax
import jax.numpy as jnp
from jax import lax
from jax.experimental import pallas as pl
from jax.experimental.pallas import tpu as pltpu

F32 = jnp.float32
BF16 = jnp.bfloat16
I32 = jnp.int32

D_MODEL = 2048
HEAD_DIM = 64
NSA_WIDTH = 512
NSA_HEADS = 8
NSA_KV_HEADS = 2
NSA_GROUP = 4
CMP_BLOCK = 32
CMP_STRIDE = 16
SEL_BLOCK = 64
N_SELECT = 16
WINDOW = 512
SSM_WIDTH = 1024
SSM_GROUP_CH = 16
SSM_GROUPS = 64
SSM_STATE = 64
N_STATE = SSM_GROUPS * SSM_STATE
DIFF_WIDTH = 512
DIFF_HEADS = 4
DIFF_HEAD_DIM = 64
DIFF_V_DIM = 128
DIFF_SLABS = 2 * DIFF_HEADS
PAGE_SIZE = 128
SCALE = HEAD_DIM ** -0.5
DIFF_SCALE = DIFF_HEAD_DIM ** -0.5
NEG = -1e30
BIG = 1e9
EPS = 1e-6
NSA_SLOPES = tuple(float(2.0 ** (-8.0 * (k + 1) / NSA_HEADS)) for k in range(NSA_HEADS))
DIFF_SLOPES = tuple(float(2.0 ** (-8.0 * (k + 1) / DIFF_HEADS)) for k in range(DIFF_HEADS))

LANES = 128
SUBLANES = 8
VMEM_LIMIT = 56 * 1024 * 1024

ORIG_SIZES = (NSA_WIDTH, 6 * NSA_KV_HEADS * HEAD_DIM, 3 * NSA_HEADS, NSA_WIDTH, SSM_WIDTH, SSM_WIDTH,
              DIFF_HEADS * 2 * DIFF_HEAD_DIM, DIFF_WIDTH, DIFF_WIDTH, DIFF_WIDTH)
ORIG_OFFS = tuple(int(v) for v in np.concatenate([[0], np.cumsum(ORIG_SIZES)]))
NEW_ORDER = (0, 3, 6, 7, 8, 9, 4, 5, 1, 2)
C_Q, C_ZN, C_DQ, C_DK, C_DV, C_ZD, C_U, C_ZS, C_KV, C_G = 0, 512, 1024, 1536, 2048, 2560, 3072, 4096, 5120, 5888
DP = 6144
N_CMP_PAD = 128
NSEL_PAD = 40
SEL_CHUNK = 256
T_PAD = 8
NSA_DEC_SEQS = 2


def _cparams(sem):
    return pltpu.CompilerParams(dimension_semantics=sem, vmem_limit_bytes=VMEM_LIMIT)


def _dot(a, b):
    return jnp.dot(a, b, preferred_element_type=F32)


def _dot_nt(a, b):
    return lax.dot_general(a, b, (((1,), (1,)), ((), ())), preferred_element_type=F32)


def _gelu(x):
    return 0.5 * x * (1.0 + jnp.tanh(math.sqrt(2.0 / math.pi) * (x + 0.044715 * (x * x * x))))


def _sigmoid(x):
    return 1.0 / (1.0 + jnp.exp(-x))


def _silu(x):
    return x * _sigmoid(x)


def _msoftmax(s, mask):
    s = jnp.where(mask, s, NEG)
    m = jnp.max(s, axis=-1, keepdims=True)
    e = jnp.exp(s - m)
    den = jnp.sum(e, axis=-1, keepdims=True)
    return jnp.where(mask, e * (1.0 / den), 0.0)


PROJ_TN = 512


def _proj_body(x_ref, nw_ref, w_ref, o_ref, dkv_ref, xn_ref):
    j = pl.program_id(1)
    tm = x_ref.shape[0]

    @pl.when(j == 0)
    def _():
        x = x_ref[...]
        ms = jnp.mean(x * x, axis=-1, keepdims=True)
        xn_ref[...] = (x * lax.rsqrt(ms + EPS) * nw_ref[...]).astype(BF16)

    acc = _dot_nt(xn_ref[...], w_ref[...])
    o_ref[...] = acc

    for col0, slab0 in ((C_DK, 0), (C_DV, DIFF_HEADS)):
        @pl.when(j == col0 // PROJ_TN)
        def _(slab0=slab0):
            for c in range(PROJ_TN // LANES):
                dkv_ref[pl.ds(slab0 + c, tm, stride=DIFF_SLABS), :] = acc[:, c * LANES:(c + 1) * LANES]


def _norm_project(x2d, nw_row, w_t, li):
    n = x2d.shape[0]
    tm = min(n, 1024)
    tn = PROJ_TN
    return pl.pallas_call(
        _proj_body,
        grid=(n // tm, DP // tn),
        in_specs=[pl.BlockSpec((tm, D_MODEL), lambda i, j: (i, 0)),
                  pl.BlockSpec((1, D_MODEL), lambda i, j: (0, 0)),
                  pl.BlockSpec((None, tn, D_MODEL), lambda i, j: (li, j, 0))],
        out_specs=[pl.BlockSpec((tm, tn), lambda i, j: (i, j)),
                   pl.BlockSpec((tm * DIFF_SLABS, LANES), lambda i, j: (i, 0))],
        out_shape=[jax.ShapeDtypeStruct((n, DP), F32),
                   jax.ShapeDtypeStruct((n * DIFF_SLABS, LANES), F32)],
        scratch_shapes=[pltpu.VMEM((tm, D_MODEL), BF16)],
        compiler_params=_cparams(("parallel", "arbitrary")),
    )(x2d, nw_row, w_t)


def _out_body(nsa_ref, ssm_ref, diff_ref, x_ref, w_ref, fw_ref, y_ref, *, final):
    acc = x_ref[...]
    acc = acc + _dot(nsa_ref[...].astype(BF16), w_ref[0:NSA_WIDTH, :])
    acc = acc + _dot(ssm_ref[...].astype(BF16), w_ref[NSA_WIDTH:NSA_WIDTH + SSM_WIDTH, :])
    acc = acc + _dot(diff_ref[...].astype(BF16), w_ref[NSA_WIDTH + SSM_WIDTH:, :])
    if final:
        ms = jnp.mean(acc * acc, axis=-1, keepdims=True)
        acc = acc * lax.rsqrt(ms + EPS) * fw_ref[...]
    y_ref[...] = acc


def _merge_out(nsa_o, ssm_o, diff_o, x2d, w_out_bf, fw_row, final):
    n = x2d.shape[0]
    tm = min(n, 512)
    return pl.pallas_call(
        functools.partial(_out_body, final=final),
        grid=(n // tm,),
        in_specs=[pl.BlockSpec((tm, NSA_WIDTH), lambda i: (i, 0)),
                  pl.BlockSpec((tm, SSM_WIDTH), lambda i: (i, 0)),
                  pl.BlockSpec((tm, DIFF_WIDTH), lambda i: (i, 0)),
                  pl.BlockSpec((tm, D_MODEL), lambda i: (i, 0)),
                  pl.BlockSpec((D_MODEL, D_MODEL), lambda i: (0, 0)),
                  pl.BlockSpec((1, D_MODEL), lambda i: (0, 0))],
        out_specs=pl.BlockSpec((tm, D_MODEL), lambda i: (i, 0)),
        out_shape=jax.ShapeDtypeStruct((n, D_MODEL), F32),
        compiler_params=_cparams(("parallel",)),
    )(nsa_o, ssm_o, diff_o, x2d, w_out_bf, fw_row)


def _pad_heads(qb):
    tq = qb.shape[0]
    low = lax.broadcasted_iota(I32, (tq, LANES), 1) < HEAD_DIM
    outs = []
    for hd in range(NSA_HEADS):
        h = hd // NSA_GROUP
        slab = qb[:, (hd // 2) * LANES:(hd // 2 + 1) * LANES]
        if hd % 2 != h:
            slab = pltpu.roll(slab, HEAD_DIM, 1)
        keep = low if h == 0 else jnp.logical_not(low)
        outs.append(jnp.where(keep, slab, 0.0).astype(BF16))
    return outs


def _compress_tokens(xs, wab, pc, w2bd):
    ab = _dot(jnp.concatenate(xs, axis=0), wab)
    pre = []
    for s in range(len(xs)):
        rows = slice(s * N_CMP_PAD, (s + 1) * N_CMP_PAD)
        pre.append(ab[rows, 0:LANES] + pltpu.roll(ab[rows, LANES:2 * LANES], N_CMP_PAD - 1, 0) + pc[0:1])
    tok = _dot(_gelu(jnp.concatenate(pre, axis=0)).astype(BF16), w2bd).astype(BF16)
    return [tok[s * N_CMP_PAD:(s + 1) * N_CMP_PAD] for s in range(len(xs))]


def _pos_const_body(pa_ref, pb_ref, wab_ref, o_ref):
    wab = wab_ref[...]
    o_ref[...] = _dot(pa_ref[...], wab)[:, 0:LANES] + _dot(pb_ref[...], wab)[:, LANES:2 * LANES]


def _compress_pos_const(pos_a, pos_b, wab):
    return pl.pallas_call(
        _pos_const_body,
        grid=(2,),
        in_specs=[pl.BlockSpec((None, SUBLANES, CMP_STRIDE * LANES), lambda k: (k, 0, 0)),
                  pl.BlockSpec((None, SUBLANES, CMP_STRIDE * LANES), lambda k: (k, 0, 0)),
                  pl.BlockSpec((None, CMP_STRIDE * LANES, 2 * LANES), lambda k: (k, 0, 0))],
        out_specs=pl.BlockSpec((None, SUBLANES, LANES), lambda k: (k, 0, 0)),
        out_shape=jax.ShapeDtypeStruct((2, SUBLANES, LANES), F32),
        compiler_params=_cparams(("parallel",)),
    )(pos_a, pos_b, wab)


def _cmp_branch(qh, kc, vc, qpos_col, values_on_rows=False):
    tq = qpos_col.shape[0]
    n_iota = lax.broadcasted_iota(I32, (1, N_CMP_PAD), 1)
    ends = n_iota * CMP_STRIDE + (CMP_BLOCK - 1)
    q_all = jnp.concatenate(qh, axis=0)
    slope_col = jnp.concatenate([jnp.full((tq, 1), NSA_SLOPES[hd], F32) for hd in range(NSA_HEADS)], axis=0)
    qpos_all = jnp.concatenate([qpos_col] * NSA_HEADS, axis=0)
    s = _dot_nt(q_all, kc) + slope_col * ends.astype(F32)
    p = _msoftmax(s, ends <= qpos_all)
    rows = [slice(hd * tq, (hd + 1) * tq) for hd in range(NSA_HEADS)]
    if values_on_rows:
        return _dot_nt(vc, p.astype(BF16)), [p[r] for r in rows]
    o = _dot(p.astype(BF16), vc)
    return [o[r] for r in rows], [p[r] for r in rows]


def _select_blocks(psum, qpos_row, ovt, n_sel, queries_on_rows=True):
    hi = psum.astype(BF16)
    r1 = psum - hi.astype(F32)
    mid = r1.astype(BF16)
    lo = (r1 - mid.astype(F32)).astype(BF16)
    sc = _dot_nt(ovt, hi) + _dot_nt(ovt, mid) + _dot_nt(ovt, lo)
    j = lax.broadcasted_iota(I32, (NSEL_PAD, LANES), 0)
    cur = qpos_row // SEL_BLOCK
    forced = (j == 0) | (j == cur) | (j == cur - 1)
    avail = j * SEL_BLOCK <= qpos_row
    sc = jnp.where(forced, BIG, jnp.where(avail, sc, -BIG))
    cnt = jnp.zeros((NSEL_PAD, LANES), I32)
    for ii in range(n_sel):
        row = sc[ii:ii + 1, :]
        beats = (row > sc) | ((row == sc) & (ii < j))
        cnt = cnt + beats.astype(I32)
    k_top = min(N_SELECT, n_sel)
    sel_t = jnp.where((cnt < k_top) & (j < n_sel), 1.0, 0.0).astype(F32)
    sel_t = jnp.concatenate([sel_t, jnp.zeros((LANES - NSEL_PAD, LANES), F32)], axis=0)
    return sel_t.T if queries_on_rows else sel_t


def _combine_heads(gs, o_cmp, o_sel, o_win, z):
    tq = z.shape[0]
    low = lax.broadcasted_iota(I32, (tq, LANES), 1) < HEAD_DIM
    placed = []
    for hd in range(NSA_HEADS):
        h = hd // NSA_GROUP
        o = (gs[:, hd:hd + 1] * o_cmp[hd] + gs[:, NSA_HEADS + hd:NSA_HEADS + hd + 1] * o_sel[hd]
             + gs[:, 2 * NSA_HEADS + hd:2 * NSA_HEADS + hd + 1] * o_win[hd])
        if hd % 2 != h:
            o = pltpu.roll(o, HEAD_DIM, 1)
        placed.append(o)
    slabs = [jnp.where(low, placed[2 * k], placed[2 * k + 1]) for k in range(NSA_HEADS // 2)]
    return jnp.concatenate(slabs, axis=1) * _silu(z)


def _compress_prompt_body(rows_ref, wab_ref, pc_ref, w2_ref, o_ref):
    xs = [rows_ref[pl.ds(l, N_CMP_PAD, stride=CMP_STRIDE), :].astype(BF16) for l in range(CMP_STRIDE)]
    x = jnp.concatenate(xs, axis=1)
    o_ref[...] = _compress_tokens([x], wab_ref[...], pc_ref[...], w2_ref[...])[0]


def _compress_prompt(p3, wab, pc, w2bd):
    b, t, _ = p3.shape
    kvblk = C_KV // LANES
    return pl.pallas_call(
        _compress_prompt_body,
        grid=(b, 2),
        in_specs=[pl.BlockSpec((None, t, LANES), lambda i, k: (i, 0, kvblk + k)),
                  pl.BlockSpec((None, CMP_STRIDE * LANES, 2 * LANES), lambda i, k: (k, 0, 0)),
                  pl.BlockSpec((None, SUBLANES, LANES), lambda i, k: (k, 0, 0)),
                  pl.BlockSpec((None, LANES, LANES), lambda i, k: (k, 0, 0))],
        out_specs=pl.BlockSpec((None, None, N_CMP_PAD, LANES), lambda i, k: (i, k, 0, 0)),
        out_shape=jax.ShapeDtypeStruct((b, 2, N_CMP_PAD, LANES), BF16),
        compiler_params=_cparams(("parallel", "parallel")),
    )(p3, wab, pc, w2bd)


def _nsa_prompt_body(q_ref, z_ref, g_ref, cmp_ref, sk_ref, sv_ref, wk_ref, wv_ref, ovt_ref, et_ref, o_ref,
                     svt_s, wvt_s, acc_s, *, n_sel):
    tq = 128
    i = pl.program_id(1)
    q0 = i * tq

    @pl.when(i == 0)
    def _():
        for c in range(svt_s.shape[0]):
            rows = slice(c * LANES, (c + 1) * LANES)
            svt_s[c] = sv_ref[rows, :].T.astype(BF16)
            wvt_s[c] = wv_ref[rows, :].T.astype(BF16)

    qh = _pad_heads(q_ref[...] * SCALE)
    q8 = jnp.concatenate(qh, axis=0)
    gs = _sigmoid(g_ref[...])
    qpos_col = q0 + lax.broadcasted_iota(I32, (tq, 1), 0)
    qpos_row = q0 + lax.broadcasted_iota(I32, (1, LANES), 1)

    w0 = pl.multiple_of(jnp.maximum(q0 - WINDOW, 0), LANES)
    wwid = WINDOW + tq
    kwin = wk_ref[pl.ds(w0, wwid), :].astype(BF16)
    stw = _dot_nt(kwin, q8)

    vc_t = cmp_ref[1].astype(F32).T.astype(BF16)
    ocmp_t, p_cmp = _cmp_branch(qh, cmp_ref[0], vc_t, qpos_col, values_on_rows=True)
    ovt = ovt_ref[...]
    kp_w = w0 + lax.broadcasted_iota(I32, (wwid, LANES), 0)
    dist = qpos_row - kp_w
    maskw = jnp.where((dist >= 0) & (dist < WINDOW), 0.0, NEG)
    kbw = kp_w.astype(F32)
    wc0 = w0 // LANES
    sub = SEL_CHUNK // LANES
    nch = (i + 2 * sub) // (2 * sub)
    hq = NSA_HEADS * tq
    cols = [slice(hd * tq, (hd + 1) * tq) for hd in range(NSA_HEADS)]

    sel_t = []
    for h in range(NSA_KV_HEADS):
        psum = p_cmp[h * NSA_GROUP]
        for hd in range(h * NSA_GROUP + 1, (h + 1) * NSA_GROUP):
            psum = psum + p_cmp[hd]
        sel_t.append(_select_blocks(psum, qpos_row, ovt, n_sel, queries_on_rows=False).astype(BF16))

    pws, linv = [], []
    for hd in range(NSA_HEADS):
        s = stw[:, cols[hd]] + NSA_SLOPES[hd] * kbw + maskw
        p = jnp.exp(s - jnp.max(s, axis=0, keepdims=True))
        linv.append(1.0 / jnp.sum(p, axis=0, keepdims=True))
        pws.append(p.astype(BF16))
    ptw = jnp.concatenate(pws, axis=1)
    owin_t = _dot(wvt_s[wc0], ptw[0:LANES])
    for k in range(1, wwid // LANES):
        owin_t = owin_t + _dot(wvt_s[wc0 + k], ptw[k * LANES:(k + 1) * LANES])
    owin_t = owin_t * jnp.concatenate(linv, axis=1)

    acc_s[...] = jnp.zeros_like(acc_s)

    def scores(c):
        k0 = pl.multiple_of(c * SEL_CHUNK, SEL_CHUNK)
        kch = sk_ref[pl.ds(k0, SEL_CHUNK), :].astype(BF16)
        et = et_ref[pl.ds(k0, SEL_CHUNK), :]
        st = _dot_nt(kch, q8)
        mexp = [_dot(et, sel_t[h]) for h in range(NSA_KV_HEADS)]
        return st, mexp

    def update(c, st, mexp, m, l):
        k0 = c * SEL_CHUNK
        kp = k0 + lax.broadcasted_iota(I32, (SEL_CHUNK, LANES), 0)
        causal = kp <= qpos_row
        kbf = kp.astype(F32)
        ms, ls, alphas, ps = [], [], [], []
        for hd in range(NSA_HEADS):
            if hd % NSA_GROUP == 0:
                maskb = jnp.where((mexp[hd // NSA_GROUP] > 0.5) & causal, 0.0, NEG)
            s = st[:, cols[hd]] + NSA_SLOPES[hd] * kbf + maskb
            m_new = jnp.maximum(m[:, cols[hd]], jnp.max(s, axis=0, keepdims=True))
            alpha = jnp.exp(m[:, cols[hd]] - m_new)
            p = jnp.exp(s - m_new)
            ms.append(m_new)
            ls.append(alpha * l[:, cols[hd]] + jnp.sum(p, axis=0, keepdims=True))
            alphas.append(alpha)
            ps.append(p.astype(BF16))
        pt = jnp.concatenate(ps, axis=1)
        ot = _dot(svt_s[c * sub], pt[0:LANES])
        for k in range(1, sub):
            ot = ot + _dot(svt_s[c * sub + k], pt[k * LANES:(k + 1) * LANES])
        acc_s[...] = acc_s[...] * jnp.concatenate(alphas, axis=1) + ot
        return jnp.concatenate(ms, axis=1), jnp.concatenate(ls, axis=1)

    def body(c2, carry):
        m, l = carry
        a = scores(2 * c2)
        b = scores(2 * c2 + 1)
        m, l = update(2 * c2, a[0], a[1], m, l)
        return update(2 * c2 + 1, b[0], b[1], m, l)

    init = (jnp.full((1, hq), NEG, F32), jnp.zeros((1, hq), F32))
    _, l = lax.fori_loop(0, nch, body, init)
    osel_t = acc_s[...] * jnp.where(l > 0.0, 1.0 / l, 0.0)

    gs_t = gs.T
    pieces = []
    for hd in range(NSA_HEADS):
        h = hd // NSA_GROUP
        o_t = (gs_t[hd:hd + 1] * ocmp_t[:, cols[hd]]
               + gs_t[NSA_HEADS + hd:NSA_HEADS + hd + 1] * osel_t[:, cols[hd]]
               + gs_t[2 * NSA_HEADS + hd:2 * NSA_HEADS + hd + 1] * owin_t[:, cols[hd]])
        pieces.append(o_t[h * HEAD_DIM:(h + 1) * HEAD_DIM])
    out_t = jnp.concatenate(pieces, axis=0)
    out = jnp.concatenate([out_t[k * LANES:(k + 1) * LANES].T for k in range(NSA_WIDTH // LANES)], axis=1)
    o_ref[...] = (out * _silu(z_ref[...])).astype(o_ref.dtype)


def _nsa_prompt(p3, cmp_tok, ovt, e3):
    b, t, _ = p3.shape
    tq = 128
    n_sel = -(-t // SEL_BLOCK)
    kvb = C_KV // LANES
    return pl.pallas_call(
        functools.partial(_nsa_prompt_body, n_sel=n_sel),
        grid=(b, t // tq),
        scratch_shapes=[pltpu.VMEM((t // LANES, LANES, LANES), BF16),
                        pltpu.VMEM((t // LANES, LANES, LANES), BF16),
                        pltpu.VMEM((LANES, NSA_HEADS * tq), F32)],
        in_specs=[pl.BlockSpec((None, tq, NSA_WIDTH), lambda bi, i: (bi, i, C_Q // NSA_WIDTH)),
                  pl.BlockSpec((None, tq, NSA_WIDTH), lambda bi, i: (bi, i, C_ZN // NSA_WIDTH)),
                  pl.BlockSpec((None, tq, LANES), lambda bi, i: (bi, i, C_G // LANES)),
                  pl.BlockSpec((None, 2, N_CMP_PAD, LANES), lambda bi, i: (bi, 0, 0, 0)),
                  pl.BlockSpec((None, t, LANES), lambda bi, i: (bi, 0, kvb + 2)),
                  pl.BlockSpec((None, t, LANES), lambda bi, i: (bi, 0, kvb + 3)),
                  pl.BlockSpec((None, t, LANES), lambda bi, i: (bi, 0, kvb + 4)),
                  pl.BlockSpec((None, t, LANES), lambda bi, i: (bi, 0, kvb + 5)),
                  pl.BlockSpec((NSEL_PAD, LANES), lambda bi, i: (0, 0)),
                  pl.BlockSpec(e3.shape, lambda bi, i: (0, 0))],
        out_specs=pl.BlockSpec((None, tq, NSA_WIDTH), lambda bi, i: (bi, i, 0)),
        out_shape=jax.ShapeDtypeStruct((b, t, NSA_WIDTH), BF16),
        compiler_params=_cparams(("parallel", "arbitrary")),
    )(p3, p3, p3, cmp_tok, p3, p3, p3, p3, ovt, e3)


def _diff_lambda(lam_ref, lam_init):
    a = lam_ref[...]
    s1 = jnp.sum(a[0:1] * a[1:2], axis=-1, keepdims=True)
    s2 = jnp.sum(a[2:3] * a[3:4], axis=-1, keepdims=True)
    return jnp.exp(s1) - jnp.exp(s2) + lam_init


def _diff_finish(o, lnw, z, lam_init):
    ms = jnp.mean(o * o, axis=-1, keepdims=True)
    return o * lax.rsqrt(ms + EPS) * lnw * (1.0 - lam_init) * _silu(z)


def _diff_prompt_body(lam_ref, q_ref, z_ref, k_ref, v_ref, lnw_ref, o_ref, vt_s, acc_s, *, lam_init):
    tq = 128
    i = pl.program_id(1)
    q0 = i * tq

    @pl.when(i == 0)
    def _():
        for hh in range(DIFF_HEADS):
            for c in range(vt_s.shape[1]):
                vt_s[hh, c] = v_ref[c * LANES:(c + 1) * LANES, hh * LANES:(hh + 1) * LANES].T.astype(BF16)

    lam = _diff_lambda(lam_ref, lam_init)
    low = lax.broadcasted_iota(I32, (tq, LANES), 1) < DIFF_HEAD_DIM
    qpos_row = q0 + lax.broadcasted_iota(I32, (1, LANES), 1)
    sub = SEL_CHUNK // LANES
    nch = (i + 2 * sub) // (2 * sub)
    q2 = []
    for hh in range(DIFF_HEADS):
        qs = q_ref[:, hh * LANES:(hh + 1) * LANES] * DIFF_SCALE
        q2.append(jnp.concatenate([jnp.where(low, qs, 0.0), jnp.where(low, 0.0, qs)], axis=0).astype(BF16))
    acc_s[...] = jnp.zeros_like(acc_s)
    ncol = 2 * tq

    def scores(c):
        k0 = pl.multiple_of(c * SEL_CHUNK, SEL_CHUNK)
        return [_dot_nt(k_ref[pl.ds(k0, SEL_CHUNK), hh * LANES:(hh + 1) * LANES].astype(BF16), q2[hh])
                for hh in range(DIFF_HEADS)]

    def update(c, sts, m, l):
        kp = c * SEL_CHUNK + lax.broadcasted_iota(I32, (SEL_CHUNK, LANES), 0)
        maskb = jnp.where(kp <= qpos_row, 0.0, NEG)
        kbf = kp.astype(F32)
        ms, ls = [], []
        for hh in range(DIFF_HEADS):
            st = sts[hh]
            bias = DIFF_SLOPES[hh] * kbf + maskb
            alphas, ps = [], []
            for comp in range(2):
                cs = slice(comp * tq, (comp + 1) * tq)
                gs_ = slice(hh * ncol + comp * tq, hh * ncol + (comp + 1) * tq)
                s = st[:, cs] + bias
                m_new = jnp.maximum(m[:, gs_], jnp.max(s, axis=0, keepdims=True))
                alpha = jnp.exp(m[:, gs_] - m_new)
                p = jnp.exp(s - m_new)
                ms.append(m_new)
                ls.append(alpha * l[:, gs_] + jnp.sum(p, axis=0, keepdims=True))
                alphas.append(alpha)
                ps.append(p.astype(BF16))
            pt = jnp.concatenate(ps, axis=1)
            ot = _dot(vt_s[hh, c * sub], pt[0:LANES])
            for k in range(1, sub):
                ot = ot + _dot(vt_s[hh, c * sub + k], pt[k * LANES:(k + 1) * LANES])
            acc_s[hh] = acc_s[hh] * jnp.concatenate(alphas, axis=1) + ot
        return jnp.concatenate(ms, axis=1), jnp.concatenate(ls, axis=1)

    def body(c2, carry):
        m, l = carry
        a = scores(2 * c2)
        b = scores(2 * c2 + 1)
        m, l = update(2 * c2, a, m, l)
        return update(2 * c2 + 1, b, m, l)

    init = (jnp.full((1, DIFF_HEADS * ncol), NEG, F32), jnp.zeros((1, DIFF_HEADS * ncol), F32))
    _, l = lax.fori_loop(0, nch, body, init)
    lnw = lnw_ref[...]
    slabs = []
    for hh in range(DIFF_HEADS):
        on = acc_s[hh] * (1.0 / l[:, hh * ncol:(hh + 1) * ncol])
        o = (on[:, 0:tq] - lam * on[:, tq:2 * tq]).T
        slabs.append(_diff_finish(o, lnw, z_ref[:, hh * LANES:(hh + 1) * LANES], lam_init))
    o_ref[...] = jnp.concatenate(slabs, axis=1).astype(o_ref.dtype)


def _diff_prompt(p3, lam4, lnw_row, lam_init):
    b, t, _ = p3.shape
    tq = 128
    return pl.pallas_call(
        functools.partial(_diff_prompt_body, lam_init=lam_init),
        grid=(b, t // tq),
        in_specs=[pl.BlockSpec((4, DIFF_HEAD_DIM), lambda bi, i: (0, 0)),
                  pl.BlockSpec((None, tq, DIFF_WIDTH), lambda bi, i: (bi, i, C_DQ // DIFF_WIDTH)),
                  pl.BlockSpec((None, tq, DIFF_WIDTH), lambda bi, i: (bi, i, C_ZD // DIFF_WIDTH)),
                  pl.BlockSpec((None, t, DIFF_WIDTH), lambda bi, i: (bi, 0, C_DK // DIFF_WIDTH)),
                  pl.BlockSpec((None, t, DIFF_WIDTH), lambda bi, i: (bi, 0, C_DV // DIFF_WIDTH)),
                  pl.BlockSpec((1, DIFF_V_DIM), lambda bi, i: (0, 0))],
        out_specs=pl.BlockSpec((None, tq, DIFF_WIDTH), lambda bi, i: (bi, i, 0)),
        out_shape=jax.ShapeDtypeStruct((b, t, DIFF_WIDTH), BF16),
        scratch_shapes=[pltpu.VMEM((DIFF_HEADS, t // LANES, LANES, LANES), BF16),
                        pltpu.VMEM((DIFF_HEADS, DIFF_V_DIM, 2 * tq), F32)],
        compiler_params=_cparams(("parallel", "arbitrary")),
    )(lam4, p3, p3, p3, p3, lnw_row)


SSM_BLOCKS = SSM_WIDTH // LANES
SSM_BS = N_STATE // SSM_BLOCKS
SSM_CHUNK = 256
SSM_SEG = SSM_CHUNK // SUBLANES
SSM_UNROLL = 32


def _discretize(lre, lim, ldt):
    dt = jnp.exp(ldt)
    mag = jnp.exp(lre * dt)
    ar = mag * jnp.cos(lim * dt)
    ai = mag * jnp.sin(lim * dt)
    den = lre * lre + lim * lim
    qr = ((ar - 1.0) * lre + ai * lim) / den
    qi = (ai * lre - (ar - 1.0) * lim) / den
    return ar, ai, qr, qi


def _bbar_block(qr, qi, bre, bim):
    return jnp.concatenate([qr * bre - qi * bim, qr * bim + qi * bre], axis=1).astype(BF16)


def _glu_tail(y, wglu, z):
    gate = _sigmoid(_dot(y.astype(BF16), wglu))
    return y * gate * _silu(z)


def _ssm_prompt_body(u_ref, z_ref, lre_ref, lim_ref, ldt_ref, bre_ref, bim_ref, cmat_ref, d_ref, wglu_ref,
                     perm_ref, o_ref, hr_ref, hi_ref,
                     bbar_s, apr_s, api_s, bu_s, y_s, carr_s, cari_s):
    first = (pl.program_id(0) == 0) & (pl.program_id(1) == 0)

    @pl.when(first)
    def _():
        ar, ai, qr, qi = _discretize(lre_ref[...], lim_ref[...], ldt_ref[...])
        for k in range(SSM_BLOCKS):
            cs = slice(k * SSM_BS, (k + 1) * SSM_BS)
            bbar_s[k] = _bbar_block(qr[:, cs], qi[:, cs], bre_ref[k], bim_ref[k])
        pr, pi = ar, ai
        apr_s[0:1, :] = pr
        api_s[0:1, :] = pi
        for k in range(1, SSM_SEG):
            pr, pi = pr * ar - pi * ai, pr * ai + pi * ar
            apr_s[k:k + 1, :] = pr
            api_s[k:k + 1, :] = pi

    @pl.when(pl.program_id(1) == 0)
    def _():
        carr_s[...] = jnp.zeros_like(carr_s)
        cari_s[...] = jnp.zeros_like(cari_s)

    re_c = slice(0, SSM_BS)
    im_c = slice(SSM_BS, 2 * SSM_BS)
    u_perm = _dot(perm_ref[0], u_ref[...].astype(BF16)).astype(BF16)
    unperm = perm_ref[1]

    for k in range(SSM_BLOCKS):
        cs = slice(k * SSM_BS, (k + 1) * SSM_BS)
        us = slice(k * LANES, (k + 1) * LANES)
        bu_s[...] = _dot(u_perm[:, us], bbar_s[k])
        ar_b = jnp.broadcast_to(apr_s[0:1, cs], (SUBLANES, SSM_BS))
        ai_b = jnp.broadcast_to(api_s[0:1, cs], (SUBLANES, SSM_BS))

        def step(t, st, ar_b=ar_b, ai_b=ai_b):
            hr, hi = st
            rows = pl.ds(pl.multiple_of(t * SUBLANES, SUBLANES), SUBLANES)
            nr = ar_b * hr - ai_b * hi + bu_s[rows, re_c]
            ni = ar_b * hi + ai_b * hr + bu_s[rows, im_c]
            bu_s[rows, re_c] = nr
            bu_s[rows, im_c] = ni
            return nr, ni

        zero = jnp.zeros((SUBLANES, SSM_BS), F32)
        er, ei = lax.fori_loop(0, SSM_SEG, step, (zero, zero), unroll=SSM_UNROLL)
        a32r = apr_s[SSM_SEG - 1:SSM_SEG, cs]
        a32i = api_s[SSM_SEG - 1:SSM_SEG, cs]
        rows_r = [carr_s[:, cs]]
        rows_i = [cari_s[:, cs]]
        for s in range(1, SUBLANES + 1):
            pr, pi = rows_r[-1], rows_i[-1]
            rows_r.append(er[s - 1:s] + a32r * pr - a32i * pi)
            rows_i.append(ei[s - 1:s] + a32r * pi + a32i * pr)
        carr_s[:, cs] = rows_r[SUBLANES]
        cari_s[:, cs] = rows_i[SUBLANES]
        hin_r = jnp.concatenate(rows_r[:SUBLANES], axis=0)
        hin_i = jnp.concatenate(rows_i[:SUBLANES], axis=0)

        def fix(t, carry, hin_r=hin_r, hin_i=hin_i, cs=cs):
            rows = pl.ds(pl.multiple_of(t * SUBLANES, SUBLANES), SUBLANES)
            pr = apr_s[pl.ds(t, 1), cs]
            pi = api_s[pl.ds(t, 1), cs]
            bu_s[rows, re_c] = bu_s[rows, re_c] + pr * hin_r - pi * hin_i
            bu_s[rows, im_c] = bu_s[rows, im_c] + pr * hin_i + pi * hin_r
            return carry

        lax.fori_loop(0, SSM_SEG, fix, 0, unroll=SSM_UNROLL)
        yp = _dot(bu_s[...].astype(BF16), cmat_ref[k])
        y_hi = yp.astype(BF16)
        r1 = yp - y_hi.astype(F32)
        y_mid = r1.astype(BF16)
        y_lo = (r1 - y_mid.astype(F32)).astype(BF16)
        yk = _dot(unperm, y_hi) + _dot(unperm, y_mid) + _dot(unperm, y_lo) + d_ref[:, us] * u_ref[:, us]
        y_s[:, us] = _gelu(yk)

    o_ref[...] = _glu_tail(y_s[...], wglu_ref[...], z_ref[...]).astype(o_ref.dtype)
    hr_ref[...] = carr_s[...]
    hi_ref[...] = cari_s[...]


def _ssm_prompt(p3, sp):
    b, t, _ = p3.shape
    full = lambda shape: pl.BlockSpec(shape, lambda bi, j: (0,) * len(shape))
    out, hr, hi = pl.pallas_call(
        _ssm_prompt_body,
        grid=(b, t // SSM_CHUNK),
        in_specs=[pl.BlockSpec((None, SSM_CHUNK, SSM_WIDTH), lambda bi, j: (bi, j, C_U // SSM_WIDTH)),
                  pl.BlockSpec((None, SSM_CHUNK, SSM_WIDTH), lambda bi, j: (bi, j, C_ZS // SSM_WIDTH)),
                  full((1, N_STATE)), full((1, N_STATE)), full((1, N_STATE)),
                  full((SSM_BLOCKS, LANES, SSM_BS)), full((SSM_BLOCKS, LANES, SSM_BS)),
                  full((SSM_BLOCKS, 2 * SSM_BS, LANES)), full((1, SSM_WIDTH)),
                  full((SSM_WIDTH, SSM_WIDTH)), full((2, SSM_CHUNK, SSM_CHUNK))],
        out_specs=[pl.BlockSpec((None, SSM_CHUNK, SSM_WIDTH), lambda bi, j: (bi, j, 0)),
                   pl.BlockSpec((None, 1, N_STATE), lambda bi, j: (bi, 0, 0)),
                   pl.BlockSpec((None, 1, N_STATE), lambda bi, j: (bi, 0, 0))],
        out_shape=[jax.ShapeDtypeStruct((b, t, SSM_WIDTH), BF16),
                   jax.ShapeDtypeStruct((b, 1, N_STATE), F32),
                   jax.ShapeDtypeStruct((b, 1, N_STATE), F32)],
        scratch_shapes=[pltpu.VMEM((SSM_BLOCKS, LANES, 2 * SSM_BS), BF16),
                        pltpu.VMEM((SSM_SEG, N_STATE), F32), pltpu.VMEM((SSM_SEG, N_STATE), F32),
                        pltpu.VMEM((SSM_CHUNK, 2 * SSM_BS), F32),
                        pltpu.VMEM((SSM_CHUNK, SSM_WIDTH), F32),
                        pltpu.VMEM((1, N_STATE), F32), pltpu.VMEM((1, N_STATE), F32)],
        compiler_params=_cparams(("arbitrary", "arbitrary")),
    )(p3, p3, sp['lre'], sp['lim'], sp['ldt'], sp['bre'], sp['bim'], sp['cmat'], sp['d'], sp['wglu'],
      _segment_permutation())
    return out, hr, hi


def _segment_permutation():
    r = np.arange(SSM_CHUNK)
    src = (r % SUBLANES) * SSM_SEG + r // SUBLANES
    pm = np.zeros((SSM_CHUNK, SSM_CHUNK), np.float32)
    pm[r, src] = 1.0
    return jnp.asarray(np.stack([pm, pm.T]), BF16)


def _ssm_sample_body(u_ref, z_ref, h0r_ref, h0i_ref, lre_ref, lim_ref, ldt_ref, bre_ref, bim_ref, cmat_ref,
                     d_ref, wglu_ref, o_ref, hr_ref, hi_ref, y_s, *, n_t):
    k = pl.program_id(0)
    ar, ai, qr, qi = _discretize(lre_ref[...], lim_ref[...], ldt_ref[...])
    bbar = _bbar_block(qr, qi, bre_ref[...], bim_ref[...])
    hr = h0r_ref[...]
    hi = h0i_ref[...]
    cmat = cmat_ref[...]
    for t in range(n_t):
        ut = u_ref[t]
        bu = _dot(ut.astype(BF16), bbar)
        hr, hi = ar * hr - ai * hi + bu[:, 0:SSM_BS], ar * hi + ai * hr + bu[:, SSM_BS:]
        hcat = jnp.concatenate([hr, hi], axis=1).astype(BF16)
        y_s[t, k] = _gelu(_dot(hcat, cmat) + d_ref[...] * ut)
    hr_ref[...] = hr
    hi_ref[...] = hi

    @pl.when(k == SSM_BLOCKS - 1)
    def _():
        wglu = wglu_ref[...]
        for t in range(n_t):
            y = jnp.concatenate([y_s[t, kk] for kk in range(SSM_BLOCKS)], axis=1)
            o_ref[t] = _glu_tail(y, wglu, z_ref[t])


def _ssm_sample(ut, zt, h0r, h0i, sp):
    n_t, nb, _ = ut.shape
    out, hr, hi = pl.pallas_call(
        functools.partial(_ssm_sample_body, n_t=n_t),
        grid=(SSM_BLOCKS,),
        in_specs=[pl.BlockSpec((n_t, nb, LANES), lambda k: (0, 0, k)),
                  pl.BlockSpec((n_t, nb, SSM_WIDTH), lambda k: (0, 0, 0)),
                  pl.BlockSpec((nb, SSM_BS), lambda k: (0, k)),
                  pl.BlockSpec((nb, SSM_BS), lambda k: (0, k)),
                  pl.BlockSpec((1, SSM_BS), lambda k: (0, k)),
                  pl.BlockSpec((1, SSM_BS), lambda k: (0, k)),
                  pl.BlockSpec((1, SSM_BS), lambda k: (0, k)),
                  pl.BlockSpec((None, LANES, SSM_BS), lambda k: (k, 0, 0)),
                  pl.BlockSpec((None, LANES, SSM_BS), lambda k: (k, 0, 0)),
                  pl.BlockSpec((None, 2 * SSM_BS, LANES), lambda k: (k, 0, 0)),
                  pl.BlockSpec((1, LANES), lambda k: (0, k)),
                  pl.BlockSpec((SSM_WIDTH, SSM_WIDTH), lambda k: (0, 0))],
        out_specs=[pl.BlockSpec((n_t, nb, SSM_WIDTH), lambda k: (0, 0, 0)),
                   pl.BlockSpec((nb, SSM_BS), lambda k: (0, k)),
                   pl.BlockSpec((nb, SSM_BS), lambda k: (0, k))],
        out_shape=[jax.ShapeDtypeStruct((n_t, nb, SSM_WIDTH), F32),
                   jax.ShapeDtypeStruct((nb, N_STATE), F32),
                   jax.ShapeDtypeStruct((nb, N_STATE), F32)],
        scratch_shapes=[pltpu.VMEM((n_t, SSM_BLOCKS, nb, LANES), F32)],
        compiler_params=_cparams(("arbitrary",)),
    )(ut, zt, h0r, h0i, sp['lre'], sp['lim'], sp['ldt'], sp['bre'], sp['bim'], sp['cmat'], sp['d'], sp['wglu'])
    return out, hr, hi


def _page_map(li, pg, b, pt):
    return (li, pt[b, pg], 0, 0)


def _page_map_seq(li, pg, sq, b, pt):
    return (li, pt[b * NSA_DEC_SEQS + sq, pg], 0, 0)


def _pad_rows(x, rows):
    return jnp.concatenate([x, jnp.zeros((rows - x.shape[0], x.shape[1]), x.dtype)], axis=0)


def _nsa_sample_body(pt_ref, *refs, n_pages, n_sel):
    nsq = NSA_DEC_SEQS
    (q_ref, z_ref, g_ref, snew_ref, wnew_ref, wb_ref, wab_ref, pc_ref, w2_ref, ovt_ref, e_ref,
     gperm_ref, o_ref, s_s) = refs[nsq * n_pages:]
    seqs = range(nsq)
    pages = [refs[sq * n_pages:(sq + 1) * n_pages] for sq in seqs]
    rs = [slice(sq * T_PAD, (sq + 1) * T_PAD) for sq in seqs]
    past = n_pages * PAGE_SIZE
    nk = past + LANES
    wb = wb_ref.shape[2]
    qpos_col = past + lax.broadcasted_iota(I32, (T_PAD, 1), 0)
    qpos64 = jnp.concatenate([qpos_col] * NSA_HEADS, axis=0)
    qpos_row = past + (lax.broadcasted_iota(I32, (1, LANES), 1) % T_PAD)
    slope_col = jnp.concatenate([jnp.full((T_PAD, 1), NSA_SLOPES[hd], F32) for hd in range(NSA_HEADS)], axis=0)
    kpos = lax.broadcasted_iota(I32, (1, nk), 1)
    kbias = slope_col * kpos.astype(F32)
    kpos_w = (past - wb) + lax.broadcasted_iota(I32, (1, wb + LANES), 1)
    wbias = slope_col * kpos_w.astype(F32)
    dist = qpos64 - kpos_w
    wmask = (dist >= 0) & (dist < WINDOW) & (kpos_w >= 0)
    causal = kpos <= qpos64
    sk_r = pl.ds(2 * LANES, LANES)
    sv_r = pl.ds(3 * LANES, LANES)

    qh = [_pad_heads(q_ref[rs[sq], :] * SCALE) for sq in seqs]
    q64 = [jnp.concatenate(qh[sq], axis=0) for sq in seqs]

    gperm = gperm_ref[...]
    gathered = [[[_dot(pages[sq][pg][kv * LANES:(kv + 1) * LANES, :].astype(BF16), gperm).T
                  for pg in range(n_pages)] for sq in seqs] for kv in range(2)]

    owin = []
    for sq in seqs:
        wnew = wnew_ref[rs[sq], :]
        kwn = _pad_rows(wnew[:, 0:LANES], LANES).astype(BF16)
        vwn = _pad_rows(wnew[:, LANES:2 * LANES], LANES).astype(BF16)
        sw = jnp.concatenate([_dot(q64[sq], wb_ref[sq, 0:LANES, :].astype(BF16)), _dot_nt(q64[sq], kwn)], axis=1)
        pw = _msoftmax(sw + wbias, wmask).astype(BF16)
        owin.append(_dot_nt(pw[:, 0:wb], wb_ref[sq, LANES:2 * LANES, :].astype(BF16)) + _dot(pw[:, wb:], vwn))

    for sq in seqs:
        for pg in range(n_pages):
            s_s[sq, :, pg * PAGE_SIZE:(pg + 1) * PAGE_SIZE] = _dot(q64[sq], pages[sq][pg][sk_r, :].astype(BF16))
        knew = _pad_rows(snew_ref[rs[sq], 0:LANES], LANES).astype(BF16)
        s_s[sq, :, past:nk] = _dot_nt(q64[sq], knew)

    toks = []
    npc = PAGE_SIZE // CMP_STRIDE
    for kv in range(2):
        xseq = []
        for sq in seqs:
            xs = [jnp.concatenate([gathered[kv][sq][pg][l * npc:(l + 1) * npc] for pg in range(n_pages)],
                                  axis=0).astype(BF16) for l in range(CMP_STRIDE)]
            xseq.append(jnp.concatenate(xs, axis=1))
        toks.append(_compress_tokens(xseq, wab_ref[kv], pc_ref[kv], w2_ref[kv]))
    cmp = [_cmp_branch(qh[sq], toks[0][sq], toks[1][sq], qpos_col) for sq in seqs]

    memb = []
    for sq in seqs:
        p_cmp = cmp[sq][1]
        psum = jnp.concatenate(
            [p_cmp[0] + p_cmp[1] + p_cmp[2] + p_cmp[3], p_cmp[4] + p_cmp[5] + p_cmp[6] + p_cmp[7],
             jnp.zeros((LANES - 2 * T_PAD, N_CMP_PAD), F32)], axis=0)
        sel = _select_blocks(psum, qpos_row, ovt_ref[...], n_sel)
        mexp = _dot(sel[0:2 * T_PAD].astype(BF16), e_ref[...])
        memb.append(
            jnp.concatenate([mexp[0:T_PAD]] * NSA_GROUP + [mexp[T_PAD:2 * T_PAD]] * NSA_GROUP, axis=0) > 0.5)

    for sq in seqs:
        p = _msoftmax(s_s[sq] + kbias, memb[sq] & causal).astype(BF16)
        vnew = _pad_rows(snew_ref[rs[sq], LANES:2 * LANES], LANES).astype(BF16)
        osel = _dot(p[:, past:nk], vnew)
        for pg in range(n_pages):
            osel = osel + _dot_nt(p[:, pg * PAGE_SIZE:(pg + 1) * PAGE_SIZE], pages[sq][pg][sv_r, :].astype(BF16))
        o_sel = [osel[hd * T_PAD:(hd + 1) * T_PAD] for hd in range(NSA_HEADS)]
        o_win = [owin[sq][hd * T_PAD:(hd + 1) * T_PAD] for hd in range(NSA_HEADS)]
        o_ref[rs[sq], :] = _combine_heads(_sigmoid(g_ref[rs[sq], :]), cmp[sq][0], o_sel, o_win, z_ref[rs[sq], :])


def _nsa_sample(p2, cache_t, li, win_t, page_table, wab, pc, w2bd, ovt, e2, n_t):
    n = p2.shape[0]
    nb, n_pages = page_table.shape
    past = n_pages * PAGE_SIZE
    n_sel = -(-(past + n_t) // SEL_BLOCK)
    nk = past + LANES
    wb = win_t.shape[3]
    kvb = C_KV // (2 * LANES)
    nsq = NSA_DEC_SEQS
    assert nb % nsq == 0
    page_specs = [pl.BlockSpec((None, None, 4 * LANES, PAGE_SIZE), functools.partial(_page_map_seq, li, pg, sq))
                  for sq in range(nsq) for pg in range(n_pages)]
    const = lambda shape: pl.BlockSpec(shape, lambda b, pt: (0,) * len(shape))
    rows = nsq * T_PAD
    in_specs = page_specs + [
        pl.BlockSpec((rows, NSA_WIDTH), lambda b, pt: (b, C_Q // NSA_WIDTH)),
        pl.BlockSpec((rows, NSA_WIDTH), lambda b, pt: (b, C_ZN // NSA_WIDTH)),
        pl.BlockSpec((rows, LANES), lambda b, pt: (b, C_G // LANES)),
        pl.BlockSpec((rows, 2 * LANES), lambda b, pt: (b, kvb + 1)),
        pl.BlockSpec((rows, 2 * LANES), lambda b, pt: (b, kvb + 2)),
        pl.BlockSpec((None, nsq, 2 * LANES, wb), lambda b, pt: (li, b, 0, 0)),
        const(wab.shape), const(pc.shape), const(w2bd.shape),
        const(ovt.shape), const(e2.shape), const((PAGE_SIZE, PAGE_SIZE))]
    return pl.pallas_call(
        functools.partial(_nsa_sample_body, n_pages=n_pages, n_sel=n_sel),
        grid_spec=pltpu.PrefetchScalarGridSpec(
            num_scalar_prefetch=1, grid=(nb // nsq,), in_specs=in_specs,
            out_specs=pl.BlockSpec((rows, NSA_WIDTH), lambda b, pt: (b, 0)),
            scratch_shapes=[pltpu.VMEM((nsq, NSA_HEADS * T_PAD, nk), F32)]),
        out_shape=jax.ShapeDtypeStruct((n, NSA_WIDTH), F32),
        compiler_params=_cparams(("parallel",)),
    )(page_table, *([cache_t] * (nsq * n_pages)), p2, p2, p2, p2, p2, win_t, wab, pc, w2bd, ovt, e2,
      _chunk_gather_permutation())


def _chunk_gather_permutation():
    k = np.arange(PAGE_SIZE)
    pm = np.zeros((PAGE_SIZE, PAGE_SIZE), np.float32)
    pm[k, (k % CMP_STRIDE) * (PAGE_SIZE // CMP_STRIDE) + k // CMP_STRIDE] = 1.0
    return jnp.asarray(pm, BF16)


def _diff_sample_body(pt_ref, *refs, n_pages, lam_init):
    pages = refs[:n_pages]
    lam_ref, q_ref, z_ref, knew_ref, vnew_ref, lnw_ref, o_ref, s_s = refs[n_pages:]
    past = n_pages * PAGE_SIZE
    nk = past + LANES
    lam = _diff_lambda(lam_ref, lam_init)
    low = lax.broadcasted_iota(I32, (T_PAD, LANES), 1) < DIFF_HEAD_DIM
    rows_h = 2 * T_PAD
    q16 = []
    for hh in range(DIFF_HEADS):
        qs = q_ref[:, hh * LANES:(hh + 1) * LANES] * DIFF_SCALE
        q16.append(jnp.concatenate([jnp.where(low, qs, 0.0), jnp.where(low, 0.0, qs)], axis=0).astype(BF16))
    for pg in range(n_pages):
        for hh in range(DIFF_HEADS):
            kp = pages[pg][pl.ds(hh, PAGE_SIZE, stride=DIFF_SLABS), :].astype(BF16)
            s_s[hh * rows_h:(hh + 1) * rows_h, pg * PAGE_SIZE:(pg + 1) * PAGE_SIZE] = _dot_nt(q16[hh], kp)
    for hh in range(DIFF_HEADS):
        kn = _pad_rows(knew_ref[:, hh * LANES:(hh + 1) * LANES], LANES).astype(BF16)
        s_s[hh * rows_h:(hh + 1) * rows_h, past:nk] = _dot_nt(q16[hh], kn)
    kpos = lax.broadcasted_iota(I32, (1, nk), 1)
    slope_col = jnp.concatenate([jnp.full((rows_h, 1), DIFF_SLOPES[hh], F32) for hh in range(DIFF_HEADS)], axis=0)
    qpos = past + (lax.broadcasted_iota(I32, (DIFF_HEADS * rows_h, 1), 0) % T_PAD)
    p = _msoftmax(s_s[...] + slope_col * kpos.astype(F32), kpos <= qpos)
    lnw = lnw_ref[...]
    slabs = []
    for hh in range(DIFF_HEADS):
        a = (p[hh * rows_h:hh * rows_h + T_PAD] - lam * p[hh * rows_h + T_PAD:(hh + 1) * rows_h]).astype(BF16)
        vn = _pad_rows(vnew_ref[:, hh * LANES:(hh + 1) * LANES], LANES).astype(BF16)
        o = _dot(a[:, past:nk], vn)
        for pg in range(n_pages):
            vp = pages[pg][pl.ds(DIFF_HEADS + hh, PAGE_SIZE, stride=DIFF_SLABS), :].astype(BF16)
            o = o + _dot(a[:, pg * PAGE_SIZE:(pg + 1) * PAGE_SIZE], vp)
        slabs.append(_diff_finish(o, lnw, z_ref[:, hh * LANES:(hh + 1) * LANES], lam_init))
    o_ref[...] = jnp.concatenate(slabs, axis=1)


def _diff_sample(p2, cache_r, li, page_table, lam4, lnw_row, lam_init):
    n = p2.shape[0]
    nb, n_pages = page_table.shape
    nk = n_pages * PAGE_SIZE + LANES
    page_specs = [pl.BlockSpec((None, None, PAGE_SIZE * DIFF_SLABS, LANES), functools.partial(_page_map, li, pg))
                  for pg in range(n_pages)]
    in_specs = page_specs + [
        pl.BlockSpec((4, DIFF_HEAD_DIM), lambda b, pt: (0, 0)),
        pl.BlockSpec((T_PAD, DIFF_WIDTH), lambda b, pt: (b, C_DQ // DIFF_WIDTH)),
        pl.BlockSpec((T_PAD, DIFF_WIDTH), lambda b, pt: (b, C_ZD // DIFF_WIDTH)),
        pl.BlockSpec((T_PAD, DIFF_WIDTH), lambda b, pt: (b, C_DK // DIFF_WIDTH)),
        pl.BlockSpec((T_PAD, DIFF_WIDTH), lambda b, pt: (b, C_DV // DIFF_WIDTH)),
        pl.BlockSpec((1, DIFF_V_DIM), lambda b, pt: (0, 0))]
    return pl.pallas_call(
        functools.partial(_diff_sample_body, n_pages=n_pages, lam_init=lam_init),
        grid_spec=pltpu.PrefetchScalarGridSpec(
            num_scalar_prefetch=1, grid=(nb,), in_specs=in_specs,
            out_specs=pl.BlockSpec((T_PAD, DIFF_WIDTH), lambda b, pt: (b, 0)),
            scratch_shapes=[pltpu.VMEM((DIFF_HEADS * 2 * T_PAD, nk), F32)]),
        out_shape=jax.ShapeDtypeStruct((n, DIFF_WIDTH), F32),
        compiler_params=_cparams(("parallel",)),
    )(page_table, *([cache_r] * n_pages), lam4, p2, p2, p2, p2, lnw_row)


def _overlap_t(n_sel):
    n = np.arange(N_CMP_PAD)[None, :]
    j = np.arange(NSEL_PAD)[:, None]
    cs = n * CMP_STRIDE
    ss = j * SEL_BLOCK
    ov = np.clip(np.minimum(cs + CMP_BLOCK, ss + SEL_BLOCK) - np.maximum(cs, ss), 0, None) / CMP_BLOCK
    ov = np.where((n < N_CMP_PAD - 1) & (j < n_sel), ov, 0.0)
    return jnp.asarray(ov, BF16)


def _block_expander(n_keys):
    j = np.arange(LANES)[:, None]
    k = np.arange(n_keys)[None, :]
    return jnp.asarray((k // SEL_BLOCK) == j, BF16)


def _layer_weights(li, norm_w, w_in, w_out, w_cmp1, pos_cmp, w_cmp2, lre, lim, ldt, bre, bim, cre, cim, sd,
                   w_glu, lam_q1, lam_k1, lam_q2, lam_k2, diff_ln_w):
    w = {}
    w['nw'] = norm_w[li][None, :]
    w['w_out'] = w_out[li].astype(BF16)
    eye2 = jnp.eye(NSA_KV_HEADS, dtype=F32)
    w1 = w_cmp1[li].reshape(2, 2, CMP_STRIDE, HEAD_DIM, HEAD_DIM)
    w['wab'] = jnp.einsum('kaldf,hg->klhdagf', w1, eye2).reshape(
        2, CMP_STRIDE * LANES, 2 * LANES).astype(BF16)
    pos = pos_cmp[li].reshape(2, 2, CMP_STRIDE, 1, HEAD_DIM)
    pos = jnp.broadcast_to(pos, (2, 2, CMP_STRIDE, NSA_KV_HEADS, HEAD_DIM)).reshape(2, 2, 1, CMP_STRIDE * LANES)
    pos = jnp.broadcast_to(pos, (2, 2, SUBLANES, CMP_STRIDE * LANES)).astype(BF16)
    w['pos_a'] = pos[:, 0]
    w['pos_b'] = pos[:, 1]
    w['w2bd'] = jnp.einsum('ked,hg->khegd', w_cmp2[li], eye2).reshape(2, LANES, LANES).astype(BF16)
    eye8 = jnp.eye(SSM_BLOCKS, dtype=F32)
    gl = SSM_BLOCKS

    def compact_b(bm):
        bm = bm.reshape(SSM_BLOCKS, gl, SSM_STATE, SSM_GROUP_CH)
        return jnp.einsum('kgpc,gh->kgchp', bm, eye8).reshape(SSM_BLOCKS, LANES, SSM_BS)

    def compact_c(cm):
        cm = cm.reshape(SSM_BLOCKS, gl, SSM_GROUP_CH, SSM_STATE)
        return jnp.einsum('kgcp,gh->khpgc', cm, eye8).reshape(SSM_BLOCKS, SSM_BS, LANES)

    sp = {'lre': lre[li].reshape(1, N_STATE), 'lim': lim[li].reshape(1, N_STATE),
          'ldt': jnp.repeat(ldt[li], SSM_STATE).reshape(1, N_STATE),
          'bre': compact_b(bre[li]), 'bim': compact_b(bim[li]),
          'cmat': jnp.concatenate([compact_c(cre[li]), -compact_c(cim[li])], axis=1).astype(BF16),
          'd': sd[li].reshape(1, SSM_WIDTH), 'wglu': w_glu[li].astype(BF16)}
    w['ssm'] = sp
    w['lam4'] = jnp.stack([lam_q1[li], lam_k1[li], lam_q2[li], lam_k2[li]])
    w['lnw'] = diff_ln_w[li][None, :]
    w['lam_init'] = 0.8 - 0.6 * math.exp(-0.3 * li)
    return w


def kernel(x_prompt, x_sample, cache_nsa_kv, cache_diff_kv, state_nsa_win, state_ssm_re, state_ssm_im, page_table, norm_w, w_in, w_out, w_cmp1, pos_cmp, w_cmp2, ssm_lambda_re, ssm_lambda_im, ssm_log_dt, ssm_b_re, ssm_b_im, ssm_c_re, ssm_c_im, ssm_d, w_glu, lam_q1, lam_k1, lam_q2, lam_k2, diff_ln_w, final_norm_w):
    depth = norm_w.shape[0]
    b, t, _ = x_prompt.shape
    nb, n_t, _ = x_sample.shape
    n_pool = cache_nsa_kv.shape[1]
    n_pages = page_table.shape[1]
    past = n_pages * PAGE_SIZE
    wb = state_nsa_win.shape[2]
    assert t % SSM_CHUNK == 0 and t >= WINDOW + 128 and n_t <= T_PAD

    ovt_p = _overlap_t(-(-t // SEL_BLOCK))
    ovt_s = _overlap_t(-(-(past + n_t) // SEL_BLOCK))
    e_p = _block_expander(t).T
    e_s = _block_expander(past + LANES)
    fw = final_norm_w[None, :]
    nsa_cache_t = cache_nsa_kv.transpose(0, 1, 3, 4, 5, 2).reshape(depth, n_pool, 4 * LANES, PAGE_SIZE)
    win_t = state_nsa_win.transpose(0, 1, 3, 4, 5, 2).reshape(depth, nb, 2 * LANES, wb)
    diff_cache_r = cache_diff_kv.reshape(depth, n_pool, PAGE_SIZE * DIFF_SLABS, LANES)
    w_rows = w_in.transpose(0, 2, 1)
    w_in_t = jnp.concatenate(
        [w_rows[:, ORIG_OFFS[s]:ORIG_OFFS[s + 1]] for s in NEW_ORDER]
        + [jnp.zeros((depth, DP - ORIG_OFFS[-1], D_MODEL), w_in.dtype)], axis=1).astype(BF16)

    xp = x_prompt.reshape(b * t, D_MODEL)
    xs = jnp.pad(x_sample, ((0, 0), (0, T_PAD - n_t), (0, 0))).reshape(nb * T_PAD, D_MODEL)
    outs = {k: [] for k in ('p_kv', 'p_dkv', 'p_win', 'p_re', 'p_im', 's_kv', 's_dkv', 's_win', 's_re', 's_im')}
    for li in range(depth):
        w = _layer_weights(li, norm_w, w_in, w_out, w_cmp1, pos_cmp, w_cmp2, ssm_lambda_re, ssm_lambda_im,
                           ssm_log_dt, ssm_b_re, ssm_b_im, ssm_c_re, ssm_c_im, ssm_d, w_glu,
                           lam_q1, lam_k1, lam_q2, lam_k2, diff_ln_w)
        final = li == depth - 1
        pp, p_dkv_rows = _norm_project(xp, w['nw'], w_in_t, li)
        p3 = pp.reshape(b, t, DP)
        pos_c = _compress_pos_const(w['pos_a'], w['pos_b'], w['wab'])
        cmp_tok = _compress_prompt(p3, w['wab'], pos_c, w['w2bd'])
        nsa_o = _nsa_prompt(p3, cmp_tok, ovt_p, e_p)
        ssm_o, hr, hi = _ssm_prompt(p3, w['ssm'])
        diff_o = _diff_prompt(p3, w['lam4'], w['lnw'], w['lam_init'])
        xp = _merge_out(nsa_o.reshape(b * t, -1), ssm_o.reshape(b * t, -1), diff_o.reshape(b * t, -1), xp,
                        w['w_out'], fw, final)
        kv = p3[:, :, C_KV:C_KV + 6 * LANES]
        outs['p_kv'].append(kv[:, :, :4 * LANES].reshape(b, t, 4, NSA_KV_HEADS, HEAD_DIM))
        outs['p_dkv'].append(p_dkv_rows.reshape(b, t, 2, DIFF_HEADS, DIFF_V_DIM))
        outs['p_win'].append(kv[:, t - min(WINDOW, t):, 4 * LANES:].reshape(b, -1, 2, NSA_KV_HEADS, HEAD_DIM))
        outs['p_re'].append(hr.reshape(b, SSM_GROUPS, SSM_STATE))
        outs['p_im'].append(hi.reshape(b, SSM_GROUPS, SSM_STATE))
        ps, s_dkv_rows = _norm_project(xs, w['nw'], w_in_t, li)
        nsa_s = _nsa_sample(ps, nsa_cache_t, li, win_t, page_table,
                            w['wab'], pos_c, w['w2bd'], ovt_s, e_s, n_t)
        ps_t = ps.reshape(nb, T_PAD, DP)[:, :n_t].transpose(1, 0, 2)
        ssm_t, shr, shi = _ssm_sample(ps_t[:, :, C_U:C_U + SSM_WIDTH], ps_t[:, :, C_ZS:C_ZS + SSM_WIDTH],
                                      state_ssm_re[li].reshape(nb, N_STATE),
                                      state_ssm_im[li].reshape(nb, N_STATE), w['ssm'])
        ssm_s = jnp.pad(ssm_t.transpose(1, 0, 2), ((0, 0), (0, T_PAD - n_t), (0, 0))).reshape(nb * T_PAD, SSM_WIDTH)
        diff_s = _diff_sample(ps, diff_cache_r, li, page_table, w['lam4'], w['lnw'], w['lam_init'])
        xs = _merge_out(nsa_s, ssm_s, diff_s, xs, w['w_out'], fw, final)
        ps3 = ps.reshape(nb, T_PAD, DP)[:, :n_t]
        skv = ps3[:, :, C_KV:C_KV + 6 * LANES]
        outs['s_kv'].append(skv[:, :, :4 * LANES].reshape(nb, n_t, 4, NSA_KV_HEADS, HEAD_DIM))
        outs['s_dkv'].append(s_dkv_rows.reshape(nb, T_PAD, 2, DIFF_HEADS, DIFF_V_DIM)[:, :n_t])
        outs['s_win'].append(skv[:, :, 4 * LANES:].transpose(0, 2, 1))
        outs['s_re'].append(shr.reshape(nb, SSM_GROUPS, SSM_STATE))
        outs['s_im'].append(shi.reshape(nb, SSM_GROUPS, SSM_STATE))
    y_prompt = xp.reshape(b, t, D_MODEL)
    y_sample = xs.reshape(nb, T_PAD, D_MODEL)[:, :n_t]
    st = lambda k: jnp.stack(outs[k])
    s_win_t = jnp.concatenate([win_t[:, :, :, n_t:], st('s_win')], axis=3)
    s_win = s_win_t.reshape(depth, nb, 2, NSA_KV_HEADS, HEAD_DIM, wb).transpose(0, 1, 5, 2, 3, 4)
    return (y_prompt, y_sample, st('p_kv'), st('p_dkv'), st('p_win'), st('p_re'), st('p_im'),
            st('s_kv'), st('s_dkv'), s_win, st('s_re'), st('s_im'))
```

```python
import functools
import math

import numpy as np
import jax
import jax.numpy as jnp
from jax import lax
from jax.experimental import pallas as pl
from jax.experimental.pallas import tpu as pltpu

F32 = jnp.float32
BF16 = jnp.bfloat16
I32 = jnp.int32

D_MODEL = 2048
HEAD_DIM = 64
NSA_WIDTH = 512
NSA_HEADS = 8
NSA_KV_HEADS = 2
NSA_GROUP = 4
CMP_BLOCK = 32
CMP_STRIDE = 16
SEL_BLOCK = 64
N_SELECT = 16
WINDOW = 512
SSM_WIDTH = 1024
SSM_GROUP_CH = 16
SSM_GROUPS = 64
SSM_STATE = 64
N_STATE = SSM_GROUPS * SSM_STATE
DIFF_WIDTH = 512
DIFF_HEADS = 4
DIFF_HEAD_DIM = 64
DIFF_V_DIM = 128
DIFF_SLABS = 2 * DIFF_HEADS
PAGE_SIZE = 128
SCALE = HEAD_DIM ** -0.5
DIFF_SCALE = DIFF_HEAD_DIM ** -0.5
NEG = -1e30
BIG = 1e9
EPS = 1e-6
NSA_SLOPES = tuple(float(2.0 ** (-8.0 * (k + 1) / NSA_HEADS)) for k in range(NSA_HEADS))
DIFF_SLOPES = tuple(float(2.0 ** (-8.0 * (k + 1) / DIFF_HEADS)) for k in range(DIFF_HEADS))

LANES = 128
SUBLANES = 8
VMEM_LIMIT = 56 * 1024 * 1024

ORIG_SIZES = (NSA_WIDTH, 6 * NSA_KV_HEADS * HEAD_DIM, 3 * NSA_HEADS, NSA_WIDTH, SSM_WIDTH, SSM_WIDTH,
              DIFF_HEADS * 2 * DIFF_HEAD_DIM, DIFF_WIDTH, DIFF_WIDTH, DIFF_WIDTH)
ORIG_OFFS = tuple(int(v) for v in np.concatenate([[0], np.cumsum(ORIG_SIZES)]))
NEW_ORDER = (0, 3, 6, 7, 8, 9, 4, 5, 1, 2)
C_Q, C_ZN, C_DQ, C_DK, C_DV, C_ZD, C_U, C_ZS, C_KV, C_G = 0, 512, 1024, 1536, 2048, 2560, 3072, 4096, 5120, 5888
DP = 6144
N_CMP_PAD = 128
NSEL_PAD = 40
SEL_CHUNK = 256
T_PAD = 8
NSA_DEC_SEQS = 2


def _cparams(sem):
    return pltpu.CompilerParams(dimension_semantics=sem, vmem_limit_bytes=VMEM_LIMIT)


def _dot(a, b):
    return jnp.dot(a, b, preferred_element_type=F32)


def _dot_nt(a, b):
    return lax.dot_general(a, b, (((1,), (1,)), ((), ())), preferred_element_type=F32)


def _gelu(x):
    return 0.5 * x * (1.0 + jnp.tanh(math.sqrt(2.0 / math.pi) * (x + 0.044715 * (x * x * x))))


def _sigmoid(x):
    return 1.0 / (1.0 + jnp.exp(-x))


def _silu(x):
    return x * _sigmoid(x)


def _msoftmax(s, mask):
    s = jnp.where(mask, s, NEG)
    m = jnp.max(s, axis=-1, keepdims=True)
    e = jnp.exp(s - m)
    den = jnp.sum(e, axis=-1, keepdims=True)
    return jnp.where(mask, e * (1.0 / den), 0.0)


PROJ_TN = 512


def _proj_body(x_ref, nw_ref, w_ref, o_ref, dkv_ref, xn_ref):
    j = pl.program_id(1)
    tm = x_ref.shape[0]

    @pl.when(j == 0)
    def _():
        x = x_ref[...]
        ms = jnp.mean(x * x, axis=-1, keepdims=True)
        xn_ref[...] = (x * lax.rsqrt(ms + EPS) * nw_ref[...]).astype(BF16)

    acc = _dot_nt(xn_ref[...], w_ref[...])
    o_ref[...] = acc

    for col0, slab0 in ((C_DK, 0), (C_DV, DIFF_HEADS)):
        @pl.when(j == col0 // PROJ_TN)
        def _(slab0=slab0):
            for c in range(PROJ_TN // LANES):
                dkv_ref[pl.ds(slab0 + c, tm, stride=DIFF_SLABS), :] = acc[:, c * LANES:(c + 1) * LANES]


def _norm_project(x2d, nw_row, w_t, li):
    n = x2d.shape[0]
    tm = min(n, 1024)
    tn = PROJ_TN
    return pl.pallas_call(
        _proj_body,
        grid=(n // tm, DP // tn),
        in_specs=[pl.BlockSpec((tm, D_MODEL), lambda i, j: (i, 0)),
                  pl.BlockSpec((1, D_MODEL), lambda i, j: (0, 0)),
                  pl.BlockSpec((None, tn, D_MODEL), lambda i, j: (li, j, 0))],
        out_specs=[pl.BlockSpec((tm, tn), lambda i, j: (i, j)),
                   pl.BlockSpec((tm * DIFF_SLABS, LANES), lambda i, j: (i, 0))],
        out_shape=[jax.ShapeDtypeStruct((n, DP), F32),
                   jax.ShapeDtypeStruct((n * DIFF_SLABS, LANES), F32)],
        scratch_shapes=[pltpu.VMEM((tm, D_MODEL), BF16)],
        compiler_params=_cparams(("parallel", "arbitrary")),
    )(x2d, nw_row, w_t)


def _out_body(nsa_ref, ssm_ref, diff_ref, x_ref, w_ref, fw_ref, y_ref, *, final):
    acc = x_ref[...]
    acc = acc + _dot(nsa_ref[...].astype(BF16), w_ref[0:NSA_WIDTH, :])
    acc = acc + _dot(ssm_ref[...].astype(BF16), w_ref[NSA_WIDTH:NSA_WIDTH + SSM_WIDTH, :])
    acc = acc + _dot(diff_ref[...].astype(BF16), w_ref[NSA_WIDTH + SSM_WIDTH:, :])
    if final:
        ms = jnp.mean(acc * acc, axis=-1, keepdims=True)
        acc = acc * lax.rsqrt(ms + EPS) * fw_ref[...]
    y_ref[...] = acc


def _merge_out(nsa_o, ssm_o, diff_o, x2d, w_out_bf, fw_row, final):
    n = x2d.shape[0]
    tm = min(n, 512)
    return pl.pallas_call(
        functools.partial(_out_body, final=final),
        grid=(n // tm,),
        in_specs=[pl.BlockSpec((tm, NSA_WIDTH), lambda i: (i, 0)),
                  pl.BlockSpec((tm, SSM_WIDTH), lambda i: (i, 0)),
                  pl.BlockSpec((tm, DIFF_WIDTH), lambda i: (i, 0)),
                  pl.BlockSpec((tm, D_MODEL), lambda i: (i, 0)),
                  pl.BlockSpec((D_MODEL, D_MODEL), lambda i: (0, 0)),
                  pl.BlockSpec((1, D_MODEL), lambda i: (0, 0))],
        out_specs=pl.BlockSpec((tm, D_MODEL), lambda i: (i, 0)),
        out_shape=jax.ShapeDtypeStruct((n, D_MODEL), F32),
        compiler_params=_cparams(("parallel",)),
    )(nsa_o, ssm_o, diff_o, x2d, w_out_bf, fw_row)


def _pad_heads(qb):
    tq = qb.shape[0]
    low = lax.broadcasted_iota(I32, (tq, LANES), 1) < HEAD_DIM
    outs = []
    for hd in range(NSA_HEADS):
        h = hd // NSA_GROUP
        slab = qb[:, (hd // 2) * LANES:(hd // 2 + 1) * LANES]
        if hd % 2 != h:
            slab = pltpu.roll(slab, HEAD_DIM, 1)
        keep = low if h == 0 else jnp.logical_not(low)
        outs.append(jnp.where(keep, slab, 0.0).astype(BF16))
    return outs


def _compress_tokens(xs, wab, pc, w2bd):
    ab = _dot(jnp.concatenate(xs, axis=0), wab)
    pre = []
    for s in range(len(xs)):
        rows = slice(s * N_CMP_PAD, (s + 1) * N_CMP_PAD)
        pre.append(ab[rows, 0:LANES] + pltpu.roll(ab[rows, LANES:2 * LANES], N_CMP_PAD - 1, 0) + pc[0:1])
    tok = _dot(_gelu(jnp.concatenate(pre, axis=0)).astype(BF16), w2bd).astype(BF16)
    return [tok[s * N_CMP_PAD:(s + 1) * N_CMP_PAD] for s in range(len(xs))]


def _pos_const_body(pa_ref, pb_ref, wab_ref, o_ref):
    wab = wab_ref[...]
    o_ref[...] = _dot(pa_ref[...], wab)[:, 0:LANES] + _dot(pb_ref[...], wab)[:, LANES:2 * LANES]


def _compress_pos_const(pos_a, pos_b, wab):
    return pl.pallas_call(
        _pos_const_body,
        grid=(2,),
        in_specs=[pl.BlockSpec((None, SUBLANES, CMP_STRIDE * LANES), lambda k: (k, 0, 0)),
                  pl.BlockSpec((None, SUBLANES, CMP_STRIDE * LANES), lambda k: (k, 0, 0)),
                  pl.BlockSpec((None, CMP_STRIDE * LANES, 2 * LANES), lambda k: (k, 0, 0))],
        out_specs=pl.BlockSpec((None, SUBLANES, LANES), lambda k: (k, 0, 0)),
        out_shape=jax.ShapeDtypeStruct((2, SUBLANES, LANES), F32),
        compiler_params=_cparams(("parallel",)),
    )(pos_a, pos_b, wab)


def _cmp_branch(qh, kc, vc, qpos_col, values_on_rows=False):
    tq = qpos_col.shape[0]
    n_iota = lax.broadcasted_iota(I32, (1, N_CMP_PAD), 1)
    ends = n_iota * CMP_STRIDE + (CMP_BLOCK - 1)
    q_all = jnp.concatenate(qh, axis=0)
    slope_col = jnp.concatenate([jnp.full((tq, 1), NSA_SLOPES[hd], F32) for hd in range(NSA_HEADS)], axis=0)
    qpos_all = jnp.concatenate([qpos_col] * NSA_HEADS, axis=0)
    s = _dot_nt(q_all, kc) + slope_col * ends.astype(F32)
    p = _msoftmax(s, ends <= qpos_all)
    rows = [slice(hd * tq, (hd + 1) * tq) for hd in range(NSA_HEADS)]
    if values_on_rows:
        return _dot_nt(vc, p.astype(BF16)), [p[r] for r in rows]
    o = _dot(p.astype(BF16), vc)
    return [o[r] for r in rows], [p[r] for r in rows]


def _select_blocks(psum, qpos_row, ovt, n_sel, queries_on_rows=True):
    hi = psum.astype(BF16)
    r1 = psum - hi.astype(F32)
    mid = r1.astype(BF16)
    lo = (r1 - mid.astype(F32)).astype(BF16)
    sc = _dot_nt(ovt, hi) + _dot_nt(ovt, mid) + _dot_nt(ovt, lo)
    j = lax.broadcasted_iota(I32, (NSEL_PAD, LANES), 0)
    cur = qpos_row // SEL_BLOCK
    forced = (j == 0) | (j == cur) | (j == cur - 1)
    avail = j * SEL_BLOCK <= qpos_row
    sc = jnp.where(forced, BIG, jnp.where(avail, sc, -BIG))
    cnt = jnp.zeros((NSEL_PAD, LANES), I32)
    for ii in range(n_sel):
        row = sc[ii:ii + 1, :]
        beats = (row > sc) | ((row == sc) & (ii < j))
        cnt = cnt + beats.astype(I32)
    k_top = min(N_SELECT, n_sel)
    sel_t = jnp.where((cnt < k_top) & (j < n_sel), 1.0, 0.0).astype(F32)
    sel_t = jnp.concatenate([sel_t, jnp.zeros((LANES - NSEL_PAD, LANES), F32)], axis=0)
    return sel_t.T if queries_on_rows else sel_t


def _combine_heads(gs, o_cmp, o_sel, o_win, z):
    tq = z.shape[0]
    low = lax.broadcasted_iota(I32, (tq, LANES), 1) < HEAD_DIM
    placed = []
    for hd in range(NSA_HEADS):
        h = hd // NSA_GROUP
        o = (gs[:, hd:hd + 1] * o_cmp[hd] + gs[:, NSA_HEADS + hd:NSA_HEADS + hd + 1] * o_sel[hd]
             + gs[:, 2 * NSA_HEADS + hd:2 * NSA_HEADS + hd + 1] * o_win[hd])
        if hd % 2 != h:
            o = pltpu.roll(o, HEAD_DIM, 1)
        placed.append(o)
    slabs = [jnp.where(low, placed[2 * k], placed[2 * k + 1]) for k in range(NSA_HEADS // 2)]
    return jnp.concatenate(slabs, axis=1) * _silu(z)


def _compress_prompt_body(rows_ref, wab_ref, pc_ref, w2_ref, o_ref):
    xs = [rows_ref[pl.ds(l, N_CMP_PAD, stride=CMP_STRIDE), :].astype(BF16) for l in range(CMP_STRIDE)]
    x = jnp.concatenate(xs, axis=1)
    o_ref[...] = _compress_tokens([x], wab_ref[...], pc_ref[...], w2_ref[...])[0]


def _compress_prompt(p3, wab, pc, w2bd):
    b, t, _ = p3.shape
    kvblk = C_KV // LANES
    return pl.pallas_call(
        _compress_prompt_body,
        grid=(b, 2),
        in_specs=[pl.BlockSpec((None, t, LANES), lambda i, k: (i, 0, kvblk + k)),
                  pl.BlockSpec((None, CMP_STRIDE * LANES, 2 * LANES), lambda i, k: (k, 0, 0)),
                  pl.BlockSpec((None, SUBLANES, LANES), lambda i, k: (k, 0, 0)),
                  pl.BlockSpec((None, LANES, LANES), lambda i, k: (k, 0, 0))],
        out_specs=pl.BlockSpec((None, None, N_CMP_PAD, LANES), lambda i, k: (i, k, 0, 0)),
        out_shape=jax.ShapeDtypeStruct((b, 2, N_CMP_PAD, LANES), BF16),
        compiler_params=_cparams(("parallel", "parallel")),
    )(p3, wab, pc, w2bd)


def _nsa_prompt_body(q_ref, z_ref, g_ref, cmp_ref, sk_ref, sv_ref, wk_ref, wv_ref, ovt_ref, et_ref, o_ref,
                     svt_s, wvt_s, acc_s, *, n_sel):
    tq = 128
    i = pl.program_id(1)
    q0 = i * tq

    @pl.when(i == 0)
    def _():
        for c in range(svt_s.shape[0]):
            rows = slice(c * LANES, (c + 1) * LANES)
            svt_s[c] = sv_ref[rows, :].T.astype(BF16)
            wvt_s[c] = wv_ref[rows, :].T.astype(BF16)

    qh = _pad_heads(q_ref[...] * SCALE)
    q8 = jnp.concatenate(qh, axis=0)
    gs = _sigmoid(g_ref[...])
    qpos_col = q0 + lax.broadcasted_iota(I32, (tq, 1), 0)
    qpos_row = q0 + lax.broadcasted_iota(I32, (1, LANES), 1)

    w0 = pl.multiple_of(jnp.maximum(q0 - WINDOW, 0), LANES)
    wwid = WINDOW + tq
    kwin = wk_ref[pl.ds(w0, wwid), :].astype(BF16)
    stw = _dot_nt(kwin, q8)

    vc_t = cmp_ref[1].astype(F32).T.astype(BF16)
    ocmp_t, p_cmp = _cmp_branch(qh, cmp_ref[0], vc_t, qpos_col, values_on_rows=True)
    ovt = ovt_ref[...]
    kp_w = w0 + lax.broadcasted_iota(I32, (wwid, LANES), 0)
    dist = qpos_row - kp_w
    maskw = jnp.where((dist >= 0) & (dist < WINDOW), 0.0, NEG)
    kbw = kp_w.astype(F32)
    wc0 = w0 // LANES
    sub = SEL_CHUNK // LANES
    nch = (i + 2 * sub) // (2 * sub)
    hq = NSA_HEADS * tq
    cols = [slice(hd * tq, (hd + 1) * tq) for hd in range(NSA_HEADS)]

    sel_t = []
    for h in range(NSA_KV_HEADS):
        psum = p_cmp[h * NSA_GROUP]
        for hd in range(h * NSA_GROUP + 1, (h + 1) * NSA_GROUP):
            psum = psum + p_cmp[hd]
        sel_t.append(_select_blocks(psum, qpos_row, ovt, n_sel, queries_on_rows=False).astype(BF16))

    pws, linv = [], []
    for hd in range(NSA_HEADS):
        s = stw[:, cols[hd]] + NSA_SLOPES[hd] * kbw + maskw
        p = jnp.exp(s - jnp.max(s, axis=0, keepdims=True))
        linv.append(1.0 / jnp.sum(p, axis=0, keepdims=True))
        pws.append(p.astype(BF16))
    ptw = jnp.concatenate(pws, axis=1)
    owin_t = _dot(wvt_s[wc0], ptw[0:LANES])
    for k in range(1, wwid // LANES):
        owin_t = owin_t + _dot(wvt_s[wc0 + k], ptw[k * LANES:(k + 1) * LANES])
    owin_t = owin_t * jnp.concatenate(linv, axis=1)

    acc_s[...] = jnp.zeros_like(acc_s)

    def scores(c):
        k0 = pl.multiple_of(c * SEL_CHUNK, SEL_CHUNK)
        kch = sk_ref[pl.ds(k0, SEL_CHUNK), :].astype(BF16)
        et = et_ref[pl.ds(k0, SEL_CHUNK), :]
        st = _dot_nt(kch, q8)
        mexp = [_dot(et, sel_t[h]) for h in range(NSA_KV_HEADS)]
        return st, mexp

    def update(c, st, mexp, m, l):
        k0 = c * SEL_CHUNK
        kp = k0 + lax.broadcasted_iota(I32, (SEL_CHUNK, LANES), 0)
        causal = kp <= qpos_row
        kbf = kp.astype(F32)
        ms, ls, alphas, ps = [], [], [], []
        for hd in range(NSA_HEADS):
            if hd % NSA_GROUP == 0:
                maskb = jnp.where((mexp[hd // NSA_GROUP] > 0.5) & causal, 0.0, NEG)
            s = st[:, cols[hd]] + NSA_SLOPES[hd] * kbf + maskb
            m_new = jnp.maximum(m[:, cols[hd]], jnp.max(s, axis=0, keepdims=True))
            alpha = jnp.exp(m[:, cols[hd]] - m_new)
            p = jnp.exp(s - m_new)
            ms.append(m_new)
            ls.append(alpha * l[:, cols[hd]] + jnp.sum(p, axis=0, keepdims=True))
            alphas.append(alpha)
            ps.append(p.astype(BF16))
        pt = jnp.concatenate(ps, axis=1)
        ot = _dot(svt_s[c * sub], pt[0:LANES])
        for k in range(1, sub):
            ot = ot + _dot(svt_s[c * sub + k], pt[k * LANES:(k + 1) * LANES])
        acc_s[...] = acc_s[...] * jnp.concatenate(alphas, axis=1) + ot
        return jnp.concatenate(ms, axis=1), jnp.concatenate(ls, axis=1)

    def body(c2, carry):
        m, l = carry
        a = scores(2 * c2)
        b = scores(2 * c2 + 1)
        m, l = update(2 * c2, a[0], a[1], m, l)
        return update(2 * c2 + 1, b[0], b[1], m, l)

    init = (jnp.full((1, hq), NEG, F32), jnp.zeros((1, hq), F32))
    _, l = lax.fori_loop(0, nch, body, init)
    osel_t = acc_s[...] * jnp.where(l > 0.0, 1.0 / l, 0.0)

    gs_t = gs.T
    pieces = []
    for hd in range(NSA_HEADS):
        h = hd // NSA_GROUP
        o_t = (gs_t[hd:hd + 1] * ocmp_t[:, cols[hd]]
               + gs_t[NSA_HEADS + hd:NSA_HEADS + hd + 1] * osel_t[:, cols[hd]]
               + gs_t[2 * NSA_HEADS + hd:2 * NSA_HEADS + hd + 1] * owin_t[:, cols[hd]])
        pieces.append(o_t[h * HEAD_DIM:(h + 1) * HEAD_DIM])
    out_t = jnp.concatenate(pieces, axis=0)
    out = jnp.concatenate([out_t[k * LANES:(k + 1) * LANES].T for k in range(NSA_WIDTH // LANES)], axis=1)
    o_ref[...] = (out * _silu(z_ref[...])).astype(o_ref.dtype)


def _nsa_prompt(p3, cmp_tok, ovt, e3):
    b, t, _ = p3.shape
    tq = 128
    n_sel = -(-t // SEL_BLOCK)
    kvb = C_KV // LANES
    return pl.pallas_call(
        functools.partial(_nsa_prompt_body, n_sel=n_sel),
        grid=(b, t // tq),
        scratch_shapes=[pltpu.VMEM((t // LANES, LANES, LANES), BF16),
                        pltpu.VMEM((t // LANES, LANES, LANES), BF16),
                        pltpu.VMEM((LANES, NSA_HEADS * tq), F32)],
        in_specs=[pl.BlockSpec((None, tq, NSA_WIDTH), lambda bi, i: (bi, i, C_Q // NSA_WIDTH)),
                  pl.BlockSpec((None, tq, NSA_WIDTH), lambda bi, i: (bi, i, C_ZN // NSA_WIDTH)),
                  pl.BlockSpec((None, tq, LANES), lambda bi, i: (bi, i, C_G // LANES)),
                  pl.BlockSpec((None, 2, N_CMP_PAD, LANES), lambda bi, i: (bi, 0, 0, 0)),
                  pl.BlockSpec((None, t, LANES), lambda bi, i: (bi, 0, kvb + 2)),
                  pl.BlockSpec((None, t, LANES), lambda bi, i: (bi, 0, kvb + 3)),
                  pl.BlockSpec((None, t, LANES), lambda bi, i: (bi, 0, kvb + 4)),
                  pl.BlockSpec((None, t, LANES), lambda bi, i: (bi, 0, kvb + 5)),
                  pl.BlockSpec((NSEL_PAD, LANES), lambda bi, i: (0, 0)),
                  pl.BlockSpec(e3.shape, lambda bi, i: (0, 0))],
        out_specs=pl.BlockSpec((None, tq, NSA_WIDTH), lambda bi, i: (bi, i, 0)),
        out_shape=jax.ShapeDtypeStruct((b, t, NSA_WIDTH), BF16),
        compiler_params=_cparams(("parallel", "arbitrary")),
    )(p3, p3, p3, cmp_tok, p3, p3, p3, p3, ovt, e3)


def _diff_lambda(lam_ref, lam_init):
    a = lam_ref[...]
    s1 = jnp.sum(a[0:1] * a[1:2], axis=-1, keepdims=True)
    s2 = jnp.sum(a[2:3] * a[3:4], axis=-1, keepdims=True)
    return jnp.exp(s1) - jnp.exp(s2) + lam_init


def _diff_finish(o, lnw, z, lam_init):
    ms = jnp.mean(o * o, axis=-1, keepdims=True)
    return o * lax.rsqrt(ms + EPS) * lnw * (1.0 - lam_init) * _silu(z)


def _diff_prompt_body(lam_ref, q_ref, z_ref, k_ref, v_ref, lnw_ref, o_ref, vt_s, acc_s, *, lam_init):
    tq = 128
    i = pl.program_id(1)
    q0 = i * tq

    @pl.when(i == 0)
    def _():
        for hh in range(DIFF_HEADS):
            for c in range(vt_s.shape[1]):
                vt_s[hh, c] = v_ref[c * LANES:(c + 1) * LANES, hh * LANES:(hh + 1) * LANES].T.astype(BF16)

    lam = _diff_lambda(lam_ref, lam_init)
    low = lax.broadcasted_iota(I32, (tq, LANES), 1) < DIFF_HEAD_DIM
    qpos_row = q0 + lax.broadcasted_iota(I32, (1, LANES), 1)
    sub = SEL_CHUNK // LANES
    nch = (i + 2 * sub) // (2 * sub)
    q2 = []
    for hh in range(DIFF_HEADS):
        qs = q_ref[:, hh * LANES:(hh + 1) * LANES] * DIFF_SCALE
        q2.append(jnp.concatenate([jnp.where(low, qs, 0.0), jnp.where(low, 0.0, qs)], axis=0).astype(BF16))
    acc_s[...] = jnp.zeros_like(acc_s)
    ncol = 2 * tq

    def scores(c):
        k0 = pl.multiple_of(c * SEL_CHUNK, SEL_CHUNK)
        return [_dot_nt(k_ref[pl.ds(k0, SEL_CHUNK), hh * LANES:(hh + 1) * LANES].astype(BF16), q2[hh])
                for hh in range(DIFF_HEADS)]

    def update(c, sts, m, l):
        kp = c * SEL_CHUNK + lax.broadcasted_iota(I32, (SEL_CHUNK, LANES), 0)
        maskb = jnp.where(kp <= qpos_row, 0.0, NEG)
        kbf = kp.astype(F32)
        ms, ls = [], []
        for hh in range(DIFF_HEADS):
            st = sts[hh]
            bias = DIFF_SLOPES[hh] * kbf + maskb
            alphas, ps = [], []
            for comp in range(2):
                cs = slice(comp * tq, (comp + 1) * tq)
                gs_ = slice(hh * ncol + comp * tq, hh * ncol + (comp + 1) * tq)
                s = st[:, cs] + bias
                m_new = jnp.maximum(m[:, gs_], jnp.max(s, axis=0, keepdims=True))
                alpha = jnp.exp(m[:, gs_] - m_new)
                p = jnp.exp(s - m_new)
                ms.append(m_new)
                ls.append(alpha * l[:, gs_] + jnp.sum(p, axis=0, keepdims=True))
                alphas.append(alpha)
                ps.append(p.astype(BF16))
            pt = jnp.concatenate(ps, axis=1)
            ot = _dot(vt_s[hh, c * sub], pt[0:LANES])
            for k in range(1, sub):
                ot = ot + _dot(vt_s[hh, c * sub + k], pt[k * LANES:(k + 1) * LANES])
            acc_s[hh] = acc_s[hh] * jnp.concatenate(alphas, axis=1) + ot
        return jnp.concatenate(ms, axis=1), jnp.concatenate(ls, axis=1)

    def body(c2, carry):
        m, l = carry
        a = scores(2 * c2)
        b = scores(2 * c2 + 1)
        m, l = update(2 * c2, a, m, l)
        return update(2 * c2 + 1, b, m, l)

    init = (jnp.full((1, DIFF_HEADS * ncol), NEG, F32), jnp.zeros((1, DIFF_HEADS * ncol), F32))
    _, l = lax.fori_loop(0, nch, body, init)
    lnw = lnw_ref[...]
    slabs = []
    for hh in range(DIFF_HEADS):
        on = acc_s[hh] * (1.0 / l[:, hh * ncol:(hh + 1) * ncol])
        o = (on[:, 0:tq] - lam * on[:, tq:2 * tq]).T
        slabs.append(_diff_finish(o, lnw, z_ref[:, hh * LANES:(hh + 1) * LANES], lam_init))
    o_ref[...] = jnp.concatenate(slabs, axis=1).astype(o_ref.dtype)


def _diff_prompt(p3, lam4, lnw_row, lam_init):
    b, t, _ = p3.shape
    tq = 128
    return pl.pallas_call(
        functools.partial(_diff_prompt_body, lam_init=lam_init),
        grid=(b, t // tq),
        in_specs=[pl.BlockSpec((4, DIFF_HEAD_DIM), lambda bi, i: (0, 0)),
                  pl.BlockSpec((None, tq, DIFF_WIDTH), lambda bi, i: (bi, i, C_DQ // DIFF_WIDTH)),
                  pl.BlockSpec((None, tq, DIFF_WIDTH), lambda bi, i: (bi, i, C_ZD // DIFF_WIDTH)),
                  pl.BlockSpec((None, t, DIFF_WIDTH), lambda bi, i: (bi, 0, C_DK // DIFF_WIDTH)),
                  pl.BlockSpec((None, t, DIFF_WIDTH), lambda bi, i: (bi, 0, C_DV // DIFF_WIDTH)),
                  pl.BlockSpec((1, DIFF_V_DIM), lambda bi, i: (0, 0))],
        out_specs=pl.BlockSpec((None, tq, DIFF_WIDTH), lambda bi, i: (bi, i, 0)),
        out_shape=jax.ShapeDtypeStruct((b, t, DIFF_WIDTH), BF16),
        scratch_shapes=[pltpu.VMEM((DIFF_HEADS, t // LANES, LANES, LANES), BF16),
                        pltpu.VMEM((DIFF_HEADS, DIFF_V_DIM, 2 * tq), F32)],
        compiler_params=_cparams(("parallel", "arbitrary")),
    )(lam4, p3, p3, p3, p3, lnw_row)


SSM_BLOCKS = SSM_WIDTH // LANES
SSM_BS = N_STATE // SSM_BLOCKS
SSM_CHUNK = 256
SSM_SEG = SSM_CHUNK // SUBLANES
SSM_UNROLL = 32


def _discretize(lre, lim, ldt):
    dt = jnp.exp(ldt)
    mag = jnp.exp(lre * dt)
    ar = mag * jnp.cos(lim * dt)
    ai = mag * jnp.sin(lim * dt)
    den = lre * lre + lim * lim
    qr = ((ar - 1.0) * lre + ai * lim) / den
    qi = (ai * lre - (ar - 1.0) * lim) / den
    return ar, ai, qr, qi


def _bbar_block(qr, qi, bre, bim):
    return jnp.concatenate([qr * bre - qi * bim, qr * bim + qi * bre], axis=1).astype(BF16)


def _glu_tail(y, wglu, z):
    gate = _sigmoid(_dot(y.astype(BF16), wglu))
    return y * gate * _silu(z)


def _ssm_prompt_body(u_ref, z_ref, lre_ref, lim_ref, ldt_ref, bre_ref, bim_ref, cmat_ref, d_ref, wglu_ref,
                     perm_ref, o_ref, hr_ref, hi_ref,
                     bbar_s, apr_s, api_s, bu_s, y_s, carr_s, cari_s):
    first = (pl.program_id(0) == 0) & (pl.program_id(1) == 0)

    @pl.when(first)
    def _():
        ar, ai, qr, qi = _discretize(lre_ref[...], lim_ref[...], ldt_ref[...])
        for k in range(SSM_BLOCKS):
            cs = slice(k * SSM_BS, (k + 1) * SSM_BS)
            bbar_s[k] = _bbar_block(qr[:, cs], qi[:, cs], bre_ref[k], bim_ref[k])
        pr, pi = ar, ai
        apr_s[0:1, :] = pr
        api_s[0:1, :] = pi
        for k in range(1, SSM_SEG):
            pr, pi = pr * ar - pi * ai, pr * ai + pi * ar
            apr_s[k:k + 1, :] = pr
            api_s[k:k + 1, :] = pi

    @pl.when(pl.program_id(1) == 0)
    def _():
        carr_s[...] = jnp.zeros_like(carr_s)
        cari_s[...] = jnp.zeros_like(cari_s)

    re_c = slice(0, SSM_BS)
    im_c = slice(SSM_BS, 2 * SSM_BS)
    u_perm = _dot(perm_ref[0], u_ref[...].astype(BF16)).astype(BF16)
    unperm = perm_ref[1]

    for k in range(SSM_BLOCKS):
        cs = slice(k * SSM_BS, (k + 1) * SSM_BS)
        us = slice(k * LANES, (k + 1) * LANES)
        bu_s[...] = _dot(u_perm[:, us], bbar_s[k])
        ar_b = jnp.broadcast_to(apr_s[0:1, cs], (SUBLANES, SSM_BS))
        ai_b = jnp.broadcast_to(api_s[0:1, cs], (SUBLANES, SSM_BS))

        def step(t, st, ar_b=ar_b, ai_b=ai_b):
            hr, hi = st
            rows = pl.ds(pl.multiple_of(t * SUBLANES, SUBLANES), SUBLANES)
            nr = ar_b * hr - ai_b * hi + bu_s[rows, re_c]
            ni = ar_b * hi + ai_b * hr + bu_s[rows, im_c]
            bu_s[rows, re_c] = nr
            bu_s[rows, im_c] = ni
            return nr, ni

        zero = jnp.zeros((SUBLANES, SSM_BS), F32)
        er, ei = lax.fori_loop(0, SSM_SEG, step, (zero, zero), unroll=SSM_UNROLL)
        a32r = apr_s[SSM_SEG - 1:SSM_SEG, cs]
        a32i = api_s[SSM_SEG - 1:SSM_SEG, cs]
        rows_r = [carr_s[:, cs]]
        rows_i = [cari_s[:, cs]]
        for s in range(1, SUBLANES + 1):
            pr, pi = rows_r[-1], rows_i[-1]
            rows_r.append(er[s - 1:s] + a32r * pr - a32i * pi)
            rows_i.append(ei[s - 1:s] + a32r * pi + a32i * pr)
        carr_s[:, cs] = rows_r[SUBLANES]
        cari_s[:, cs] = rows_i[SUBLANES]
        hin_r = jnp.concatenate(rows_r[:SUBLANES], axis=0)
        hin_i = jnp.concatenate(rows_i[:SUBLANES], axis=0)

        def fix(t, carry, hin_r=hin_r, hin_i=hin_i, cs=cs):
            rows = pl.ds(pl.multiple_of(t * SUBLANES, SUBLANES), SUBLANES)
            pr = apr_s[pl.ds(t, 1), cs]
            pi = api_s[pl.ds(t, 1), cs]
            bu_s[rows, re_c] = bu_s[rows, re_c] + pr * hin_r - pi * hin_i
            bu_s[rows, im_c] = bu_s[rows, im_c] + pr * hin_i + pi * hin_r
            return carry

        lax.fori_loop(0, SSM_SEG, fix, 0, unroll=SSM_UNROLL)
        yp = _dot(bu_s[...].astype(BF16), cmat_ref[k])
        y_hi = yp.astype(BF16)
        r1 = yp - y_hi.astype(F32)
        y_mid = r1.astype(BF16)
        y_lo = (r1 - y_mid.astype(F32)).astype(BF16)
        yk = _dot(unperm, y_hi) + _dot(unperm, y_mid) + _dot(unperm, y_lo) + d_ref[:, us] * u_ref[:, us]
        y_s[:, us] = _gelu(yk)

    o_ref[...] = _glu_tail(y_s[...], wglu_ref[...], z_ref[...]).astype(o_ref.dtype)
    hr_ref[...] = carr_s[...]
    hi_ref[...] = cari_s[...]


def _ssm_prompt(p3, sp):
    b, t, _ = p3.shape
    full = lambda shape: pl.BlockSpec(shape, lambda bi, j: (0,) * len(shape))
    out, hr, hi = pl.pallas_call(
        _ssm_prompt_body,
        grid=(b, t // SSM_CHUNK),
        in_specs=[pl.BlockSpec((None, SSM_CHUNK, SSM_WIDTH), lambda bi, j: (bi, j, C_U // SSM_WIDTH)),
                  pl.BlockSpec((None, SSM_CHUNK, SSM_WIDTH), lambda bi, j: (bi, j, C_ZS // SSM_WIDTH)),
                  full((1, N_STATE)), full((1, N_STATE)), full((1, N_STATE)),
                  full((SSM_BLOCKS, LANES, SSM_BS)), full((SSM_BLOCKS, LANES, SSM_BS)),
                  full((SSM_BLOCKS, 2 * SSM_BS, LANES)), full((1, SSM_WIDTH)),
                  full((SSM_WIDTH, SSM_WIDTH)), full((2, SSM_CHUNK, SSM_CHUNK))],
        out_specs=[pl.BlockSpec((None, SSM_CHUNK, SSM_WIDTH), lambda bi, j: (bi, j, 0)),
                   pl.BlockSpec((None, 1, N_STATE), lambda bi, j: (bi, 0, 0)),
                   pl.BlockSpec((None, 1, N_STATE), lambda bi, j: (bi, 0, 0))],
        out_shape=[jax.ShapeDtypeStruct((b, t, SSM_WIDTH), BF16),
                   jax.ShapeDtypeStruct((b, 1, N_STATE), F32),
                   jax.ShapeDtypeStruct((b, 1, N_STATE), F32)],
        scratch_shapes=[pltpu.VMEM((SSM_BLOCKS, LANES, 2 * SSM_BS), BF16),
                        pltpu.VMEM((SSM_SEG, N_STATE), F32), pltpu.VMEM((SSM_SEG, N_STATE), F32),
                        pltpu.VMEM((SSM_CHUNK, 2 * SSM_BS), F32),
                        pltpu.VMEM((SSM_CHUNK, SSM_WIDTH), F32),
                        pltpu.VMEM((1, N_STATE), F32), pltpu.VMEM((1, N_STATE), F32)],
        compiler_params=_cparams(("arbitrary", "arbitrary")),
    )(p3, p3, sp['lre'], sp['lim'], sp['ldt'], sp['bre'], sp['bim'], sp['cmat'], sp['d'], sp['wglu'],
      _segment_permutation())
    return out, hr, hi


def _segment_permutation():
    r = np.arange(SSM_CHUNK)
    src = (r % SUBLANES) * SSM_SEG + r // SUBLANES
    pm = np.zeros((SSM_CHUNK, SSM_CHUNK), np.float32)
    pm[r, src] = 1.0
    return jnp.asarray(np.stack([pm, pm.T]), BF16)


def _ssm_sample_body(u_ref, z_ref, h0r_ref, h0i_ref, lre_ref, lim_ref, ldt_ref, bre_ref, bim_ref, cmat_ref,
                     d_ref, wglu_ref, o_ref, hr_ref, hi_ref, y_s, *, n_t):
    k = pl.program_id(0)
    ar, ai, qr, qi = _discretize(lre_ref[...], lim_ref[...], ldt_ref[...])
    bbar = _bbar_block(qr, qi, bre_ref[...], bim_ref[...])
    hr = h0r_ref[...]
    hi = h0i_ref[...]
    cmat = cmat_ref[...]
    for t in range(n_t):
        ut = u_ref[t]
        bu = _dot(ut.astype(BF16), bbar)
        hr, hi = ar * hr - ai * hi + bu[:, 0:SSM_BS], ar * hi + ai * hr + bu[:, SSM_BS:]
        hcat = jnp.concatenate([hr, hi], axis=1).astype(BF16)
        y_s[t, k] = _gelu(_dot(hcat, cmat) + d_ref[...] * ut)
    hr_ref[...] = hr
    hi_ref[...] = hi

    @pl.when(k == SSM_BLOCKS - 1)
    def _():
        wglu = wglu_ref[...]
        for t in range(n_t):
            y = jnp.concatenate([y_s[t, kk] for kk in range(SSM_BLOCKS)], axis=1)
            o_ref[t] = _glu_tail(y, wglu, z_ref[t])


def _ssm_sample(ut, zt, h0r, h0i, sp):
    n_t, nb, _ = ut.shape
    out, hr, hi = pl.pallas_call(
        functools.partial(_ssm_sample_body, n_t=n_t),
        grid=(SSM_BLOCKS,),
        in_specs=[pl.BlockSpec((n_t, nb, LANES), lambda k: (0, 0, k)),
                  pl.BlockSpec((n_t, nb, SSM_WIDTH), lambda k: (0, 0, 0)),
                  pl.BlockSpec((nb, SSM_BS), lambda k: (0, k)),
                  pl.BlockSpec((nb, SSM_BS), lambda k: (0, k)),
                  pl.BlockSpec((1, SSM_BS), lambda k: (0, k)),
                  pl.BlockSpec((1, SSM_BS), lambda k: (0, k)),
                  pl.BlockSpec((1, SSM_BS), lambda k: (0, k)),
                  pl.BlockSpec((None, LANES, SSM_BS), lambda k: (k, 0, 0)),
                  pl.BlockSpec((None, LANES, SSM_BS), lambda k: (k, 0, 0)),
                  pl.BlockSpec((None, 2 * SSM_BS, LANES), lambda k: (k, 0, 0)),
                  pl.BlockSpec((1, LANES), lambda k: (0, k)),
                  pl.BlockSpec((SSM_WIDTH, SSM_WIDTH), lambda k: (0, 0))],
        out_specs=[pl.BlockSpec((n_t, nb, SSM_WIDTH), lambda k: (0, 0, 0)),
                   pl.BlockSpec((nb, SSM_BS), lambda k: (0, k)),
                   pl.BlockSpec((nb, SSM_BS), lambda k: (0, k))],
        out_shape=[jax.ShapeDtypeStruct((n_t, nb, SSM_WIDTH), F32),
                   jax.ShapeDtypeStruct((nb, N_STATE), F32),
                   jax.ShapeDtypeStruct((nb, N_STATE), F32)],
        scratch_shapes=[pltpu.VMEM((n_t, SSM_BLOCKS, nb, LANES), F32)],
        compiler_params=_cparams(("arbitrary",)),
    )(ut, zt, h0r, h0i, sp['lre'], sp['lim'], sp['ldt'], sp['bre'], sp['bim'], sp['cmat'], sp['d'], sp['wglu'])
    return out, hr, hi


def _page_map(li, pg, b, pt):
    return (li, pt[b, pg], 0, 0)


def _page_map_seq(li, pg, sq, b, pt):
    return (li, pt[b * NSA_DEC_SEQS + sq, pg], 0, 0)


def _pad_rows(x, rows):
    return jnp.concatenate([x, jnp.zeros((rows - x.shape[0], x.shape[1]), x.dtype)], axis=0)


def _nsa_sample_body(pt_ref, *refs, n_pages, n_sel):
    nsq = NSA_DEC_SEQS
    (q_ref, z_ref, g_ref, snew_ref, wnew_ref, wb_ref, wab_ref, pc_ref, w2_ref, ovt_ref, e_ref,
     gperm_ref, o_ref, s_s) = refs[nsq * n_pages:]
    seqs = range(nsq)
    pages = [refs[sq * n_pages:(sq + 1) * n_pages] for sq in seqs]
    rs = [slice(sq * T_PAD, (sq + 1) * T_PAD) for sq in seqs]
    past = n_pages * PAGE_SIZE
    nk = past + LANES
    wb = wb_ref.shape[2]
    qpos_col = past + lax.broadcasted_iota(I32, (T_PAD, 1), 0)
    qpos64 = jnp.concatenate([qpos_col] * NSA_HEADS, axis=0)
    qpos_row = past + (lax.broadcasted_iota(I32, (1, LANES), 1) % T_PAD)
    slope_col = jnp.concatenate([jnp.full((T_PAD, 1), NSA_SLOPES[hd], F32) for hd in range(NSA_HEADS)], axis=0)
    kpos = lax.broadcasted_iota(I32, (1, nk), 1)
    kbias = slope_col * kpos.astype(F32)
    kpos_w = (past - wb) + lax.broadcasted_iota(I32, (1, wb + LANES), 1)
    wbias = slope_col * kpos_w.astype(F32)
    dist = qpos64 - kpos_w
    wmask = (dist >= 0) & (dist < WINDOW) & (kpos_w >= 0)
    causal = kpos <= qpos64
    sk_r = pl.ds(2 * LANES, LANES)
    sv_r = pl.ds(3 * LANES, LANES)

    qh = [_pad_heads(q_ref[rs[sq], :] * SCALE) for sq in seqs]
    q64 = [jnp.concatenate(qh[sq], axis=0) for sq in seqs]

    gperm = gperm_ref[...]
    gathered = [[[_dot(pages[sq][pg][kv * LANES:(kv + 1) * LANES, :].astype(BF16), gperm).T
                  for pg in range(n_pages)] for sq in seqs] for kv in range(2)]

    owin = []
    for sq in seqs:
        wnew = wnew_ref[rs[sq], :]
        kwn = _pad_rows(wnew[:, 0:LANES], LANES).astype(BF16)
        vwn = _pad_rows(wnew[:, LANES:2 * LANES], LANES).astype(BF16)
        sw = jnp.concatenate([_dot(q64[sq], wb_ref[sq, 0:LANES, :].astype(BF16)), _dot_nt(q64[sq], kwn)], axis=1)
        pw = _msoftmax(sw + wbias, wmask).astype(BF16)
        owin.append(_dot_nt(pw[:, 0:wb], wb_ref[sq, LANES:2 * LANES, :].astype(BF16)) + _dot(pw[:, wb:], vwn))

    for sq in seqs:
        for pg in range(n_pages):
            s_s[sq, :, pg * PAGE_SIZE:(pg + 1) * PAGE_SIZE] = _dot(q64[sq], pages[sq][pg][sk_r, :].astype(BF16))
        knew = _pad_rows(snew_ref[rs[sq], 0:LANES], LANES).astype(BF16)
        s_s[sq, :, past:nk] = _dot_nt(q64[sq], knew)

    toks = []
    npc = PAGE_SIZE // CMP_STRIDE
    for kv in range(2):
        xseq = []
        for sq in seqs:
            xs = [jnp.concatenate([gathered[kv][sq][pg][l * npc:(l + 1) * npc] for pg in range(n_pages)],
                                  axis=0).astype(BF16) for l in range(CMP_STRIDE)]
            xseq.append(jnp.concatenate(xs, axis=1))
        toks.append(_compress_tokens(xseq, wab_ref[kv], pc_ref[kv], w2_ref[kv]))
    cmp = [_cmp_branch(qh[sq], toks[0][sq], toks[1][sq], qpos_col) for sq in seqs]

    memb = []
    for sq in seqs:
        p_cmp = cmp[sq][1]
        psum = jnp.concatenate(
            [p_cmp[0] + p_cmp[1] + p_cmp[2] + p_cmp[3], p_cmp[4] + p_cmp[5] + p_cmp[6] + p_cmp[7],
             jnp.zeros((LANES - 2 * T_PAD, N_CMP_PAD), F32)], axis=0)
        sel = _select_blocks(psum, qpos_row, ovt_ref[...], n_sel)
        mexp = _dot(sel[0:2 * T_PAD].astype(BF16), e_ref[...])
        memb.append(
            jnp.concatenate([mexp[0:T_PAD]] * NSA_GROUP + [mexp[T_PAD:2 * T_PAD]] * NSA_GROUP, axis=0) > 0.5)

    for sq in seqs:
        p = _msoftmax(s_s[sq] + kbias, memb[sq] & causal).astype(BF16)
        vnew = _pad_rows(snew_ref[rs[sq], LANES:2 * LANES], LANES).astype(BF16)
        osel = _dot(p[:, past:nk], vnew)
        for pg in range(n_pages):
            osel = osel + _dot_nt(p[:, pg * PAGE_SIZE:(pg + 1) * PAGE_SIZE], pages[sq][pg][sv_r, :].astype(BF16))
        o_sel = [osel[hd * T_PAD:(hd + 1) * T_PAD] for hd in range(NSA_HEADS)]
        o_win = [owin[sq][hd * T_PAD:(hd + 1) * T_PAD] for hd in range(NSA_HEADS)]
        o_ref[rs[sq], :] = _combine_heads(_sigmoid(g_ref[rs[sq], :]), cmp[sq][0], o_sel, o_win, z_ref[rs[sq], :])


def _nsa_sample(p2, cache_t, li, win_t, page_table, wab, pc, w2bd, ovt, e2, n_t):
    n = p2.shape[0]
    nb, n_pages = page_table.shape
    past = n_pages * PAGE_SIZE
    n_sel = -(-(past + n_t) // SEL_BLOCK)
    nk = past + LANES
    wb = win_t.shape[3]
    kvb = C_KV // (2 * LANES)
    nsq = NSA_DEC_SEQS
    assert nb % nsq == 0
    page_specs = [pl.BlockSpec((None, None, 4 * LANES, PAGE_SIZE), functools.partial(_page_map_seq, li, pg, sq))
                  for sq in range(nsq) for pg in range(n_pages)]
    const = lambda shape: pl.BlockSpec(shape, lambda b, pt: (0,) * len(shape))
    rows = nsq * T_PAD
    in_specs = page_specs + [
        pl.BlockSpec((rows, NSA_WIDTH), lambda b, pt: (b, C_Q // NSA_WIDTH)),
        pl.BlockSpec((rows, NSA_WIDTH), lambda b, pt: (b, C_ZN // NSA_WIDTH)),
        pl.BlockSpec((rows, LANES), lambda b, pt: (b, C_G // LANES)),
        pl.BlockSpec((rows, 2 * LANES), lambda b, pt: (b, kvb + 1)),
        pl.BlockSpec((rows, 2 * LANES), lambda b, pt: (b, kvb + 2)),
        pl.BlockSpec((None, nsq, 2 * LANES, wb), lambda b, pt: (li, b, 0, 0)),
        const(wab.shape), const(pc.shape), const(w2bd.shape),
        const(ovt.shape), const(e2.shape), const((PAGE_SIZE, PAGE_SIZE))]
    return pl.pallas_call(
        functools.partial(_nsa_sample_body, n_pages=n_pages, n_sel=n_sel),
        grid_spec=pltpu.PrefetchScalarGridSpec(
            num_scalar_prefetch=1, grid=(nb // nsq,), in_specs=in_specs,
            out_specs=pl.BlockSpec((rows, NSA_WIDTH), lambda b, pt: (b, 0)),
            scratch_shapes=[pltpu.VMEM((nsq, NSA_HEADS * T_PAD, nk), F32)]),
        out_shape=jax.ShapeDtypeStruct((n, NSA_WIDTH), F32),
        compiler_params=_cparams(("parallel",)),
    )(page_table, *([cache_t] * (nsq * n_pages)), p2, p2, p2, p2, p2, win_t, wab, pc, w2bd, ovt, e2,
      _chunk_gather_permutation())


def _chunk_gather_permutation():
    k = np.arange(PAGE_SIZE)
    pm = np.zeros((PAGE_SIZE, PAGE_SIZE), np.float32)
    pm[k, (k % CMP_STRIDE) * (PAGE_SIZE // CMP_STRIDE) + k // CMP_STRIDE] = 1.0
    return jnp.asarray(pm, BF16)


def _diff_sample_body(pt_ref, *refs, n_pages, lam_init):
    pages = refs[:n_pages]
    lam_ref, q_ref, z_ref, knew_ref, vnew_ref, lnw_ref, o_ref, s_s = refs[n_pages:]
    past = n_pages * PAGE_SIZE
    nk = past + LANES
    lam = _diff_lambda(lam_ref, lam_init)
    low = lax.broadcasted_iota(I32, (T_PAD, LANES), 1) < DIFF_HEAD_DIM
    rows_h = 2 * T_PAD
    q16 = []
    for hh in range(DIFF_HEADS):
        qs = q_ref[:, hh * LANES:(hh + 1) * LANES] * DIFF_SCALE
        q16.append(jnp.concatenate([jnp.where(low, qs, 0.0), jnp.where(low, 0.0, qs)], axis=0).astype(BF16))
    for pg in range(n_pages):
        for hh in range(DIFF_HEADS):
            kp = pages[pg][pl.ds(hh, PAGE_SIZE, stride=DIFF_SLABS), :].astype(BF16)
            s_s[hh * rows_h:(hh + 1) * rows_h, pg * PAGE_SIZE:(pg + 1) * PAGE_SIZE] = _dot_nt(q16[hh], kp)
    for hh in range(DIFF_HEADS):
        kn = _pad_rows(knew_ref[:, hh * LANES:(hh + 1) * LANES], LANES).astype(BF16)
        s_s[hh * rows_h:(hh + 1) * rows_h, past:nk] = _dot_nt(q16[hh], kn)
    kpos = lax.broadcasted_iota(I32, (1, nk), 1)
    slope_col = jnp.concatenate([jnp.full((rows_h, 1), DIFF_SLOPES[hh], F32) for hh in range(DIFF_HEADS)], axis=0)
    qpos = past + (lax.broadcasted_iota(I32, (DIFF_HEADS * rows_h, 1), 0) % T_PAD)
    p = _msoftmax(s_s[...] + slope_col * kpos.astype(F32), kpos <= qpos)
    lnw = lnw_ref[...]
    slabs = []
    for hh in range(DIFF_HEADS):
        a = (p[hh * rows_h:hh * rows_h + T_PAD] - lam * p[hh * rows_h + T_PAD:(hh + 1) * rows_h]).astype(BF16)
        vn = _pad_rows(vnew_ref[:, hh * LANES:(hh + 1) * LANES], LANES).astype(BF16)
        o = _dot(a[:, past:nk], vn)
        for pg in range(n_pages):
            vp = pages[pg][pl.ds(DIFF_HEADS + hh, PAGE_SIZE, stride=DIFF_SLABS), :].astype(BF16)
            o = o + _dot(a[:, pg * PAGE_SIZE:(pg + 1) * PAGE_SIZE], vp)
        slabs.append(_diff_finish(o, lnw, z_ref[:, hh * LANES:(hh + 1) * LANES], lam_init))
    o_ref[...] = jnp.concatenate(slabs, axis=1)


def _diff_sample(p2, cache_r, li, page_table, lam4, lnw_row, lam_init):
    n = p2.shape[0]
    nb, n_pages = page_table.shape
    nk = n_pages * PAGE_SIZE + LANES
    page_specs = [pl.BlockSpec((None, None, PAGE_SIZE * DIFF_SLABS, LANES), functools.partial(_page_map, li, pg))
                  for pg in range(n_pages)]
    in_specs = page_specs + [
        pl.BlockSpec((4, DIFF_HEAD_DIM), lambda b, pt: (0, 0)),
        pl.BlockSpec((T_PAD, DIFF_WIDTH), lambda b, pt: (b, C_DQ // DIFF_WIDTH)),
        pl.BlockSpec((T_PAD, DIFF_WIDTH), lambda b, pt: (b, C_ZD // DIFF_WIDTH)),
        pl.BlockSpec((T_PAD, DIFF_WIDTH), lambda b, pt: (b, C_DK // DIFF_WIDTH)),
        pl.BlockSpec((T_PAD, DIFF_WIDTH), lambda b, pt: (b, C_DV // DIFF_WIDTH)),
        pl.BlockSpec((1, DIFF_V_DIM), lambda b, pt: (0, 0))]
    return pl.pallas_call(
        functools.partial(_diff_sample_body, n_pages=n_pages, lam_init=lam_init),
        grid_spec=pltpu.PrefetchScalarGridSpec(
            num_scalar_prefetch=1, grid=(nb,), in_specs=in_specs,
            out_specs=pl.BlockSpec((T_PAD, DIFF_WIDTH), lambda b, pt: (b, 0)),
            scratch_shapes=[pltpu.VMEM((DIFF_HEADS * 2 * T_PAD, nk), F32)]),
        out_shape=jax.ShapeDtypeStruct((n, DIFF_WIDTH), F32),
        compiler_params=_cparams(("parallel",)),
    )(page_table, *([cache_r] * n_pages), lam4, p2, p2, p2, p2, lnw_row)


WIN_ROLL_SEQS = 8


def _roll_window_body(win_ref, *refs, n_t):
    new_refs, o_ref = refs[:-1], refs[-1]
    li = pl.program_id(0)
    nbk, f, wb = win_ref.shape
    new_all = new_refs[0][...]
    for d in range(1, len(new_refs)):
        new_all = jnp.where(li == d, new_refs[d][...], new_all)
    tail = lax.broadcasted_iota(I32, (f, LANES), 1) >= LANES - n_t
    for bb in range(nbk):
        rolled = pltpu.roll(win_ref[bb], wb - n_t, 1)
        new_t = _pad_rows(new_all[bb * T_PAD:(bb + 1) * T_PAD], LANES).T
        new_t = pltpu.roll(new_t, LANES - n_t, 1)
        o_ref[bb, :, 0:wb - LANES] = rolled[:, 0:wb - LANES]
        o_ref[bb, :, wb - LANES:wb] = jnp.where(tail, new_t, rolled[:, wb - LANES:wb])


def _roll_window(win_t, ps_list, n_t):
    depth, nb, f, wb = win_t.shape
    nbk = WIN_ROLL_SEQS
    wcol = (C_KV + 4 * LANES) // f
    return pl.pallas_call(
        functools.partial(_roll_window_body, n_t=n_t),
        grid=(depth, nb // nbk),
        in_specs=[pl.BlockSpec((None, nbk, f, wb), lambda li, i: (li, i, 0, 0))]
        + [pl.BlockSpec((nbk * T_PAD, f), lambda li, i: (i, wcol)) for _ in ps_list],
        out_specs=pl.BlockSpec((None, nbk, f, wb), lambda li, i: (li, i, 0, 0)),
        out_shape=jax.ShapeDtypeStruct(win_t.shape, F32),
        compiler_params=_cparams(("parallel", "parallel")),
    )(win_t, *ps_list)


def _overlap_t(n_sel):
    n = np.arange(N_CMP_PAD)[None, :]
    j = np.arange(NSEL_PAD)[:, None]
    cs = n * CMP_STRIDE
    ss = j * SEL_BLOCK
    ov = np.clip(np.minimum(cs + CMP_BLOCK, ss + SEL_BLOCK) - np.maximum(cs, ss), 0, None) / CMP_BLOCK
    ov = np.where((n < N_CMP_PAD - 1) & (j < n_sel), ov, 0.0)
    return jnp.asarray(ov, BF16)


def _block_expander(n_keys):
    j = np.arange(LANES)[:, None]
    k = np.arange(n_keys)[None, :]
    return jnp.asarray((k // SEL_BLOCK) == j, BF16)


def _layer_weights(li, norm_w, w_in, w_out, w_cmp1, pos_cmp, w_cmp2, lre, lim, ldt, bre, bim, cre, cim, sd,
                   w_glu, lam_q1, lam_k1, lam_q2, lam_k2, diff_ln_w):
    w = {}
    w['nw'] = norm_w[li][None, :]
    w['w_out'] = w_out[li].astype(BF16)
    eye2 = jnp.eye(NSA_KV_HEADS, dtype=F32)
    w1 = w_cmp1[li].reshape(2, 2, CMP_STRIDE, HEAD_DIM, HEAD_DIM)
    w['wab'] = jnp.einsum('kaldf,hg->klhdagf', w1, eye2).reshape(
        2, CMP_STRIDE * LANES, 2 * LANES).astype(BF16)
    pos = pos_cmp[li].reshape(2, 2, CMP_STRIDE, 1, HEAD_DIM)
    pos = jnp.broadcast_to(pos, (2, 2, CMP_STRIDE, NSA_KV_HEADS, HEAD_DIM)).reshape(2, 2, 1, CMP_STRIDE * LANES)
    pos = jnp.broadcast_to(pos, (2, 2, SUBLANES, CMP_STRIDE * LANES)).astype(BF16)
    w['pos_a'] = pos[:, 0]
    w['pos_b'] = pos[:, 1]
    w['w2bd'] = jnp.einsum('ked,hg->khegd', w_cmp2[li], eye2).reshape(2, LANES, LANES).astype(BF16)
    eye8 = jnp.eye(SSM_BLOCKS, dtype=F32)
    gl = SSM_BLOCKS

    def compact_b(bm):
        bm = bm.reshape(SSM_BLOCKS, gl, SSM_STATE, SSM_GROUP_CH)
        return jnp.einsum('kgpc,gh->kgchp', bm, eye8).reshape(SSM_BLOCKS, LANES, SSM_BS)

    def compact_c(cm):
        cm = cm.reshape(SSM_BLOCKS, gl, SSM_GROUP_CH, SSM_STATE)
        return jnp.einsum('kgcp,gh->khpgc', cm, eye8).reshape(SSM_BLOCKS, SSM_BS, LANES)

    sp = {'lre': lre[li].reshape(1, N_STATE), 'lim': lim[li].reshape(1, N_STATE),
          'ldt': jnp.repeat(ldt[li], SSM_STATE).reshape(1, N_STATE),
          'bre': compact_b(bre[li]), 'bim': compact_b(bim[li]),
          'cmat': jnp.concatenate([compact_c(cre[li]), -compact_c(cim[li])], axis=1).astype(BF16),
          'd': sd[li].reshape(1, SSM_WIDTH), 'wglu': w_glu[li].astype(BF16)}
    w['ssm'] = sp
    w['lam4'] = jnp.stack([lam_q1[li], lam_k1[li], lam_q2[li], lam_k2[li]])
    w['lnw'] = diff_ln_w[li][None, :]
    w['lam_init'] = 0.8 - 0.6 * math.exp(-0.3 * li)
    return w


def kernel(x_prompt, x_sample, cache_nsa_kv, cache_diff_kv, state_nsa_win, state_ssm_re, state_ssm_im, page_table, norm_w, w_in, w_out, w_cmp1, pos_cmp, w_cmp2, ssm_lambda_re, ssm_lambda_im, ssm_log_dt, ssm_b_re, ssm_b_im, ssm_c_re, ssm_c_im, ssm_d, w_glu, lam_q1, lam_k1, lam_q2, lam_k2, diff_ln_w, final_norm_w):
    depth = norm_w.shape[0]
    b, t, _ = x_prompt.shape
    nb, n_t, _ = x_sample.shape
    n_pool = cache_nsa_kv.shape[1]
    n_pages = page_table.shape[1]
    past = n_pages * PAGE_SIZE
    wb = state_nsa_win.shape[2]
    assert t % SSM_CHUNK == 0 and t >= WINDOW + 128 and n_t <= T_PAD

    ovt_p = _overlap_t(-(-t // SEL_BLOCK))
    ovt_s = _overlap_t(-(-(past + n_t) // SEL_BLOCK))
    e_p = _block_expander(t).T
    e_s = _block_expander(past + LANES)
    fw = final_norm_w[None, :]
    nsa_cache_t = cache_nsa_kv.transpose(0, 1, 3, 4, 5, 2).reshape(depth, n_pool, 4 * LANES, PAGE_SIZE)
    win_t = state_nsa_win.transpose(0, 1, 3, 4, 5, 2).reshape(depth, nb, 2 * LANES, wb)
    diff_cache_r = cache_diff_kv.reshape(depth, n_pool, PAGE_SIZE * DIFF_SLABS, LANES)
    w_rows = w_in.transpose(0, 2, 1)
    w_in_t = jnp.concatenate(
        [w_rows[:, ORIG_OFFS[s]:ORIG_OFFS[s + 1]] for s in NEW_ORDER]
        + [jnp.zeros((depth, DP - ORIG_OFFS[-1], D_MODEL), w_in.dtype)], axis=1).astype(BF16)

    xp = x_prompt.reshape(b * t, D_MODEL)
    xs = jnp.pad(x_sample, ((0, 0), (0, T_PAD - n_t), (0, 0))).reshape(nb * T_PAD, D_MODEL)
    outs = {k: [] for k in ('p_kv', 'p_dkv', 'p_win', 'p_re', 'p_im', 's_kv', 's_dkv', 's_win', 's_re', 's_im')}
    for li in range(depth):
        w = _layer_weights(li, norm_w, w_in, w_out, w_cmp1, pos_cmp, w_cmp2, ssm_lambda_re, ssm_lambda_im,
                           ssm_log_dt, ssm_b_re, ssm_b_im, ssm_c_re, ssm_c_im, ssm_d, w_glu,
                           lam_q1, lam_k1, lam_q2, lam_k2, diff_ln_w)
        final = li == depth - 1
        pp, p_dkv_rows = _norm_project(xp, w['nw'], w_in_t, li)
        p3 = pp.reshape(b, t, DP)
        pos_c = _compress_pos_const(w['pos_a'], w['pos_b'], w['wab'])
        cmp_tok = _compress_prompt(p3, w['wab'], pos_c, w['w2bd'])
        nsa_o = _nsa_prompt(p3, cmp_tok, ovt_p, e_p)
        ssm_o, hr, hi = _ssm_prompt(p3, w['ssm'])
        diff_o = _diff_prompt(p3, w['lam4'], w['lnw'], w['lam_init'])
        xp = _merge_out(nsa_o.reshape(b * t, -1), ssm_o.reshape(b * t, -1), diff_o.reshape(b * t, -1), xp,
                        w['w_out'], fw, final)
        kv = p3[:, :, C_KV:C_KV + 6 * LANES]
        outs['p_kv'].append(kv[:, :, :4 * LANES].reshape(b, t, 4, NSA_KV_HEADS, HEAD_DIM))
        outs['p_dkv'].append(p_dkv_rows.reshape(b, t, 2, DIFF_HEADS, DIFF_V_DIM))
        outs['p_win'].append(kv[:, t - min(WINDOW, t):, 4 * LANES:].reshape(b, -1, 2, NSA_KV_HEADS, HEAD_DIM))
        outs['p_re'].append(hr.reshape(b, SSM_GROUPS, SSM_STATE))
        outs['p_im'].append(hi.reshape(b, SSM_GROUPS, SSM_STATE))
        ps, s_dkv_rows = _norm_project(xs, w['nw'], w_in_t, li)
        nsa_s = _nsa_sample(ps, nsa_cache_t, li, win_t, page_table,
                            w['wab'], pos_c, w['w2bd'], ovt_s, e_s, n_t)
        ps_t = ps.reshape(nb, T_PAD, DP)[:, :n_t].transpose(1, 0, 2)
        ssm_t, shr, shi = _ssm_sample(ps_t[:, :, C_U:C_U + SSM_WIDTH], ps_t[:, :, C_ZS:C_ZS + SSM_WIDTH],
                                      state_ssm_re[li].reshape(nb, N_STATE),
                                      state_ssm_im[li].reshape(nb, N_STATE), w['ssm'])
        ssm_s = jnp.pad(ssm_t.transpose(1, 0, 2), ((0, 0), (0, T_PAD - n_t), (0, 0))).reshape(nb * T_PAD, SSM_WIDTH)
        diff_s = _diff_sample(ps, diff_cache_r, li, page_table, w['lam4'], w['lnw'], w['lam_init'])
        xs = _merge_out(nsa_s, ssm_s, diff_s, xs, w['w_out'], fw, final)
        ps3 = ps.reshape(nb, T_PAD, DP)[:, :n_t]
        skv = ps3[:, :, C_KV:C_KV + 6 * LANES]
        outs['s_kv'].append(skv[:, :, :4 * LANES].reshape(nb, n_t, 4, NSA_KV_HEADS, HEAD_DIM))
        outs['s_dkv'].append(s_dkv_rows.reshape(nb, T_PAD, 2, DIFF_HEADS, DIFF_V_DIM)[:, :n_t])
        outs['s_win'].append(ps)
        outs['s_re'].append(shr.reshape(nb, SSM_GROUPS, SSM_STATE))
        outs['s_im'].append(shi.reshape(nb, SSM_GROUPS, SSM_STATE))
    y_prompt = xp.reshape(b, t, D_MODEL)
    y_sample = xs.reshape(nb, T_PAD, D_MODEL)[:, :n_t]
    st = lambda k: jnp.stack(outs[k])
    s_win_t = _roll_window(win_t, outs['s_win'], n_t)
    s_win = s_win_t.reshape(depth, nb, 2, NSA_KV_HEADS, HEAD_DIM, wb).transpose(0, 1, 5, 2, 3, 4)
    return (y_prompt, y_sample, st('p_kv'), st('p_dkv'), st('p_win'), st('p_re'), st('p_im'),
            st('s_kv'), st('s_dkv'), s_win, st('s_re'), st('s_im'))
```

```python
import functools
import math

import numpy as np
import jax
import jax.numpy as jnp
from jax import lax
from jax.experimental import pallas as pl
from jax.experimental.pallas import tpu as pltpu

F32 = jnp.float32
BF16 = jnp.bfloat16
I32 = jnp.int32

D_MODEL = 2048
HEAD_DIM = 64
NSA_WIDTH = 512
NSA_HEADS = 8
NSA_KV_HEADS = 2
NSA_GROUP = 4
CMP_BLOCK = 32
CMP_STRIDE = 16
SEL_BLOCK = 64
N_SELECT = 16
WINDOW = 512
SSM_WIDTH = 1024
SSM_GROUP_CH = 16
SSM_GROUPS = 64
SSM_STATE = 64
N_STATE = SSM_GROUPS * SSM_STATE
DIFF_WIDTH = 512
DIFF_HEADS = 4
DIFF_HEAD_DIM = 64
DIFF_V_DIM = 128
DIFF_SLABS = 2 * DIFF_HEADS
PAGE_SIZE = 128
SCALE = HEAD_DIM ** -0.5
DIFF_SCALE = DIFF_HEAD_DIM ** -0.5
NEG = -1e30
LOG2E = math.log2(math.e)
BIG = 1e9
EPS = 1e-6
NSA_SLOPES = tuple(float(2.0 ** (-8.0 * (k + 1) / NSA_HEADS)) for k in range(NSA_HEADS))
DIFF_SLOPES = tuple(float(2.0 ** (-8.0 * (k + 1) / DIFF_HEADS)) for k in range(DIFF_HEADS))

LANES = 128
SUBLANES = 8
VMEM_LIMIT = 56 * 1024 * 1024

ORIG_SIZES = (NSA_WIDTH, 6 * NSA_KV_HEADS * HEAD_DIM, 3 * NSA_HEADS, NSA_WIDTH, SSM_WIDTH, SSM_WIDTH,
              DIFF_HEADS * 2 * DIFF_HEAD_DIM, DIFF_WIDTH, DIFF_WIDTH, DIFF_WIDTH)
ORIG_OFFS = tuple(int(v) for v in np.concatenate([[0], np.cumsum(ORIG_SIZES)]))
NEW_ORDER = (0, 3, 6, 7, 8, 9, 4, 5, 1, 2)
C_Q, C_ZN, C_DQ, C_DK, C_DV, C_ZD, C_U, C_ZS, C_KV, C_G = 0, 512, 1024, 1536, 2048, 2560, 3072, 4096, 5120, 5888
DP = 6144
N_CMP_PAD = 128
NSEL_PAD = 40
SEL_CHUNK = 256
T_PAD = 8
NSA_DEC_SEQS = 2


def _cparams(sem):
    return pltpu.CompilerParams(dimension_semantics=sem, vmem_limit_bytes=VMEM_LIMIT)


def _dot(a, b):
    return jnp.dot(a, b, preferred_element_type=F32)


def _dot_nt(a, b):
    return lax.dot_general(a, b, (((1,), (1,)), ((), ())), preferred_element_type=F32)


def _gelu(x):
    return 0.5 * x * (1.0 + jnp.tanh(math.sqrt(2.0 / math.pi) * (x + 0.044715 * (x * x * x))))


def _sigmoid(x):
    return 1.0 / (1.0 + jnp.exp(-x))


def _silu(x):
    return x * _sigmoid(x)


def _msoftmax(s, mask):
    s = jnp.where(mask, s, NEG)
    m = jnp.max(s, axis=-1, keepdims=True)
    e = jnp.exp(s - m)
    den = jnp.sum(e, axis=-1, keepdims=True)
    return jnp.where(mask, e * (1.0 / den), 0.0)


PROJ_TN = 512


def _proj_body(x_ref, nw_ref, w_ref, o_ref, dkv_ref, xn_ref):
    j = pl.program_id(1)
    tm = x_ref.shape[0]

    @pl.when(j == 0)
    def _():
        x = x_ref[...]
        ms = jnp.mean(x * x, axis=-1, keepdims=True)
        xn_ref[...] = (x * lax.rsqrt(ms + EPS) * nw_ref[...]).astype(BF16)

    acc = _dot_nt(xn_ref[...], w_ref[...])
    o_ref[...] = acc

    for col0, slab0 in ((C_DK, 0), (C_DV, DIFF_HEADS)):
        @pl.when(j == col0 // PROJ_TN)
        def _(slab0=slab0):
            for c in range(PROJ_TN // LANES):
                dkv_ref[pl.ds(slab0 + c, tm, stride=DIFF_SLABS), :] = acc[:, c * LANES:(c + 1) * LANES]


def _norm_project(x2d, nw_row, w_t, li):
    n = x2d.shape[0]
    tm = min(n, 1024)
    tn = PROJ_TN
    return pl.pallas_call(
        _proj_body,
        grid=(n // tm, DP // tn),
        in_specs=[pl.BlockSpec((tm, D_MODEL), lambda i, j: (i, 0)),
                  pl.BlockSpec((1, D_MODEL), lambda i, j: (0, 0)),
                  pl.BlockSpec((None, tn, D_MODEL), lambda i, j: (li, j, 0))],
        out_specs=[pl.BlockSpec((tm, tn), lambda i, j: (i, j)),
                   pl.BlockSpec((tm * DIFF_SLABS, LANES), lambda i, j: (i, 0))],
        out_shape=[jax.ShapeDtypeStruct((n, DP), F32),
                   jax.ShapeDtypeStruct((n * DIFF_SLABS, LANES), F32)],
        scratch_shapes=[pltpu.VMEM((tm, D_MODEL), BF16)],
        compiler_params=_cparams(("parallel", "arbitrary")),
    )(x2d, nw_row, w_t)


def _out_body(nsa_ref, ssm_ref, diff_ref, x_ref, w_ref, fw_ref, y_ref, *, final):
    acc = x_ref[...]
    acc = acc + _dot(nsa_ref[...].astype(BF16), w_ref[0:NSA_WIDTH, :])
    acc = acc + _dot(ssm_ref[...].astype(BF16), w_ref[NSA_WIDTH:NSA_WIDTH + SSM_WIDTH, :])
    acc = acc + _dot(diff_ref[...].astype(BF16), w_ref[NSA_WIDTH + SSM_WIDTH:, :])
    if final:
        ms = jnp.mean(acc * acc, axis=-1, keepdims=True)
        acc = acc * lax.rsqrt(ms + EPS) * fw_ref[...]
    y_ref[...] = acc


def _merge_out(nsa_o, ssm_o, diff_o, x2d, w_out_bf, fw_row, final):
    n = x2d.shape[0]
    tm = min(n, 512)
    return pl.pallas_call(
        functools.partial(_out_body, final=final),
        grid=(n // tm,),
        in_specs=[pl.BlockSpec((tm, NSA_WIDTH), lambda i: (i, 0)),
                  pl.BlockSpec((tm, SSM_WIDTH), lambda i: (i, 0)),
                  pl.BlockSpec((tm, DIFF_WIDTH), lambda i: (i, 0)),
                  pl.BlockSpec((tm, D_MODEL), lambda i: (i, 0)),
                  pl.BlockSpec((D_MODEL, D_MODEL), lambda i: (0, 0)),
                  pl.BlockSpec((1, D_MODEL), lambda i: (0, 0))],
        out_specs=pl.BlockSpec((tm, D_MODEL), lambda i: (i, 0)),
        out_shape=jax.ShapeDtypeStruct((n, D_MODEL), F32),
        compiler_params=_cparams(("parallel",)),
    )(nsa_o, ssm_o, diff_o, x2d, w_out_bf, fw_row)


def _pad_heads(qb):
    tq = qb.shape[0]
    low = lax.broadcasted_iota(I32, (tq, LANES), 1) < HEAD_DIM
    outs = []
    for hd in range(NSA_HEADS):
        h = hd // NSA_GROUP
        slab = qb[:, (hd // 2) * LANES:(hd // 2 + 1) * LANES]
        if hd % 2 != h:
            slab = pltpu.roll(slab, HEAD_DIM, 1)
        keep = low if h == 0 else jnp.logical_not(low)
        outs.append(jnp.where(keep, slab, 0.0).astype(BF16))
    return outs


def _compress_tokens(xs, wab, pc, w2bd):
    ab = _dot(jnp.concatenate(xs, axis=0), wab)
    pre = []
    for s in range(len(xs)):
        rows = slice(s * N_CMP_PAD, (s + 1) * N_CMP_PAD)
        pre.append(ab[rows, 0:LANES] + pltpu.roll(ab[rows, LANES:2 * LANES], N_CMP_PAD - 1, 0) + pc[0:1])
    tok = _dot(_gelu(jnp.concatenate(pre, axis=0)).astype(BF16), w2bd).astype(BF16)
    return [tok[s * N_CMP_PAD:(s + 1) * N_CMP_PAD] for s in range(len(xs))]


def _pos_const_body(pa_ref, pb_ref, wab_ref, o_ref):
    wab = wab_ref[...]
    o_ref[...] = _dot(pa_ref[...], wab)[:, 0:LANES] + _dot(pb_ref[...], wab)[:, LANES:2 * LANES]


def _compress_pos_const(pos_a, pos_b, wab):
    return pl.pallas_call(
        _pos_const_body,
        grid=(2,),
        in_specs=[pl.BlockSpec((None, SUBLANES, CMP_STRIDE * LANES), lambda k: (k, 0, 0)),
                  pl.BlockSpec((None, SUBLANES, CMP_STRIDE * LANES), lambda k: (k, 0, 0)),
                  pl.BlockSpec((None, CMP_STRIDE * LANES, 2 * LANES), lambda k: (k, 0, 0))],
        out_specs=pl.BlockSpec((None, SUBLANES, LANES), lambda k: (k, 0, 0)),
        out_shape=jax.ShapeDtypeStruct((2, SUBLANES, LANES), F32),
        compiler_params=_cparams(("parallel",)),
    )(pos_a, pos_b, wab)


def _cmp_branch(qh, kc, vc, qpos_col, values_on_rows=False):
    tq = qpos_col.shape[0]
    n_iota = lax.broadcasted_iota(I32, (1, N_CMP_PAD), 1)
    ends = n_iota * CMP_STRIDE + (CMP_BLOCK - 1)
    q_all = jnp.concatenate(qh, axis=0)
    slope_col = jnp.concatenate([jnp.full((tq, 1), NSA_SLOPES[hd], F32) for hd in range(NSA_HEADS)], axis=0)
    qpos_all = jnp.concatenate([qpos_col] * NSA_HEADS, axis=0)
    s = _dot_nt(q_all, kc) + slope_col * ends.astype(F32)
    p = _msoftmax(s, ends <= qpos_all)
    rows = [slice(hd * tq, (hd + 1) * tq) for hd in range(NSA_HEADS)]
    if values_on_rows:
        return _dot_nt(vc, p.astype(BF16)), [p[r] for r in rows]
    o = _dot(p.astype(BF16), vc)
    return [o[r] for r in rows], [p[r] for r in rows]


def _select_blocks(psum, qpos_row, ovt, n_sel, queries_on_rows=True):
    hi = psum.astype(BF16)
    r1 = psum - hi.astype(F32)
    mid = r1.astype(BF16)
    lo = (r1 - mid.astype(F32)).astype(BF16)
    sc = _dot_nt(ovt, hi) + _dot_nt(ovt, mid) + _dot_nt(ovt, lo)
    j = lax.broadcasted_iota(I32, (NSEL_PAD, LANES), 0)
    cur = qpos_row // SEL_BLOCK
    forced = (j == 0) | (j == cur) | (j == cur - 1)
    avail = j * SEL_BLOCK <= qpos_row
    sc = jnp.where(forced, BIG, jnp.where(avail, sc, -BIG))
    cnt = jnp.zeros((NSEL_PAD, LANES), I32)
    for ii in range(n_sel):
        row = sc[ii:ii + 1, :]
        beats = (row > sc) | ((row == sc) & (ii < j))
        cnt = cnt + beats.astype(I32)
    k_top = min(N_SELECT, n_sel)
    sel_t = jnp.where((cnt < k_top) & (j < n_sel), 1.0, 0.0).astype(F32)
    sel_t = jnp.concatenate([sel_t, jnp.zeros((LANES - NSEL_PAD, LANES), F32)], axis=0)
    return sel_t.T if queries_on_rows else sel_t


def _combine_heads(gs, o_cmp, o_sel, o_win, z):
    tq = z.shape[0]
    low = lax.broadcasted_iota(I32, (tq, LANES), 1) < HEAD_DIM
    placed = []
    for hd in range(NSA_HEADS):
        h = hd // NSA_GROUP
        o = (gs[:, hd:hd + 1] * o_cmp[hd] + gs[:, NSA_HEADS + hd:NSA_HEADS + hd + 1] * o_sel[hd]
             + gs[:, 2 * NSA_HEADS + hd:2 * NSA_HEADS + hd + 1] * o_win[hd])
        if hd % 2 != h:
            o = pltpu.roll(o, HEAD_DIM, 1)
        placed.append(o)
    slabs = [jnp.where(low, placed[2 * k], placed[2 * k + 1]) for k in range(NSA_HEADS // 2)]
    return jnp.concatenate(slabs, axis=1) * _silu(z)


def _compress_prompt_body(rows_ref, wab_ref, pc_ref, w2_ref, o_ref):
    xs = [rows_ref[pl.ds(l, N_CMP_PAD, stride=CMP_STRIDE), :].astype(BF16) for l in range(CMP_STRIDE)]
    x = jnp.concatenate(xs, axis=1)
    o_ref[...] = _compress_tokens([x], wab_ref[...], pc_ref[...], w2_ref[...])[0]


def _compress_prompt(p3, wab, pc, w2bd):
    b, t, _ = p3.shape
    kvblk = C_KV // LANES
    return pl.pallas_call(
        _compress_prompt_body,
        grid=(b, 2),
        in_specs=[pl.BlockSpec((None, t, LANES), lambda i, k: (i, 0, kvblk + k)),
                  pl.BlockSpec((None, CMP_STRIDE * LANES, 2 * LANES), lambda i, k: (k, 0, 0)),
                  pl.BlockSpec((None, SUBLANES, LANES), lambda i, k: (k, 0, 0)),
                  pl.BlockSpec((None, LANES, LANES), lambda i, k: (k, 0, 0))],
        out_specs=pl.BlockSpec((None, None, N_CMP_PAD, LANES), lambda i, k: (i, k, 0, 0)),
        out_shape=jax.ShapeDtypeStruct((b, 2, N_CMP_PAD, LANES), BF16),
        compiler_params=_cparams(("parallel", "parallel")),
    )(p3, wab, pc, w2bd)


def _nsa_prompt_body(q_ref, z_ref, g_ref, cmp_ref, sk_ref, sv_ref, wk_ref, wv_ref, ovt_ref, et_ref, o_ref,
                     svt_s, wvt_s, acc_s, *, n_sel):
    tq = 128
    i = pl.program_id(1)
    q0 = i * tq

    @pl.when(i == 0)
    def _():
        for c in range(svt_s.shape[0]):
            rows = slice(c * LANES, (c + 1) * LANES)
            svt_s[c] = sv_ref[rows, :].T.astype(BF16)
            wvt_s[c] = wv_ref[rows, :].T.astype(BF16)

    qh = _pad_heads(q_ref[...] * SCALE)
    q8 = jnp.concatenate(_pad_heads(q_ref[...] * (SCALE * LOG2E)), axis=0)
    gs = _sigmoid(g_ref[...])
    qpos_col = q0 + lax.broadcasted_iota(I32, (tq, 1), 0)
    qpos_row = q0 + lax.broadcasted_iota(I32, (1, LANES), 1)

    w0 = pl.multiple_of(jnp.maximum(q0 - WINDOW, 0), LANES)
    wwid = WINDOW + tq
    kwin = wk_ref[pl.ds(w0, wwid), :].astype(BF16)
    stw = _dot_nt(kwin, q8)

    vc_t = cmp_ref[1].astype(F32).T.astype(BF16)
    ocmp_t, p_cmp = _cmp_branch(qh, cmp_ref[0], vc_t, qpos_col, values_on_rows=True)
    ovt = ovt_ref[...]
    kp_w = w0 + lax.broadcasted_iota(I32, (wwid, LANES), 0)
    dist = qpos_row - kp_w
    maskw = jnp.where((dist >= 0) & (dist < WINDOW), 0.0, NEG)
    kbw = kp_w.astype(F32)
    wc0 = w0 // LANES
    sub = SEL_CHUNK // LANES
    nch = (i + 2 * sub) // (2 * sub)
    hq = NSA_HEADS * tq
    cols = [slice(hd * tq, (hd + 1) * tq) for hd in range(NSA_HEADS)]

    sel_t = []
    for h in range(NSA_KV_HEADS):
        psum = p_cmp[h * NSA_GROUP]
        for hd in range(h * NSA_GROUP + 1, (h + 1) * NSA_GROUP):
            psum = psum + p_cmp[hd]
        sel_t.append(_select_blocks(psum, qpos_row, ovt, n_sel, queries_on_rows=False).astype(BF16))

    pws, linv = [], []
    for hd in range(NSA_HEADS):
        s = stw[:, cols[hd]] + (NSA_SLOPES[hd] * LOG2E) * kbw + maskw
        p = jnp.exp2(s - jnp.max(s, axis=0, keepdims=True))
        linv.append(1.0 / jnp.sum(p, axis=0, keepdims=True))
        pws.append(p.astype(BF16))
    ptw = jnp.concatenate(pws, axis=1)
    owin_t = _dot(wvt_s[wc0], ptw[0:LANES])
    for k in range(1, wwid // LANES):
        owin_t = owin_t + _dot(wvt_s[wc0 + k], ptw[k * LANES:(k + 1) * LANES])
    owin_t = owin_t * jnp.concatenate(linv, axis=1)

    acc_s[...] = jnp.zeros_like(acc_s)

    def scores(c):
        k0 = pl.multiple_of(c * SEL_CHUNK, SEL_CHUNK)
        kch = sk_ref[pl.ds(k0, SEL_CHUNK), :].astype(BF16)
        et = et_ref[pl.ds(k0, SEL_CHUNK), :]
        st = _dot_nt(kch, q8)
        mexp = [_dot(et, sel_t[h]) for h in range(NSA_KV_HEADS)]
        return st, mexp

    def update(c, st, mexp, m, l):
        k0 = c * SEL_CHUNK
        kp = k0 + lax.broadcasted_iota(I32, (SEL_CHUNK, LANES), 0)
        causal = kp <= qpos_row
        kbf = kp.astype(F32)
        ms, ls, alphas, ps = [], [], [], []
        for hd in range(NSA_HEADS):
            if hd % NSA_GROUP == 0:
                maskb = jnp.where((mexp[hd // NSA_GROUP] > 0.5) & causal, 0.0, NEG)
            s = st[:, cols[hd]] + (NSA_SLOPES[hd] * LOG2E) * kbf + maskb
            m_new = jnp.maximum(m[:, cols[hd]], jnp.max(s, axis=0, keepdims=True))
            alpha = jnp.exp2(m[:, cols[hd]] - m_new)
            p = jnp.exp2(s - m_new)
            ms.append(m_new)
            ls.append(alpha * l[:, cols[hd]] + jnp.sum(p, axis=0, keepdims=True))
            alphas.append(alpha)
            ps.append(p.astype(BF16))
        pt = jnp.concatenate(ps, axis=1)
        ot = _dot(svt_s[c * sub], pt[0:LANES])
        for k in range(1, sub):
            ot = ot + _dot(svt_s[c * sub + k], pt[k * LANES:(k + 1) * LANES])
        acc_s[...] = acc_s[...] * jnp.concatenate(alphas, axis=1) + ot
        return jnp.concatenate(ms, axis=1), jnp.concatenate(ls, axis=1)

    def body(c2, carry):
        m, l = carry
        a = scores(2 * c2)
        b = scores(2 * c2 + 1)
        m, l = update(2 * c2, a[0], a[1], m, l)
        return update(2 * c2 + 1, b[0], b[1], m, l)

    init = (jnp.full((1, hq), NEG, F32), jnp.zeros((1, hq), F32))
    _, l = lax.fori_loop(0, nch, body, init)
    osel_t = acc_s[...] * jnp.where(l > 0.0, 1.0 / l, 0.0)

    gs_t = gs.T
    pieces = []
    for hd in range(NSA_HEADS):
        h = hd // NSA_GROUP
        o_t = (gs_t[hd:hd + 1] * ocmp_t[:, cols[hd]]
               + gs_t[NSA_HEADS + hd:NSA_HEADS + hd + 1] * osel_t[:, cols[hd]]
               + gs_t[2 * NSA_HEADS + hd:2 * NSA_HEADS + hd + 1] * owin_t[:, cols[hd]])
        pieces.append(o_t[h * HEAD_DIM:(h + 1) * HEAD_DIM])
    out_t = jnp.concatenate(pieces, axis=0)
    out = jnp.concatenate([out_t[k * LANES:(k + 1) * LANES].T for k in range(NSA_WIDTH // LANES)], axis=1)
    o_ref[...] = (out * _silu(z_ref[...])).astype(o_ref.dtype)


def _nsa_prompt(p3, cmp_tok, ovt, e3):
    b, t, _ = p3.shape
    tq = 128
    n_sel = -(-t // SEL_BLOCK)
    kvb = C_KV // LANES
    return pl.pallas_call(
        functools.partial(_nsa_prompt_body, n_sel=n_sel),
        grid=(b, t // tq),
        scratch_shapes=[pltpu.VMEM((t // LANES, LANES, LANES), BF16),
                        pltpu.VMEM((t // LANES, LANES, LANES), BF16),
                        pltpu.VMEM((LANES, NSA_HEADS * tq), F32)],
        in_specs=[pl.BlockSpec((None, tq, NSA_WIDTH), lambda bi, i: (bi, i, C_Q // NSA_WIDTH)),
                  pl.BlockSpec((None, tq, NSA_WIDTH), lambda bi, i: (bi, i, C_ZN // NSA_WIDTH)),
                  pl.BlockSpec((None, tq, LANES), lambda bi, i: (bi, i, C_G // LANES)),
                  pl.BlockSpec((None, 2, N_CMP_PAD, LANES), lambda bi, i: (bi, 0, 0, 0)),
                  pl.BlockSpec((None, t, LANES), lambda bi, i: (bi, 0, kvb + 2)),
                  pl.BlockSpec((None, t, LANES), lambda bi, i: (bi, 0, kvb + 3)),
                  pl.BlockSpec((None, t, LANES), lambda bi, i: (bi, 0, kvb + 4)),
                  pl.BlockSpec((None, t, LANES), lambda bi, i: (bi, 0, kvb + 5)),
                  pl.BlockSpec((NSEL_PAD, LANES), lambda bi, i: (0, 0)),
                  pl.BlockSpec(e3.shape, lambda bi, i: (0, 0))],
        out_specs=pl.BlockSpec((None, tq, NSA_WIDTH), lambda bi, i: (bi, i, 0)),
        out_shape=jax.ShapeDtypeStruct((b, t, NSA_WIDTH), BF16),
        compiler_params=_cparams(("parallel", "arbitrary")),
    )(p3, p3, p3, cmp_tok, p3, p3, p3, p3, ovt, e3)


def _diff_lambda(lam_ref, lam_init):
    a = lam_ref[...]
    s1 = jnp.sum(a[0:1] * a[1:2], axis=-1, keepdims=True)
    s2 = jnp.sum(a[2:3] * a[3:4], axis=-1, keepdims=True)
    return jnp.exp(s1) - jnp.exp(s2) + lam_init


def _diff_finish(o, lnw, z, lam_init):
    ms = jnp.mean(o * o, axis=-1, keepdims=True)
    return o * lax.rsqrt(ms + EPS) * lnw * (1.0 - lam_init) * _silu(z)


def _diff_prompt_body(lam_ref, q_ref, z_ref, k_ref, v_ref, lnw_ref, o_ref, vt_s, acc_s, *, lam_init):
    tq = 128
    i = pl.program_id(1)
    q0 = i * tq

    @pl.when(i == 0)
    def _():
        for hh in range(DIFF_HEADS):
            for c in range(vt_s.shape[1]):
                vt_s[hh, c] = v_ref[c * LANES:(c + 1) * LANES, hh * LANES:(hh + 1) * LANES].T.astype(BF16)

    lam = _diff_lambda(lam_ref, lam_init)
    low = lax.broadcasted_iota(I32, (tq, LANES), 1) < DIFF_HEAD_DIM
    qpos_row = q0 + lax.broadcasted_iota(I32, (1, LANES), 1)
    sub = SEL_CHUNK // LANES
    nch = (i + 2 * sub) // (2 * sub)
    q2 = []
    for hh in range(DIFF_HEADS):
        qs = q_ref[:, hh * LANES:(hh + 1) * LANES] * (DIFF_SCALE * LOG2E)
        q2.append(jnp.concatenate([jnp.where(low, qs, 0.0), jnp.where(low, 0.0, qs)], axis=0).astype(BF16))
    acc_s[...] = jnp.zeros_like(acc_s)
    ncol = 2 * tq

    def scores(c):
        k0 = pl.multiple_of(c * SEL_CHUNK, SEL_CHUNK)
        return [_dot_nt(k_ref[pl.ds(k0, SEL_CHUNK), hh * LANES:(hh + 1) * LANES].astype(BF16), q2[hh])
                for hh in range(DIFF_HEADS)]

    def update(c, sts, m, l):
        kp = c * SEL_CHUNK + lax.broadcasted_iota(I32, (SEL_CHUNK, LANES), 0)
        maskb = jnp.where(kp <= qpos_row, 0.0, NEG)
        kbf = kp.astype(F32)
        ms, ls = [], []
        for hh in range(DIFF_HEADS):
            st = sts[hh]
            bias = (DIFF_SLOPES[hh] * LOG2E) * kbf + maskb
            alphas, ps = [], []
            for comp in range(2):
                cs = slice(comp * tq, (comp + 1) * tq)
                gs_ = slice(hh * ncol + comp * tq, hh * ncol + (comp + 1) * tq)
                s = st[:, cs] + bias
                m_new = jnp.maximum(m[:, gs_], jnp.max(s, axis=0, keepdims=True))
                alpha = jnp.exp2(m[:, gs_] - m_new)
                p = jnp.exp2(s - m_new)
                ms.append(m_new)
                ls.append(alpha * l[:, gs_] + jnp.sum(p, axis=0, keepdims=True))
                alphas.append(alpha)
                ps.append(p.astype(BF16))
            pt = jnp.concatenate(ps, axis=1)
            ot = _dot(vt_s[hh, c * sub], pt[0:LANES])
            for k in range(1, sub):
                ot = ot + _dot(vt_s[hh, c * sub + k], pt[k * LANES:(k + 1) * LANES])
            acc_s[hh] = acc_s[hh] * jnp.concatenate(alphas, axis=1) + ot
        return jnp.concatenate(ms, axis=1), jnp.concatenate(ls, axis=1)

    def body(c2, carry):
        m, l = carry
        a = scores(2 * c2)
        b = scores(2 * c2 + 1)
        m, l = update(2 * c2, a, m, l)
        return update(2 * c2 + 1, b, m, l)

    init = (jnp.full((1, DIFF_HEADS * ncol), NEG, F32), jnp.zeros((1, DIFF_HEADS * ncol), F32))
    _, l = lax.fori_loop(0, nch, body, init)
    lnw = lnw_ref[...]
    slabs = []
    for hh in range(DIFF_HEADS):
        on = acc_s[hh] * (1.0 / l[:, hh * ncol:(hh + 1) * ncol])
        o = (on[:, 0:tq] - lam * on[:, tq:2 * tq]).T
        slabs.append(_diff_finish(o, lnw, z_ref[:, hh * LANES:(hh + 1) * LANES], lam_init))
    o_ref[...] = jnp.concatenate(slabs, axis=1).astype(o_ref.dtype)


def _diff_prompt(p3, lam4, lnw_row, lam_init):
    b, t, _ = p3.shape
    tq = 128
    return pl.pallas_call(
        functools.partial(_diff_prompt_body, lam_init=lam_init),
        grid=(b, t // tq),
        in_specs=[pl.BlockSpec((4, DIFF_HEAD_DIM), lambda bi, i: (0, 0)),
                  pl.BlockSpec((None, tq, DIFF_WIDTH), lambda bi, i: (bi, i, C_DQ // DIFF_WIDTH)),
                  pl.BlockSpec((None, tq, DIFF_WIDTH), lambda bi, i: (bi, i, C_ZD // DIFF_WIDTH)),
                  pl.BlockSpec((None, t, DIFF_WIDTH), lambda bi, i: (bi, 0, C_DK // DIFF_WIDTH)),
                  pl.BlockSpec((None, t, DIFF_WIDTH), lambda bi, i: (bi, 0, C_DV // DIFF_WIDTH)),
                  pl.BlockSpec((1, DIFF_V_DIM), lambda bi, i: (0, 0))],
        out_specs=pl.BlockSpec((None, tq, DIFF_WIDTH), lambda bi, i: (bi, i, 0)),
        out_shape=jax.ShapeDtypeStruct((b, t, DIFF_WIDTH), BF16),
        scratch_shapes=[pltpu.VMEM((DIFF_HEADS, t // LANES, LANES, LANES), BF16),
                        pltpu.VMEM((DIFF_HEADS, DIFF_V_DIM, 2 * tq), F32)],
        compiler_params=_cparams(("parallel", "arbitrary")),
    )(lam4, p3, p3, p3, p3, lnw_row)


SSM_BLOCKS = SSM_WIDTH // LANES
SSM_BS = N_STATE // SSM_BLOCKS
SSM_CHUNK = 256
SSM_SEG = SSM_CHUNK // SUBLANES
SSM_UNROLL = 32


def _discretize(lre, lim, ldt):
    dt = jnp.exp(ldt)
    mag = jnp.exp(lre * dt)
    ar = mag * jnp.cos(lim * dt)
    ai = mag * jnp.sin(lim * dt)
    den = lre * lre + lim * lim
    qr = ((ar - 1.0) * lre + ai * lim) / den
    qi = (ai * lre - (ar - 1.0) * lim) / den
    return ar, ai, qr, qi


def _bbar_block(qr, qi, bre, bim):
    return jnp.concatenate([qr * bre - qi * bim, qr * bim + qi * bre], axis=1).astype(BF16)


def _glu_tail(y, wglu, z):
    gate = _sigmoid(_dot(y.astype(BF16), wglu))
    return y * gate * _silu(z)


def _ssm_prompt_body(u_ref, z_ref, lre_ref, lim_ref, ldt_ref, bre_ref, bim_ref, cmat_ref, d_ref, wglu_ref,
                     perm_ref, o_ref, hr_ref, hi_ref,
                     bbar_s, apr_s, api_s, bu_s, y_s, carr_s, cari_s):
    first = (pl.program_id(0) == 0) & (pl.program_id(1) == 0)

    @pl.when(first)
    def _():
        ar, ai, qr, qi = _discretize(lre_ref[...], lim_ref[...], ldt_ref[...])
        for k in range(SSM_BLOCKS):
            cs = slice(k * SSM_BS, (k + 1) * SSM_BS)
            bbar_s[k] = _bbar_block(qr[:, cs], qi[:, cs], bre_ref[k], bim_ref[k])
        pr, pi = ar, ai
        apr_s[0:1, :] = pr
        api_s[0:1, :] = pi
        for k in range(1, SSM_SEG):
            pr, pi = pr * ar - pi * ai, pr * ai + pi * ar
            apr_s[k:k + 1, :] = pr
            api_s[k:k + 1, :] = pi

    @pl.when(pl.program_id(1) == 0)
    def _():
        carr_s[...] = jnp.zeros_like(carr_s)
        cari_s[...] = jnp.zeros_like(cari_s)

    re_c = slice(0, SSM_BS)
    im_c = slice(SSM_BS, 2 * SSM_BS)
    u_perm = _dot(perm_ref[0], u_ref[...].astype(BF16)).astype(BF16)
    unperm = perm_ref[1]

    for k in range(SSM_BLOCKS):
        cs = slice(k * SSM_BS, (k + 1) * SSM_BS)
        us = slice(k * LANES, (k + 1) * LANES)
        bu_s[...] = _dot(u_perm[:, us], bbar_s[k])
        ar_b = jnp.broadcast_to(apr_s[0:1, cs], (SUBLANES, SSM_BS))
        ai_b = jnp.broadcast_to(api_s[0:1, cs], (SUBLANES, SSM_BS))

        def step(t, st, ar_b=ar_b, ai_b=ai_b):
            hr, hi = st
            rows = pl.ds(pl.multiple_of(t * SUBLANES, SUBLANES), SUBLANES)
            nr = ar_b * hr - ai_b * hi + bu_s[rows, re_c]
            ni = ar_b * hi + ai_b * hr + bu_s[rows, im_c]
            bu_s[rows, re_c] = nr
            bu_s[rows, im_c] = ni
            return nr, ni

        zero = jnp.zeros((SUBLANES, SSM_BS), F32)
        er, ei = lax.fori_loop(0, SSM_SEG, step, (zero, zero), unroll=SSM_UNROLL)
        a32r = apr_s[SSM_SEG - 1:SSM_SEG, cs]
        a32i = api_s[SSM_SEG - 1:SSM_SEG, cs]
        rows_r = [carr_s[:, cs]]
        rows_i = [cari_s[:, cs]]
        for s in range(1, SUBLANES + 1):
            pr, pi = rows_r[-1], rows_i[-1]
            rows_r.append(er[s - 1:s] + a32r * pr - a32i * pi)
            rows_i.append(ei[s - 1:s] + a32r * pi + a32i * pr)
        carr_s[:, cs] = rows_r[SUBLANES]
        cari_s[:, cs] = rows_i[SUBLANES]
        hin_r = jnp.concatenate(rows_r[:SUBLANES], axis=0)
        hin_i = jnp.concatenate(rows_i[:SUBLANES], axis=0)

        def fix(t, carry, hin_r=hin_r, hin_i=hin_i, cs=cs):
            rows = pl.ds(pl.multiple_of(t * SUBLANES, SUBLANES), SUBLANES)
            pr = apr_s[pl.ds(t, 1), cs]
            pi = api_s[pl.ds(t, 1), cs]
            bu_s[rows, re_c] = bu_s[rows, re_c] + pr * hin_r - pi * hin_i
            bu_s[rows, im_c] = bu_s[rows, im_c] + pr * hin_i + pi * hin_r
            return carry

        lax.fori_loop(0, SSM_SEG, fix, 0, unroll=SSM_UNROLL)
        yp = _dot(bu_s[...].astype(BF16), cmat_ref[k])
        y_hi = yp.astype(BF16)
        r1 = yp - y_hi.astype(F32)
        y_mid = r1.astype(BF16)
        y_lo = (r1 - y_mid.astype(F32)).astype(BF16)
        yk = _dot(unperm, y_hi) + _dot(unperm, y_mid) + _dot(unperm, y_lo) + d_ref[:, us] * u_ref[:, us]
        y_s[:, us] = _gelu(yk)

    o_ref[...] = _glu_tail(y_s[...], wglu_ref[...], z_ref[...]).astype(o_ref.dtype)
    hr_ref[...] = carr_s[...]
    hi_ref[...] = cari_s[...]


def _ssm_prompt(p3, sp):
    b, t, _ = p3.shape
    full = lambda shape: pl.BlockSpec(shape, lambda bi, j: (0,) * len(shape))
    out, hr, hi = pl.pallas_call(
        _ssm_prompt_body,
        grid=(b, t // SSM_CHUNK),
        in_specs=[pl.BlockSpec((None, SSM_CHUNK, SSM_WIDTH), lambda bi, j: (bi, j, C_U // SSM_WIDTH)),
                  pl.BlockSpec((None, SSM_CHUNK, SSM_WIDTH), lambda bi, j: (bi, j, C_ZS // SSM_WIDTH)),
                  full((1, N_STATE)), full((1, N_STATE)), full((1, N_STATE)),
                  full((SSM_BLOCKS, LANES, SSM_BS)), full((SSM_BLOCKS, LANES, SSM_BS)),
                  full((SSM_BLOCKS, 2 * SSM_BS, LANES)), full((1, SSM_WIDTH)),
                  full((SSM_WIDTH, SSM_WIDTH)), full((2, SSM_CHUNK, SSM_CHUNK))],
        out_specs=[pl.BlockSpec((None, SSM_CHUNK, SSM_WIDTH), lambda bi, j: (bi, j, 0)),
                   pl.BlockSpec((None, 1, N_STATE), lambda bi, j: (bi, 0, 0)),
                   pl.BlockSpec((None, 1, N_STATE), lambda bi, j: (bi, 0, 0))],
        out_shape=[jax.ShapeDtypeStruct((b, t, SSM_WIDTH), BF16),
                   jax.ShapeDtypeStruct((b, 1, N_STATE), F32),
                   jax.ShapeDtypeStruct((b, 1, N_STATE), F32)],
        scratch_shapes=[pltpu.VMEM((SSM_BLOCKS, LANES, 2 * SSM_BS), BF16),
                        pltpu.VMEM((SSM_SEG, N_STATE), F32), pltpu.VMEM((SSM_SEG, N_STATE), F32),
                        pltpu.VMEM((SSM_CHUNK, 2 * SSM_BS), F32),
                        pltpu.VMEM((SSM_CHUNK, SSM_WIDTH), F32),
                        pltpu.VMEM((1, N_STATE), F32), pltpu.VMEM((1, N_STATE), F32)],
        compiler_params=_cparams(("arbitrary", "arbitrary")),
    )(p3, p3, sp['lre'], sp['lim'], sp['ldt'], sp['bre'], sp['bim'], sp['cmat'], sp['d'], sp['wglu'],
      _segment_permutation())
    return out, hr, hi


def _segment_permutation():
    r = np.arange(SSM_CHUNK)
    src = (r % SUBLANES) * SSM_SEG + r // SUBLANES
    pm = np.zeros((SSM_CHUNK, SSM_CHUNK), np.float32)
    pm[r, src] = 1.0
    return jnp.asarray(np.stack([pm, pm.T]), BF16)


def _ssm_sample_body(u_ref, z_ref, h0r_ref, h0i_ref, lre_ref, lim_ref, ldt_ref, bre_ref, bim_ref, cmat_ref,
                     d_ref, wglu_ref, o_ref, hr_ref, hi_ref, y_s, *, n_t):
    k = pl.program_id(0)
    ar, ai, qr, qi = _discretize(lre_ref[...], lim_ref[...], ldt_ref[...])
    bbar = _bbar_block(qr, qi, bre_ref[...], bim_ref[...])
    hr = h0r_ref[...]
    hi = h0i_ref[...]
    cmat = cmat_ref[...]
    for t in range(n_t):
        ut = u_ref[t]
        bu = _dot(ut.astype(BF16), bbar)
        hr, hi = ar * hr - ai * hi + bu[:, 0:SSM_BS], ar * hi + ai * hr + bu[:, SSM_BS:]
        hcat = jnp.concatenate([hr, hi], axis=1).astype(BF16)
        y_s[t, k] = _gelu(_dot(hcat, cmat) + d_ref[...] * ut)
    hr_ref[...] = hr
    hi_ref[...] = hi

    @pl.when(k == SSM_BLOCKS - 1)
    def _():
        wglu = wglu_ref[...]
        for t in range(n_t):
            y = jnp.concatenate([y_s[t, kk] for kk in range(SSM_BLOCKS)], axis=1)
            o_ref[t] = _glu_tail(y, wglu, z_ref[t])


def _ssm_sample(ut, zt, h0r, h0i, sp):
    n_t, nb, _ = ut.shape
    out, hr, hi = pl.pallas_call(
        functools.partial(_ssm_sample_body, n_t=n_t),
        grid=(SSM_BLOCKS,),
        in_specs=[pl.BlockSpec((n_t, nb, LANES), lambda k: (0, 0, k)),
                  pl.BlockSpec((n_t, nb, SSM_WIDTH), lambda k: (0, 0, 0)),
                  pl.BlockSpec((nb, SSM_BS), lambda k: (0, k)),
                  pl.BlockSpec((nb, SSM_BS), lambda k: (0, k)),
                  pl.BlockSpec((1, SSM_BS), lambda k: (0, k)),
                  pl.BlockSpec((1, SSM_BS), lambda k: (0, k)),
                  pl.BlockSpec((1, SSM_BS), lambda k: (0, k)),
                  pl.BlockSpec((None, LANES, SSM_BS), lambda k: (k, 0, 0)),
                  pl.BlockSpec((None, LANES, SSM_BS), lambda k: (k, 0, 0)),
                  pl.BlockSpec((None, 2 * SSM_BS, LANES), lambda k: (k, 0, 0)),
                  pl.BlockSpec((1, LANES), lambda k: (0, k)),
                  pl.BlockSpec((SSM_WIDTH, SSM_WIDTH), lambda k: (0, 0))],
        out_specs=[pl.BlockSpec((n_t, nb, SSM_WIDTH), lambda k: (0, 0, 0)),
                   pl.BlockSpec((nb, SSM_BS), lambda k: (0, k)),
                   pl.BlockSpec((nb, SSM_BS), lambda k: (0, k))],
        out_shape=[jax.ShapeDtypeStruct((n_t, nb, SSM_WIDTH), F32),
                   jax.ShapeDtypeStruct((nb, N_STATE), F32),
                   jax.ShapeDtypeStruct((nb, N_STATE), F32)],
        scratch_shapes=[pltpu.VMEM((n_t, SSM_BLOCKS, nb, LANES), F32)],
        compiler_params=_cparams(("arbitrary",)),
    )(ut, zt, h0r, h0i, sp['lre'], sp['lim'], sp['ldt'], sp['bre'], sp['bim'], sp['cmat'], sp['d'], sp['wglu'])
    return out, hr, hi


def _page_map(li, pg, b, pt):
    return (li, pt[b, pg], 0, 0)


def _page_map_seq(li, pg, sq, b, pt):
    return (li, pt[b * NSA_DEC_SEQS + sq, pg], 0, 0)


def _pad_rows(x, rows):
    return jnp.concatenate([x, jnp.zeros((rows - x.shape[0], x.shape[1]), x.dtype)], axis=0)


def _nsa_sample_body(pt_ref, *refs, n_pages, n_sel):
    nsq = NSA_DEC_SEQS
    (q_ref, z_ref, g_ref, snew_ref, wnew_ref, wb_ref, wab_ref, pc_ref, w2_ref, ovt_ref, e_ref,
     gperm_ref, o_ref, s_s) = refs[nsq * n_pages:]
    seqs = range(nsq)
    pages = [refs[sq * n_pages:(sq + 1) * n_pages] for sq in seqs]
    rs = [slice(sq * T_PAD, (sq + 1) * T_PAD) for sq in seqs]
    past = n_pages * PAGE_SIZE
    nk = past + LANES
    wb = wb_ref.shape[2]
    qpos_col = past + lax.broadcasted_iota(I32, (T_PAD, 1), 0)
    qpos64 = jnp.concatenate([qpos_col] * NSA_HEADS, axis=0)
    qpos_row = past + (lax.broadcasted_iota(I32, (1, LANES), 1) % T_PAD)
    slope_col = jnp.concatenate([jnp.full((T_PAD, 1), NSA_SLOPES[hd], F32) for hd in range(NSA_HEADS)], axis=0)
    kpos = lax.broadcasted_iota(I32, (1, nk), 1)
    kbias = slope_col * kpos.astype(F32)
    kpos_w = (past - wb) + lax.broadcasted_iota(I32, (1, wb + LANES), 1)
    wbias = slope_col * kpos_w.astype(F32)
    dist = qpos64 - kpos_w
    wmask = (dist >= 0) & (dist < WINDOW) & (kpos_w >= 0)
    causal = kpos <= qpos64
    sk_r = pl.ds(2 * LANES, LANES)
    sv_r = pl.ds(3 * LANES, LANES)

    qh = [_pad_heads(q_ref[rs[sq], :] * SCALE) for sq in seqs]
    q64 = [jnp.concatenate(qh[sq], axis=0) for sq in seqs]

    gperm = gperm_ref[...]
    gathered = [[[_dot(pages[sq][pg][kv * LANES:(kv + 1) * LANES, :].astype(BF16), gperm).T
                  for pg in range(n_pages)] for sq in seqs] for kv in range(2)]

    owin = []
    for sq in seqs:
        wnew = wnew_ref[rs[sq], :]
        kwn = _pad_rows(wnew[:, 0:LANES], LANES).astype(BF16)
        vwn = _pad_rows(wnew[:, LANES:2 * LANES], LANES).astype(BF16)
        sw = jnp.concatenate([_dot(q64[sq], wb_ref[sq, 0:LANES, :].astype(BF16)), _dot_nt(q64[sq], kwn)], axis=1)
        pw = _msoftmax(sw + wbias, wmask).astype(BF16)
        owin.append(_dot_nt(pw[:, 0:wb], wb_ref[sq, LANES:2 * LANES, :].astype(BF16)) + _dot(pw[:, wb:], vwn))

    for sq in seqs:
        for pg in range(n_pages):
            s_s[sq, :, pg * PAGE_SIZE:(pg + 1) * PAGE_SIZE] = _dot(q64[sq], pages[sq][pg][sk_r, :].astype(BF16))
        knew = _pad_rows(snew_ref[rs[sq], 0:LANES], LANES).astype(BF16)
        s_s[sq, :, past:nk] = _dot_nt(q64[sq], knew)

    toks = []
    npc = PAGE_SIZE // CMP_STRIDE
    for kv in range(2):
        xseq = []
        for sq in seqs:
            xs = [jnp.concatenate([gathered[kv][sq][pg][l * npc:(l + 1) * npc] for pg in range(n_pages)],
                                  axis=0).astype(BF16) for l in range(CMP_STRIDE)]
            xseq.append(jnp.concatenate(xs, axis=1))
        toks.append(_compress_tokens(xseq, wab_ref[kv], pc_ref[kv], w2_ref[kv]))
    cmp = [_cmp_branch(qh[sq], toks[0][sq], toks[1][sq], qpos_col) for sq in seqs]

    memb = []
    for sq in seqs:
        p_cmp = cmp[sq][1]
        psum = jnp.concatenate(
            [p_cmp[0] + p_cmp[1] + p_cmp[2] + p_cmp[3], p_cmp[4] + p_cmp[5] + p_cmp[6] + p_cmp[7],
             jnp.zeros((LANES - 2 * T_PAD, N_CMP_PAD), F32)], axis=0)
        sel = _select_blocks(psum, qpos_row, ovt_ref[...], n_sel)
        mexp = _dot(sel[0:2 * T_PAD].astype(BF16), e_ref[...])
        memb.append(
            jnp.concatenate([mexp[0:T_PAD]] * NSA_GROUP + [mexp[T_PAD:2 * T_PAD]] * NSA_GROUP, axis=0) > 0.5)

    for sq in seqs:
        p = _msoftmax(s_s[sq] + kbias, memb[sq] & causal).astype(BF16)
        vnew = _pad_rows(snew_ref[rs[sq], LANES:2 * LANES], LANES).astype(BF16)
        osel = _dot(p[:, past:nk], vnew)
        for pg in range(n_pages):
            osel = osel + _dot_nt(p[:, pg * PAGE_SIZE:(pg + 1) * PAGE_SIZE], pages[sq][pg][sv_r, :].astype(BF16))
        o_sel = [osel[hd * T_PAD:(hd + 1) * T_PAD] for hd in range(NSA_HEADS)]
        o_win = [owin[sq][hd * T_PAD:(hd + 1) * T_PAD] for hd in range(NSA_HEADS)]
        o_ref[rs[sq], :] = _combine_heads(_sigmoid(g_ref[rs[sq], :]), cmp[sq][0], o_sel, o_win, z_ref[rs[sq], :])


def _nsa_sample(p2, cache_t, li, win_t, page_table, wab, pc, w2bd, ovt, e2, n_t):
    n = p2.shape[0]
    nb, n_pages = page_table.shape
    past = n_pages * PAGE_SIZE
    n_sel = -(-(past + n_t) // SEL_BLOCK)
    nk = past + LANES
    wb = win_t.shape[3]
    kvb = C_KV // (2 * LANES)
    nsq = NSA_DEC_SEQS
    assert nb % nsq == 0
    page_specs = [pl.BlockSpec((None, None, 4 * LANES, PAGE_SIZE), functools.partial(_page_map_seq, li, pg, sq))
                  for sq in range(nsq) for pg in range(n_pages)]
    const = lambda shape: pl.BlockSpec(shape, lambda b, pt: (0,) * len(shape))
    rows = nsq * T_PAD
    in_specs = page_specs + [
        pl.BlockSpec((rows, NSA_WIDTH), lambda b, pt: (b, C_Q // NSA_WIDTH)),
        pl.BlockSpec((rows, NSA_WIDTH), lambda b, pt: (b, C_ZN // NSA_WIDTH)),
        pl.BlockSpec((rows, LANES), lambda b, pt: (b, C_G // LANES)),
        pl.BlockSpec((rows, 2 * LANES), lambda b, pt: (b, kvb + 1)),
        pl.BlockSpec((rows, 2 * LANES), lambda b, pt: (b, kvb + 2)),
        pl.BlockSpec((None, nsq, 2 * LANES, wb), lambda b, pt: (li, b, 0, 0)),
        const(wab.shape), const(pc.shape), const(w2bd.shape),
        const(ovt.shape), const(e2.shape), const((PAGE_SIZE, PAGE_SIZE))]
    return pl.pallas_call(
        functools.partial(_nsa_sample_body, n_pages=n_pages, n_sel=n_sel),
        grid_spec=pltpu.PrefetchScalarGridSpec(
            num_scalar_prefetch=1, grid=(nb // nsq,), in_specs=in_specs,
            out_specs=pl.BlockSpec((rows, NSA_WIDTH), lambda b, pt: (b, 0)),
            scratch_shapes=[pltpu.VMEM((nsq, NSA_HEADS * T_PAD, nk), F32)]),
        out_shape=jax.ShapeDtypeStruct((n, NSA_WIDTH), F32),
        compiler_params=_cparams(("parallel",)),
    )(page_table, *([cache_t] * (nsq * n_pages)), p2, p2, p2, p2, p2, win_t, wab, pc, w2bd, ovt, e2,
      _chunk_gather_permutation())


def _chunk_gather_permutation():
    k = np.arange(PAGE_SIZE)
    pm = np.zeros((PAGE_SIZE, PAGE_SIZE), np.float32)
    pm[k, (k % CMP_STRIDE) * (PAGE_SIZE // CMP_STRIDE) + k // CMP_STRIDE] = 1.0
    return jnp.asarray(pm, BF16)


def _diff_sample_body(pt_ref, *refs, n_pages, lam_init):
    pages = refs[:n_pages]
    lam_ref, q_ref, z_ref, knew_ref, vnew_ref, lnw_ref, o_ref, s_s = refs[n_pages:]
    past = n_pages * PAGE_SIZE
    nk = past + LANES
    lam = _diff_lambda(lam_ref, lam_init)
    low = lax.broadcasted_iota(I32, (T_PAD, LANES), 1) < DIFF_HEAD_DIM
    rows_h = 2 * T_PAD
    q16 = []
    for hh in range(DIFF_HEADS):
        qs = q_ref[:, hh * LANES:(hh + 1) * LANES] * DIFF_SCALE
        q16.append(jnp.concatenate([jnp.where(low, qs, 0.0), jnp.where(low, 0.0, qs)], axis=0).astype(BF16))
    for pg in range(n_pages):
        for hh in range(DIFF_HEADS):
            kp = pages[pg][pl.ds(hh, PAGE_SIZE, stride=DIFF_SLABS), :].astype(BF16)
            s_s[hh * rows_h:(hh + 1) * rows_h, pg * PAGE_SIZE:(pg + 1) * PAGE_SIZE] = _dot_nt(q16[hh], kp)
    for hh in range(DIFF_HEADS):
        kn = _pad_rows(knew_ref[:, hh * LANES:(hh + 1) * LANES], LANES).astype(BF16)
        s_s[hh * rows_h:(hh + 1) * rows_h, past:nk] = _dot_nt(q16[hh], kn)
    kpos = lax.broadcasted_iota(I32, (1, nk), 1)
    slope_col = jnp.concatenate([jnp.full((rows_h, 1), DIFF_SLOPES[hh], F32) for hh in range(DIFF_HEADS)], axis=0)
    qpos = past + (lax.broadcasted_iota(I32, (DIFF_HEADS * rows_h, 1), 0) % T_PAD)
    p = _msoftmax(s_s[...] + slope_col * kpos.astype(F32), kpos <= qpos)
    lnw = lnw_ref[...]
    slabs = []
    for hh in range(DIFF_HEADS):
        a = (p[hh * rows_h:hh * rows_h + T_PAD] - lam * p[hh * rows_h + T_PAD:(hh + 1) * rows_h]).astype(BF16)
        vn = _pad_rows(vnew_ref[:, hh * LANES:(hh + 1) * LANES], LANES).astype(BF16)
        o = _dot(a[:, past:nk], vn)
        for pg in range(n_pages):
            vp = pages[pg][pl.ds(DIFF_HEADS + hh, PAGE_SIZE, stride=DIFF_SLABS), :].astype(BF16)
            o = o + _dot(a[:, pg * PAGE_SIZE:(pg + 1) * PAGE_SIZE], vp)
        slabs.append(_diff_finish(o, lnw, z_ref[:, hh * LANES:(hh + 1) * LANES], lam_init))
    o_ref[...] = jnp.concatenate(slabs, axis=1)


def _diff_sample(p2, cache_r, li, page_table, lam4, lnw_row, lam_init):
    n = p2.shape[0]
    nb, n_pages = page_table.shape
    nk = n_pages * PAGE_SIZE + LANES
    page_specs = [pl.BlockSpec((None, None, PAGE_SIZE * DIFF_SLABS, LANES), functools.partial(_page_map, li, pg))
                  for pg in range(n_pages)]
    in_specs = page_specs + [
        pl.BlockSpec((4, DIFF_HEAD_DIM), lambda b, pt: (0, 0)),
        pl.BlockSpec((T_PAD, DIFF_WIDTH), lambda b, pt: (b, C_DQ // DIFF_WIDTH)),
        pl.BlockSpec((T_PAD, DIFF_WIDTH), lambda b, pt: (b, C_ZD // DIFF_WIDTH)),
        pl.BlockSpec((T_PAD, DIFF_WIDTH), lambda b, pt: (b, C_DK // DIFF_WIDTH)),
        pl.BlockSpec((T_PAD, DIFF_WIDTH), lambda b, pt: (b, C_DV // DIFF_WIDTH)),
        pl.BlockSpec((1, DIFF_V_DIM), lambda b, pt: (0, 0))]
    return pl.pallas_call(
        functools.partial(_diff_sample_body, n_pages=n_pages, lam_init=lam_init),
        grid_spec=pltpu.PrefetchScalarGridSpec(
            num_scalar_prefetch=1, grid=(nb,), in_specs=in_specs,
            out_specs=pl.BlockSpec((T_PAD, DIFF_WIDTH), lambda b, pt: (b, 0)),
            scratch_shapes=[pltpu.VMEM((DIFF_HEADS * 2 * T_PAD, nk), F32)]),
        out_shape=jax.ShapeDtypeStruct((n, DIFF_WIDTH), F32),
        compiler_params=_cparams(("parallel",)),
    )(page_table, *([cache_r] * n_pages), lam4, p2, p2, p2, p2, lnw_row)


WIN_ROLL_SEQS = 8


def _roll_window_body(win_ref, *refs, n_t):
    new_refs, o_ref = refs[:-1], refs[-1]
    li = pl.program_id(0)
    nbk, f, wb = win_ref.shape
    new_all = new_refs[0][...]
    for d in range(1, len(new_refs)):
        new_all = jnp.where(li == d, new_refs[d][...], new_all)
    tail = lax.broadcasted_iota(I32, (f, LANES), 1) >= LANES - n_t
    for bb in range(nbk):
        rolled = pltpu.roll(win_ref[bb], wb - n_t, 1)
        new_t = _pad_rows(new_all[bb * T_PAD:(bb + 1) * T_PAD], LANES).T
        new_t = pltpu.roll(new_t, LANES - n_t, 1)
        o_ref[bb, :, 0:wb - LANES] = rolled[:, 0:wb - LANES]
        o_ref[bb, :, wb - LANES:wb] = jnp.where(tail, new_t, rolled[:, wb - LANES:wb])


def _roll_window(win_t, ps_list, n_t):
    depth, nb, f, wb = win_t.shape
    nbk = WIN_ROLL_SEQS
    wcol = (C_KV + 4 * LANES) // f
    return pl.pallas_call(
        functools.partial(_roll_window_body, n_t=n_t),
        grid=(depth, nb // nbk),
        in_specs=[pl.BlockSpec((None, nbk, f, wb), lambda li, i: (li, i, 0, 0))]
        + [pl.BlockSpec((nbk * T_PAD, f), lambda li, i: (i, wcol)) for _ in ps_list],
        out_specs=pl.BlockSpec((None, nbk, f, wb), lambda li, i: (li, i, 0, 0)),
        out_shape=jax.ShapeDtypeStruct(win_t.shape, F32),
        compiler_params=_cparams(("parallel", "parallel")),
    )(win_t, *ps_list)


def _overlap_t(n_sel):
    n = np.arange(N_CMP_PAD)[None, :]
    j = np.arange(NSEL_PAD)[:, None]
    cs = n * CMP_STRIDE
    ss = j * SEL_BLOCK
    ov = np.clip(np.minimum(cs + CMP_BLOCK, ss + SEL_BLOCK) - np.maximum(cs, ss), 0, None) / CMP_BLOCK
    ov = np.where((n < N_CMP_PAD - 1) & (j < n_sel), ov, 0.0)
    return jnp.asarray(ov, BF16)


def _block_expander(n_keys):
    j = np.arange(LANES)[:, None]
    k = np.arange(n_keys)[None, :]
    return jnp.asarray((k // SEL_BLOCK) == j, BF16)


def _layer_weights(li, norm_w, w_in, w_out, w_cmp1, pos_cmp, w_cmp2, lre, lim, ldt, bre, bim, cre, cim, sd,
                   w_glu, lam_q1, lam_k1, lam_q2, lam_k2, diff_ln_w):
    w = {}
    w['nw'] = norm_w[li][None, :]
    w['w_out'] = w_out[li].astype(BF16)
    eye2 = jnp.eye(NSA_KV_HEADS, dtype=F32)
    w1 = w_cmp1[li].reshape(2, 2, CMP_STRIDE, HEAD_DIM, HEAD_DIM)
    w['wab'] = jnp.einsum('kaldf,hg->klhdagf', w1, eye2).reshape(
        2, CMP_STRIDE * LANES, 2 * LANES).astype(BF16)
    pos = pos_cmp[li].reshape(2, 2, CMP_STRIDE, 1, HEAD_DIM)
    pos = jnp.broadcast_to(pos, (2, 2, CMP_STRIDE, NSA_KV_HEADS, HEAD_DIM)).reshape(2, 2, 1, CMP_STRIDE * LANES)
    pos = jnp.broadcast_to(pos, (2, 2, SUBLANES, CMP_STRIDE * LANES)).astype(BF16)
    w['pos_a'] = pos[:, 0]
    w['pos_b'] = pos[:, 1]
    w['w2bd'] = jnp.einsum('ked,hg->khegd', w_cmp2[li], eye2).reshape(2, LANES, LANES).astype(BF16)
    eye8 = jnp.eye(SSM_BLOCKS, dtype=F32)
    gl = SSM_BLOCKS

    def compact_b(bm):
        bm = bm.reshape(SSM_BLOCKS, gl, SSM_STATE, SSM_GROUP_CH)
        return jnp.einsum('kgpc,gh->kgchp', bm, eye8).reshape(SSM_BLOCKS, LANES, SSM_BS)

    def compact_c(cm):
        cm = cm.reshape(SSM_BLOCKS, gl, SSM_GROUP_CH, SSM_STATE)
        return jnp.einsum('kgcp,gh->khpgc', cm, eye8).reshape(SSM_BLOCKS, SSM_BS, LANES)

    sp = {'lre': lre[li].reshape(1, N_STATE), 'lim': lim[li].reshape(1, N_STATE),
          'ldt': jnp.repeat(ldt[li], SSM_STATE).reshape(1, N_STATE),
          'bre': compact_b(bre[li]), 'bim': compact_b(bim[li]),
          'cmat': jnp.concatenate([compact_c(cre[li]), -compact_c(cim[li])], axis=1).astype(BF16),
          'd': sd[li].reshape(1, SSM_WIDTH), 'wglu': w_glu[li].astype(BF16)}
    w['ssm'] = sp
    w['lam4'] = jnp.stack([lam_q1[li], lam_k1[li], lam_q2[li], lam_k2[li]])
    w['lnw'] = diff_ln_w[li][None, :]
    w['lam_init'] = 0.8 - 0.6 * math.exp(-0.3 * li)
    return w


def kernel(x_prompt, x_sample, cache_nsa_kv, cache_diff_kv, state_nsa_win, state_ssm_re, state_ssm_im, page_table, norm_w, w_in, w_out, w_cmp1, pos_cmp, w_cmp2, ssm_lambda_re, ssm_lambda_im, ssm_log_dt, ssm_b_re, ssm_b_im, ssm_c_re, ssm_c_im, ssm_d, w_glu, lam_q1, lam_k1, lam_q2, lam_k2, diff_ln_w, final_norm_w):
    depth = norm_w.shape[0]
    b, t, _ = x_prompt.shape
    nb, n_t, _ = x_sample.shape
    n_pool = cache_nsa_kv.shape[1]
    n_pages = page_table.shape[1]
    past = n_pages * PAGE_SIZE
    wb = state_nsa_win.shape[2]
    assert t % SSM_CHUNK == 0 and t >= WINDOW + 128 and n_t <= T_PAD

    ovt_p = _overlap_t(-(-t // SEL_BLOCK))
    ovt_s = _overlap_t(-(-(past + n_t) // SEL_BLOCK))
    e_p = _block_expander(t).T
    e_s = _block_expander(past + LANES)
    fw = final_norm_w[None, :]
    nsa_cache_t = cache_nsa_kv.transpose(0, 1, 3, 4, 5, 2).reshape(depth, n_pool, 4 * LANES, PAGE_SIZE)
    win_t = state_nsa_win.transpose(0, 1, 3, 4, 5, 2).reshape(depth, nb, 2 * LANES, wb)
    diff_cache_r = cache_diff_kv.reshape(depth, n_pool, PAGE_SIZE * DIFF_SLABS, LANES)
    w_rows = w_in.transpose(0, 2, 1)
    w_in_t = jnp.concatenate(
        [w_rows[:, ORIG_OFFS[s]:ORIG_OFFS[s + 1]] for s in NEW_ORDER]
        + [jnp.zeros((depth, DP - ORIG_OFFS[-1], D_MODEL), w_in.dtype)], axis=1).astype(BF16)

    xp = x_prompt.reshape(b * t, D_MODEL)
    xs = jnp.pad(x_sample, ((0, 0), (0, T_PAD - n_t), (0, 0))).reshape(nb * T_PAD, D_MODEL)
    outs = {k: [] for k in ('p_kv', 'p_dkv', 'p_win', 'p_re', 'p_im', 's_kv', 's_dkv', 's_win', 's_re', 's_im')}
    for li in range(depth):
        w = _layer_weights(li, norm_w, w_in, w_out, w_cmp1, pos_cmp, w_cmp2, ssm_lambda_re, ssm_lambda_im,
                           ssm_log_dt, ssm_b_re, ssm_b_im, ssm_c_re, ssm_c_im, ssm_d, w_glu,
                           lam_q1, lam_k1, lam_q2, lam_k2, diff_ln_w)
        final = li == depth - 1
        pp, p_dkv_rows = _norm_project(xp, w['nw'], w_in_t, li)
        p3 = pp.reshape(b, t, DP)
        pos_c = _compress_pos_const(w['pos_a'], w['pos_b'], w['wab'])
        cmp_tok = _compress_prompt(p3, w['wab'], pos_c, w['w2bd'])
        nsa_o = _nsa_prompt(p3, cmp_tok, ovt_p, e_p)
        ssm_o, hr, hi = _ssm_prompt(p3, w['ssm'])
        diff_o = _diff_prompt(p3, w['lam4'], w['lnw'], w['lam_init'])
        xp = _merge_out(nsa_o.reshape(b * t, -1), ssm_o.reshape(b * t, -1), diff_o.reshape(b * t, -1), xp,
                        w['w_out'], fw, final)
        kv = p3[:, :, C_KV:C_KV + 6 * LANES]
        outs['p_kv'].append(kv[:, :, :4 * LANES].reshape(b, t, 4, NSA_KV_HEADS, HEAD_DIM))
        outs['p_dkv'].append(p_dkv_rows.reshape(b, t, 2, DIFF_HEADS, DIFF_V_DIM))
        outs['p_win'].append(kv[:, t - min(WINDOW, t):, 4 * LANES:].reshape(b, -1, 2, NSA_KV_HEADS, HEAD_DIM))
        outs['p_re'].append(hr.reshape(b, SSM_GROUPS, SSM_STATE))
        outs['p_im'].append(hi.reshape(b, SSM_GROUPS, SSM_STATE))
        ps, s_dkv_rows = _norm_project(xs, w['nw'], w_in_t, li)
        nsa_s = _nsa_sample(ps, nsa_cache_t, li, win_t, page_table,
                            w['wab'], pos_c, w['w2bd'], ovt_s, e_s, n_t)
        ps_t = ps.reshape(nb, T_PAD, DP)[:, :n_t].transpose(1, 0, 2)
        ssm_t, shr, shi = _ssm_sample(ps_t[:, :, C_U:C_U + SSM_WIDTH], ps_t[:, :, C_ZS:C_ZS + SSM_WIDTH],
                                      state_ssm_re[li].reshape(nb, N_STATE),
                                      state_ssm_im[li].reshape(nb, N_STATE), w['ssm'])
        ssm_s = jnp.pad(ssm_t.transpose(1, 0, 2), ((0, 0), (0, T_PAD - n_t), (0, 0))).reshape(nb * T_PAD, SSM_WIDTH)
        diff_s = _diff_sample(ps, diff_cache_r, li, page_table, w['lam4'], w['lnw'], w['lam_init'])
        xs = _merge_out(nsa_s, ssm_s, diff_s, xs, w['w_out'], fw, final)
        ps3 = ps.reshape(nb, T_PAD, DP)[:, :n_t]
        skv = ps3[:, :, C_KV:C_KV + 6 * LANES]
        outs['s_kv'].append(skv[:, :, :4 * LANES].reshape(nb, n_t, 4, NSA_KV_HEADS, HEAD_DIM))
        outs['s_dkv'].append(s_dkv_rows.reshape(nb, T_PAD, 2, DIFF_HEADS, DIFF_V_DIM)[:, :n_t])
        outs['s_win'].append(ps)
        outs['s_re'].append(shr.reshape(nb, SSM_GROUPS, SSM_STATE))
        outs['s_im'].append(shi.reshape(nb, SSM_GROUPS, SSM_STATE))
    y_prompt = xp.reshape(b, t, D_MODEL)
    y_sample = xs.reshape(nb, T_PAD, D_MODEL)[:, :n_t]
    st = lambda k: jnp.stack(outs[k])
    s_win_t = _roll_window(win_t, outs['s_win'], n_t)
    s_win = s_win_t.reshape(depth, nb, 2, NSA_KV_HEADS, HEAD_DIM, wb).transpose(0, 1, 5, 2, 3, 4)
    return (y_prompt, y_sample, st('p_kv'), st('p_dkv'), st('p_win'), st('p_re'), st('p_im'),
            st('s_kv'), st('s_dkv'), s_win, st('s_re'), st('s_im'))
```

```python
import functools
import math

import numpy as np
import jax
import jax.numpy as jnp
from jax import lax
from jax.experimental import pallas as pl
from jax.experimental.pallas import tpu as pltpu

F32 = jnp.float32
BF16 = jnp.bfloat16
I32 = jnp.int32

D_MODEL = 2048
HEAD_DIM = 64
NSA_WIDTH = 512
NSA_HEADS = 8
NSA_KV_HEADS = 2
NSA_GROUP = 4
CMP_BLOCK = 32
CMP_STRIDE = 16
SEL_BLOCK = 64
N_SELECT = 16
WINDOW = 512
SSM_WIDTH = 1024
SSM_GROUP_CH = 16
SSM_GROUPS = 64
SSM_STATE = 64
N_STATE = SSM_GROUPS * SSM_STATE
DIFF_WIDTH = 512
DIFF_HEADS = 4
DIFF_HEAD_DIM = 64
DIFF_V_DIM = 128
DIFF_SLABS = 2 * DIFF_HEADS
PAGE_SIZE = 128
SCALE = HEAD_DIM ** -0.5
DIFF_SCALE = DIFF_HEAD_DIM ** -0.5
NEG = -1e30
LOG2E = math.log2(math.e)
BIG = 1e9
EPS = 1e-6
NSA_SLOPES = tuple(float(2.0 ** (-8.0 * (k + 1) / NSA_HEADS)) for k in range(NSA_HEADS))
DIFF_SLOPES = tuple(float(2.0 ** (-8.0 * (k + 1) / DIFF_HEADS)) for k in range(DIFF_HEADS))

LANES = 128
SUBLANES = 8
VMEM_LIMIT = 56 * 1024 * 1024

ORIG_SIZES = (NSA_WIDTH, 6 * NSA_KV_HEADS * HEAD_DIM, 3 * NSA_HEADS, NSA_WIDTH, SSM_WIDTH, SSM_WIDTH,
              DIFF_HEADS * 2 * DIFF_HEAD_DIM, DIFF_WIDTH, DIFF_WIDTH, DIFF_WIDTH)
ORIG_OFFS = tuple(int(v) for v in np.concatenate([[0], np.cumsum(ORIG_SIZES)]))
NEW_ORDER = (0, 3, 6, 7, 8, 9, 4, 5, 1, 2)
C_Q, C_ZN, C_DQ, C_DK, C_DV, C_ZD, C_U, C_ZS, C_KV, C_G = 0, 512, 1024, 1536, 2048, 2560, 3072, 4096, 5120, 5888
DP = 6144
N_CMP_PAD = 128
NSEL_PAD = 40
SEL_CHUNK = 256
T_PAD = 8
NSA_DEC_SEQS = 2


def _cparams(sem):
    return pltpu.CompilerParams(dimension_semantics=sem, vmem_limit_bytes=VMEM_LIMIT)


def _dot(a, b):
    return jnp.dot(a, b, preferred_element_type=F32)


def _dot_nt(a, b):
    return lax.dot_general(a, b, (((1,), (1,)), ((), ())), preferred_element_type=F32)


def _gelu(x):
    return 0.5 * x * (1.0 + jnp.tanh(math.sqrt(2.0 / math.pi) * (x + 0.044715 * (x * x * x))))


def _sigmoid(x):
    return 1.0 / (1.0 + jnp.exp(-x))


def _silu(x):
    return x * _sigmoid(x)


def _msoftmax(s, mask):
    s = jnp.where(mask, s, NEG)
    m = jnp.max(s, axis=-1, keepdims=True)
    e = jnp.exp(s - m)
    den = jnp.sum(e, axis=-1, keepdims=True)
    return jnp.where(mask, e * (1.0 / den), 0.0)


PROJ_TN = 512


def _proj_body(x_ref, nw_ref, w_ref, o_ref, dkv_ref, xn_ref):
    j = pl.program_id(1)
    tm = x_ref.shape[0]

    @pl.when(j == 0)
    def _():
        x = x_ref[...]
        ms = jnp.mean(x * x, axis=-1, keepdims=True)
        xn_ref[...] = (x * lax.rsqrt(ms + EPS) * nw_ref[...]).astype(BF16)

    acc = _dot_nt(xn_ref[...], w_ref[...])
    o_ref[...] = acc

    for col0, slab0 in ((C_DK, 0), (C_DV, DIFF_HEADS)):
        @pl.when(j == col0 // PROJ_TN)
        def _(slab0=slab0):
            for c in range(PROJ_TN // LANES):
                dkv_ref[pl.ds(slab0 + c, tm, stride=DIFF_SLABS), :] = acc[:, c * LANES:(c + 1) * LANES]


def _norm_project(x2d, nw_row, w_t, li):
    n = x2d.shape[0]
    tm = min(n, 1024)
    tn = PROJ_TN
    return pl.pallas_call(
        _proj_body,
        grid=(n // tm, DP // tn),
        in_specs=[pl.BlockSpec((tm, D_MODEL), lambda i, j: (i, 0)),
                  pl.BlockSpec((1, D_MODEL), lambda i, j: (0, 0)),
                  pl.BlockSpec((None, tn, D_MODEL), lambda i, j: (li, j, 0))],
        out_specs=[pl.BlockSpec((tm, tn), lambda i, j: (i, j)),
                   pl.BlockSpec((tm * DIFF_SLABS, LANES), lambda i, j: (i, 0))],
        out_shape=[jax.ShapeDtypeStruct((n, DP), F32),
                   jax.ShapeDtypeStruct((n * DIFF_SLABS, LANES), F32)],
        scratch_shapes=[pltpu.VMEM((tm, D_MODEL), BF16)],
        compiler_params=_cparams(("parallel", "arbitrary")),
    )(x2d, nw_row, w_t)


def _out_body(nsa_ref, ssm_ref, diff_ref, x_ref, w_ref, fw_ref, y_ref, *, final):
    acc = x_ref[...]
    acc = acc + _dot(nsa_ref[...].astype(BF16), w_ref[0:NSA_WIDTH, :])
    acc = acc + _dot(ssm_ref[...].astype(BF16), w_ref[NSA_WIDTH:NSA_WIDTH + SSM_WIDTH, :])
    acc = acc + _dot(diff_ref[...].astype(BF16), w_ref[NSA_WIDTH + SSM_WIDTH:, :])
    if final:
        ms = jnp.mean(acc * acc, axis=-1, keepdims=True)
        acc = acc * lax.rsqrt(ms + EPS) * fw_ref[...]
    y_ref[...] = acc


def _merge_out(nsa_o, ssm_o, diff_o, x2d, w_out_bf, fw_row, final):
    n = x2d.shape[0]
    tm = min(n, 512)
    return pl.pallas_call(
        functools.partial(_out_body, final=final),
        grid=(n // tm,),
        in_specs=[pl.BlockSpec((tm, NSA_WIDTH), lambda i: (i, 0)),
                  pl.BlockSpec((tm, SSM_WIDTH), lambda i: (i, 0)),
                  pl.BlockSpec((tm, DIFF_WIDTH), lambda i: (i, 0)),
                  pl.BlockSpec((tm, D_MODEL), lambda i: (i, 0)),
                  pl.BlockSpec((D_MODEL, D_MODEL), lambda i: (0, 0)),
                  pl.BlockSpec((1, D_MODEL), lambda i: (0, 0))],
        out_specs=pl.BlockSpec((tm, D_MODEL), lambda i: (i, 0)),
        out_shape=jax.ShapeDtypeStruct((n, D_MODEL), F32),
        compiler_params=_cparams(("parallel",)),
    )(nsa_o, ssm_o, diff_o, x2d, w_out_bf, fw_row)


def _pad_heads(qb):
    tq = qb.shape[0]
    low = lax.broadcasted_iota(I32, (tq, LANES), 1) < HEAD_DIM
    outs = []
    for hd in range(NSA_HEADS):
        h = hd // NSA_GROUP
        slab = qb[:, (hd // 2) * LANES:(hd // 2 + 1) * LANES]
        if hd % 2 != h:
            slab = pltpu.roll(slab, HEAD_DIM, 1)
        keep = low if h == 0 else jnp.logical_not(low)
        outs.append(jnp.where(keep, slab, 0.0).astype(BF16))
    return outs


def _compress_tokens(xs, wab, pc, w2bd):
    ab = _dot(jnp.concatenate(xs, axis=0), wab)
    pre = []
    for s in range(len(xs)):
        rows = slice(s * N_CMP_PAD, (s + 1) * N_CMP_PAD)
        pre.append(ab[rows, 0:LANES] + pltpu.roll(ab[rows, LANES:2 * LANES], N_CMP_PAD - 1, 0) + pc[0:1])
    tok = _dot(_gelu(jnp.concatenate(pre, axis=0)).astype(BF16), w2bd).astype(BF16)
    return [tok[s * N_CMP_PAD:(s + 1) * N_CMP_PAD] for s in range(len(xs))]


def _pos_const_body(pa_ref, pb_ref, wab_ref, o_ref):
    wab = wab_ref[...]
    o_ref[...] = _dot(pa_ref[...], wab)[:, 0:LANES] + _dot(pb_ref[...], wab)[:, LANES:2 * LANES]


def _compress_pos_const(pos_a, pos_b, wab):
    return pl.pallas_call(
        _pos_const_body,
        grid=(2,),
        in_specs=[pl.BlockSpec((None, SUBLANES, CMP_STRIDE * LANES), lambda k: (k, 0, 0)),
                  pl.BlockSpec((None, SUBLANES, CMP_STRIDE * LANES), lambda k: (k, 0, 0)),
                  pl.BlockSpec((None, CMP_STRIDE * LANES, 2 * LANES), lambda k: (k, 0, 0))],
        out_specs=pl.BlockSpec((None, SUBLANES, LANES), lambda k: (k, 0, 0)),
        out_shape=jax.ShapeDtypeStruct((2, SUBLANES, LANES), F32),
        compiler_params=_cparams(("parallel",)),
    )(pos_a, pos_b, wab)


def _cmp_branch(qh, kc, vc, qpos_col, values_on_rows=False):
    tq = qpos_col.shape[0]
    n_iota = lax.broadcasted_iota(I32, (1, N_CMP_PAD), 1)
    ends = n_iota * CMP_STRIDE + (CMP_BLOCK - 1)
    q_all = jnp.concatenate(qh, axis=0)
    slope_col = jnp.concatenate([jnp.full((tq, 1), NSA_SLOPES[hd], F32) for hd in range(NSA_HEADS)], axis=0)
    qpos_all = jnp.concatenate([qpos_col] * NSA_HEADS, axis=0)
    s = _dot_nt(q_all, kc) + slope_col * ends.astype(F32)
    p = _msoftmax(s, ends <= qpos_all)
    rows = [slice(hd * tq, (hd + 1) * tq) for hd in range(NSA_HEADS)]
    if values_on_rows:
        return _dot_nt(vc, p.astype(BF16)), [p[r] for r in rows]
    o = _dot(p.astype(BF16), vc)
    return [o[r] for r in rows], [p[r] for r in rows]


def _select_blocks(psum, qpos_row, ovt, n_sel, queries_on_rows=True):
    hi = psum.astype(BF16)
    r1 = psum - hi.astype(F32)
    mid = r1.astype(BF16)
    lo = (r1 - mid.astype(F32)).astype(BF16)
    sc = _dot_nt(ovt, hi) + _dot_nt(ovt, mid) + _dot_nt(ovt, lo)
    j = lax.broadcasted_iota(I32, (NSEL_PAD, LANES), 0)
    cur = qpos_row // SEL_BLOCK
    forced = (j == 0) | (j == cur) | (j == cur - 1)
    avail = j * SEL_BLOCK <= qpos_row
    sc = jnp.where(forced, BIG, jnp.where(avail, sc, -BIG))
    cnt = jnp.zeros((NSEL_PAD, LANES), I32)
    for ii in range(n_sel):
        row = sc[ii:ii + 1, :]
        beats = (row > sc) | ((row == sc) & (ii < j))
        cnt = cnt + beats.astype(I32)
    k_top = min(N_SELECT, n_sel)
    sel_t = jnp.where((cnt < k_top) & (j < n_sel), 1.0, 0.0).astype(F32)
    sel_t = jnp.concatenate([sel_t, jnp.zeros((LANES - NSEL_PAD, LANES), F32)], axis=0)
    return sel_t.T if queries_on_rows else sel_t


def _combine_heads(gs, o_cmp, o_sel, o_win, z):
    tq = z.shape[0]
    low = lax.broadcasted_iota(I32, (tq, LANES), 1) < HEAD_DIM
    placed = []
    for hd in range(NSA_HEADS):
        h = hd // NSA_GROUP
        o = (gs[:, hd:hd + 1] * o_cmp[hd] + gs[:, NSA_HEADS + hd:NSA_HEADS + hd + 1] * o_sel[hd]
             + gs[:, 2 * NSA_HEADS + hd:2 * NSA_HEADS + hd + 1] * o_win[hd])
        if hd % 2 != h:
            o = pltpu.roll(o, HEAD_DIM, 1)
        placed.append(o)
    slabs = [jnp.where(low, placed[2 * k], placed[2 * k + 1]) for k in range(NSA_HEADS // 2)]
    return jnp.concatenate(slabs, axis=1) * _silu(z)


def _compress_prompt_body(rows_ref, wab_ref, pc_ref, w2_ref, o_ref):
    xs = [rows_ref[pl.ds(l, N_CMP_PAD, stride=CMP_STRIDE), :].astype(BF16) for l in range(CMP_STRIDE)]
    x = jnp.concatenate(xs, axis=1)
    o_ref[...] = _compress_tokens([x], wab_ref[...], pc_ref[...], w2_ref[...])[0]


def _compress_prompt(p3, wab, pc, w2bd):
    b, t, _ = p3.shape
    kvblk = C_KV // LANES
    return pl.pallas_call(
        _compress_prompt_body,
        grid=(b, 2),
        in_specs=[pl.BlockSpec((None, t, LANES), lambda i, k: (i, 0, kvblk + k)),
                  pl.BlockSpec((None, CMP_STRIDE * LANES, 2 * LANES), lambda i, k: (k, 0, 0)),
                  pl.BlockSpec((None, SUBLANES, LANES), lambda i, k: (k, 0, 0)),
                  pl.BlockSpec((None, LANES, LANES), lambda i, k: (k, 0, 0))],
        out_specs=pl.BlockSpec((None, None, N_CMP_PAD, LANES), lambda i, k: (i, k, 0, 0)),
        out_shape=jax.ShapeDtypeStruct((b, 2, N_CMP_PAD, LANES), BF16),
        compiler_params=_cparams(("parallel", "parallel")),
    )(p3, wab, pc, w2bd)


def _nsa_prompt_body(q_ref, z_ref, g_ref, cmp_ref, sk_ref, sv_ref, wk_ref, wv_ref, ovt_ref, et_ref, o_ref,
                     svt_s, wvt_s, acc_s, *, n_sel):
    tq = 128
    i = pl.program_id(1)
    q0 = i * tq

    @pl.when(i == 0)
    def _():
        for c in range(svt_s.shape[0]):
            rows = slice(c * LANES, (c + 1) * LANES)
            svt_s[c] = sv_ref[rows, :].T.astype(BF16)
            wvt_s[c] = wv_ref[rows, :].T.astype(BF16)

    qh = _pad_heads(q_ref[...] * SCALE)
    q8 = jnp.concatenate(_pad_heads(q_ref[...] * (SCALE * LOG2E)), axis=0)
    gs = _sigmoid(g_ref[...])
    qpos_col = q0 + lax.broadcasted_iota(I32, (tq, 1), 0)
    qpos_row = q0 + lax.broadcasted_iota(I32, (1, LANES), 1)

    w0 = pl.multiple_of(jnp.maximum(q0 - WINDOW, 0), LANES)
    wwid = WINDOW + tq
    kwin = wk_ref[pl.ds(w0, wwid), :].astype(BF16)
    stw = _dot_nt(kwin, q8)

    vc_t = cmp_ref[1].astype(F32).T.astype(BF16)
    ocmp_t, p_cmp = _cmp_branch(qh, cmp_ref[0], vc_t, qpos_col, values_on_rows=True)
    ovt = ovt_ref[...]
    kp_w = w0 + lax.broadcasted_iota(I32, (wwid, LANES), 0)
    dist = qpos_row - kp_w
    maskw = jnp.where((dist >= 0) & (dist < WINDOW), 0.0, NEG)
    kbw = kp_w.astype(F32)
    wc0 = w0 // LANES
    sub = SEL_CHUNK // LANES
    nch = (i + 2 * sub) // (2 * sub)
    hq = NSA_HEADS * tq
    cols = [slice(hd * tq, (hd + 1) * tq) for hd in range(NSA_HEADS)]

    sel_t = []
    for h in range(NSA_KV_HEADS):
        psum = p_cmp[h * NSA_GROUP]
        for hd in range(h * NSA_GROUP + 1, (h + 1) * NSA_GROUP):
            psum = psum + p_cmp[hd]
        sel_t.append(_select_blocks(psum, qpos_row, ovt, n_sel, queries_on_rows=False).astype(BF16))

    pws, linv = [], []
    for hd in range(NSA_HEADS):
        s = stw[:, cols[hd]] + (NSA_SLOPES[hd] * LOG2E) * kbw + maskw
        p = jnp.exp2(s - jnp.max(s, axis=0, keepdims=True))
        linv.append(1.0 / jnp.sum(p, axis=0, keepdims=True))
        pws.append(p.astype(BF16))
    ptw = jnp.concatenate(pws, axis=1)
    owin_t = _dot(wvt_s[wc0], ptw[0:LANES])
    for k in range(1, wwid // LANES):
        owin_t = owin_t + _dot(wvt_s[wc0 + k], ptw[k * LANES:(k + 1) * LANES])
    owin_t = owin_t * jnp.concatenate(linv, axis=1)

    acc_s[...] = jnp.zeros_like(acc_s)

    def scores(c):
        k0 = pl.multiple_of(c * SEL_CHUNK, SEL_CHUNK)
        kch = sk_ref[pl.ds(k0, SEL_CHUNK), :].astype(BF16)
        et = et_ref[pl.ds(k0, SEL_CHUNK), :]
        st = _dot_nt(kch, q8)
        mexp = [_dot(et, sel_t[h]) for h in range(NSA_KV_HEADS)]
        return st, mexp

    def update(c, st, mexp, m, l):
        k0 = c * SEL_CHUNK
        kp = k0 + lax.broadcasted_iota(I32, (SEL_CHUNK, LANES), 0)
        causal = kp <= qpos_row
        kbf = kp.astype(F32)
        ms, ls, alphas, ps = [], [], [], []
        for hd in range(NSA_HEADS):
            if hd % NSA_GROUP == 0:
                maskb = jnp.where((mexp[hd // NSA_GROUP] > 0.5) & causal, 0.0, NEG)
            s = st[:, cols[hd]] + (NSA_SLOPES[hd] * LOG2E) * kbf + maskb
            m_new = jnp.maximum(m[:, cols[hd]], jnp.max(s, axis=0, keepdims=True))
            alpha = jnp.exp2(m[:, cols[hd]] - m_new)
            p = jnp.exp2(s - m_new)
            ms.append(m_new)
            ls.append(alpha * l[:, cols[hd]] + jnp.sum(p, axis=0, keepdims=True))
            alphas.append(alpha)
            ps.append(p.astype(BF16))
        pt = jnp.concatenate(ps, axis=1)
        ot = _dot(svt_s[c * sub], pt[0:LANES])
        for k in range(1, sub):
            ot = ot + _dot(svt_s[c * sub + k], pt[k * LANES:(k + 1) * LANES])
        acc_s[...] = acc_s[...] * jnp.concatenate(alphas, axis=1) + ot
        return jnp.concatenate(ms, axis=1), jnp.concatenate(ls, axis=1)

    def body(c2, carry):
        m, l = carry
        a = scores(2 * c2)
        b = scores(2 * c2 + 1)
        m, l = update(2 * c2, a[0], a[1], m, l)
        return update(2 * c2 + 1, b[0], b[1], m, l)

    init = (jnp.full((1, hq), NEG, F32), jnp.zeros((1, hq), F32))
    _, l = lax.fori_loop(0, nch, body, init)
    osel_t = acc_s[...] * jnp.where(l > 0.0, 1.0 / l, 0.0)

    gs_t = gs.T
    pieces = []
    for hd in range(NSA_HEADS):
        h = hd // NSA_GROUP
        o_t = (gs_t[hd:hd + 1] * ocmp_t[:, cols[hd]]
               + gs_t[NSA_HEADS + hd:NSA_HEADS + hd + 1] * osel_t[:, cols[hd]]
               + gs_t[2 * NSA_HEADS + hd:2 * NSA_HEADS + hd + 1] * owin_t[:, cols[hd]])
        pieces.append(o_t[h * HEAD_DIM:(h + 1) * HEAD_DIM])
    out_t = jnp.concatenate(pieces, axis=0)
    out = jnp.concatenate([out_t[k * LANES:(k + 1) * LANES].T for k in range(NSA_WIDTH // LANES)], axis=1)
    o_ref[...] = (out * _silu(z_ref[...])).astype(o_ref.dtype)


def _nsa_prompt(p3, cmp_tok, ovt, e3):
    b, t, _ = p3.shape
    tq = 128
    n_sel = -(-t // SEL_BLOCK)
    kvb = C_KV // LANES
    return pl.pallas_call(
        functools.partial(_nsa_prompt_body, n_sel=n_sel),
        grid=(b, t // tq),
        scratch_shapes=[pltpu.VMEM((t // LANES, LANES, LANES), BF16),
                        pltpu.VMEM((t // LANES, LANES, LANES), BF16),
                        pltpu.VMEM((LANES, NSA_HEADS * tq), F32)],
        in_specs=[pl.BlockSpec((None, tq, NSA_WIDTH), lambda bi, i: (bi, i, C_Q // NSA_WIDTH)),
                  pl.BlockSpec((None, tq, NSA_WIDTH), lambda bi, i: (bi, i, C_ZN // NSA_WIDTH)),
                  pl.BlockSpec((None, tq, LANES), lambda bi, i: (bi, i, C_G // LANES)),
                  pl.BlockSpec((None, 2, N_CMP_PAD, LANES), lambda bi, i: (bi, 0, 0, 0)),
                  pl.BlockSpec((None, t, LANES), lambda bi, i: (bi, 0, kvb + 2)),
                  pl.BlockSpec((None, t, LANES), lambda bi, i: (bi, 0, kvb + 3)),
                  pl.BlockSpec((None, t, LANES), lambda bi, i: (bi, 0, kvb + 4)),
                  pl.BlockSpec((None, t, LANES), lambda bi, i: (bi, 0, kvb + 5)),
                  pl.BlockSpec((NSEL_PAD, LANES), lambda bi, i: (0, 0)),
                  pl.BlockSpec(e3.shape, lambda bi, i: (0, 0))],
        out_specs=pl.BlockSpec((None, tq, NSA_WIDTH), lambda bi, i: (bi, i, 0)),
        out_shape=jax.ShapeDtypeStruct((b, t, NSA_WIDTH), BF16),
        compiler_params=_cparams(("parallel", "arbitrary")),
    )(p3, p3, p3, cmp_tok, p3, p3, p3, p3, ovt, e3)


def _diff_lambda(lam_ref, lam_init):
    a = lam_ref[...]
    s1 = jnp.sum(a[0:1] * a[1:2], axis=-1, keepdims=True)
    s2 = jnp.sum(a[2:3] * a[3:4], axis=-1, keepdims=True)
    return jnp.exp(s1) - jnp.exp(s2) + lam_init


def _diff_finish(o, lnw, z, lam_init):
    ms = jnp.mean(o * o, axis=-1, keepdims=True)
    return o * lax.rsqrt(ms + EPS) * lnw * (1.0 - lam_init) * _silu(z)


def _diff_prompt_body(lam_ref, q_ref, z_ref, k_ref, v_ref, lnw_ref, o_ref, vt_s, acc_s, *, lam_init):
    tq = 128
    i = pl.program_id(1)
    q0 = i * tq

    @pl.when(i == 0)
    def _():
        for hh in range(DIFF_HEADS):
            for c in range(vt_s.shape[1]):
                vt_s[hh, c] = v_ref[c * LANES:(c + 1) * LANES, hh * LANES:(hh + 1) * LANES].T.astype(BF16)

    lam = _diff_lambda(lam_ref, lam_init)
    low = lax.broadcasted_iota(I32, (tq, LANES), 1) < DIFF_HEAD_DIM
    qpos_row = q0 + lax.broadcasted_iota(I32, (1, LANES), 1)
    sub = SEL_CHUNK // LANES
    nch = (i + 2 * sub) // (2 * sub)
    q2 = []
    for hh in range(DIFF_HEADS):
        qs = q_ref[:, hh * LANES:(hh + 1) * LANES] * (DIFF_SCALE * LOG2E)
        q2.append(jnp.concatenate([jnp.where(low, qs, 0.0), jnp.where(low, 0.0, qs)], axis=0).astype(BF16))
    acc_s[...] = jnp.zeros_like(acc_s)
    ncol = 2 * tq

    def scores(c):
        k0 = pl.multiple_of(c * SEL_CHUNK, SEL_CHUNK)
        return [_dot_nt(k_ref[pl.ds(k0, SEL_CHUNK), hh * LANES:(hh + 1) * LANES].astype(BF16), q2[hh])
                for hh in range(DIFF_HEADS)]

    def update(c, sts, m, l):
        kp = c * SEL_CHUNK + lax.broadcasted_iota(I32, (SEL_CHUNK, LANES), 0)
        maskb = jnp.where(kp <= qpos_row, 0.0, NEG)
        kbf = kp.astype(F32)
        ms, ls = [], []
        for hh in range(DIFF_HEADS):
            st = sts[hh]
            bias = (DIFF_SLOPES[hh] * LOG2E) * kbf + maskb
            alphas, ps = [], []
            for comp in range(2):
                cs = slice(comp * tq, (comp + 1) * tq)
                gs_ = slice(hh * ncol + comp * tq, hh * ncol + (comp + 1) * tq)
                s = st[:, cs] + bias
                m_new = jnp.maximum(m[:, gs_], jnp.max(s, axis=0, keepdims=True))
                alpha = jnp.exp2(m[:, gs_] - m_new)
                p = jnp.exp2(s - m_new)
                ms.append(m_new)
                ls.append(alpha * l[:, gs_] + jnp.sum(p, axis=0, keepdims=True))
                alphas.append(alpha)
                ps.append(p.astype(BF16))
            pt = jnp.concatenate(ps, axis=1)
            ot = _dot(vt_s[hh, c * sub], pt[0:LANES])
            for k in range(1, sub):
                ot = ot + _dot(vt_s[hh, c * sub + k], pt[k * LANES:(k + 1) * LANES])
            acc_s[hh] = acc_s[hh] * jnp.concatenate(alphas, axis=1) + ot
        return jnp.concatenate(ms, axis=1), jnp.concatenate(ls, axis=1)

    def body(c2, carry):
        m, l = carry
        a = scores(2 * c2)
        b = scores(2 * c2 + 1)
        m, l = update(2 * c2, a, m, l)
        return update(2 * c2 + 1, b, m, l)

    init = (jnp.full((1, DIFF_HEADS * ncol), NEG, F32), jnp.zeros((1, DIFF_HEADS * ncol), F32))
    _, l = lax.fori_loop(0, nch, body, init)
    lnw = lnw_ref[...]
    slabs = []
    for hh in range(DIFF_HEADS):
        on = acc_s[hh] * (1.0 / l[:, hh * ncol:(hh + 1) * ncol])
        o = (on[:, 0:tq] - lam * on[:, tq:2 * tq]).T
        slabs.append(_diff_finish(o, lnw, z_ref[:, hh * LANES:(hh + 1) * LANES], lam_init))
    o_ref[...] = jnp.concatenate(slabs, axis=1).astype(o_ref.dtype)


def _diff_prompt(p3, lam4, lnw_row, lam_init):
    b, t, _ = p3.shape
    tq = 128
    return pl.pallas_call(
        functools.partial(_diff_prompt_body, lam_init=lam_init),
        grid=(b, t // tq),
        in_specs=[pl.BlockSpec((4, DIFF_HEAD_DIM), lambda bi, i: (0, 0)),
                  pl.BlockSpec((None, tq, DIFF_WIDTH), lambda bi, i: (bi, i, C_DQ // DIFF_WIDTH)),
                  pl.BlockSpec((None, tq, DIFF_WIDTH), lambda bi, i: (bi, i, C_ZD // DIFF_WIDTH)),
                  pl.BlockSpec((None, t, DIFF_WIDTH), lambda bi, i: (bi, 0, C_DK // DIFF_WIDTH)),
                  pl.BlockSpec((None, t, DIFF_WIDTH), lambda bi, i: (bi, 0, C_DV // DIFF_WIDTH)),
                  pl.BlockSpec((1, DIFF_V_DIM), lambda bi, i: (0, 0))],
        out_specs=pl.BlockSpec((None, tq, DIFF_WIDTH), lambda bi, i: (bi, i, 0)),
        out_shape=jax.ShapeDtypeStruct((b, t, DIFF_WIDTH), BF16),
        scratch_shapes=[pltpu.VMEM((DIFF_HEADS, t // LANES, LANES, LANES), BF16),
                        pltpu.VMEM((DIFF_HEADS, DIFF_V_DIM, 2 * tq), F32)],
        compiler_params=_cparams(("parallel", "arbitrary")),
    )(lam4, p3, p3, p3, p3, lnw_row)


SSM_BLOCKS = SSM_WIDTH // LANES
SSM_BS = N_STATE // SSM_BLOCKS
SSM_CHUNK = 256
SSM_SEG = SSM_CHUNK // SUBLANES
SSM_UNROLL = 32


def _discretize(lre, lim, ldt):
    dt = jnp.exp(ldt)
    mag = jnp.exp(lre * dt)
    ar = mag * jnp.cos(lim * dt)
    ai = mag * jnp.sin(lim * dt)
    den = lre * lre + lim * lim
    qr = ((ar - 1.0) * lre + ai * lim) / den
    qi = (ai * lre - (ar - 1.0) * lim) / den
    return ar, ai, qr, qi


def _bbar_block(qr, qi, bre, bim):
    return jnp.concatenate([qr * bre - qi * bim, qr * bim + qi * bre], axis=1).astype(BF16)


def _glu_tail(y, wglu, z):
    gate = _sigmoid(_dot(y.astype(BF16), wglu))
    return y * gate * _silu(z)


def _ssm_prompt_body(u_ref, z_ref, lre_ref, lim_ref, ldt_ref, bre_ref, bim_ref, cmat_ref, d_ref, wglu_ref,
                     perm_ref, o_ref, hr_ref, hi_ref,
                     bbar_s, apr_s, api_s, bu_s, y_s, carr_s, cari_s):
    first = (pl.program_id(0) == 0) & (pl.program_id(1) == 0)

    @pl.when(first)
    def _():
        ar, ai, qr, qi = _discretize(lre_ref[...], lim_ref[...], ldt_ref[...])
        for k in range(SSM_BLOCKS):
            cs = slice(k * SSM_BS, (k + 1) * SSM_BS)
            bbar_s[k] = _bbar_block(qr[:, cs], qi[:, cs], bre_ref[k], bim_ref[k])
        pr, pi = ar, ai
        apr_s[0:1, :] = pr
        api_s[0:1, :] = pi
        for k in range(1, SSM_SEG):
            pr, pi = pr * ar - pi * ai, pr * ai + pi * ar
            apr_s[k:k + 1, :] = pr
            api_s[k:k + 1, :] = pi

    @pl.when(pl.program_id(1) == 0)
    def _():
        carr_s[...] = jnp.zeros_like(carr_s)
        cari_s[...] = jnp.zeros_like(cari_s)

    re_c = slice(0, SSM_BS)
    im_c = slice(SSM_BS, 2 * SSM_BS)
    u_perm = _dot(perm_ref[0], u_ref[...].astype(BF16)).astype(BF16)
    unperm = perm_ref[1]

    for k in range(SSM_BLOCKS):
        cs = slice(k * SSM_BS, (k + 1) * SSM_BS)
        us = slice(k * LANES, (k + 1) * LANES)
        bu_s[...] = _dot(u_perm[:, us], bbar_s[k])
        ar_b = jnp.broadcast_to(apr_s[0:1, cs], (SUBLANES, SSM_BS))
        ai_b = jnp.broadcast_to(api_s[0:1, cs], (SUBLANES, SSM_BS))

        def step(t, st, ar_b=ar_b, ai_b=ai_b):
            hr, hi = st
            rows = pl.ds(pl.multiple_of(t * SUBLANES, SUBLANES), SUBLANES)
            nr = ar_b * hr - ai_b * hi + bu_s[rows, re_c]
            ni = ar_b * hi + ai_b * hr + bu_s[rows, im_c]
            bu_s[rows, re_c] = nr
            bu_s[rows, im_c] = ni
            return nr, ni

        zero = jnp.zeros((SUBLANES, SSM_BS), F32)
        er, ei = lax.fori_loop(0, SSM_SEG, step, (zero, zero), unroll=SSM_UNROLL)
        a32r = apr_s[SSM_SEG - 1:SSM_SEG, cs]
        a32i = api_s[SSM_SEG - 1:SSM_SEG, cs]
        rows_r = [carr_s[:, cs]]
        rows_i = [cari_s[:, cs]]
        for s in range(1, SUBLANES + 1):
            pr, pi = rows_r[-1], rows_i[-1]
            rows_r.append(er[s - 1:s] + a32r * pr - a32i * pi)
            rows_i.append(ei[s - 1:s] + a32r * pi + a32i * pr)
        carr_s[:, cs] = rows_r[SUBLANES]
        cari_s[:, cs] = rows_i[SUBLANES]
        hin_r = jnp.concatenate(rows_r[:SUBLANES], axis=0)
        hin_i = jnp.concatenate(rows_i[:SUBLANES], axis=0)

        def fix(t, carry, hin_r=hin_r, hin_i=hin_i, cs=cs):
            rows = pl.ds(pl.multiple_of(t * SUBLANES, SUBLANES), SUBLANES)
            pr = apr_s[pl.ds(t, 1), cs]
            pi = api_s[pl.ds(t, 1), cs]
            bu_s[rows, re_c] = bu_s[rows, re_c] + pr * hin_r - pi * hin_i
            bu_s[rows, im_c] = bu_s[rows, im_c] + pr * hin_i + pi * hin_r
            return carry

        lax.fori_loop(0, SSM_SEG, fix, 0, unroll=SSM_UNROLL)
        yp = _dot(bu_s[...].astype(BF16), cmat_ref[k])
        y_hi = yp.astype(BF16)
        r1 = yp - y_hi.astype(F32)
        y_mid = r1.astype(BF16)
        y_lo = (r1 - y_mid.astype(F32)).astype(BF16)
        yk = _dot(unperm, y_hi) + _dot(unperm, y_mid) + _dot(unperm, y_lo) + d_ref[:, us] * u_ref[:, us]
        y_s[:, us] = _gelu(yk)

    o_ref[...] = _glu_tail(y_s[...], wglu_ref[...], z_ref[...]).astype(o_ref.dtype)
    hr_ref[...] = carr_s[...]
    hi_ref[...] = cari_s[...]


def _ssm_prompt(p3, sp):
    b, t, _ = p3.shape
    full = lambda shape: pl.BlockSpec(shape, lambda bi, j: (0,) * len(shape))
    out, hr, hi = pl.pallas_call(
        _ssm_prompt_body,
        grid=(b, t // SSM_CHUNK),
        in_specs=[pl.BlockSpec((None, SSM_CHUNK, SSM_WIDTH), lambda bi, j: (bi, j, C_U // SSM_WIDTH)),
                  pl.BlockSpec((None, SSM_CHUNK, SSM_WIDTH), lambda bi, j: (bi, j, C_ZS // SSM_WIDTH)),
                  full((1, N_STATE)), full((1, N_STATE)), full((1, N_STATE)),
                  full((SSM_BLOCKS, LANES, SSM_BS)), full((SSM_BLOCKS, LANES, SSM_BS)),
                  full((SSM_BLOCKS, 2 * SSM_BS, LANES)), full((1, SSM_WIDTH)),
                  full((SSM_WIDTH, SSM_WIDTH)), full((2, SSM_CHUNK, SSM_CHUNK))],
        out_specs=[pl.BlockSpec((None, SSM_CHUNK, SSM_WIDTH), lambda bi, j: (bi, j, 0)),
                   pl.BlockSpec((None, 1, N_STATE), lambda bi, j: (bi, 0, 0)),
                   pl.BlockSpec((None, 1, N_STATE), lambda bi, j: (bi, 0, 0))],
        out_shape=[jax.ShapeDtypeStruct((b, t, SSM_WIDTH), BF16),
                   jax.ShapeDtypeStruct((b, 1, N_STATE), F32),
                   jax.ShapeDtypeStruct((b, 1, N_STATE), F32)],
        scratch_shapes=[pltpu.VMEM((SSM_BLOCKS, LANES, 2 * SSM_BS), BF16),
                        pltpu.VMEM((SSM_SEG, N_STATE), F32), pltpu.VMEM((SSM_SEG, N_STATE), F32),
                        pltpu.VMEM((SSM_CHUNK, 2 * SSM_BS), F32),
                        pltpu.VMEM((SSM_CHUNK, SSM_WIDTH), F32),
                        pltpu.VMEM((1, N_STATE), F32), pltpu.VMEM((1, N_STATE), F32)],
        compiler_params=_cparams(("arbitrary", "arbitrary")),
    )(p3, p3, sp['lre'], sp['lim'], sp['ldt'], sp['bre'], sp['bim'], sp['cmat'], sp['d'], sp['wglu'],
      _segment_permutation())
    return out, hr, hi


def _segment_permutation():
    r = np.arange(SSM_CHUNK)
    src = (r % SUBLANES) * SSM_SEG + r // SUBLANES
    pm = np.zeros((SSM_CHUNK, SSM_CHUNK), np.float32)
    pm[r, src] = 1.0
    return jnp.asarray(np.stack([pm, pm.T]), BF16)


def _ssm_sample_body(u_ref, z_ref, h0r_ref, h0i_ref, lre_ref, lim_ref, ldt_ref, bre_ref, bim_ref, cmat_ref,
                     d_ref, wglu_ref, o_ref, hr_ref, hi_ref, y_s, *, n_t):
    k = pl.program_id(0)
    ar, ai, qr, qi = _discretize(lre_ref[...], lim_ref[...], ldt_ref[...])
    bbar = _bbar_block(qr, qi, bre_ref[...], bim_ref[...])
    hr = h0r_ref[...]
    hi = h0i_ref[...]
    cmat = cmat_ref[...]
    for t in range(n_t):
        ut = u_ref[t]
        bu = _dot(ut.astype(BF16), bbar)
        hr, hi = ar * hr - ai * hi + bu[:, 0:SSM_BS], ar * hi + ai * hr + bu[:, SSM_BS:]
        hcat = jnp.concatenate([hr, hi], axis=1).astype(BF16)
        y_s[t, k] = _gelu(_dot(hcat, cmat) + d_ref[...] * ut)
    hr_ref[...] = hr
    hi_ref[...] = hi

    @pl.when(k == SSM_BLOCKS - 1)
    def _():
        wglu = wglu_ref[...]
        for t in range(n_t):
            y = jnp.concatenate([y_s[t, kk] for kk in range(SSM_BLOCKS)], axis=1)
            o_ref[t] = _glu_tail(y, wglu, z_ref[t])


def _ssm_sample(ut, zt, h0r, h0i, sp):
    n_t, nb, _ = ut.shape
    out, hr, hi = pl.pallas_call(
        functools.partial(_ssm_sample_body, n_t=n_t),
        grid=(SSM_BLOCKS,),
        in_specs=[pl.BlockSpec((n_t, nb, LANES), lambda k: (0, 0, k)),
                  pl.BlockSpec((n_t, nb, SSM_WIDTH), lambda k: (0, 0, 0)),
                  pl.BlockSpec((nb, SSM_BS), lambda k: (0, k)),
                  pl.BlockSpec((nb, SSM_BS), lambda k: (0, k)),
                  pl.BlockSpec((1, SSM_BS), lambda k: (0, k)),
                  pl.BlockSpec((1, SSM_BS), lambda k: (0, k)),
                  pl.BlockSpec((1, SSM_BS), lambda k: (0, k)),
                  pl.BlockSpec((None, LANES, SSM_BS), lambda k: (k, 0, 0)),
                  pl.BlockSpec((None, LANES, SSM_BS), lambda k: (k, 0, 0)),
                  pl.BlockSpec((None, 2 * SSM_BS, LANES), lambda k: (k, 0, 0)),
                  pl.BlockSpec((1, LANES), lambda k: (0, k)),
                  pl.BlockSpec((SSM_WIDTH, SSM_WIDTH), lambda k: (0, 0))],
        out_specs=[pl.BlockSpec((n_t, nb, SSM_WIDTH), lambda k: (0, 0, 0)),
                   pl.BlockSpec((nb, SSM_BS), lambda k: (0, k)),
                   pl.BlockSpec((nb, SSM_BS), lambda k: (0, k))],
        out_shape=[jax.ShapeDtypeStruct((n_t, nb, SSM_WIDTH), F32),
                   jax.ShapeDtypeStruct((nb, N_STATE), F32),
                   jax.ShapeDtypeStruct((nb, N_STATE), F32)],
        scratch_shapes=[pltpu.VMEM((n_t, SSM_BLOCKS, nb, LANES), F32)],
        compiler_params=_cparams(("arbitrary",)),
    )(ut, zt, h0r, h0i, sp['lre'], sp['lim'], sp['ldt'], sp['bre'], sp['bim'], sp['cmat'], sp['d'], sp['wglu'])
    return out, hr, hi


def _page_map(li, pg, b, pt):
    return (li, pt[b, pg], 0, 0)


def _page_map_seq(li, pg, sq, b, pt):
    return (li, pt[b * NSA_DEC_SEQS + sq, pg], 0, 0)


def _pad_rows(x, rows):
    return jnp.concatenate([x, jnp.zeros((rows - x.shape[0], x.shape[1]), x.dtype)], axis=0)


def _nsa_sample_body(pt_ref, *refs, n_pages, n_sel):
    nsq = NSA_DEC_SEQS
    (q_ref, z_ref, g_ref, snew_ref, wnew_ref, wb_ref, wab_ref, pc_ref, w2_ref, ovt_ref, e_ref,
     gperm_ref, o_ref, s_s) = refs[nsq * n_pages:]
    seqs = range(nsq)
    pages = [refs[sq * n_pages:(sq + 1) * n_pages] for sq in seqs]
    rs = [slice(sq * T_PAD, (sq + 1) * T_PAD) for sq in seqs]
    past = n_pages * PAGE_SIZE
    nk = past + LANES
    wb = wb_ref.shape[2]
    qpos_col = past + lax.broadcasted_iota(I32, (T_PAD, 1), 0)
    qpos64 = jnp.concatenate([qpos_col] * NSA_HEADS, axis=0)
    qpos_row = past + (lax.broadcasted_iota(I32, (1, LANES), 1) % T_PAD)
    slope_col = jnp.concatenate([jnp.full((T_PAD, 1), NSA_SLOPES[hd], F32) for hd in range(NSA_HEADS)], axis=0)
    kpos = lax.broadcasted_iota(I32, (1, nk), 1)
    kbias = slope_col * kpos.astype(F32)
    kpos_w = (past - wb) + lax.broadcasted_iota(I32, (1, wb + LANES), 1)
    wbias = slope_col * kpos_w.astype(F32)
    dist = qpos64 - kpos_w
    wmask = (dist >= 0) & (dist < WINDOW) & (kpos_w >= 0)
    causal = kpos <= qpos64
    sk_r = pl.ds(2 * LANES, LANES)
    sv_r = pl.ds(3 * LANES, LANES)

    qh = [_pad_heads(q_ref[rs[sq], :] * SCALE) for sq in seqs]
    q64 = [jnp.concatenate(qh[sq], axis=0) for sq in seqs]

    gperm = gperm_ref[...]
    gathered = [[[_dot(pages[sq][pg][kv * LANES:(kv + 1) * LANES, :].astype(BF16), gperm).T
                  for pg in range(n_pages)] for sq in seqs] for kv in range(2)]

    owin = []
    for sq in seqs:
        wnew = wnew_ref[rs[sq], :]
        kwn = _pad_rows(wnew[:, 0:LANES], LANES).astype(BF16)
        vwn = _pad_rows(wnew[:, LANES:2 * LANES], LANES).astype(BF16)
        sw = jnp.concatenate([_dot(q64[sq], wb_ref[sq, 0:LANES, :].astype(BF16)), _dot_nt(q64[sq], kwn)], axis=1)
        pw = _msoftmax(sw + wbias, wmask).astype(BF16)
        owin.append(_dot_nt(pw[:, 0:wb], wb_ref[sq, LANES:2 * LANES, :].astype(BF16)) + _dot(pw[:, wb:], vwn))

    for sq in seqs:
        for pg in range(n_pages):
            s_s[sq, :, pg * PAGE_SIZE:(pg + 1) * PAGE_SIZE] = _dot(q64[sq], pages[sq][pg][sk_r, :].astype(BF16))
        knew = _pad_rows(snew_ref[rs[sq], 0:LANES], LANES).astype(BF16)
        s_s[sq, :, past:nk] = _dot_nt(q64[sq], knew)

    toks = []
    npc = PAGE_SIZE // CMP_STRIDE
    for kv in range(2):
        xseq = []
        for sq in seqs:
            xs = [jnp.concatenate([gathered[kv][sq][pg][l * npc:(l + 1) * npc] for pg in range(n_pages)],
                                  axis=0).astype(BF16) for l in range(CMP_STRIDE)]
            xseq.append(jnp.concatenate(xs, axis=1))
        toks.append(_compress_tokens(xseq, wab_ref[kv], pc_ref[kv], w2_ref[kv]))
    cmp = [_cmp_branch(qh[sq], toks[0][sq], toks[1][sq], qpos_col) for sq in seqs]

    memb = []
    for sq in seqs:
        p_cmp = cmp[sq][1]
        psum = jnp.concatenate(
            [p_cmp[0] + p_cmp[1] + p_cmp[2] + p_cmp[3], p_cmp[4] + p_cmp[5] + p_cmp[6] + p_cmp[7],
             jnp.zeros((LANES - 2 * T_PAD, N_CMP_PAD), F32)], axis=0)
        sel = _select_blocks(psum, qpos_row, ovt_ref[...], n_sel)
        mexp = _dot(sel[0:2 * T_PAD].astype(BF16), e_ref[...])
        memb.append(
            jnp.concatenate([mexp[0:T_PAD]] * NSA_GROUP + [mexp[T_PAD:2 * T_PAD]] * NSA_GROUP, axis=0) > 0.5)

    for sq in seqs:
        p = _msoftmax(s_s[sq] + kbias, memb[sq] & causal).astype(BF16)
        vnew = _pad_rows(snew_ref[rs[sq], LANES:2 * LANES], LANES).astype(BF16)
        osel = _dot(p[:, past:nk], vnew)
        for pg in range(n_pages):
            osel = osel + _dot_nt(p[:, pg * PAGE_SIZE:(pg + 1) * PAGE_SIZE], pages[sq][pg][sv_r, :].astype(BF16))
        o_sel = [osel[hd * T_PAD:(hd + 1) * T_PAD] for hd in range(NSA_HEADS)]
        o_win = [owin[sq][hd * T_PAD:(hd + 1) * T_PAD] for hd in range(NSA_HEADS)]
        o_ref[rs[sq], :] = _combine_heads(_sigmoid(g_ref[rs[sq], :]), cmp[sq][0], o_sel, o_win, z_ref[rs[sq], :])


def _nsa_sample(p2, cache_t, li, win_t, page_table, wab, pc, w2bd, ovt, e2, n_t):
    n = p2.shape[0]
    nb, n_pages = page_table.shape
    past = n_pages * PAGE_SIZE
    n_sel = -(-(past + n_t) // SEL_BLOCK)
    nk = past + LANES
    wb = win_t.shape[3]
    kvb = C_KV // (2 * LANES)
    nsq = NSA_DEC_SEQS
    assert nb % nsq == 0
    page_specs = [pl.BlockSpec((None, None, 4 * LANES, PAGE_SIZE), functools.partial(_page_map_seq, li, pg, sq))
                  for sq in range(nsq) for pg in range(n_pages)]
    const = lambda shape: pl.BlockSpec(shape, lambda b, pt: (0,) * len(shape))
    rows = nsq * T_PAD
    in_specs = page_specs + [
        pl.BlockSpec((rows, NSA_WIDTH), lambda b, pt: (b, C_Q // NSA_WIDTH)),
        pl.BlockSpec((rows, NSA_WIDTH), lambda b, pt: (b, C_ZN // NSA_WIDTH)),
        pl.BlockSpec((rows, LANES), lambda b, pt: (b, C_G // LANES)),
        pl.BlockSpec((rows, 2 * LANES), lambda b, pt: (b, kvb + 1)),
        pl.BlockSpec((rows, 2 * LANES), lambda b, pt: (b, kvb + 2)),
        pl.BlockSpec((None, nsq, 2 * LANES, wb), lambda b, pt: (li, b, 0, 0)),
        const(wab.shape), const(pc.shape), const(w2bd.shape),
        const(ovt.shape), const(e2.shape), const((PAGE_SIZE, PAGE_SIZE))]
    return pl.pallas_call(
        functools.partial(_nsa_sample_body, n_pages=n_pages, n_sel=n_sel),
        grid_spec=pltpu.PrefetchScalarGridSpec(
            num_scalar_prefetch=1, grid=(nb // nsq,), in_specs=in_specs,
            out_specs=pl.BlockSpec((rows, NSA_WIDTH), lambda b, pt: (b, 0)),
            scratch_shapes=[pltpu.VMEM((nsq, NSA_HEADS * T_PAD, nk), F32)]),
        out_shape=jax.ShapeDtypeStruct((n, NSA_WIDTH), F32),
        compiler_params=_cparams(("parallel",)),
    )(page_table, *([cache_t] * (nsq * n_pages)), p2, p2, p2, p2, p2, win_t, wab, pc, w2bd, ovt, e2,
      _chunk_gather_permutation())


def _chunk_gather_permutation():
    k = np.arange(PAGE_SIZE)
    pm = np.zeros((PAGE_SIZE, PAGE_SIZE), np.float32)
    pm[k, (k % CMP_STRIDE) * (PAGE_SIZE // CMP_STRIDE) + k // CMP_STRIDE] = 1.0
    return jnp.asarray(pm, BF16)


DIFF_PAGE_BUFS = 3


def _diff_sample_body(pt_ref, cache_ref, lam_ref, q_ref, z_ref, knew_ref, vnew_ref, lnw_ref, o_ref,
                      s_s, buf, sem, *, li, n_pages, lam_init):
    b = pl.program_id(0)
    nb = pl.num_programs(0)

    def page_copies(seq, slot):
        return [pltpu.make_async_copy(cache_ref.at[li, pt_ref[seq, pg]], buf.at[slot, pg], sem.at[slot, pg])
                for pg in range(n_pages)]

    @pl.when(b == 0)
    def _():
        for s0 in range(DIFF_PAGE_BUFS - 1):
            @pl.when(s0 < nb)
            def _(s0=s0):
                for cp in page_copies(s0, s0):
                    cp.start()

    ahead = b + (DIFF_PAGE_BUFS - 1)

    @pl.when(ahead < nb)
    def _():
        for cp in page_copies(ahead, ahead % DIFF_PAGE_BUFS):
            cp.start()

    slot = b % DIFF_PAGE_BUFS
    for cp in page_copies(b, slot):
        cp.wait()
    pages = [buf.at[slot, pg] for pg in range(n_pages)]
    past = n_pages * PAGE_SIZE
    nk = past + LANES
    lam = _diff_lambda(lam_ref, lam_init)
    low = lax.broadcasted_iota(I32, (T_PAD, LANES), 1) < DIFF_HEAD_DIM
    rows_h = 2 * T_PAD
    q16 = []
    for hh in range(DIFF_HEADS):
        qs = q_ref[:, hh * LANES:(hh + 1) * LANES] * DIFF_SCALE
        q16.append(jnp.concatenate([jnp.where(low, qs, 0.0), jnp.where(low, 0.0, qs)], axis=0).astype(BF16))
    for pg in range(n_pages):
        for hh in range(DIFF_HEADS):
            kp = pages[pg][pl.ds(hh, PAGE_SIZE, stride=DIFF_SLABS), :].astype(BF16)
            s_s[hh * rows_h:(hh + 1) * rows_h, pg * PAGE_SIZE:(pg + 1) * PAGE_SIZE] = _dot_nt(q16[hh], kp)
    for hh in range(DIFF_HEADS):
        kn = _pad_rows(knew_ref[:, hh * LANES:(hh + 1) * LANES], LANES).astype(BF16)
        s_s[hh * rows_h:(hh + 1) * rows_h, past:nk] = _dot_nt(q16[hh], kn)
    kpos = lax.broadcasted_iota(I32, (1, nk), 1)
    slope_col = jnp.concatenate([jnp.full((rows_h, 1), DIFF_SLOPES[hh], F32) for hh in range(DIFF_HEADS)], axis=0)
    qpos = past + (lax.broadcasted_iota(I32, (DIFF_HEADS * rows_h, 1), 0) % T_PAD)
    p = _msoftmax(s_s[...] + slope_col * kpos.astype(F32), kpos <= qpos)
    lnw = lnw_ref[...]
    slabs = []
    for hh in range(DIFF_HEADS):
        a = (p[hh * rows_h:hh * rows_h + T_PAD] - lam * p[hh * rows_h + T_PAD:(hh + 1) * rows_h]).astype(BF16)
        vn = _pad_rows(vnew_ref[:, hh * LANES:(hh + 1) * LANES], LANES).astype(BF16)
        o = _dot(a[:, past:nk], vn)
        for pg in range(n_pages):
            vp = pages[pg][pl.ds(DIFF_HEADS + hh, PAGE_SIZE, stride=DIFF_SLABS), :].astype(BF16)
            o = o + _dot(a[:, pg * PAGE_SIZE:(pg + 1) * PAGE_SIZE], vp)
        slabs.append(_diff_finish(o, lnw, z_ref[:, hh * LANES:(hh + 1) * LANES], lam_init))
    o_ref[...] = jnp.concatenate(slabs, axis=1)


def _diff_sample(p2, cache_r, li, page_table, lam4, lnw_row, lam_init):
    n = p2.shape[0]
    nb, n_pages = page_table.shape
    nk = n_pages * PAGE_SIZE + LANES
    in_specs = [
        pl.BlockSpec(memory_space=pl.ANY),
        pl.BlockSpec((4, DIFF_HEAD_DIM), lambda b, pt: (0, 0)),
        pl.BlockSpec((T_PAD, DIFF_WIDTH), lambda b, pt: (b, C_DQ // DIFF_WIDTH)),
        pl.BlockSpec((T_PAD, DIFF_WIDTH), lambda b, pt: (b, C_ZD // DIFF_WIDTH)),
        pl.BlockSpec((T_PAD, DIFF_WIDTH), lambda b, pt: (b, C_DK // DIFF_WIDTH)),
        pl.BlockSpec((T_PAD, DIFF_WIDTH), lambda b, pt: (b, C_DV // DIFF_WIDTH)),
        pl.BlockSpec((1, DIFF_V_DIM), lambda b, pt: (0, 0))]
    return pl.pallas_call(
        functools.partial(_diff_sample_body, li=li, n_pages=n_pages, lam_init=lam_init),
        grid_spec=pltpu.PrefetchScalarGridSpec(
            num_scalar_prefetch=1, grid=(nb,), in_specs=in_specs,
            out_specs=pl.BlockSpec((T_PAD, DIFF_WIDTH), lambda b, pt: (b, 0)),
            scratch_shapes=[pltpu.VMEM((DIFF_HEADS * 2 * T_PAD, nk), F32),
                            pltpu.VMEM((DIFF_PAGE_BUFS, n_pages, PAGE_SIZE * DIFF_SLABS, LANES), F32),
                            pltpu.SemaphoreType.DMA((DIFF_PAGE_BUFS, n_pages))]),
        out_shape=jax.ShapeDtypeStruct((n, DIFF_WIDTH), F32),
        compiler_params=_cparams(("arbitrary",)),
    )(page_table, cache_r, lam4, p2, p2, p2, p2, lnw_row)


WIN_ROLL_SEQS = 8


def _roll_window_body(win_ref, *refs, n_t):
    new_refs, o_ref = refs[:-1], refs[-1]
    li = pl.program_id(0)
    nbk, f, wb = win_ref.shape
    new_all = new_refs[0][...]
    for d in range(1, len(new_refs)):
        new_all = jnp.where(li == d, new_refs[d][...], new_all)
    tail = lax.broadcasted_iota(I32, (f, LANES), 1) >= LANES - n_t
    for bb in range(nbk):
        rolled = pltpu.roll(win_ref[bb], wb - n_t, 1)
        new_t = _pad_rows(new_all[bb * T_PAD:(bb + 1) * T_PAD], LANES).T
        new_t = pltpu.roll(new_t, LANES - n_t, 1)
        o_ref[bb, :, 0:wb - LANES] = rolled[:, 0:wb - LANES]
        o_ref[bb, :, wb - LANES:wb] = jnp.where(tail, new_t, rolled[:, wb - LANES:wb])


def _roll_window(win_t, ps_list, n_t):
    depth, nb, f, wb = win_t.shape
    nbk = WIN_ROLL_SEQS
    wcol = (C_KV + 4 * LANES) // f
    return pl.pallas_call(
        functools.partial(_roll_window_body, n_t=n_t),
        grid=(depth, nb // nbk),
        in_specs=[pl.BlockSpec((None, nbk, f, wb), lambda li, i: (li, i, 0, 0))]
        + [pl.BlockSpec((nbk * T_PAD, f), lambda li, i: (i, wcol)) for _ in ps_list],
        out_specs=pl.BlockSpec((None, nbk, f, wb), lambda li, i: (li, i, 0, 0)),
        out_shape=jax.ShapeDtypeStruct(win_t.shape, F32),
        compiler_params=_cparams(("parallel", "parallel")),
    )(win_t, *ps_list)


def _overlap_t(n_sel):
    n = np.arange(N_CMP_PAD)[None, :]
    j = np.arange(NSEL_PAD)[:, None]
    cs = n * CMP_STRIDE
    ss = j * SEL_BLOCK
    ov = np.clip(np.minimum(cs + CMP_BLOCK, ss + SEL_BLOCK) - np.maximum(cs, ss), 0, None) / CMP_BLOCK
    ov = np.where((n < N_CMP_PAD - 1) & (j < n_sel), ov, 0.0)
    return jnp.asarray(ov, BF16)


def _block_expander(n_keys):
    j = np.arange(LANES)[:, None]
    k = np.arange(n_keys)[None, :]
    return jnp.asarray((k // SEL_BLOCK) == j, BF16)


def _layer_weights(li, norm_w, w_in, w_out, w_cmp1, pos_cmp, w_cmp2, lre, lim, ldt, bre, bim, cre, cim, sd,
                   w_glu, lam_q1, lam_k1, lam_q2, lam_k2, diff_ln_w):
    w = {}
    w['nw'] = norm_w[li][None, :]
    w['w_out'] = w_out[li].astype(BF16)
    eye2 = jnp.eye(NSA_KV_HEADS, dtype=F32)
    w1 = w_cmp1[li].reshape(2, 2, CMP_STRIDE, HEAD_DIM, HEAD_DIM)
    w['wab'] = jnp.einsum('kaldf,hg->klhdagf', w1, eye2).reshape(
        2, CMP_STRIDE * LANES, 2 * LANES).astype(BF16)
    pos = pos_cmp[li].reshape(2, 2, CMP_STRIDE, 1, HEAD_DIM)
    pos = jnp.broadcast_to(pos, (2, 2, CMP_STRIDE, NSA_KV_HEADS, HEAD_DIM)).reshape(2, 2, 1, CMP_STRIDE * LANES)
    pos = jnp.broadcast_to(pos, (2, 2, SUBLANES, CMP_STRIDE * LANES)).astype(BF16)
    w['pos_a'] = pos[:, 0]
    w['pos_b'] = pos[:, 1]
    w['w2bd'] = jnp.einsum('ked,hg->khegd', w_cmp2[li], eye2).reshape(2, LANES, LANES).astype(BF16)
    eye8 = jnp.eye(SSM_BLOCKS, dtype=F32)
    gl = SSM_BLOCKS

    def compact_b(bm):
        bm = bm.reshape(SSM_BLOCKS, gl, SSM_STATE, SSM_GROUP_CH)
        return jnp.einsum('kgpc,gh->kgchp', bm, eye8).reshape(SSM_BLOCKS, LANES, SSM_BS)

    def compact_c(cm):
        cm = cm.reshape(SSM_BLOCKS, gl, SSM_GROUP_CH, SSM_STATE)
        return jnp.einsum('kgcp,gh->khpgc', cm, eye8).reshape(SSM_BLOCKS, SSM_BS, LANES)

    sp = {'lre': lre[li].reshape(1, N_STATE), 'lim': lim[li].reshape(1, N_STATE),
          'ldt': jnp.repeat(ldt[li], SSM_STATE).reshape(1, N_STATE),
          'bre': compact_b(bre[li]), 'bim': compact_b(bim[li]),
          'cmat': jnp.concatenate([compact_c(cre[li]), -compact_c(cim[li])], axis=1).astype(BF16),
          'd': sd[li].reshape(1, SSM_WIDTH), 'wglu': w_glu[li].astype(BF16)}
    w['ssm'] = sp
    w['lam4'] = jnp.stack([lam_q1[li], lam_k1[li], lam_q2[li], lam_k2[li]])
    w['lnw'] = diff_ln_w[li][None, :]
    w['lam_init'] = 0.8 - 0.6 * math.exp(-0.3 * li)
    return w


def kernel(x_prompt, x_sample, cache_nsa_kv, cache_diff_kv, state_nsa_win, state_ssm_re, state_ssm_im, page_table, norm_w, w_in, w_out, w_cmp1, pos_cmp, w_cmp2, ssm_lambda_re, ssm_lambda_im, ssm_log_dt, ssm_b_re, ssm_b_im, ssm_c_re, ssm_c_im, ssm_d, w_glu, lam_q1, lam_k1, lam_q2, lam_k2, diff_ln_w, final_norm_w):
    depth = norm_w.shape[0]
    b, t, _ = x_prompt.shape
    nb, n_t, _ = x_sample.shape
    n_pool = cache_nsa_kv.shape[1]
    n_pages = page_table.shape[1]
    past = n_pages * PAGE_SIZE
    wb = state_nsa_win.shape[2]
    assert t % SSM_CHUNK == 0 and t >= WINDOW + 128 and n_t <= T_PAD

    ovt_p = _overlap_t(-(-t // SEL_BLOCK))
    ovt_s = _overlap_t(-(-(past + n_t) // SEL_BLOCK))
    e_p = _block_expander(t).T
    e_s = _block_expander(past + LANES)
    fw = final_norm_w[None, :]
    nsa_cache_t = cache_nsa_kv.transpose(0, 1, 3, 4, 5, 2).reshape(depth, n_pool, 4 * LANES, PAGE_SIZE)
    win_t = state_nsa_win.transpose(0, 1, 3, 4, 5, 2).reshape(depth, nb, 2 * LANES, wb)
    diff_cache_r = cache_diff_kv.reshape(depth, n_pool, PAGE_SIZE * DIFF_SLABS, LANES)
    w_rows = w_in.transpose(0, 2, 1)
    w_in_t = jnp.concatenate(
        [w_rows[:, ORIG_OFFS[s]:ORIG_OFFS[s + 1]] for s in NEW_ORDER]
        + [jnp.zeros((depth, DP - ORIG_OFFS[-1], D_MODEL), w_in.dtype)], axis=1).astype(BF16)

    xp = x_prompt.reshape(b * t, D_MODEL)
    xs = jnp.pad(x_sample, ((0, 0), (0, T_PAD - n_t), (0, 0))).reshape(nb * T_PAD, D_MODEL)
    outs = {k: [] for k in ('p_kv', 'p_dkv', 'p_win', 'p_re', 'p_im', 's_kv', 's_dkv', 's_win', 's_re', 's_im')}
    for li in range(depth):
        w = _layer_weights(li, norm_w, w_in, w_out, w_cmp1, pos_cmp, w_cmp2, ssm_lambda_re, ssm_lambda_im,
                           ssm_log_dt, ssm_b_re, ssm_b_im, ssm_c_re, ssm_c_im, ssm_d, w_glu,
                           lam_q1, lam_k1, lam_q2, lam_k2, diff_ln_w)
        final = li == depth - 1
        pp, p_dkv_rows = _norm_project(xp, w['nw'], w_in_t, li)
        p3 = pp.reshape(b, t, DP)
        pos_c = _compress_pos_const(w['pos_a'], w['pos_b'], w['wab'])
        cmp_tok = _compress_prompt(p3, w['wab'], pos_c, w['w2bd'])
        nsa_o = _nsa_prompt(p3, cmp_tok, ovt_p, e_p)
        ssm_o, hr, hi = _ssm_prompt(p3, w['ssm'])
        diff_o = _diff_prompt(p3, w['lam4'], w['lnw'], w['lam_init'])
        xp = _merge_out(nsa_o.reshape(b * t, -1), ssm_o.reshape(b * t, -1), diff_o.reshape(b * t, -1), xp,
                        w['w_out'], fw, final)
        kv = p3[:, :, C_KV:C_KV + 6 * LANES]
        outs['p_kv'].append(kv[:, :, :4 * LANES].reshape(b, t, 4, NSA_KV_HEADS, HEAD_DIM))
        outs['p_dkv'].append(p_dkv_rows.reshape(b, t, 2, DIFF_HEADS, DIFF_V_DIM))
        outs['p_win'].append(kv[:, t - min(WINDOW, t):, 4 * LANES:].reshape(b, -1, 2, NSA_KV_HEADS, HEAD_DIM))
        outs['p_re'].append(hr.reshape(b, SSM_GROUPS, SSM_STATE))
        outs['p_im'].append(hi.reshape(b, SSM_GROUPS, SSM_STATE))
        ps, s_dkv_rows = _norm_project(xs, w['nw'], w_in_t, li)
        nsa_s = _nsa_sample(ps, nsa_cache_t, li, win_t, page_table,
                            w['wab'], pos_c, w['w2bd'], ovt_s, e_s, n_t)
        ps_t = ps.reshape(nb, T_PAD, DP)[:, :n_t].transpose(1, 0, 2)
        ssm_t, shr, shi = _ssm_sample(ps_t[:, :, C_U:C_U + SSM_WIDTH], ps_t[:, :, C_ZS:C_ZS + SSM_WIDTH],
                                      state_ssm_re[li].reshape(nb, N_STATE),
                                      state_ssm_im[li].reshape(nb, N_STATE), w['ssm'])
        ssm_s = jnp.pad(ssm_t.transpose(1, 0, 2), ((0, 0), (0, T_PAD - n_t), (0, 0))).reshape(nb * T_PAD, SSM_WIDTH)
        diff_s = _diff_sample(ps, diff_cache_r, li, page_table, w['lam4'], w['lnw'], w['lam_init'])
        xs = _merge_out(nsa_s, ssm_s, diff_s, xs, w['w_out'], fw, final)
        ps3 = ps.reshape(nb, T_PAD, DP)[:, :n_t]
        skv = ps3[:, :, C_KV:C_KV + 6 * LANES]
        outs['s_kv'].append(skv[:, :, :4 * LANES].reshape(nb, n_t, 4, NSA_KV_HEADS, HEAD_DIM))
        outs['s_dkv'].append(s_dkv_rows.reshape(nb, T_PAD, 2, DIFF_HEADS, DIFF_V_DIM)[:, :n_t])
        outs['s_win'].append(ps)
        outs['s_re'].append(shr.reshape(nb, SSM_GROUPS, SSM_STATE))
        outs['s_im'].append(shi.reshape(nb, SSM_GROUPS, SSM_STATE))
    y_prompt = xp.reshape(b, t, D_MODEL)
    y_sample = xs.reshape(nb, T_PAD, D_MODEL)[:, :n_t]
    st = lambda k: jnp.stack(outs[k])
    s_win_t = _roll_window(win_t, outs['s_win'], n_t)
    s_win = s_win_t.reshape(depth, nb, 2, NSA_KV_HEADS, HEAD_DIM, wb).transpose(0, 1, 5, 2, 3, 4)
    return (y_prompt, y_sample, st('p_kv'), st('p_dkv'), st('p_win'), st('p_re'), st('p_im'),
            st('s_kv'), st('s_dkv'), s_win, st('s_re'), st('s_im'))
```
